```python
import jax, jax.numpy as jnp
from jax import lax
import numpy as np

D_MODEL = 1024
BATCH = 8
SEQ = 4096
DEPTH = 2

D_MIX = D_MODEL
HEAD_DIM = 64
CONV_WIDTH = (D_MIX * 3) // 8
FOURIER_WIDTH = D_MIX // 4
SGU_WIDTH = D_MIX - CONV_WIDTH - FOURIER_WIDTH
CONV_HEADS = CONV_WIDTH // HEAD_DIM
FOURIER_GROUPS = FOURIER_WIDTH // HEAD_DIM
SGU_HEADS = SGU_WIDTH // HEAD_DIM
W_IN_COLS = 2 * CONV_WIDTH + FOURIER_WIDTH + 2 * SGU_WIDTH
CONV_KSIZE = 31
SGU_CHUNK = 128
N_EXPERTS = 64
N_EXPERT_GROUPS = 8
EXPERTS_PER_GROUP = N_EXPERTS // N_EXPERT_GROUPS
TOP_K = 2
D_FF_EXPERT = D_MODEL // 2
EXPERT_BLOCK = 128
N_MOD = 6
EPS = 1e-6

kernel_name = "hybrid_conv_fourier_sgu_grouped_moe_encoder"


def rmsnorm(x, g):
    xf = x.astype(jnp.float32)
    y = xf * lax.rsqrt(jnp.mean(xf * xf, axis=-1, keepdims=True) + EPS)
    return (y * g.astype(jnp.float32)).astype(x.dtype)


def layernorm(x, g, b):
    xf = x.astype(jnp.float32)
    mu = jnp.mean(xf, axis=-1, keepdims=True)
    var = jnp.mean(jnp.square(xf - mu), axis=-1, keepdims=True)
    y = (xf - mu) * lax.rsqrt(var + EPS)
    return (y * g.astype(jnp.float32) + b.astype(jnp.float32)).astype(x.dtype)


def conv_module(za, conv_w, conv_b, gn_g, gn_b):
    a, g = jnp.split(za, 2, axis=-1)
    h = a * jax.nn.sigmoid(g)
    h = lax.conv_general_dilated(
        h, conv_w.reshape(CONV_KSIZE, 1, CONV_WIDTH).astype(h.dtype),
        window_strides=(1,), padding=[(CONV_KSIZE // 2, CONV_KSIZE // 2)],
        dimension_numbers=("NWC", "WIO", "NWC"),
        feature_group_count=CONV_WIDTH) + conv_b
    Bsz, S, _ = h.shape
    hg = h.reshape(Bsz, S, CONV_HEADS, HEAD_DIM)
    hg = layernorm(hg, jnp.ones((HEAD_DIM,), jnp.float32), jnp.zeros((HEAD_DIM,), jnp.float32))
    h = hg.reshape(Bsz, S, CONV_WIDTH) * gn_g + gn_b
    return jax.nn.silu(h)


def fourier_mix(zb):
    Bsz, S, _ = zb.shape
    xg = zb.reshape(Bsz, S, FOURIER_GROUPS, HEAD_DIM).astype(jnp.float32)
    f = jnp.fft.fftn(xg, axes=(1, 3), norm="ortho").real
    return f.reshape(Bsz, S, FOURIER_WIDTH).astype(zb.dtype)


def spatial_gating(zc, ln_g, ln_b, w_s, b_s):
    z = jax.nn.gelu(zc)
    u, v = jnp.split(z, 2, axis=-1)
    v = layernorm(v, ln_g, ln_b)
    Bsz, S, _ = v.shape
    vc = v.reshape(Bsz, S // SGU_CHUNK, SGU_CHUNK, SGU_HEADS, HEAD_DIM)
    vs = jnp.einsum("hpq,bnqhd->bnphd", w_s, vc) + b_s.T[:, :, None]
    return u * vs.reshape(Bsz, S, SGU_WIDTH)


def token_mixer(h, w_in, w_out, conv_w, conv_b, gn_g, gn_b, ln_g, ln_b, w_s, b_s):
    z = h @ w_in
    za = z[..., :2 * CONV_WIDTH]
    zb = z[..., 2 * CONV_WIDTH:2 * CONV_WIDTH + FOURIER_WIDTH]
    zc = z[..., 2 * CONV_WIDTH + FOURIER_WIDTH:]
    ya = conv_module(za, conv_w, conv_b, gn_g, gn_b)
    yb = fourier_mix(zb)
    yc = spatial_gating(zc, ln_g, ln_b, w_s, b_s)
    return jnp.concatenate([ya, yb, yc], axis=-1) @ w_out


def moe(h, router_w, router_b, w1, w3, w2):
    Bsz, S, D = h.shape
    N = Bsz * S
    M = N * TOP_K
    xt = h.reshape(N, D)
    probs = jax.nn.softmax((xt @ router_w).astype(jnp.float32), axis=-1)
    sel = probs + router_b.astype(jnp.float32)
    sel_g = sel.reshape(N, N_EXPERT_GROUPS, EXPERTS_PER_GROUP)
    gscore = jnp.sum(lax.top_k(sel_g, TOP_K)[0], axis=-1)
    gidx = jnp.argmax(gscore, axis=-1)
    in_grp = jnp.take_along_axis(sel_g, gidx[:, None, None], axis=1)[:, 0]
    _, local = lax.top_k(in_grp, TOP_K)
    eidx = gidx[:, None] * EXPERTS_PER_GROUP + local
    gw = jnp.take_along_axis(probs, eidx, axis=1)
    gw = gw / jnp.sum(gw, axis=-1, keepdims=True)

    e_flat = eidx.reshape(M).astype(jnp.int32)
    tok = jnp.arange(M, dtype=jnp.int32) // TOP_K
    w_flat = gw.reshape(M)
    order = jnp.argsort(e_flat)
    e_s, t_s, w_s = e_flat[order], tok[order], w_flat[order]
    counts = jnp.bincount(e_flat, length=N_EXPERTS).astype(jnp.int32)
    starts = jnp.cumsum(counts) - counts
    pcounts = (counts + EXPERT_BLOCK - 1) // EXPERT_BLOCK * EXPERT_BLOCK
    pends = jnp.cumsum(pcounts)
    pstarts = pends - pcounts
    dest = pstarts[e_s] + (jnp.arange(M, dtype=jnp.int32) - starts[e_s])
    n_blocks = (M + N_EXPERTS * (EXPERT_BLOCK - 1) + EXPERT_BLOCK - 1) // EXPERT_BLOCK
    xp = jnp.zeros((n_blocks * EXPERT_BLOCK, D), h.dtype).at[dest].set(xt[t_s])
    blk_start = jnp.arange(n_blocks, dtype=jnp.int32) * EXPERT_BLOCK
    blk_e = jnp.minimum(jnp.searchsorted(pends, blk_start, side="right"), N_EXPERTS - 1)

    def expert_block(args):
        xb, e = args
        return (jax.nn.silu(xb @ w1[e]) * (xb @ w3[e])) @ w2[e]

    yp = lax.map(expert_block, (xp.reshape(n_blocks, EXPERT_BLOCK, D), blk_e))
    yp = yp.reshape(n_blocks * EXPERT_BLOCK, D)
    contrib = (yp[dest] * w_s[:, None]).astype(h.dtype)
    y = jnp.zeros((N, D), h.dtype).at[t_s].add(contrib)
    return y.reshape(Bsz, S, D)


def setup_inputs(seed: int = 0) -> dict:
    key = jax.random.key(seed)
    ks = jax.random.split(key, 24)
    f32 = jnp.float32
    nrm = lambda k, shape, s: jax.random.normal(k, shape, f32) * s
    return {
        "x": nrm(ks[0], (BATCH, SEQ, D_MODEL), 1.0),
        "c": nrm(ks[1], (BATCH, D_MODEL), 1.0),
        "ln1_g": 1.0 + nrm(ks[2], (DEPTH, D_MODEL), 0.02),
        "ln2_g": 1.0 + nrm(ks[3], (DEPTH, D_MODEL), 0.02),
        "w_ada": nrm(ks[4], (DEPTH, D_MODEL, N_MOD * D_MODEL), 0.5 * D_MODEL ** -0.5),
        "b_ada": nrm(ks[5], (DEPTH, N_MOD * D_MODEL), 0.02),
        "w_in": nrm(ks[6], (DEPTH, D_MODEL, W_IN_COLS), D_MODEL ** -0.5),
        "w_out": nrm(ks[7], (DEPTH, D_MIX, D_MODEL), D_MIX ** -0.5),
        "conv_w": nrm(ks[8], (DEPTH, CONV_KSIZE, CONV_WIDTH), CONV_KSIZE ** -0.5),
        "conv_b": nrm(ks[9], (DEPTH, CONV_WIDTH), 0.02),
        "conv_gn_g": 1.0 + nrm(ks[10], (DEPTH, CONV_WIDTH), 0.02),
        "conv_gn_b": nrm(ks[11], (DEPTH, CONV_WIDTH), 0.02),
        "sgu_ln_g": 1.0 + nrm(ks[12], (DEPTH, SGU_WIDTH), 0.02),
        "sgu_ln_b": nrm(ks[13], (DEPTH, SGU_WIDTH), 0.02),
        "sgu_w": nrm(ks[14], (DEPTH, SGU_HEADS, SGU_CHUNK, SGU_CHUNK), SGU_CHUNK ** -0.5),
        "sgu_b": 1.0 + nrm(ks[15], (DEPTH, SGU_HEADS, SGU_CHUNK), 0.02),
        "router_w": nrm(ks[16], (D_MODEL, N_EXPERTS), D_MODEL ** -0.5),
        "router_b": nrm(ks[17], (N_EXPERTS,), 0.01),
        "exp_w1": nrm(ks[18], (DEPTH, N_EXPERTS, D_MODEL, D_FF_EXPERT), D_MODEL ** -0.5),
        "exp_w3": nrm(ks[19], (DEPTH, N_EXPERTS, D_MODEL, D_FF_EXPERT), D_MODEL ** -0.5),
        "exp_w2": nrm(ks[20], (DEPTH, N_EXPERTS, D_FF_EXPERT, D_MODEL), D_FF_EXPERT ** -0.5),
        "final_g": 1.0 + nrm(ks[21], (D_MODEL,), 0.02),
    }


def reference(x, c, ln1_g, ln2_g, w_ada, b_ada, w_in, w_out, conv_w, conv_b, conv_gn_g, conv_gn_b,
              sgu_ln_g, sgu_ln_b, sgu_w, sgu_b, router_w, router_b, exp_w1, exp_w3, exp_w2, final_g):
    c_act = jax.nn.silu(c)
    for l in range(DEPTH):
        mod = (c_act @ w_ada[l] + b_ada[l]).reshape(c.shape[0], N_MOD, D_MODEL)[:, :, None, :]
        shift1, scale1, gate1 = mod[:, 0], mod[:, 1], mod[:, 2]
        shift2, scale2, gate2 = mod[:, 3], mod[:, 4], mod[:, 5]
        h = rmsnorm(x, ln1_g[l]) * (1.0 + scale1) + shift1
        x = x + gate1 * token_mixer(h, w_in[l], w_out[l], conv_w[l], conv_b[l], conv_gn_g[l],
                                    conv_gn_b[l], sgu_ln_g[l], sgu_ln_b[l], sgu_w[l], sgu_b[l])
        h = rmsnorm(x, ln2_g[l]) * (1.0 + scale2) + shift2
        x = x + gate2 * moe(h, router_w, router_b, exp_w1[l], exp_w3[l], exp_w2[l])
    return rmsnorm(x, final_g)
```

```python
import functools

import jax
import jax.numpy as jnp
from jax import lax
from jax.experimental import pallas as pl
from jax.experimental.pallas import tpu as pltpu

F32 = jnp.float32
BF16 = jnp.bfloat16
I32 = jnp.int32
U32 = jnp.uint32
HIGHEST = lax.Precision.HIGHEST

D = 1024
HEAD = 64
CONV_W = 384
FOUR_W = 256
SGU_W = 384
SGU_HEADS = SGU_W // HEAD
Z_COLS = 2 * CONV_W + FOUR_W + 2 * SGU_W
KSIZE = 31
HALO = 16
CHUNK = 128
N_EXP = 64
N_GRP = 8
EPG = N_EXP // N_GRP
D_FF = D // 2
EPS = 1e-6

T_MIX = 512
T_DFT = 512
T_TOK = 512
T_CMB = 256
BLK = 256
CONV_ROWS = 64
VMEM_LIMIT = 56 * 1024 * 1024


def _cparams(sem):
    return pltpu.CompilerParams(dimension_semantics=sem, vmem_limit_bytes=VMEM_LIMIT)


def _pack_bf16_pair(a, b):
    ua = lax.bitcast_convert_type(a.astype(BF16).astype(F32), U32) >> 16
    ub = lax.bitcast_convert_type(b.astype(BF16).astype(F32), U32) & jnp.uint32(0xFFFF0000)
    return ua | ub


def _unpack_bf16_pair(p):
    a = lax.bitcast_convert_type(p << 16, F32)
    b = lax.bitcast_convert_type(p & jnp.uint32(0xFFFF0000), F32)
    return a, b


def _ada_kernel(c_ref, w_ref, b_ref, o_ref):
    c = c_ref[...]
    ca = c * jax.nn.sigmoid(c)
    o_ref[0] = jnp.dot(ca, w_ref[0], precision=HIGHEST, preferred_element_type=F32) + b_ref[0]


def _ada_mod(c, w_ada, b_ada):
    depth, _, ncol = w_ada.shape
    bsz = c.shape[0]
    tn = 1536
    return pl.pallas_call(
        _ada_kernel,
        grid=(depth, ncol // tn),
        in_specs=[pl.BlockSpec((bsz, D), lambda l, j: (0, 0)),
                  pl.BlockSpec((1, D, tn), lambda l, j: (l, 0, j)),
                  pl.BlockSpec((1, 1, tn), lambda l, j: (l, 0, j))],
        out_specs=pl.BlockSpec((1, bsz, tn), lambda l, j: (l, 0, j)),
        out_shape=jax.ShapeDtypeStruct((depth, bsz, ncol), F32),
        compiler_params=_cparams(("arbitrary", "arbitrary")),
        name="ada_mod",
    )(c, w_ada, b_ada.reshape(depth, 1, ncol))


def _mixer_in_kernel(xm_ref, xp_ref, xn_ref, mod_ref, g1_ref, win_ref, cw_ref, cb_ref, gng_ref, gnb_ref,
                     lng_ref, lnb_ref, sw_ref, sb_ref, dft_ref, gavg_ref,
                     yac_ref, pq_ref, glu_scr, sh_scr, conv_scr):
    i = pl.program_id(1)
    n_i = pl.num_programs(1)
    T = T_MIX
    mod = mod_ref[0, 0]
    shift1 = mod[0:1, :]
    scale1 = mod[1:2, :]
    g1 = g1_ref[0]

    def norm_mod(x):
        ms = jnp.mean(x * x, axis=-1, keepdims=True)
        return (x * lax.rsqrt(ms + EPS) * g1) * (1.0 + scale1) + shift1

    h = norm_mod(xm_ref[0]).astype(BF16)
    z = jnp.dot(h, win_ref[0], preferred_element_type=F32)

    hh = norm_mod(jnp.concatenate([xp_ref[0], xn_ref[0]], axis=0)).astype(BF16)
    zh = jnp.dot(hh, win_ref[0, :, 0:2 * CONV_W], preferred_element_type=F32)
    glu_h = zh[:, 0:CONV_W] * jax.nn.sigmoid(zh[:, CONV_W:2 * CONV_W])
    glu_scr[0:HALO, :] = jnp.where(i > 0, glu_h[0:HALO], 0.0)
    glu_scr[HALO + T:2 * HALO + T, :] = jnp.where(i < n_i - 1, glu_h[HALO:2 * HALO], 0.0)
    glu_scr[HALO:HALO + T, :] = z[:, 0:CONV_W] * jax.nn.sigmoid(z[:, CONV_W:2 * CONV_W])

    off = HALO - KSIZE // 2
    for b in range(8):
        sh_scr[b] = glu_scr[b:b + T + 3 * 8, :]

    def conv_chunk(c, carry):
        r0 = pl.multiple_of(c * CONV_ROWS, CONV_ROWS)
        acc = jnp.broadcast_to(cb_ref[0], (CONV_ROWS, CONV_W))
        for k in range(KSIZE):
            a, b = divmod(k + off, 8)
            acc = acc + sh_scr[b, pl.ds(r0 + 8 * a, CONV_ROWS), :] * cw_ref[0, k:k + 1, :]
        conv_scr[pl.ds(r0, CONV_ROWS), :] = acc
        return carry

    lax.fori_loop(0, T // CONV_ROWS, conv_chunk, 0)
    hc = conv_scr[...]
    gavg = gavg_ref[...]
    mu = jnp.dot(hc.astype(BF16), gavg, preferred_element_type=F32)
    dc = hc - mu
    var = jnp.dot((dc * dc).astype(BF16), gavg, preferred_element_type=F32)
    hn = dc * lax.rsqrt(var + EPS) * gng_ref[0] + gnb_ref[0]
    ya = hn * jax.nn.sigmoid(hn)
    yac_ref[0, :, 0:CONV_W] = ya.astype(BF16)

    zb = z[:, 2 * CONV_W:2 * CONV_W + FOUR_W].astype(BF16)
    pq = jnp.dot(zb, dft_ref[...], preferred_element_type=F32)
    pq_ref[0, 0] = pq[:, 0:FOUR_W].astype(BF16)
    pq_ref[0, 1] = pq[:, FOUR_W:2 * FOUR_W].astype(BF16)

    c0 = 2 * CONV_W + FOUR_W
    zc = z[:, c0:c0 + 2 * SGU_W]
    zc = 0.5 * zc * (1.0 + jnp.tanh(0.7978845608028654 * (zc + 0.044715 * (zc * zc * zc))))
    u = zc[:, 0:SGU_W]
    v = zc[:, SGU_W:2 * SGU_W]
    vm = jnp.mean(v, axis=-1, keepdims=True)
    vd = v - vm
    vv = jnp.mean(vd * vd, axis=-1, keepdims=True)
    vn = (vd * lax.rsqrt(vv + EPS) * lng_ref[0] + lnb_ref[0]).astype(BF16)
    n_chunk = T // CHUNK
    lane = lax.broadcasted_iota(I32, (CHUNK, 2 * HEAD), 1)
    for pr in range(SGU_HEADS // 2):
        cols = slice(2 * HEAD * pr, 2 * HEAD * (pr + 1))
        rhs = jnp.concatenate([vn[n * CHUNK:(n + 1) * CHUNK, cols] for n in range(n_chunk)], axis=1)
        lo = jnp.dot(sw_ref[0, 2 * pr], rhs, preferred_element_type=F32)
        hi = jnp.dot(sw_ref[0, 2 * pr + 1], rhs, preferred_element_type=F32)
        for n in range(n_chunk):
            sl = slice(n * 2 * HEAD, (n + 1) * 2 * HEAD)
            vs = jnp.where(lane < HEAD, lo[:, sl], hi[:, sl]) + sb_ref[0, :, cols]
            rows = slice(n * CHUNK, (n + 1) * CHUNK)
            yac_ref[0, rows, CONV_W + 2 * HEAD * pr:CONV_W + 2 * HEAD * (pr + 1)] = (u[rows, cols] * vs).astype(BF16)


def _mixer_in(x, mod, l, ln1_g, w_in_bf, conv_w, conv_b, gn_g, gn_b, ln_g, ln_b, sgu_w_bf, sgu_bias, dft64, gavg):
    bsz, seq, _ = x.shape
    T = T_MIX
    n_i = seq // T
    hb = T // HALO
    n_h = seq // HALO
    vec = lambda w: pl.BlockSpec((1, 1, w), lambda b, i: (l, 0, 0))
    return pl.pallas_call(
        _mixer_in_kernel,
        grid=(bsz, n_i),
        in_specs=[
            pl.BlockSpec((1, T, D), lambda b, i: (b, i, 0)),
            pl.BlockSpec((1, HALO, D), lambda b, i: (b, jnp.maximum(i * hb - 1, 0), 0)),
            pl.BlockSpec((1, HALO, D), lambda b, i: (b, jnp.minimum((i + 1) * hb, n_h - 1), 0)),
            pl.BlockSpec((1, 1, 8, D), lambda b, i: (l, b, 0, 0)),
            vec(D),
            pl.BlockSpec((1, D, Z_COLS), lambda b, i: (l, 0, 0)),
            pl.BlockSpec((1, KSIZE + 1, CONV_W), lambda b, i: (l, 0, 0)),
            vec(CONV_W), vec(CONV_W), vec(CONV_W), vec(SGU_W), vec(SGU_W),
            pl.BlockSpec((1, SGU_HEADS, CHUNK, CHUNK), lambda b, i: (l, 0, 0, 0)),
            pl.BlockSpec((1, CHUNK, SGU_W), lambda b, i: (l, 0, 0)),
            pl.BlockSpec((FOUR_W, 2 * FOUR_W), lambda b, i: (0, 0)),
            pl.BlockSpec((CONV_W, CONV_W), lambda b, i: (0, 0)),
        ],
        out_specs=[
            pl.BlockSpec((1, T, CONV_W + SGU_W), lambda b, i: (b, i, 0)),
            pl.BlockSpec((1, 2, T, FOUR_W), lambda b, i: (b, 0, i, 0)),
        ],
        out_shape=[
            jax.ShapeDtypeStruct((bsz, seq, CONV_W + SGU_W), BF16),
            jax.ShapeDtypeStruct((bsz, 2, seq, FOUR_W), BF16),
        ],
        scratch_shapes=[
            pltpu.VMEM((T + 2 * HALO, CONV_W), F32),
            pltpu.VMEM((8, T + 3 * 8, CONV_W), F32),
            pltpu.VMEM((T, CONV_W), F32),
        ],
        compiler_params=_cparams(("arbitrary", "arbitrary")),
        name="mixer_in",
    )(x, x, x, mod, ln1_g, w_in_bf, conv_w, conv_b, gn_g, gn_b, ln_g, ln_b, sgu_w_bf, sgu_bias, dft64, gavg)


def _seq_dft_kernel(cs_ref, pq_ref, o_ref):
    o_ref[0] = jnp.dot(cs_ref[...], pq_ref[0], preferred_element_type=F32).astype(BF16)


def _seq_dft(cs, pq):
    bsz, two_s, _ = pq.shape
    seq = two_s // 2
    return pl.pallas_call(
        _seq_dft_kernel,
        grid=(seq // T_DFT, bsz),
        in_specs=[pl.BlockSpec((T_DFT, two_s), lambda k, b: (k, 0)),
                  pl.BlockSpec((1, two_s, FOUR_W), lambda k, b: (b, 0, 0))],
        out_specs=pl.BlockSpec((1, T_DFT, FOUR_W), lambda k, b: (b, k, 0)),
        out_shape=jax.ShapeDtypeStruct((bsz, seq, FOUR_W), BF16),
        compiler_params=_cparams(("arbitrary", "arbitrary")),
        name="seq_dft",
    )(cs, pq)


def _mixer_out_kernel(yac_ref, yb_ref, x_ref, mod_ref, g2_ref, wout_ref, rwt_ref, rb_ref,
                      x1_ref, h2p_ref, info_ref, cnt_ref, cnt_scr):
    i = pl.program_id(0)
    T = T_MIX

    @pl.when(i == 0)
    def _():
        cnt_scr[...] = jnp.zeros_like(cnt_scr)

    mod = mod_ref[0, 0]
    gate1 = mod[2:3, :]
    shift2 = mod[3:4, :]
    scale2 = mod[4:5, :]
    yac = yac_ref[...]
    ycat = jnp.concatenate([yac[:, 0:CONV_W], yb_ref[...], yac[:, CONV_W:CONV_W + SGU_W]], axis=1)
    o = jnp.dot(ycat, wout_ref[0], preferred_element_type=F32)
    x1 = x_ref[...] + gate1 * o
    x1_ref[...] = x1
    ms = jnp.mean(x1 * x1, axis=-1, keepdims=True)
    h2 = (x1 * lax.rsqrt(ms + EPS) * g2_ref[0]) * (1.0 + scale2) + shift2
    h2p_ref[...] = _pack_bf16_pair(h2[:, 0:D // 2], h2[:, D // 2:D])

    logits = lax.dot_general(rwt_ref[...], h2, (((1,), (1,)), ((), ())),
                             precision=HIGHEST, preferred_element_type=F32)
    mx = jnp.max(logits, axis=0, keepdims=True)
    ex = jnp.exp(logits - mx)
    probs = ex / jnp.sum(ex, axis=0, keepdims=True)
    sel = probs + rb_ref[...]
    sel3 = sel.reshape(N_GRP, EPG, T)
    probs3 = probs.reshape(N_GRP, EPG, T)
    jj = lax.broadcasted_iota(I32, (N_GRP, EPG, T), 1)
    m1 = jnp.max(sel3, axis=1, keepdims=True)
    i1 = jnp.min(jnp.where(sel3 == m1, jj, EPG), axis=1, keepdims=True)
    rest = jnp.where(jj == i1, -jnp.inf, sel3)
    m2 = jnp.max(rest, axis=1, keepdims=True)
    i2 = jnp.min(jnp.where(rest == m2, jj, EPG), axis=1, keepdims=True)
    gscore = m1 + m2
    gg = lax.broadcasted_iota(I32, (N_GRP, 1, T), 0)
    gmax = jnp.max(gscore, axis=0, keepdims=True)
    gidx = jnp.min(jnp.where(gscore == gmax, gg, N_GRP), axis=0, keepdims=True)
    ing = gg == gidx
    pick = lambda a, zero: jnp.sum(jnp.where(ing, a, zero), axis=0)
    p1 = jnp.sum(jnp.where(jj == i1, probs3, 0.0), axis=1, keepdims=True)
    p2 = jnp.sum(jnp.where(jj == i2, probs3, 0.0), axis=1, keepdims=True)
    pa = pick(p1, 0.0)
    pb = pick(p2, 0.0)
    gbase = gidx[0] * EPG
    e0 = gbase + pick(i1, 0)
    e1 = gbase + pick(i2, 0)
    den = pa + pb
    gw0 = pa / den
    gw1 = pb / den

    ee = lax.broadcasted_iota(I32, (N_EXP, T), 0)
    oh0 = ee == e0
    oh1 = ee == e1
    amat = jnp.where(oh0 | oh1, 1.0, 0.0)
    rr = lax.broadcasted_iota(I32, (T, T), 0)
    cc = lax.broadcasted_iota(I32, (T, T), 1)
    upper = jnp.where(rr < cc, 1.0, 0.0).astype(BF16)
    before = jnp.dot(amat.astype(BF16), upper, preferred_element_type=F32) + cnt_scr[...]
    r0 = jnp.sum(jnp.where(oh0, before, 0.0), axis=0, keepdims=True)
    r1 = jnp.sum(jnp.where(oh1, before, 0.0), axis=0, keepdims=True)
    cnt_scr[...] = cnt_scr[...] + jnp.sum(amat, axis=1, keepdims=True)
    cnt_ref[...] = cnt_scr[...]

    rid = lax.broadcasted_iota(I32, (8, T), 0)
    rows = jnp.zeros((8, T), F32)
    for k, val in enumerate((gw0, gw1, e0.astype(F32), e1.astype(F32), r0, r1)):
        rows = jnp.where(rid == k, val, rows)
    rows = jnp.concatenate([rows, jnp.zeros((CHUNK - 8, T), F32)], axis=0)
    info_ref[...] = rows.T


def _mixer_out(yac, yb, x, mod, l, ln2_g, w_out_bf, router_wt, router_b, bsz):
    n_tok = x.shape[0]
    T = T_MIX
    per_b = n_tok // bsz // T
    row = lambda w: pl.BlockSpec((T, w), lambda i: (i, 0))
    return pl.pallas_call(
        _mixer_out_kernel,
        grid=(n_tok // T,),
        in_specs=[
            row(CONV_W + SGU_W), row(FOUR_W), row(D),
            pl.BlockSpec((1, 1, 8, D), lambda i: (l, i // per_b, 0, 0)),
            pl.BlockSpec((1, 1, D), lambda i: (l, 0, 0)),
            pl.BlockSpec((1, D, D), lambda i: (l, 0, 0)),
            pl.BlockSpec((N_EXP, D), lambda i: (0, 0)),
            pl.BlockSpec((N_EXP, 1), lambda i: (0, 0)),
        ],
        out_specs=[row(D), row(D // 2), row(CHUNK), pl.BlockSpec((N_EXP, 1), lambda i: (0, 0))],
        out_shape=[
            jax.ShapeDtypeStruct((n_tok, D), F32),
            jax.ShapeDtypeStruct((n_tok, D // 2), U32),
            jax.ShapeDtypeStruct((n_tok, CHUNK), F32),
            jax.ShapeDtypeStruct((N_EXP, 1), F32),
        ],
        scratch_shapes=[pltpu.VMEM((N_EXP, 1), F32)],
        compiler_params=_cparams(("arbitrary",)),
        name="mixer_out",
    )(yac, yb, x, mod, ln2_g, w_out_bf, router_wt, router_b)


def _dispatch_kernel(dest_ref, h_ref, xp_in_ref, xp_ref, sem):
    del xp_in_ref
    i = pl.program_id(0)
    n_tok = dest_ref.shape[0] // 2
    base = i * T_TOK

    def row_copy(t, d):
        return pltpu.make_async_copy(h_ref.at[pl.ds(t, 1)], xp_ref.at[pl.ds(d, 1)], sem)

    def issue(t, carry):
        row_copy(t, dest_ref[base + t]).start()
        row_copy(t, dest_ref[n_tok + base + t]).start()
        return carry

    lax.fori_loop(0, T_TOK, issue, 0, unroll=8)

    def drain(t, carry):
        row_copy(t, 0).wait()
        row_copy(t, 0).wait()
        return carry

    lax.fori_loop(0, T_TOK, drain, 0, unroll=8)


def _dispatch(dest, h2p, n_rows):
    n_tok = h2p.shape[0]
    xp0 = jnp.zeros((n_rows, D // 2), U32)
    return pl.pallas_call(
        _dispatch_kernel,
        grid_spec=pltpu.PrefetchScalarGridSpec(
            num_scalar_prefetch=1,
            grid=(n_tok // T_TOK,),
            in_specs=[pl.BlockSpec((T_TOK, D // 2), lambda i, d: (i, 0)),
                      pl.BlockSpec(memory_space=pl.ANY)],
            out_specs=pl.BlockSpec(memory_space=pl.ANY),
            scratch_shapes=[pltpu.SemaphoreType.DMA(())],
        ),
        out_shape=jax.ShapeDtypeStruct((n_rows, D // 2), U32),
        input_output_aliases={2: 0},
        compiler_params=_cparams(("arbitrary",)),
        name="dispatch",
    )(dest, h2p, xp0)


def _experts_kernel(be_ref, first_ref, nused_ref, xp_ref, w1_ref, w3_ref, w2_ref, yp_ref, w13_scr, w2_scr):
    i = pl.program_id(0)

    @pl.when(i < nused_ref[0])
    def _():
        @pl.when(first_ref[i] == 1)
        def _():
            w13_scr[:, 0:D_FF] = w1_ref[0, 0].astype(BF16)
            w13_scr[:, D_FF:2 * D_FF] = w3_ref[0, 0].astype(BF16)
            w2_scr[...] = w2_ref[0, 0].astype(BF16)

        a, b = _unpack_bf16_pair(xp_ref[...])
        x = jnp.concatenate([a.astype(BF16), b.astype(BF16)], axis=1)
        h13 = jnp.dot(x, w13_scr[...], preferred_element_type=F32)
        h1 = h13[:, 0:D_FF]
        act = (h1 * jax.nn.sigmoid(h1) * h13[:, D_FF:2 * D_FF]).astype(BF16)
        y = jnp.dot(act, w2_scr[...], preferred_element_type=F32)
        yp_ref[...] = _pack_bf16_pair(y[:, 0:D // 2], y[:, D // 2:D])

    @pl.when(i >= nused_ref[0])
    def _():
        yp_ref[...] = jnp.zeros_like(yp_ref)


def _experts(blk_e, blk_first, n_used, xp, w1, w3, w2, l):
    n_rows = xp.shape[0]
    n_blk = n_rows // BLK
    blk = lambda i, be, bf, nu: (jnp.minimum(i, nu[0] - 1), 0)
    return pl.pallas_call(
        _experts_kernel,
        grid_spec=pltpu.PrefetchScalarGridSpec(
            num_scalar_prefetch=3,
            grid=(n_blk,),
            in_specs=[
                pl.BlockSpec((BLK, D // 2), blk),
                pl.BlockSpec((1, 1, D, D_FF), lambda i, be, bf, nu: (l, be[i], 0, 0)),
                pl.BlockSpec((1, 1, D, D_FF), lambda i, be, bf, nu: (l, be[i], 0, 0)),
                pl.BlockSpec((1, 1, D_FF, D), lambda i, be, bf, nu: (l, be[i], 0, 0)),
            ],
            out_specs=pl.BlockSpec((BLK, D // 2), lambda i, be, bf, nu: (i, 0)),
            scratch_shapes=[pltpu.VMEM((D, 2 * D_FF), BF16), pltpu.VMEM((D_FF, D), BF16)],
        ),
        out_shape=jax.ShapeDtypeStruct((n_rows, D // 2), U32),
        compiler_params=_cparams(("arbitrary",)),
        name="experts",
    )(blk_e, blk_first, n_used, xp, w1, w3, w2)


def _combine_kernel(final, dest_ref, yp_ref, x1_ref, info_ref, mod_ref, fg_ref, o_ref, buf, sem):
    i = pl.program_id(0)
    n_tok = dest_ref.shape[0] // 2
    base = i * T_CMB

    def row_copy(k, t, d):
        return pltpu.make_async_copy(yp_ref.at[pl.ds(d, 1)], buf.at[k, pl.ds(t, 1)], sem)

    def issue(t, carry):
        row_copy(0, t, dest_ref[base + t]).start()
        row_copy(1, t, dest_ref[n_tok + base + t]).start()
        return carry

    lax.fori_loop(0, T_CMB, issue, 0, unroll=8)

    def drain(t, carry):
        row_copy(0, t, 0).wait()
        row_copy(1, t, 0).wait()
        return carry

    lax.fori_loop(0, T_CMB, drain, 0, unroll=8)

    info = info_ref[...]
    gw0 = info[:, 0:1]
    gw1 = info[:, 1:2]
    a0, b0 = _unpack_bf16_pair(buf[0])
    a1, b1 = _unpack_bf16_pair(buf[1])
    y = jnp.concatenate([gw0 * a0 + gw1 * a1, gw0 * b0 + gw1 * b1], axis=1)
    gate2 = mod_ref[0, 0][5:6, :]
    x2 = x1_ref[...] + gate2 * y
    if final:
        ms = jnp.mean(x2 * x2, axis=-1, keepdims=True)
        x2 = x2 * lax.rsqrt(ms + EPS) * fg_ref[...]
    o_ref[...] = x2


def _combine(dest, yp, x1, info, mod, l, final_g, bsz, final):
    n_tok = x1.shape[0]
    per_b = n_tok // bsz // T_CMB
    return pl.pallas_call(
        functools.partial(_combine_kernel, final),
        grid_spec=pltpu.PrefetchScalarGridSpec(
            num_scalar_prefetch=1,
            grid=(n_tok // T_CMB,),
            in_specs=[
                pl.BlockSpec(memory_space=pl.ANY),
                pl.BlockSpec((T_CMB, D), lambda i, d: (i, 0)),
                pl.BlockSpec((T_CMB, CHUNK), lambda i, d: (i, 0)),
                pl.BlockSpec((1, 1, 8, D), lambda i, d: (l, i // per_b, 0, 0)),
                pl.BlockSpec((1, D), lambda i, d: (0, 0)),
            ],
            out_specs=pl.BlockSpec((T_CMB, D), lambda i, d: (i, 0)),
            scratch_shapes=[pltpu.VMEM((2, T_CMB, D // 2), U32), pltpu.SemaphoreType.DMA(())],
        ),
        out_shape=jax.ShapeDtypeStruct((n_tok, D), F32),
        compiler_params=_cparams(("arbitrary",)),
        name="combine",
    )(dest, yp, x1, info, mod, final_g)


def _dft_tables(seq):
    scale = 1.0 / (seq * HEAD) ** 0.5
    k = lax.broadcasted_iota(I32, (seq, seq), 0)
    n = lax.broadcasted_iota(I32, (seq, seq), 1)
    ang = ((k * n) % seq).astype(F32) * (2.0 * jnp.pi / seq)
    cs = jnp.concatenate([jnp.cos(ang) * scale, -jnp.sin(ang) * scale], axis=1).astype(BF16)
    d = lax.broadcasted_iota(I32, (FOUR_W, FOUR_W), 0)
    q = lax.broadcasted_iota(I32, (FOUR_W, FOUR_W), 1)
    same = (d // HEAD) == (q // HEAD)
    ang64 = ((d * q) % HEAD).astype(F32) * (2.0 * jnp.pi / HEAD)
    dft64 = jnp.concatenate([jnp.where(same, jnp.cos(ang64), 0.0),
                             jnp.where(same, jnp.sin(ang64), 0.0)], axis=1).astype(BF16)
    return cs, dft64


def _routing_tables(info, counts_f, n_blk):
    counts = counts_f[:, 0].astype(I32)
    pc = (counts + BLK - 1) // BLK * BLK
    pends = jnp.cumsum(pc)
    pstarts = pends - pc
    eid = lax.broadcasted_iota(I32, (1, N_EXP), 1)

    def dest_of(e, r):
        return jnp.sum(jnp.where(e[:, None] == eid, pstarts[None, :], 0), axis=1) + r

    e0 = info[:, 2].astype(I32)
    e1 = info[:, 3].astype(I32)
    r0 = info[:, 4].astype(I32)
    r1 = info[:, 5].astype(I32)
    dest = jnp.concatenate([dest_of(e0, r0), dest_of(e1, r1)])
    n_used = pends[-1] // BLK
    bi = jnp.arange(n_blk, dtype=I32)
    blk_e = jnp.sum((pends[None, :] <= (jnp.minimum(bi, n_used - 1) * BLK)[:, None]).astype(I32), axis=1)
    blk_e = jnp.minimum(blk_e, N_EXP - 1)
    first = jnp.concatenate([jnp.ones((1,), I32), (blk_e[1:] != blk_e[:-1]).astype(I32)])
    return dest, blk_e, first, n_used.reshape(1).astype(I32)


def kernel(x, c, ln1_g, ln2_g, w_ada, b_ada, w_in, w_out, conv_w, conv_b, conv_gn_g, conv_gn_b, sgu_ln_g,
           sgu_ln_b, sgu_w, sgu_b, router_w, router_b, exp_w1, exp_w3, exp_w2, final_g):
    bsz, seq, _ = x.shape
    depth = w_in.shape[0]
    n_tok = bsz * seq
    n_blk = (2 * n_tok + N_EXP * (BLK - 1) + BLK - 1) // BLK
    r3 = lambda a: a.reshape(depth, 1, a.shape[-1])

    mod = jnp.pad(_ada_mod(c, w_ada, b_ada).reshape(depth, bsz, 6, D), ((0, 0), (0, 0), (0, 2), (0, 0)))
    cs, dft64 = _dft_tables(seq)
    hd = lax.broadcasted_iota(I32, (CONV_W, CONV_W), 0) // HEAD
    gavg = jnp.where(hd == hd.T, 1.0 / HEAD, 0.0).astype(BF16)
    w_in_bf = w_in.astype(BF16)
    w_out_bf = w_out.astype(BF16)
    sgu_w_bf = sgu_w.astype(BF16)
    sgu_bias = jnp.repeat(jnp.swapaxes(sgu_b, 1, 2), HEAD, axis=2)
    conv_w_p = jnp.pad(conv_w, ((0, 0), (0, 1), (0, 0)))
    router_wt = router_w.T
    router_bc = router_b.reshape(N_EXP, 1)
    fg = final_g.reshape(1, D)

    for l in range(depth):
        yac, pq = _mixer_in(x, mod, l, r3(ln1_g), w_in_bf, conv_w_p, r3(conv_b), r3(conv_gn_g), r3(conv_gn_b),
                            r3(sgu_ln_g), r3(sgu_ln_b), sgu_w_bf, sgu_bias, dft64, gavg)
        yb = _seq_dft(cs, pq.reshape(bsz, 2 * seq, FOUR_W))
        x1, h2p, info, counts = _mixer_out(yac.reshape(n_tok, CONV_W + SGU_W), yb.reshape(n_tok, FOUR_W),
                                           x.reshape(n_tok, D), mod, l, r3(ln2_g), w_out_bf, router_wt,
                                           router_bc, bsz)
        dest, blk_e, blk_first, n_used = _routing_tables(info, counts, n_blk)
        xp = _dispatch(dest, h2p, n_blk * BLK)
        yp = _experts(blk_e, blk_first, n_used, xp, exp_w1, exp_w3, exp_w2, l)
        x = _combine(dest, yp, x1, info, mod, l, fg, bsz, l == depth - 1).reshape(bsz, seq, D)
    return x
```

```python
import functools

import jax
import jax.numpy as jnp
from jax import lax
from jax.experimental import pallas as pl
from jax.experimental.pallas import tpu as pltpu

F32 = jnp.float32
BF16 = jnp.bfloat16
I32 = jnp.int32
U32 = jnp.uint32
HIGHEST = lax.Precision.HIGHEST

D = 1024
HEAD = 64
CONV_W = 384
FOUR_W = 256
SGU_W = 384
SGU_HEADS = SGU_W // HEAD
Z_COLS = 2 * CONV_W + FOUR_W + 2 * SGU_W
KSIZE = 31
HALO = 16
CHUNK = 128
N_EXP = 64
N_GRP = 8
EPG = N_EXP // N_GRP
D_FF = D // 2
EPS = 1e-6

T_MIX = 512
T_DFT = 512
T_TOK = 512
T_CMB = 256
BLK = 256
CONV_ROWS = 64
TAB_ROWS = 64
VMEM_LIMIT = 56 * 1024 * 1024


def _cparams(sem):
    return pltpu.CompilerParams(dimension_semantics=sem, vmem_limit_bytes=VMEM_LIMIT)


def _pack_bf16_pair(a, b):
    ua = lax.bitcast_convert_type(a.astype(BF16).astype(F32), U32) >> 16
    ub = lax.bitcast_convert_type(b.astype(BF16).astype(F32), U32) & jnp.uint32(0xFFFF0000)
    return ua | ub


def _unpack_bf16_pair(p):
    a = lax.bitcast_convert_type(p << 16, F32)
    b = lax.bitcast_convert_type(p & jnp.uint32(0xFFFF0000), F32)
    return a, b


def _ada_kernel(c_ref, w_ref, b_ref, o_ref):
    c = c_ref[...]
    ca = c * jax.nn.sigmoid(c)
    o_ref[0] = jnp.dot(ca, w_ref[0], precision=HIGHEST, preferred_element_type=F32) + b_ref[0]


def _ada_mod(c, w_ada, b_ada):
    depth, _, ncol = w_ada.shape
    bsz = c.shape[0]
    tn = 1536
    return pl.pallas_call(
        _ada_kernel,
        grid=(depth, ncol // tn),
        in_specs=[pl.BlockSpec((bsz, D), lambda l, j: (0, 0)),
                  pl.BlockSpec((1, D, tn), lambda l, j: (l, 0, j)),
                  pl.BlockSpec((1, 1, tn), lambda l, j: (l, 0, j))],
        out_specs=pl.BlockSpec((1, bsz, tn), lambda l, j: (l, 0, j)),
        out_shape=jax.ShapeDtypeStruct((depth, bsz, ncol), F32),
        compiler_params=_cparams(("arbitrary", "arbitrary")),
        name="ada_mod",
    )(c, w_ada, b_ada.reshape(depth, 1, ncol))


def _mixer_in_kernel(xm_ref, xp_ref, xn_ref, mod_ref, g1_ref, win_ref, cw_ref, cb_ref, gng_ref, gnb_ref,
                     lng_ref, lnb_ref, sw_ref, sb_ref, dft_ref, gavg_ref,
                     yac_ref, pq_ref, glu_scr, sh_scr, conv_scr):
    i = pl.program_id(1)
    n_i = pl.num_programs(1)
    T = T_MIX
    mod = mod_ref[0, 0]
    shift1 = mod[0:1, :]
    scale1 = mod[1:2, :]
    g1 = g1_ref[0]

    def norm_mod(x):
        ms = jnp.mean(x * x, axis=-1, keepdims=True)
        return (x * lax.rsqrt(ms + EPS) * g1) * (1.0 + scale1) + shift1

    h = norm_mod(xm_ref[0]).astype(BF16)
    z = jnp.dot(h, win_ref[0], preferred_element_type=F32)

    hh = norm_mod(jnp.concatenate([xp_ref[0], xn_ref[0]], axis=0)).astype(BF16)
    zh = jnp.dot(hh, win_ref[0, :, 0:2 * CONV_W], preferred_element_type=F32)
    glu_h = zh[:, 0:CONV_W] * jax.nn.sigmoid(zh[:, CONV_W:2 * CONV_W])
    glu_scr[0:HALO, :] = jnp.where(i > 0, glu_h[0:HALO], 0.0)
    glu_scr[HALO + T:2 * HALO + T, :] = jnp.where(i < n_i - 1, glu_h[HALO:2 * HALO], 0.0)
    glu_scr[HALO:HALO + T, :] = z[:, 0:CONV_W] * jax.nn.sigmoid(z[:, CONV_W:2 * CONV_W])

    off = HALO - KSIZE // 2
    for b in range(8):
        sh_scr[b] = glu_scr[b:b + T + 3 * 8, :]

    for c in range(T // CONV_ROWS):
        r0 = c * CONV_ROWS
        acc = jnp.broadcast_to(cb_ref[0], (CONV_ROWS, CONV_W))
        for k in range(KSIZE):
            a, b = divmod(k + off, 8)
            acc = acc + sh_scr[b, r0 + 8 * a:r0 + 8 * a + CONV_ROWS, :] * cw_ref[0, k:k + 1, :]
        conv_scr[r0:r0 + CONV_ROWS, :] = acc
    hc = conv_scr[...]
    gavg = gavg_ref[...]
    mu = jnp.dot(hc.astype(BF16), gavg, preferred_element_type=F32)
    dc = hc - mu
    var = jnp.dot((dc * dc).astype(BF16), gavg, preferred_element_type=F32)
    hn = dc * lax.rsqrt(var + EPS) * gng_ref[0] + gnb_ref[0]
    ya = hn * jax.nn.sigmoid(hn)
    yac_ref[0, :, 0:CONV_W] = ya.astype(BF16)

    zb = z[:, 2 * CONV_W:2 * CONV_W + FOUR_W].astype(BF16)
    pq = jnp.dot(zb, dft_ref[...], preferred_element_type=F32)
    pq_ref[0, 0] = pq[:, 0:FOUR_W].astype(BF16)
    pq_ref[0, 1] = pq[:, FOUR_W:2 * FOUR_W].astype(BF16)

    c0 = 2 * CONV_W + FOUR_W
    zc = z[:, c0:c0 + 2 * SGU_W]
    zc = 0.5 * zc * (1.0 + jnp.tanh(0.7978845608028654 * (zc + 0.044715 * (zc * zc * zc))))
    u = zc[:, 0:SGU_W]
    v = zc[:, SGU_W:2 * SGU_W]
    vm = jnp.mean(v, axis=-1, keepdims=True)
    vd = v - vm
    vv = jnp.mean(vd * vd, axis=-1, keepdims=True)
    vn = (vd * lax.rsqrt(vv + EPS) * lng_ref[0] + lnb_ref[0]).astype(BF16)
    n_chunk = T // CHUNK
    lane = lax.broadcasted_iota(I32, (CHUNK, 2 * HEAD), 1)
    for pr in range(SGU_HEADS // 2):
        cols = slice(2 * HEAD * pr, 2 * HEAD * (pr + 1))
        rhs = jnp.concatenate([vn[n * CHUNK:(n + 1) * CHUNK, cols] for n in range(n_chunk)], axis=1)
        lo = jnp.dot(sw_ref[0, 2 * pr], rhs, preferred_element_type=F32)
        hi = jnp.dot(sw_ref[0, 2 * pr + 1], rhs, preferred_element_type=F32)
        for n in range(n_chunk):
            sl = slice(n * 2 * HEAD, (n + 1) * 2 * HEAD)
            vs = jnp.where(lane < HEAD, lo[:, sl], hi[:, sl]) + sb_ref[0, :, cols]
            rows = slice(n * CHUNK, (n + 1) * CHUNK)
            yac_ref[0, rows, CONV_W + 2 * HEAD * pr:CONV_W + 2 * HEAD * (pr + 1)] = (u[rows, cols] * vs).astype(BF16)


def _mixer_in(x, mod, l, ln1_g, w_in_bf, conv_w, conv_b, gn_g, gn_b, ln_g, ln_b, sgu_w_bf, sgu_bias, dft64, gavg):
    bsz, seq, _ = x.shape
    T = T_MIX
    n_i = seq // T
    hb = T // HALO
    n_h = seq // HALO
    vec = lambda w: pl.BlockSpec((1, 1, w), lambda b, i: (l, 0, 0))
    return pl.pallas_call(
        _mixer_in_kernel,
        grid=(bsz, n_i),
        in_specs=[
            pl.BlockSpec((1, T, D), lambda b, i: (b, i, 0)),
            pl.BlockSpec((1, HALO, D), lambda b, i: (b, jnp.maximum(i * hb - 1, 0), 0)),
            pl.BlockSpec((1, HALO, D), lambda b, i: (b, jnp.minimum((i + 1) * hb, n_h - 1), 0)),
            pl.BlockSpec((1, 1, 8, D), lambda b, i: (l, b, 0, 0)),
            vec(D),
            pl.BlockSpec((1, D, Z_COLS), lambda b, i: (l, 0, 0)),
            pl.BlockSpec((1, KSIZE + 1, CONV_W), lambda b, i: (l, 0, 0)),
            vec(CONV_W), vec(CONV_W), vec(CONV_W), vec(SGU_W), vec(SGU_W),
            pl.BlockSpec((1, SGU_HEADS, CHUNK, CHUNK), lambda b, i: (l, 0, 0, 0)),
            pl.BlockSpec((1, CHUNK, SGU_W), lambda b, i: (l, 0, 0)),
            pl.BlockSpec((FOUR_W, 2 * FOUR_W), lambda b, i: (0, 0)),
            pl.BlockSpec((CONV_W, CONV_W), lambda b, i: (0, 0)),
        ],
        out_specs=[
            pl.BlockSpec((1, T, CONV_W + SGU_W), lambda b, i: (b, i, 0)),
            pl.BlockSpec((1, 2, T, FOUR_W), lambda b, i: (b, 0, i, 0)),
        ],
        out_shape=[
            jax.ShapeDtypeStruct((bsz, seq, CONV_W + SGU_W), BF16),
            jax.ShapeDtypeStruct((bsz, 2, seq, FOUR_W), BF16),
        ],
        scratch_shapes=[
            pltpu.VMEM((T + 2 * HALO, CONV_W), F32),
            pltpu.VMEM((8, T + 3 * 8, CONV_W), F32),
            pltpu.VMEM((T, CONV_W), F32),
        ],
        compiler_params=_cparams(("arbitrary", "arbitrary")),
        name="mixer_in",
    )(x, x, x, mod, ln1_g, w_in_bf, conv_w, conv_b, gn_g, gn_b, ln_g, ln_b, sgu_w_bf, sgu_bias, dft64, gavg)


def _seq_dft_kernel(cs_ref, pq_ref, o_ref):
    o_ref[0] = jnp.dot(cs_ref[...], pq_ref[0], preferred_element_type=F32).astype(BF16)


def _seq_dft(cs, pq):
    bsz, two_s, _ = pq.shape
    seq = two_s // 2
    return pl.pallas_call(
        _seq_dft_kernel,
        grid=(seq // T_DFT, bsz),
        in_specs=[pl.BlockSpec((T_DFT, two_s), lambda k, b: (k, 0)),
                  pl.BlockSpec((1, two_s, FOUR_W), lambda k, b: (b, 0, 0))],
        out_specs=pl.BlockSpec((1, T_DFT, FOUR_W), lambda k, b: (b, k, 0)),
        out_shape=jax.ShapeDtypeStruct((bsz, seq, FOUR_W), BF16),
        compiler_params=_cparams(("arbitrary", "arbitrary")),
        name="seq_dft",
    )(cs, pq)


def _mixer_out_kernel(yac_ref, yb_ref, x_ref, mod_ref, g2_ref, wout_ref, rwt_ref, rb_ref,
                      x1_ref, h2p_ref, info_ref, cnt_ref, cnt_scr):
    i = pl.program_id(0)
    T = T_MIX

    @pl.when(i == 0)
    def _():
        cnt_scr[...] = jnp.zeros_like(cnt_scr)

    mod = mod_ref[0, 0]
    gate1 = mod[2:3, :]
    shift2 = mod[3:4, :]
    scale2 = mod[4:5, :]
    yac = yac_ref[...]
    ycat = jnp.concatenate([yac[:, 0:CONV_W], yb_ref[...], yac[:, CONV_W:CONV_W + SGU_W]], axis=1)
    o = jnp.dot(ycat, wout_ref[0], preferred_element_type=F32)
    x1 = x_ref[...] + gate1 * o
    x1_ref[...] = x1
    ms = jnp.mean(x1 * x1, axis=-1, keepdims=True)
    h2 = (x1 * lax.rsqrt(ms + EPS) * g2_ref[0]) * (1.0 + scale2) + shift2
    h2p_ref[...] = _pack_bf16_pair(h2[:, 0:D // 2], h2[:, D // 2:D])

    logits = lax.dot_general(rwt_ref[...], h2, (((1,), (1,)), ((), ())),
                             precision=HIGHEST, preferred_element_type=F32)
    mx = jnp.max(logits, axis=0, keepdims=True)
    ex = jnp.exp(logits - mx)
    probs = ex / jnp.sum(ex, axis=0, keepdims=True)
    sel = probs + rb_ref[...]
    sel3 = sel.reshape(N_GRP, EPG, T)
    probs3 = probs.reshape(N_GRP, EPG, T)
    jj = lax.broadcasted_iota(I32, (N_GRP, EPG, T), 1)
    m1 = jnp.max(sel3, axis=1, keepdims=True)
    i1 = jnp.min(jnp.where(sel3 == m1, jj, EPG), axis=1, keepdims=True)
    rest = jnp.where(jj == i1, -jnp.inf, sel3)
    m2 = jnp.max(rest, axis=1, keepdims=True)
    i2 = jnp.min(jnp.where(rest == m2, jj, EPG), axis=1, keepdims=True)
    gscore = m1 + m2
    gg = lax.broadcasted_iota(I32, (N_GRP, 1, T), 0)
    gmax = jnp.max(gscore, axis=0, keepdims=True)
    gidx = jnp.min(jnp.where(gscore == gmax, gg, N_GRP), axis=0, keepdims=True)
    ing = gg == gidx
    pick = lambda a, zero: jnp.sum(jnp.where(ing, a, zero), axis=0)
    p1 = jnp.sum(jnp.where(jj == i1, probs3, 0.0), axis=1, keepdims=True)
    p2 = jnp.sum(jnp.where(jj == i2, probs3, 0.0), axis=1, keepdims=True)
    pa = pick(p1, 0.0)
    pb = pick(p2, 0.0)
    gbase = gidx[0] * EPG
    e0 = gbase + pick(i1, 0)
    e1 = gbase + pick(i2, 0)
    den = pa + pb
    gw0 = pa / den
    gw1 = pb / den

    ee = lax.broadcasted_iota(I32, (N_EXP, T), 0)
    oh0 = ee == e0
    oh1 = ee == e1
    amat = jnp.where(oh0 | oh1, 1.0, 0.0)
    rr = lax.broadcasted_iota(I32, (T, T), 0)
    cc = lax.broadcasted_iota(I32, (T, T), 1)
    upper = jnp.where(rr < cc, 1.0, 0.0).astype(BF16)
    before = jnp.dot(amat.astype(BF16), upper, preferred_element_type=F32) + cnt_scr[...]
    r0 = jnp.sum(jnp.where(oh0, before, 0.0), axis=0, keepdims=True)
    r1 = jnp.sum(jnp.where(oh1, before, 0.0), axis=0, keepdims=True)
    cnt_scr[...] = cnt_scr[...] + jnp.sum(amat, axis=1, keepdims=True)
    cnt_ref[...] = cnt_scr[...]

    rid = lax.broadcasted_iota(I32, (8, T), 0)
    rows = jnp.zeros((8, T), F32)
    for k, val in enumerate((gw0, gw1, e0.astype(F32), e1.astype(F32), r0, r1)):
        rows = jnp.where(rid == k, val, rows)
    rows = jnp.concatenate([rows, jnp.zeros((CHUNK - 8, T), F32)], axis=0)
    info_ref[...] = rows.T


def _mixer_out(yac, yb, x, mod, l, ln2_g, w_out_bf, router_wt, router_b, bsz):
    n_tok = x.shape[0]
    T = T_MIX
    per_b = n_tok // bsz // T
    row = lambda w: pl.BlockSpec((T, w), lambda i: (i, 0))
    return pl.pallas_call(
        _mixer_out_kernel,
        grid=(n_tok // T,),
        in_specs=[
            row(CONV_W + SGU_W), row(FOUR_W), row(D),
            pl.BlockSpec((1, 1, 8, D), lambda i: (l, i // per_b, 0, 0)),
            pl.BlockSpec((1, 1, D), lambda i: (l, 0, 0)),
            pl.BlockSpec((1, D, D), lambda i: (l, 0, 0)),
            pl.BlockSpec((N_EXP, D), lambda i: (0, 0)),
            pl.BlockSpec((N_EXP, 1), lambda i: (0, 0)),
        ],
        out_specs=[row(D), row(D // 2), row(CHUNK), pl.BlockSpec((N_EXP, 1), lambda i: (0, 0))],
        out_shape=[
            jax.ShapeDtypeStruct((n_tok, D), F32),
            jax.ShapeDtypeStruct((n_tok, D // 2), U32),
            jax.ShapeDtypeStruct((n_tok, CHUNK), F32),
            jax.ShapeDtypeStruct((N_EXP, 1), F32),
        ],
        scratch_shapes=[pltpu.VMEM((N_EXP, 1), F32)],
        compiler_params=_cparams(("arbitrary",)),
        name="mixer_out",
    )(yac, yb, x, mod, ln2_g, w_out_bf, router_wt, router_b)


def _dispatch_kernel(dest_ref, h_ref, xp_in_ref, xp_ref, sem):
    del xp_in_ref
    i = pl.program_id(0)
    n_tok = dest_ref.shape[0] // 2
    base = i * T_TOK

    def row_copy(t, d):
        return pltpu.make_async_copy(h_ref.at[pl.ds(t, 1)], xp_ref.at[pl.ds(d, 1)], sem)

    def issue(t, carry):
        row_copy(t, dest_ref[base + t]).start()
        row_copy(t, dest_ref[n_tok + base + t]).start()
        return carry

    lax.fori_loop(0, T_TOK, issue, 0, unroll=8)

    def drain(t, carry):
        row_copy(t, 0).wait()
        row_copy(t, 0).wait()
        return carry

    lax.fori_loop(0, T_TOK, drain, 0, unroll=8)


def _dispatch(dest, h2p, n_rows):
    n_tok = h2p.shape[0]
    xp0 = jnp.zeros((n_rows, D // 2), U32)
    return pl.pallas_call(
        _dispatch_kernel,
        grid_spec=pltpu.PrefetchScalarGridSpec(
            num_scalar_prefetch=1,
            grid=(n_tok // T_TOK,),
            in_specs=[pl.BlockSpec((T_TOK, D // 2), lambda i, d: (i, 0)),
                      pl.BlockSpec(memory_space=pl.ANY)],
            out_specs=pl.BlockSpec(memory_space=pl.ANY),
            scratch_shapes=[pltpu.SemaphoreType.DMA(())],
        ),
        out_shape=jax.ShapeDtypeStruct((n_rows, D // 2), U32),
        input_output_aliases={2: 0},
        compiler_params=_cparams(("arbitrary",)),
        name="dispatch",
    )(dest, h2p, xp0)


def _experts_kernel(n_blk, start_ref, nchunk_ref, xp_ref, w1_ref, w3_ref, w2_ref, yp_ref,
                    w13_scr, w2_scr, xbuf, ybuf, xsem, ysem):
    e = pl.program_id(0)
    nc = nchunk_ref[e]
    chunk0 = start_ref[e]

    def x_copy(c, slot):
        rows = pl.ds(pl.multiple_of((chunk0 + c) * BLK, BLK), BLK)
        return pltpu.make_async_copy(xp_ref.at[rows], xbuf.at[slot], xsem.at[slot])

    def y_copy(c, slot):
        rows = pl.ds(pl.multiple_of((chunk0 + c) * BLK, BLK), BLK)
        return pltpu.make_async_copy(ybuf.at[slot], yp_ref.at[rows], ysem.at[slot])

    @pl.when((e < N_EXP) & (nc > 0))
    def _():
        x_copy(0, 0).start()
        w13_scr[:, 0:D_FF] = w1_ref[0, 0].astype(BF16)
        w13_scr[:, D_FF:2 * D_FF] = w3_ref[0, 0].astype(BF16)
        w2_scr[...] = w2_ref[0, 0].astype(BF16)

        def chunk(c, carry):
            slot = c % 2

            @pl.when(c + 1 < nc)
            def _():
                x_copy(c + 1, 1 - slot).start()

            x_copy(c, slot).wait()

            @pl.when(c >= 2)
            def _():
                y_copy(c - 2, slot).wait()

            a, b = _unpack_bf16_pair(xbuf[slot])
            x = jnp.concatenate([a.astype(BF16), b.astype(BF16)], axis=1)
            h13 = jnp.dot(x, w13_scr[...], preferred_element_type=F32)
            h1 = h13[:, 0:D_FF]
            act = (h1 * jax.nn.sigmoid(h1) * h13[:, D_FF:2 * D_FF]).astype(BF16)
            y = jnp.dot(act, w2_scr[...], preferred_element_type=F32)
            ybuf[slot] = _pack_bf16_pair(y[:, 0:D // 2], y[:, D // 2:D])
            y_copy(c, slot).start()
            return carry

        lax.fori_loop(0, nc, chunk, 0)

        @pl.when(nc >= 2)
        def _():
            y_copy(nc - 2, nc % 2).wait()

        y_copy(nc - 1, (nc - 1) % 2).wait()

    @pl.when(e == N_EXP)
    def _():
        ybuf[0] = jnp.zeros((BLK, D // 2), U32)

        def fill(c, carry):
            y_copy(c, 0).start()
            y_copy(c, 0).wait()
            return carry

        lax.fori_loop(0, n_blk - chunk0, fill, 0)


def _experts(chunk_start, n_chunk, xp, w1, w3, w2, l):
    n_rows = xp.shape[0]
    n_blk = n_rows // BLK
    wspec = lambda r, c: pl.BlockSpec((1, 1, r, c), lambda e, cs, nc: (l, jnp.minimum(e, N_EXP - 1), 0, 0))
    return pl.pallas_call(
        functools.partial(_experts_kernel, n_blk),
        grid_spec=pltpu.PrefetchScalarGridSpec(
            num_scalar_prefetch=2,
            grid=(N_EXP + 1,),
            in_specs=[pl.BlockSpec(memory_space=pl.ANY), wspec(D, D_FF), wspec(D, D_FF), wspec(D_FF, D)],
            out_specs=pl.BlockSpec(memory_space=pl.ANY),
            scratch_shapes=[pltpu.VMEM((D, 2 * D_FF), BF16), pltpu.VMEM((D_FF, D), BF16),
                            pltpu.VMEM((2, BLK, D // 2), U32), pltpu.VMEM((2, BLK, D // 2), U32),
                            pltpu.SemaphoreType.DMA((2,)), pltpu.SemaphoreType.DMA((2,))],
        ),
        out_shape=jax.ShapeDtypeStruct((n_rows, D // 2), U32),
        compiler_params=_cparams(("arbitrary",)),
        name="experts",
    )(chunk_start, n_chunk, xp, w1, w3, w2)


def _combine_kernel(final, dest_ref, yp_ref, x1_ref, info_ref, mod_ref, fg_ref, o_ref, buf, sem):
    i = pl.program_id(0)
    n_tok = dest_ref.shape[0] // 2
    base = i * T_CMB

    def row_copy(k, t, d):
        return pltpu.make_async_copy(yp_ref.at[pl.ds(d, 1)], buf.at[k, pl.ds(t, 1)], sem)

    def issue(t, carry):
        row_copy(0, t, dest_ref[base + t]).start()
        row_copy(1, t, dest_ref[n_tok + base + t]).start()
        return carry

    lax.fori_loop(0, T_CMB, issue, 0, unroll=8)

    def drain(t, carry):
        row_copy(0, t, 0).wait()
        row_copy(1, t, 0).wait()
        return carry

    lax.fori_loop(0, T_CMB, drain, 0, unroll=8)

    info = info_ref[...]
    gw0 = info[:, 0:1]
    gw1 = info[:, 1:2]
    a0, b0 = _unpack_bf16_pair(buf[0])
    a1, b1 = _unpack_bf16_pair(buf[1])
    y = jnp.concatenate([gw0 * a0 + gw1 * a1, gw0 * b0 + gw1 * b1], axis=1)
    gate2 = mod_ref[0, 0][5:6, :]
    x2 = x1_ref[...] + gate2 * y
    if final:
        ms = jnp.mean(x2 * x2, axis=-1, keepdims=True)
        x2 = x2 * lax.rsqrt(ms + EPS) * fg_ref[...]
    o_ref[...] = x2


def _combine(dest, yp, x1, info, mod, l, final_g, bsz, final):
    n_tok = x1.shape[0]
    per_b = n_tok // bsz // T_CMB
    return pl.pallas_call(
        functools.partial(_combine_kernel, final),
        grid_spec=pltpu.PrefetchScalarGridSpec(
            num_scalar_prefetch=1,
            grid=(n_tok // T_CMB,),
            in_specs=[
                pl.BlockSpec(memory_space=pl.ANY),
                pl.BlockSpec((T_CMB, D), lambda i, d: (i, 0)),
                pl.BlockSpec((T_CMB, CHUNK), lambda i, d: (i, 0)),
                pl.BlockSpec((1, 1, 8, D), lambda i, d: (l, i // per_b, 0, 0)),
                pl.BlockSpec((1, D), lambda i, d: (0, 0)),
            ],
            out_specs=pl.BlockSpec((T_CMB, D), lambda i, d: (i, 0)),
            scratch_shapes=[pltpu.VMEM((2, T_CMB, D // 2), U32), pltpu.SemaphoreType.DMA(())],
        ),
        out_shape=jax.ShapeDtypeStruct((n_tok, D), F32),
        compiler_params=_cparams(("arbitrary",)),
        name="combine",
    )(dest, yp, x1, info, mod, final_g)


def _dft_table_kernel(t1_ref, t2_ref, o_ref):
    seq = t2_ref.shape[2]
    c1 = t1_ref[0, 0:1, :]
    s1 = t1_ref[0, 1:2, :]
    c2 = t2_ref[0]
    s2 = t2_ref[1]
    o_ref[:, 0:seq] = (c1 * c2 - s1 * s2).astype(BF16)
    o_ref[:, seq:2 * seq] = (-(s1 * c2 + c1 * s2)).astype(BF16)


def _dft_tables(seq):
    scale = 1.0 / (seq * HEAD) ** 0.5
    n_hi = seq // TAB_ROWS
    n = lax.broadcasted_iota(I32, (1, seq), 1)
    kh = lax.broadcasted_iota(I32, (n_hi, 1), 0)
    a1 = ((kh * n) % n_hi).astype(F32) * (2.0 * jnp.pi / n_hi)
    t1 = jnp.stack([jnp.cos(a1), jnp.sin(a1)], axis=1)
    kl = lax.broadcasted_iota(I32, (TAB_ROWS, 1), 0)
    a2 = ((kl * n) % seq).astype(F32) * (2.0 * jnp.pi / seq)
    t2 = jnp.stack([jnp.cos(a2) * scale, jnp.sin(a2) * scale], axis=0)
    cs = pl.pallas_call(
        _dft_table_kernel,
        grid=(n_hi,),
        in_specs=[pl.BlockSpec((1, 2, seq), lambda i: (i, 0, 0)),
                  pl.BlockSpec((2, TAB_ROWS, seq), lambda i: (0, 0, 0))],
        out_specs=pl.BlockSpec((TAB_ROWS, 2 * seq), lambda i: (i, 0)),
        out_shape=jax.ShapeDtypeStruct((seq, 2 * seq), BF16),
        compiler_params=_cparams(("arbitrary",)),
        name="dft_table",
    )(t1, t2)
    d = lax.broadcasted_iota(I32, (FOUR_W, FOUR_W), 0)
    q = lax.broadcasted_iota(I32, (FOUR_W, FOUR_W), 1)
    same = (d // HEAD) == (q // HEAD)
    ang64 = ((d * q) % HEAD).astype(F32) * (2.0 * jnp.pi / HEAD)
    dft64 = jnp.concatenate([jnp.where(same, jnp.cos(ang64), 0.0),
                             jnp.where(same, jnp.sin(ang64), 0.0)], axis=1).astype(BF16)
    return cs, dft64


def _routing_tables(info, counts_f):
    counts = counts_f[:, 0].astype(I32)
    pc = (counts + BLK - 1) // BLK * BLK
    pends = jnp.cumsum(pc)
    pstarts = pends - pc
    eid = lax.broadcasted_iota(I32, (1, N_EXP), 1)

    def dest_of(e, r):
        return jnp.sum(jnp.where(e[:, None] == eid, pstarts[None, :], 0), axis=1) + r

    e0 = info[:, 2].astype(I32)
    e1 = info[:, 3].astype(I32)
    r0 = info[:, 4].astype(I32)
    r1 = info[:, 5].astype(I32)
    dest = jnp.concatenate([dest_of(e0, r0), dest_of(e1, r1)])
    chunk_start = jnp.concatenate([pstarts, pends[-1:]]) // BLK
    n_chunk = jnp.concatenate([pc // BLK, jnp.zeros((1,), I32)])
    return dest, chunk_start, n_chunk


def kernel(x, c, ln1_g, ln2_g, w_ada, b_ada, w_in, w_out, conv_w, conv_b, conv_gn_g, conv_gn_b, sgu_ln_g,
           sgu_ln_b, sgu_w, sgu_b, router_w, router_b, exp_w1, exp_w3, exp_w2, final_g):
    bsz, seq, _ = x.shape
    depth = w_in.shape[0]
    n_tok = bsz * seq
    n_blk = (2 * n_tok + N_EXP * (BLK - 1) + BLK - 1) // BLK
    r3 = lambda a: a.reshape(depth, 1, a.shape[-1])

    mod = jnp.pad(_ada_mod(c, w_ada, b_ada).reshape(depth, bsz, 6, D), ((0, 0), (0, 0), (0, 2), (0, 0)))
    cs, dft64 = _dft_tables(seq)
    hd = lax.broadcasted_iota(I32, (CONV_W, CONV_W), 0) // HEAD
    gavg = jnp.where(hd == hd.T, 1.0 / HEAD, 0.0).astype(BF16)
    w_in_bf = w_in.astype(BF16)
    w_out_bf = w_out.astype(BF16)
    sgu_w_bf = sgu_w.astype(BF16)
    sgu_bias = jnp.repeat(jnp.swapaxes(sgu_b, 1, 2), HEAD, axis=2)
    conv_w_p = jnp.pad(conv_w, ((0, 0), (0, 1), (0, 0)))
    router_wt = router_w.T
    router_bc = router_b.reshape(N_EXP, 1)
    fg = final_g.reshape(1, D)

    for l in range(depth):
        yac, pq = _mixer_in(x, mod, l, r3(ln1_g), w_in_bf, conv_w_p, r3(conv_b), r3(conv_gn_g), r3(conv_gn_b),
                            r3(sgu_ln_g), r3(sgu_ln_b), sgu_w_bf, sgu_bias, dft64, gavg)
        yb = _seq_dft(cs, pq.reshape(bsz, 2 * seq, FOUR_W))
        x1, h2p, info, counts = _mixer_out(yac.reshape(n_tok, CONV_W + SGU_W), yb.reshape(n_tok, FOUR_W),
                                           x.reshape(n_tok, D), mod, l, r3(ln2_g), w_out_bf, router_wt,
                                           router_bc, bsz)
        dest, chunk_start, n_chunk = _routing_tables(info, counts)
        xp = _dispatch(dest, h2p, n_blk * BLK)
        yp = _experts(chunk_start, n_chunk, xp, exp_w1, exp_w3, exp_w2, l)
        x = _combine(dest, yp, x1, info, mod, l, fg, bsz, l == depth - 1).reshape(bsz, seq, D)
    return x
```

```python
import functools

import jax
import jax.numpy as jnp
from jax import lax
from jax.experimental import pallas as pl
from jax.experimental.pallas import tpu as pltpu
from jax.experimental.pallas import tpu_sc as plsc

F32 = jnp.float32
BF16 = jnp.bfloat16
I32 = jnp.int32
U32 = jnp.uint32
HIGHEST = lax.Precision.HIGHEST

D = 1024
HEAD = 64
CONV_W = 384
FOUR_W = 256
SGU_W = 384
SGU_HEADS = SGU_W // HEAD
Z_COLS = 2 * CONV_W + FOUR_W + 2 * SGU_W
KSIZE = 31
HALO = 16
CHUNK = 128
N_EXP = 64
N_GRP = 8
EPG = N_EXP // N_GRP
D_FF = D // 2
EPS = 1e-6

T_MIX = 512
T_DFT = 512
T_CMB = 512
ROW_W = 256
SC_WIN = 128
BLK = 256
CONV_ROWS = 64
TAB_ROWS = 64
W_PIECES = 4
VMEM_LIMIT = 56 * 1024 * 1024


def _cparams(sem):
    return pltpu.CompilerParams(dimension_semantics=sem, vmem_limit_bytes=VMEM_LIMIT)


def _pack_bf16_pair(a, b):
    ua = lax.bitcast_convert_type(a.astype(BF16).astype(F32), U32) >> 16
    ub = lax.bitcast_convert_type(b.astype(BF16).astype(F32), U32) & jnp.uint32(0xFFFF0000)
    return ua | ub


def _unpack_bf16_pair(p):
    a = lax.bitcast_convert_type(p << 16, F32)
    b = lax.bitcast_convert_type(p & jnp.uint32(0xFFFF0000), F32)
    return a, b


def _ada_kernel(c_ref, w_ref, b_ref, o_ref):
    c = c_ref[...]
    ca = c * jax.nn.sigmoid(c)
    o_ref[0] = jnp.dot(ca, w_ref[0], precision=HIGHEST, preferred_element_type=F32) + b_ref[0]


def _ada_mod(c, w_ada, b_ada):
    depth, _, ncol = w_ada.shape
    bsz = c.shape[0]
    tn = 1536
    return pl.pallas_call(
        _ada_kernel,
        grid=(depth, ncol // tn),
        in_specs=[pl.BlockSpec((bsz, D), lambda l, j: (0, 0)),
                  pl.BlockSpec((1, D, tn), lambda l, j: (l, 0, j)),
                  pl.BlockSpec((1, 1, tn), lambda l, j: (l, 0, j))],
        out_specs=pl.BlockSpec((1, bsz, tn), lambda l, j: (l, 0, j)),
        out_shape=jax.ShapeDtypeStruct((depth, bsz, ncol), F32),
        compiler_params=_cparams(("arbitrary", "arbitrary")),
        name="ada_mod",
    )(c, w_ada, b_ada.reshape(depth, 1, ncol))


def _mixer_in_kernel(xm_ref, xp_ref, xn_ref, mod_ref, g1_ref, win_ref, cw_ref, cb_ref, gng_ref, gnb_ref,
                     lng_ref, lnb_ref, sw_ref, sb_ref, dft_ref, gavg_ref,
                     yac_ref, pq_ref, glu_scr, sh_scr, conv_scr):
    i = pl.program_id(1)
    n_i = pl.num_programs(1)
    T = T_MIX
    mod = mod_ref[0, 0]
    shift1 = mod[0:1, :]
    scale1 = mod[1:2, :]
    g1 = g1_ref[0]

    def norm_mod(x):
        ms = jnp.mean(x * x, axis=-1, keepdims=True)
        return (x * lax.rsqrt(ms + EPS) * g1) * (1.0 + scale1) + shift1

    h = norm_mod(xm_ref[0]).astype(BF16)
    z = jnp.dot(h, win_ref[0], preferred_element_type=F32)

    hh = norm_mod(jnp.concatenate([xp_ref[0], xn_ref[0]], axis=0)).astype(BF16)
    zh = jnp.dot(hh, win_ref[0, :, 0:2 * CONV_W], preferred_element_type=F32)
    glu_h = zh[:, 0:CONV_W] * jax.nn.sigmoid(zh[:, CONV_W:2 * CONV_W])
    glu_scr[0:HALO, :] = jnp.where(i > 0, glu_h[0:HALO], 0.0)
    glu_scr[HALO + T:2 * HALO + T, :] = jnp.where(i < n_i - 1, glu_h[HALO:2 * HALO], 0.0)
    glu_scr[HALO:HALO + T, :] = z[:, 0:CONV_W] * jax.nn.sigmoid(z[:, CONV_W:2 * CONV_W])

    off = HALO - KSIZE // 2
    for b in range(8):
        sh_scr[b] = glu_scr[b:b + T + 3 * 8, :]

    for c in range(T // CONV_ROWS):
        r0 = c * CONV_ROWS
        acc = jnp.broadcast_to(cb_ref[0], (CONV_ROWS, CONV_W))
        for k in range(KSIZE):
            a, b = divmod(k + off, 8)
            acc = acc + sh_scr[b, r0 + 8 * a:r0 + 8 * a + CONV_ROWS, :] * cw_ref[0, k:k + 1, :]
        conv_scr[r0:r0 + CONV_ROWS, :] = acc
    hc = conv_scr[...]
    gavg = gavg_ref[...]
    mu = jnp.dot(hc.astype(BF16), gavg, preferred_element_type=F32)
    dc = hc - mu
    var = jnp.dot((dc * dc).astype(BF16), gavg, preferred_element_type=F32)
    hn = dc * lax.rsqrt(var + EPS) * gng_ref[0] + gnb_ref[0]
    ya = hn * jax.nn.sigmoid(hn)
    yac_ref[0, :, 0:CONV_W] = ya.astype(BF16)

    zb = z[:, 2 * CONV_W:2 * CONV_W + FOUR_W].astype(BF16)
    pq = jnp.dot(zb, dft_ref[...], preferred_element_type=F32)
    pq_ref[0, 0] = pq[:, 0:FOUR_W].astype(BF16)
    pq_ref[0, 1] = pq[:, FOUR_W:2 * FOUR_W].astype(BF16)

    c0 = 2 * CONV_W + FOUR_W
    zc = z[:, c0:c0 + 2 * SGU_W]
    zc = 0.5 * zc * (1.0 + jnp.tanh(0.7978845608028654 * (zc + 0.044715 * (zc * zc * zc))))
    u = zc[:, 0:SGU_W]
    v = zc[:, SGU_W:2 * SGU_W]
    vm = jnp.mean(v, axis=-1, keepdims=True)
    vd = v - vm
    vv = jnp.mean(vd * vd, axis=-1, keepdims=True)
    vn = (vd * lax.rsqrt(vv + EPS) * lng_ref[0] + lnb_ref[0]).astype(BF16)
    n_chunk = T // CHUNK
    lane = lax.broadcasted_iota(I32, (CHUNK, 2 * HEAD), 1)
    for pr in range(SGU_HEADS // 2):
        cols = slice(2 * HEAD * pr, 2 * HEAD * (pr + 1))
        rhs = jnp.concatenate([vn[n * CHUNK:(n + 1) * CHUNK, cols] for n in range(n_chunk)], axis=1)
        lo = jnp.dot(sw_ref[0, 2 * pr], rhs, preferred_element_type=F32)
        hi = jnp.dot(sw_ref[0, 2 * pr + 1], rhs, preferred_element_type=F32)
        for n in range(n_chunk):
            sl = slice(n * 2 * HEAD, (n + 1) * 2 * HEAD)
            vs = jnp.where(lane < HEAD, lo[:, sl], hi[:, sl]) + sb_ref[0, :, cols]
            rows = slice(n * CHUNK, (n + 1) * CHUNK)
            yac_ref[0, rows, CONV_W + 2 * HEAD * pr:CONV_W + 2 * HEAD * (pr + 1)] = (u[rows, cols] * vs).astype(BF16)


def _mixer_in(x, mod, l, ln1_g, w_in_bf, conv_w, conv_b, gn_g, gn_b, ln_g, ln_b, sgu_w_bf, sgu_bias, dft64, gavg):
    bsz, seq, _ = x.shape
    T = T_MIX
    n_i = seq // T
    hb = T // HALO
    n_h = seq // HALO
    vec = lambda w: pl.BlockSpec((1, 1, w), lambda b, i: (l, 0, 0))
    return pl.pallas_call(
        _mixer_in_kernel,
        grid=(bsz, n_i),
        in_specs=[
            pl.BlockSpec((1, T, D), lambda b, i: (b, i, 0)),
            pl.BlockSpec((1, HALO, D), lambda b, i: (b, jnp.maximum(i * hb - 1, 0), 0)),
            pl.BlockSpec((1, HALO, D), lambda b, i: (b, jnp.minimum((i + 1) * hb, n_h - 1), 0)),
            pl.BlockSpec((1, 1, 8, D), lambda b, i: (l, b, 0, 0)),
            vec(D),
            pl.BlockSpec((1, D, Z_COLS), lambda b, i: (l, 0, 0)),
            pl.BlockSpec((1, KSIZE + 1, CONV_W), lambda b, i: (l, 0, 0)),
            vec(CONV_W), vec(CONV_W), vec(CONV_W), vec(SGU_W), vec(SGU_W),
            pl.BlockSpec((1, SGU_HEADS, CHUNK, CHUNK), lambda b, i: (l, 0, 0, 0)),
            pl.BlockSpec((1, CHUNK, SGU_W), lambda b, i: (l, 0, 0)),
            pl.BlockSpec((FOUR_W, 2 * FOUR_W), lambda b, i: (0, 0)),
            pl.BlockSpec((CONV_W, CONV_W), lambda b, i: (0, 0)),
        ],
        out_specs=[
            pl.BlockSpec((1, T, CONV_W + SGU_W), lambda b, i: (b, i, 0)),
            pl.BlockSpec((1, 2, T, FOUR_W), lambda b, i: (b, 0, i, 0)),
        ],
        out_shape=[
            jax.ShapeDtypeStruct((bsz, seq, CONV_W + SGU_W), BF16),
            jax.ShapeDtypeStruct((bsz, 2, seq, FOUR_W), BF16),
        ],
        scratch_shapes=[
            pltpu.VMEM((T + 2 * HALO, CONV_W), F32),
            pltpu.VMEM((8, T + 3 * 8, CONV_W), F32),
            pltpu.VMEM((T, CONV_W), F32),
        ],
        compiler_params=_cparams(("arbitrary", "arbitrary")),
        name="mixer_in",
    )(x, x, x, mod, ln1_g, w_in_bf, conv_w, conv_b, gn_g, gn_b, ln_g, ln_b, sgu_w_bf, sgu_bias, dft64, gavg)


def _seq_dft_kernel(cs_ref, pq_ref, o_ref):
    o_ref[0] = jnp.dot(cs_ref[...], pq_ref[0], preferred_element_type=F32).astype(BF16)


def _seq_dft(cs, pq):
    bsz, two_s, _ = pq.shape
    seq = two_s // 2
    return pl.pallas_call(
        _seq_dft_kernel,
        grid=(seq // T_DFT, bsz),
        in_specs=[pl.BlockSpec((T_DFT, two_s), lambda k, b: (k, 0)),
                  pl.BlockSpec((1, two_s, FOUR_W), lambda k, b: (b, 0, 0))],
        out_specs=pl.BlockSpec((1, T_DFT, FOUR_W), lambda k, b: (b, k, 0)),
        out_shape=jax.ShapeDtypeStruct((bsz, seq, FOUR_W), BF16),
        compiler_params=_cparams(("arbitrary", "arbitrary")),
        name="seq_dft",
    )(cs, pq)


def _mixer_out_kernel(yac_ref, yb_ref, x_ref, mod_ref, g2_ref, wout_ref, rwt_ref, rb_ref,
                      x1_ref, h2pa_ref, h2pb_ref, info_ref, cnt_ref, cnt_scr):
    i = pl.program_id(0)
    T = T_MIX

    @pl.when(i == 0)
    def _():
        cnt_scr[...] = jnp.zeros_like(cnt_scr)

    mod = mod_ref[0, 0]
    gate1 = mod[2:3, :]
    shift2 = mod[3:4, :]
    scale2 = mod[4:5, :]
    yac = yac_ref[...]
    ycat = jnp.concatenate([yac[:, 0:CONV_W], yb_ref[...], yac[:, CONV_W:CONV_W + SGU_W]], axis=1)
    o = jnp.dot(ycat, wout_ref[0], preferred_element_type=F32)
    x1 = x_ref[...] + gate1 * o
    x1_ref[...] = x1
    ms = jnp.mean(x1 * x1, axis=-1, keepdims=True)
    h2 = (x1 * lax.rsqrt(ms + EPS) * g2_ref[0]) * (1.0 + scale2) + shift2
    h2p = _pack_bf16_pair(h2[:, 0:D // 2], h2[:, D // 2:D])
    h2pa_ref[...] = h2p[:, 0:ROW_W]
    h2pb_ref[...] = h2p[:, ROW_W:2 * ROW_W]

    logits = lax.dot_general(rwt_ref[...], h2, (((1,), (1,)), ((), ())),
                             precision=HIGHEST, preferred_element_type=F32)
    mx = jnp.max(logits, axis=0, keepdims=True)
    ex = jnp.exp(logits - mx)
    probs = ex / jnp.sum(ex, axis=0, keepdims=True)
    sel = probs + rb_ref[...]
    sel3 = sel.reshape(N_GRP, EPG, T)
    probs3 = probs.reshape(N_GRP, EPG, T)
    jj = lax.broadcasted_iota(I32, (N_GRP, EPG, T), 1)
    m1 = jnp.max(sel3, axis=1, keepdims=True)
    i1 = jnp.min(jnp.where(sel3 == m1, jj, EPG), axis=1, keepdims=True)
    rest = jnp.where(jj == i1, -jnp.inf, sel3)
    m2 = jnp.max(rest, axis=1, keepdims=True)
    i2 = jnp.min(jnp.where(rest == m2, jj, EPG), axis=1, keepdims=True)
    gscore = m1 + m2
    gg = lax.broadcasted_iota(I32, (N_GRP, 1, T), 0)
    gmax = jnp.max(gscore, axis=0, keepdims=True)
    gidx = jnp.min(jnp.where(gscore == gmax, gg, N_GRP), axis=0, keepdims=True)
    ing = gg == gidx
    pick = lambda a, zero: jnp.sum(jnp.where(ing, a, zero), axis=0)
    p1 = jnp.sum(jnp.where(jj == i1, probs3, 0.0), axis=1, keepdims=True)
    p2 = jnp.sum(jnp.where(jj == i2, probs3, 0.0), axis=1, keepdims=True)
    pa = pick(p1, 0.0)
    pb = pick(p2, 0.0)
    gbase = gidx[0] * EPG
    e0 = gbase + pick(i1, 0)
    e1 = gbase + pick(i2, 0)
    den = pa + pb
    gw0 = pa / den
    gw1 = pb / den

    ee = lax.broadcasted_iota(I32, (N_EXP, T), 0)
    oh0 = ee == e0
    oh1 = ee == e1
    amat = jnp.where(oh0 | oh1, 1.0, 0.0)
    rr = lax.broadcasted_iota(I32, (T, T), 0)
    cc = lax.broadcasted_iota(I32, (T, T), 1)
    upper = jnp.where(rr < cc, 1.0, 0.0).astype(BF16)
    before = jnp.dot(amat.astype(BF16), upper, preferred_element_type=F32) + cnt_scr[...]
    r0 = jnp.sum(jnp.where(oh0, before, 0.0), axis=0, keepdims=True)
    r1 = jnp.sum(jnp.where(oh1, before, 0.0), axis=0, keepdims=True)
    cnt_scr[...] = cnt_scr[...] + jnp.sum(amat, axis=1, keepdims=True)
    cnt_ref[...] = cnt_scr[...]

    rid = lax.broadcasted_iota(I32, (8, T), 0)
    rows = jnp.zeros((8, T), F32)
    for k, val in enumerate((gw0, gw1, e0.astype(F32), e1.astype(F32), r0, r1)):
        rows = jnp.where(rid == k, val, rows)
    rows = jnp.concatenate([rows, jnp.zeros((CHUNK - 8, T), F32)], axis=0)
    info_ref[...] = rows.T


def _mixer_out(yac, yb, x, mod, l, ln2_g, w_out_bf, router_wt, router_b, bsz):
    n_tok = x.shape[0]
    T = T_MIX
    per_b = n_tok // bsz // T
    row = lambda w: pl.BlockSpec((T, w), lambda i: (i, 0))
    return pl.pallas_call(
        _mixer_out_kernel,
        grid=(n_tok // T,),
        in_specs=[
            row(CONV_W + SGU_W), row(FOUR_W), row(D),
            pl.BlockSpec((1, 1, 8, D), lambda i: (l, i // per_b, 0, 0)),
            pl.BlockSpec((1, 1, D), lambda i: (l, 0, 0)),
            pl.BlockSpec((1, D, D), lambda i: (l, 0, 0)),
            pl.BlockSpec((N_EXP, D), lambda i: (0, 0)),
            pl.BlockSpec((N_EXP, 1), lambda i: (0, 0)),
        ],
        out_specs=[row(D), row(ROW_W), row(ROW_W), row(CHUNK), pl.BlockSpec((N_EXP, 1), lambda i: (0, 0))],
        out_shape=[
            jax.ShapeDtypeStruct((n_tok, D), F32),
            jax.ShapeDtypeStruct((n_tok, ROW_W), U32),
            jax.ShapeDtypeStruct((n_tok, ROW_W), U32),
            jax.ShapeDtypeStruct((n_tok, CHUNK), F32),
            jax.ShapeDtypeStruct((N_EXP, 1), F32),
        ],
        scratch_shapes=[pltpu.VMEM((N_EXP, 1), F32)],
        compiler_params=_cparams(("arbitrary",)),
        name="mixer_out",
    )(yac, yb, x, mod, ln2_g, w_out_bf, router_wt, router_b)


def _sc_mesh():
    return plsc.VectorSubcoreMesh(core_axis_name="c", subcore_axis_name="s")


def _sc_scatter2(src, idx0, idx1, n_rows):
    n = src.shape[0]

    @functools.partial(pl.kernel, out_type=jax.ShapeDtypeStruct((n_rows, ROW_W), src.dtype), mesh=_sc_mesh())
    def scatter(x_hbm, i0_hbm, i1_hbm, o_hbm):
        def body(x_vmem, i0_vmem, i1_vmem):
            pltpu.sync_copy(x_vmem, o_hbm.at[i0_vmem.at[0]])
            pltpu.sync_copy(x_vmem, o_hbm.at[i1_vmem.at[0]])

        pltpu.emit_pipeline(
            body, grid=(n // SC_WIN,),
            in_specs=[pl.BlockSpec((SC_WIN, ROW_W), index_map=lambda i: (i, 0)),
                      pl.BlockSpec((1, SC_WIN), index_map=lambda i: (0, i)),
                      pl.BlockSpec((1, SC_WIN), index_map=lambda i: (0, i))],
            out_specs=[],
            core_axis_name=("c", "s"), dimension_semantics=(pltpu.PARALLEL,),
        )(x_hbm, i0_hbm, i1_hbm)

    return scatter(src, idx0, idx1)


def _sc_gather(src, idx):
    m = idx.shape[1]

    @functools.partial(pl.kernel, out_type=jax.ShapeDtypeStruct((m, ROW_W), src.dtype), mesh=_sc_mesh())
    def gather(x_hbm, i_hbm, o_hbm):
        def body(i_vmem, o_vmem):
            pltpu.sync_copy(x_hbm.at[i_vmem.at[0]], o_vmem)

        pltpu.emit_pipeline(
            body, grid=(m // SC_WIN,),
            in_specs=[pl.BlockSpec((1, SC_WIN), index_map=lambda i: (0, i))],
            out_specs=[pl.BlockSpec((SC_WIN, ROW_W), index_map=lambda i: (i, 0))],
            core_axis_name=("c", "s"), dimension_semantics=(pltpu.PARALLEL,),
        )(i_hbm, o_hbm)

    return gather(src, idx)


def _experts_kernel(l, n_blk, start_ref, nchunk_ref, xpa_ref, xpb_ref, w1_ref, w3_ref, w2_ref, ypa_ref, ypb_ref,
                    w13_scr, w2_scr, wbuf13, wbuf2, xbuf, ybuf, wsem, xsem, ysem):
    e = pl.program_id(0)
    nc = nchunk_ref[e]
    chunk0 = start_ref[e]

    def w_copies(ex, slot):
        cps = []
        for p in range(W_PIECES):
            r13 = pl.ds(p * (D // W_PIECES), D // W_PIECES)
            r2 = pl.ds(p * (D_FF // W_PIECES), D_FF // W_PIECES)
            cps.append(pltpu.make_async_copy(w1_ref.at[l, ex, r13], wbuf13.at[slot, 0, r13], wsem.at[slot]))
            cps.append(pltpu.make_async_copy(w3_ref.at[l, ex, r13], wbuf13.at[slot, 1, r13], wsem.at[slot]))
            cps.append(pltpu.make_async_copy(w2_ref.at[l, ex, r2], wbuf2.at[slot, r2], wsem.at[slot]))
        return cps

    class _Pair:
        def __init__(self, cps):
            self.cps = cps

        def start(self):
            for cp in self.cps:
                cp.start()

        def wait(self):
            for cp in self.cps:
                cp.wait()

    def x_copy(c, slot):
        rows = pl.ds(pl.multiple_of((chunk0 + c) * BLK, BLK), BLK)
        return _Pair([pltpu.make_async_copy(src.at[rows], xbuf.at[slot, h], xsem.at[slot])
                      for h, src in enumerate((xpa_ref, xpb_ref))])

    def y_copy(c, slot):
        rows = pl.ds(pl.multiple_of((chunk0 + c) * BLK, BLK), BLK)
        return _Pair([pltpu.make_async_copy(ybuf.at[slot, h], dst.at[rows], ysem.at[slot])
                      for h, dst in enumerate((ypa_ref, ypb_ref))])

    @pl.when(e < N_EXP)
    def _():
        wslot = e % 2

        @pl.when(nc > 0)
        def _():
            x_copy(0, 0).start()

        @pl.when(e == 0)
        def _():
            for cp in w_copies(0, 0):
                cp.start()

        for cp in w_copies(e, wslot):
            cp.wait()

        @pl.when(e + 1 < N_EXP)
        def _():
            for cp in w_copies(e + 1, 1 - wslot):
                cp.start()

        @pl.when(nc > 0)
        def _():
            w13_scr[:, 0:D_FF] = wbuf13[wslot, 0].astype(BF16)
            w13_scr[:, D_FF:2 * D_FF] = wbuf13[wslot, 1].astype(BF16)
            w2_scr[...] = wbuf2[wslot].astype(BF16)

            def chunk(c, carry):
                slot = c % 2

                @pl.when(c + 1 < nc)
                def _():
                    x_copy(c + 1, 1 - slot).start()

                x_copy(c, slot).wait()

                @pl.when(c >= 2)
                def _():
                    y_copy(c - 2, slot).wait()

                a, b = _unpack_bf16_pair(jnp.concatenate([xbuf[slot, 0], xbuf[slot, 1]], axis=1))
                x = jnp.concatenate([a.astype(BF16), b.astype(BF16)], axis=1)
                h13 = jnp.dot(x, w13_scr[...], preferred_element_type=F32)
                h1 = h13[:, 0:D_FF]
                act = (h1 * jax.nn.sigmoid(h1) * h13[:, D_FF:2 * D_FF]).astype(BF16)
                y = jnp.dot(act, w2_scr[...], preferred_element_type=F32)
                yp = _pack_bf16_pair(y[:, 0:D // 2], y[:, D // 2:D])
                ybuf[slot, 0] = yp[:, 0:ROW_W]
                ybuf[slot, 1] = yp[:, ROW_W:2 * ROW_W]
                y_copy(c, slot).start()
                return carry

            lax.fori_loop(0, nc, chunk, 0)

            @pl.when(nc >= 2)
            def _():
                y_copy(nc - 2, nc % 2).wait()

            y_copy(nc - 1, (nc - 1) % 2).wait()

    @pl.when(e == N_EXP)
    def _():
        ybuf[0] = jnp.zeros((2, BLK, ROW_W), U32)

        def fill(c, carry):
            y_copy(c, 0).start()
            y_copy(c, 0).wait()
            return carry

        lax.fori_loop(0, n_blk - chunk0, fill, 0)


def _experts(chunk_start, n_chunk, xpa, xpb, w1, w3, w2, l):
    n_rows = xpa.shape[0]
    n_blk = n_rows // BLK
    hbm = pl.BlockSpec(memory_space=pl.ANY)
    half = jax.ShapeDtypeStruct((n_rows, ROW_W), U32)
    return pl.pallas_call(
        functools.partial(_experts_kernel, l, n_blk),
        grid_spec=pltpu.PrefetchScalarGridSpec(
            num_scalar_prefetch=2,
            grid=(N_EXP + 1,),
            in_specs=[hbm, hbm, hbm, hbm, hbm],
            out_specs=[hbm, hbm],
            scratch_shapes=[pltpu.VMEM((D, 2 * D_FF), BF16), pltpu.VMEM((D_FF, D), BF16),
                            pltpu.VMEM((2, 2, D, D_FF), F32), pltpu.VMEM((2, D_FF, D), F32),
                            pltpu.VMEM((2, 2, BLK, ROW_W), U32), pltpu.VMEM((2, 2, BLK, ROW_W), U32),
                            pltpu.SemaphoreType.DMA((2,)), pltpu.SemaphoreType.DMA((2,)),
                            pltpu.SemaphoreType.DMA((2,))],
        ),
        out_shape=[half, half],
        compiler_params=_cparams(("arbitrary",)),
        name="experts",
    )(chunk_start, n_chunk, xpa, xpb, w1, w3, w2)


def _combine_kernel(final, ga0_ref, gb0_ref, ga1_ref, gb1_ref, x1_ref, info_ref, mod_ref, fg_ref, o_ref):
    info = info_ref[...]
    gw0 = info[:, 0:1]
    gw1 = info[:, 1:2]
    a0, b0 = _unpack_bf16_pair(jnp.concatenate([ga0_ref[...], gb0_ref[...]], axis=1))
    a1, b1 = _unpack_bf16_pair(jnp.concatenate([ga1_ref[...], gb1_ref[...]], axis=1))
    y = jnp.concatenate([gw0 * a0 + gw1 * a1, gw0 * b0 + gw1 * b1], axis=1)
    gate2 = mod_ref[0, 0][5:6, :]
    x2 = x1_ref[...] + gate2 * y
    if final:
        ms = jnp.mean(x2 * x2, axis=-1, keepdims=True)
        x2 = x2 * lax.rsqrt(ms + EPS) * fg_ref[...]
    o_ref[...] = x2


def _combine(ga, gb, x1, info, mod, l, final_g, bsz, final):
    n_tok = x1.shape[0]
    n_i = n_tok // T_CMB
    per_b = n_i // bsz
    first = pl.BlockSpec((T_CMB, ROW_W), lambda i: (i, 0))
    second = pl.BlockSpec((T_CMB, ROW_W), lambda i: (i + n_i, 0))
    return pl.pallas_call(
        functools.partial(_combine_kernel, final),
        grid=(n_i,),
        in_specs=[
            first, first, second, second,
            pl.BlockSpec((T_CMB, D), lambda i: (i, 0)),
            pl.BlockSpec((T_CMB, CHUNK), lambda i: (i, 0)),
            pl.BlockSpec((1, 1, 8, D), lambda i: (l, i // per_b, 0, 0)),
            pl.BlockSpec((1, D), lambda i: (0, 0)),
        ],
        out_specs=pl.BlockSpec((T_CMB, D), lambda i: (i, 0)),
        out_shape=jax.ShapeDtypeStruct((n_tok, D), F32),
        compiler_params=_cparams(("arbitrary",)),
        name="combine",
    )(ga, gb, ga, gb, x1, info, mod, final_g)


def _dft_table_kernel(t1_ref, t2_ref, o_ref):
    seq = t2_ref.shape[2]
    c1 = t1_ref[0, 0:1, :]
    s1 = t1_ref[0, 1:2, :]
    c2 = t2_ref[0]
    s2 = t2_ref[1]
    o_ref[:, 0:seq] = (c1 * c2 - s1 * s2).astype(BF16)
    o_ref[:, seq:2 * seq] = (-(s1 * c2 + c1 * s2)).astype(BF16)


def _dft_tables(seq):
    scale = 1.0 / (seq * HEAD) ** 0.5
    n_hi = seq // TAB_ROWS
    n = lax.broadcasted_iota(I32, (1, seq), 1)
    kh = lax.broadcasted_iota(I32, (n_hi, 1), 0)
    a1 = ((kh * n) % n_hi).astype(F32) * (2.0 * jnp.pi / n_hi)
    t1 = jnp.stack([jnp.cos(a1), jnp.sin(a1)], axis=1)
    kl = lax.broadcasted_iota(I32, (TAB_ROWS, 1), 0)
    a2 = ((kl * n) % seq).astype(F32) * (2.0 * jnp.pi / seq)
    t2 = jnp.stack([jnp.cos(a2) * scale, jnp.sin(a2) * scale], axis=0)
    cs = pl.pallas_call(
        _dft_table_kernel,
        grid=(n_hi,),
        in_specs=[pl.BlockSpec((1, 2, seq), lambda i: (i, 0, 0)),
                  pl.BlockSpec((2, TAB_ROWS, seq), lambda i: (0, 0, 0))],
        out_specs=pl.BlockSpec((TAB_ROWS, 2 * seq), lambda i: (i, 0)),
        out_shape=jax.ShapeDtypeStruct((seq, 2 * seq), BF16),
        compiler_params=_cparams(("arbitrary",)),
        name="dft_table",
    )(t1, t2)
    d = lax.broadcasted_iota(I32, (FOUR_W, FOUR_W), 0)
    q = lax.broadcasted_iota(I32, (FOUR_W, FOUR_W), 1)
    same = (d // HEAD) == (q // HEAD)
    ang64 = ((d * q) % HEAD).astype(F32) * (2.0 * jnp.pi / HEAD)
    dft64 = jnp.concatenate([jnp.where(same, jnp.cos(ang64), 0.0),
                             jnp.where(same, jnp.sin(ang64), 0.0)], axis=1).astype(BF16)
    return cs, dft64


def _routing_tables(info, counts_f):
    counts = counts_f[:, 0].astype(I32)
    pc = (counts + BLK - 1) // BLK * BLK
    pends = jnp.cumsum(pc)
    pstarts = pends - pc
    eid = lax.broadcasted_iota(I32, (1, N_EXP), 1)

    def dest_of(e, r):
        return jnp.sum(jnp.where(e[:, None] == eid, pstarts[None, :], 0), axis=1) + r

    e0 = info[:, 2].astype(I32)
    e1 = info[:, 3].astype(I32)
    r0 = info[:, 4].astype(I32)
    r1 = info[:, 5].astype(I32)
    dest = jnp.concatenate([dest_of(e0, r0), dest_of(e1, r1)])
    chunk_start = jnp.concatenate([pstarts, pends[-1:]]) // BLK
    n_chunk = jnp.concatenate([pc // BLK, jnp.zeros((1,), I32)])
    return dest, chunk_start, n_chunk


def kernel(x, c, ln1_g, ln2_g, w_ada, b_ada, w_in, w_out, conv_w, conv_b, conv_gn_g, conv_gn_b, sgu_ln_g,
           sgu_ln_b, sgu_w, sgu_b, router_w, router_b, exp_w1, exp_w3, exp_w2, final_g):
    bsz, seq, _ = x.shape
    depth = w_in.shape[0]
    n_tok = bsz * seq
    n_blk = (2 * n_tok + N_EXP * (BLK - 1) + BLK - 1) // BLK
    r3 = lambda a: a.reshape(depth, 1, a.shape[-1])

    mod = jnp.pad(_ada_mod(c, w_ada, b_ada).reshape(depth, bsz, 6, D), ((0, 0), (0, 0), (0, 2), (0, 0)))
    cs, dft64 = _dft_tables(seq)
    hd = lax.broadcasted_iota(I32, (CONV_W, CONV_W), 0) // HEAD
    gavg = jnp.where(hd == hd.T, 1.0 / HEAD, 0.0).astype(BF16)
    w_in_bf = w_in.astype(BF16)
    w_out_bf = w_out.astype(BF16)
    sgu_w_bf = sgu_w.astype(BF16)
    sgu_bias = jnp.repeat(jnp.swapaxes(sgu_b, 1, 2), HEAD, axis=2)
    conv_w_p = jnp.pad(conv_w, ((0, 0), (0, 1), (0, 0)))
    router_wt = router_w.T
    router_bc = router_b.reshape(N_EXP, 1)
    fg = final_g.reshape(1, D)

    for l in range(depth):
        yac, pq = _mixer_in(x, mod, l, r3(ln1_g), w_in_bf, conv_w_p, r3(conv_b), r3(conv_gn_g), r3(conv_gn_b),
                            r3(sgu_ln_g), r3(sgu_ln_b), sgu_w_bf, sgu_bias, dft64, gavg)
        yb = _seq_dft(cs, pq.reshape(bsz, 2 * seq, FOUR_W))
        x1, h2pa, h2pb, info, counts = _mixer_out(yac.reshape(n_tok, CONV_W + SGU_W), yb.reshape(n_tok, FOUR_W),
                                                  x.reshape(n_tok, D), mod, l, r3(ln2_g), w_out_bf, router_wt,
                                                  router_bc, bsz)
        dest, chunk_start, n_chunk = _routing_tables(info, counts)
        d0 = dest[:n_tok].reshape(1, n_tok)
        d1 = dest[n_tok:].reshape(1, n_tok)
        xpa = _sc_scatter2(h2pa, d0, d1, n_blk * BLK)
        xpb = _sc_scatter2(h2pb, d0, d1, n_blk * BLK)
        ypa, ypb = _experts(chunk_start, n_chunk, xpa, xpb, exp_w1, exp_w3, exp_w2, l)
        dall = dest.reshape(1, 2 * n_tok)
        x = _combine(_sc_gather(ypa, dall), _sc_gather(ypb, dall), x1, info, mod, l, fg, bsz,
                     l == depth - 1).reshape(bsz, seq, D)
    return x
```

```python
import functools

import jax
import jax.numpy as jnp
from jax import lax
from jax.experimental import pallas as pl
from jax.experimental.pallas import tpu as pltpu
from jax.experimental.pallas import tpu_sc as plsc

F32 = jnp.float32
BF16 = jnp.bfloat16
I32 = jnp.int32
U32 = jnp.uint32
HIGHEST = lax.Precision.HIGHEST

D = 1024
HEAD = 64
CONV_W = 384
FOUR_W = 256
SGU_W = 384
SGU_HEADS = SGU_W // HEAD
Z_COLS = 2 * CONV_W + FOUR_W + 2 * SGU_W
KSIZE = 31
HALO = 16
CHUNK = 128
N_EXP = 64
N_GRP = 8
EPG = N_EXP // N_GRP
D_FF = D // 2
EPS = 1e-6

T_MIX = 512
T_DFT = 512
T_CMB = 512
ROW_W = 256
SC_WIN = 128
BLK = 512
X_AHEAD = 3
CONV_ROWS = 64
TAB_ROWS = 64
W_PIECES = 4
VMEM_LIMIT = 56 * 1024 * 1024


def _cparams(sem):
    return pltpu.CompilerParams(dimension_semantics=sem, vmem_limit_bytes=VMEM_LIMIT)


def _pack_bf16_pair(a, b):
    ua = lax.bitcast_convert_type(a.astype(BF16).astype(F32), U32) >> 16
    ub = lax.bitcast_convert_type(b.astype(BF16).astype(F32), U32) & jnp.uint32(0xFFFF0000)
    return ua | ub


def _unpack_bf16_pair(p):
    a = lax.bitcast_convert_type(p << 16, F32)
    b = lax.bitcast_convert_type(p & jnp.uint32(0xFFFF0000), F32)
    return a, b


def _ada_kernel(c_ref, w_ref, b_ref, o_ref):
    c = c_ref[...]
    ca = c * jax.nn.sigmoid(c)
    o_ref[0] = jnp.dot(ca, w_ref[0], precision=HIGHEST, preferred_element_type=F32) + b_ref[0]


def _ada_mod(c, w_ada, b_ada):
    depth, _, ncol = w_ada.shape
    bsz = c.shape[0]
    tn = 1536
    return pl.pallas_call(
        _ada_kernel,
        grid=(depth, ncol // tn),
        in_specs=[pl.BlockSpec((bsz, D), lambda l, j: (0, 0)),
                  pl.BlockSpec((1, D, tn), lambda l, j: (l, 0, j)),
                  pl.BlockSpec((1, 1, tn), lambda l, j: (l, 0, j))],
        out_specs=pl.BlockSpec((1, bsz, tn), lambda l, j: (l, 0, j)),
        out_shape=jax.ShapeDtypeStruct((depth, bsz, ncol), F32),
        compiler_params=_cparams(("arbitrary", "arbitrary")),
        name="ada_mod",
    )(c, w_ada, b_ada.reshape(depth, 1, ncol))


def _mixer_in_kernel(xm_ref, xp_ref, xn_ref, mod_ref, g1_ref, win_ref, cw_ref, cb_ref, gng_ref, gnb_ref,
                     lng_ref, lnb_ref, sw_ref, sb_ref, dft_ref, gavg_ref,
                     yac_ref, pq_ref, glu_scr, sh_scr, conv_scr):
    i = pl.program_id(1)
    n_i = pl.num_programs(1)
    T = T_MIX
    mod = mod_ref[0, 0]
    shift1 = mod[0:1, :]
    scale1 = mod[1:2, :]
    g1 = g1_ref[0]

    def norm_mod(x):
        ms = jnp.mean(x * x, axis=-1, keepdims=True)
        return (x * lax.rsqrt(ms + EPS) * g1) * (1.0 + scale1) + shift1

    h = norm_mod(xm_ref[0]).astype(BF16)
    z = jnp.dot(h, win_ref[0], preferred_element_type=F32)

    hh = norm_mod(jnp.concatenate([xp_ref[0], xn_ref[0]], axis=0)).astype(BF16)
    zh = jnp.dot(hh, win_ref[0, :, 0:2 * CONV_W], preferred_element_type=F32)
    glu_h = zh[:, 0:CONV_W] * jax.nn.sigmoid(zh[:, CONV_W:2 * CONV_W])
    glu_scr[0:HALO, :] = jnp.where(i > 0, glu_h[0:HALO], 0.0)
    glu_scr[HALO + T:2 * HALO + T, :] = jnp.where(i < n_i - 1, glu_h[HALO:2 * HALO], 0.0)
    glu_scr[HALO:HALO + T, :] = z[:, 0:CONV_W] * jax.nn.sigmoid(z[:, CONV_W:2 * CONV_W])

    off = HALO - KSIZE // 2
    for b in range(8):
        sh_scr[b] = glu_scr[b:b + T + 3 * 8, :]

    for c in range(T // CONV_ROWS):
        r0 = c * CONV_ROWS
        acc = jnp.broadcast_to(cb_ref[0], (CONV_ROWS, CONV_W))
        for k in range(KSIZE):
            a, b = divmod(k + off, 8)
            acc = acc + sh_scr[b, r0 + 8 * a:r0 + 8 * a + CONV_ROWS, :] * cw_ref[0, k:k + 1, :]
        conv_scr[r0:r0 + CONV_ROWS, :] = acc
    hc = conv_scr[...]
    gavg = gavg_ref[...]
    mu = jnp.dot(hc.astype(BF16), gavg, preferred_element_type=F32)
    dc = hc - mu
    var = jnp.dot((dc * dc).astype(BF16), gavg, preferred_element_type=F32)
    hn = dc * lax.rsqrt(var + EPS) * gng_ref[0] + gnb_ref[0]
    ya = hn * jax.nn.sigmoid(hn)
    yac_ref[0, :, 0:CONV_W] = ya.astype(BF16)

    zb = z[:, 2 * CONV_W:2 * CONV_W + FOUR_W].astype(BF16)
    pq = jnp.dot(zb, dft_ref[...], preferred_element_type=F32)
    pq_ref[0, 0] = pq[:, 0:FOUR_W].astype(BF16)
    pq_ref[0, 1] = pq[:, FOUR_W:2 * FOUR_W].astype(BF16)

    c0 = 2 * CONV_W + FOUR_W
    zc = z[:, c0:c0 + 2 * SGU_W]
    zc = 0.5 * zc * (1.0 + jnp.tanh(0.7978845608028654 * (zc + 0.044715 * (zc * zc * zc))))
    u = zc[:, 0:SGU_W]
    v = zc[:, SGU_W:2 * SGU_W]
    vm = jnp.mean(v, axis=-1, keepdims=True)
    vd = v - vm
    vv = jnp.mean(vd * vd, axis=-1, keepdims=True)
    vn = (vd * lax.rsqrt(vv + EPS) * lng_ref[0] + lnb_ref[0]).astype(BF16)
    n_chunk = T // CHUNK
    lane = lax.broadcasted_iota(I32, (CHUNK, 2 * HEAD), 1)
    for pr in range(SGU_HEADS // 2):
        cols = slice(2 * HEAD * pr, 2 * HEAD * (pr + 1))
        rhs = jnp.concatenate([vn[n * CHUNK:(n + 1) * CHUNK, cols] for n in range(n_chunk)], axis=1)
        lo = jnp.dot(sw_ref[0, 2 * pr], rhs, preferred_element_type=F32)
        hi = jnp.dot(sw_ref[0, 2 * pr + 1], rhs, preferred_element_type=F32)
        for n in range(n_chunk):
            sl = slice(n * 2 * HEAD, (n + 1) * 2 * HEAD)
            vs = jnp.where(lane < HEAD, lo[:, sl], hi[:, sl]) + sb_ref[0, :, cols]
            rows = slice(n * CHUNK, (n + 1) * CHUNK)
            yac_ref[0, rows, CONV_W + 2 * HEAD * pr:CONV_W + 2 * HEAD * (pr + 1)] = (u[rows, cols] * vs).astype(BF16)


def _mixer_in(x, mod, l, ln1_g, w_in_bf, conv_w, conv_b, gn_g, gn_b, ln_g, ln_b, sgu_w_bf, sgu_bias, dft64, gavg):
    bsz, seq, _ = x.shape
    T = T_MIX
    n_i = seq // T
    hb = T // HALO
    n_h = seq // HALO
    vec = lambda w: pl.BlockSpec((1, 1, w), lambda b, i: (l, 0, 0))
    return pl.pallas_call(
        _mixer_in_kernel,
        grid=(bsz, n_i),
        in_specs=[
            pl.BlockSpec((1, T, D), lambda b, i: (b, i, 0)),
            pl.BlockSpec((1, HALO, D), lambda b, i: (b, jnp.maximum(i * hb - 1, 0), 0)),
            pl.BlockSpec((1, HALO, D), lambda b, i: (b, jnp.minimum((i + 1) * hb, n_h - 1), 0)),
            pl.BlockSpec((1, 1, 8, D), lambda b, i: (l, b, 0, 0)),
            vec(D),
            pl.BlockSpec((1, D, Z_COLS), lambda b, i: (l, 0, 0)),
            pl.BlockSpec((1, KSIZE + 1, CONV_W), lambda b, i: (l, 0, 0)),
            vec(CONV_W), vec(CONV_W), vec(CONV_W), vec(SGU_W), vec(SGU_W),
            pl.BlockSpec((1, SGU_HEADS, CHUNK, CHUNK), lambda b, i: (l, 0, 0, 0)),
            pl.BlockSpec((1, CHUNK, SGU_W), lambda b, i: (l, 0, 0)),
            pl.BlockSpec((FOUR_W, 2 * FOUR_W), lambda b, i: (0, 0)),
            pl.BlockSpec((CONV_W, CONV_W), lambda b, i: (0, 0)),
        ],
        out_specs=[
            pl.BlockSpec((1, T, CONV_W + SGU_W), lambda b, i: (b, i, 0)),
            pl.BlockSpec((1, 2, T, FOUR_W), lambda b, i: (b, 0, i, 0)),
        ],
        out_shape=[
            jax.ShapeDtypeStruct((bsz, seq, CONV_W + SGU_W), BF16),
            jax.ShapeDtypeStruct((bsz, 2, seq, FOUR_W), BF16),
        ],
        scratch_shapes=[
            pltpu.VMEM((T + 2 * HALO, CONV_W), F32),
            pltpu.VMEM((8, T + 3 * 8, CONV_W), F32),
            pltpu.VMEM((T, CONV_W), F32),
        ],
        compiler_params=_cparams(("arbitrary", "arbitrary")),
        name="mixer_in",
    )(x, x, x, mod, ln1_g, w_in_bf, conv_w, conv_b, gn_g, gn_b, ln_g, ln_b, sgu_w_bf, sgu_bias, dft64, gavg)


def _seq_dft_kernel(cs_ref, pq_ref, o_ref):
    o_ref[0] = jnp.dot(cs_ref[...], pq_ref[0], preferred_element_type=F32).astype(BF16)


def _seq_dft(cs, pq):
    bsz, two_s, _ = pq.shape
    seq = two_s // 2
    return pl.pallas_call(
        _seq_dft_kernel,
        grid=(seq // T_DFT, bsz),
        in_specs=[pl.BlockSpec((T_DFT, two_s), lambda k, b: (k, 0)),
                  pl.BlockSpec((1, two_s, FOUR_W), lambda k, b: (b, 0, 0))],
        out_specs=pl.BlockSpec((1, T_DFT, FOUR_W), lambda k, b: (b, k, 0)),
        out_shape=jax.ShapeDtypeStruct((bsz, seq, FOUR_W), BF16),
        compiler_params=_cparams(("arbitrary", "arbitrary")),
        name="seq_dft",
    )(cs, pq)


def _mixer_out_kernel(yac_ref, yb_ref, x_ref, mod_ref, g2_ref, wout_ref, rwt_ref, rb_ref,
                      x1_ref, h2pa_ref, h2pb_ref, info_ref, cnt_ref, cnt_scr):
    i = pl.program_id(0)
    T = T_MIX

    @pl.when(i == 0)
    def _():
        cnt_scr[...] = jnp.zeros_like(cnt_scr)

    mod = mod_ref[0, 0]
    gate1 = mod[2:3, :]
    shift2 = mod[3:4, :]
    scale2 = mod[4:5, :]
    yac = yac_ref[...]
    ycat = jnp.concatenate([yac[:, 0:CONV_W], yb_ref[...], yac[:, CONV_W:CONV_W + SGU_W]], axis=1)
    o = jnp.dot(ycat, wout_ref[0], preferred_element_type=F32)
    x1 = x_ref[...] + gate1 * o
    x1_ref[...] = x1
    ms = jnp.mean(x1 * x1, axis=-1, keepdims=True)
    h2 = (x1 * lax.rsqrt(ms + EPS) * g2_ref[0]) * (1.0 + scale2) + shift2
    h2p = _pack_bf16_pair(h2[:, 0:D // 2], h2[:, D // 2:D])
    h2pa_ref[...] = h2p[:, 0:ROW_W]
    h2pb_ref[...] = h2p[:, ROW_W:2 * ROW_W]

    logits = lax.dot_general(rwt_ref[...], h2, (((1,), (1,)), ((), ())),
                             precision=HIGHEST, preferred_element_type=F32)
    mx = jnp.max(logits, axis=0, keepdims=True)
    ex = jnp.exp(logits - mx)
    probs = ex / jnp.sum(ex, axis=0, keepdims=True)
    sel = probs + rb_ref[...]
    sel3 = sel.reshape(N_GRP, EPG, T)
    probs3 = probs.reshape(N_GRP, EPG, T)
    jj = lax.broadcasted_iota(I32, (N_GRP, EPG, T), 1)
    m1 = jnp.max(sel3, axis=1, keepdims=True)
    i1 = jnp.min(jnp.where(sel3 == m1, jj, EPG), axis=1, keepdims=True)
    rest = jnp.where(jj == i1, -jnp.inf, sel3)
    m2 = jnp.max(rest, axis=1, keepdims=True)
    i2 = jnp.min(jnp.where(rest == m2, jj, EPG), axis=1, keepdims=True)
    gscore = m1 + m2
    gg = lax.broadcasted_iota(I32, (N_GRP, 1, T), 0)
    gmax = jnp.max(gscore, axis=0, keepdims=True)
    gidx = jnp.min(jnp.where(gscore == gmax, gg, N_GRP), axis=0, keepdims=True)
    ing = gg == gidx
    pick = lambda a, zero: jnp.sum(jnp.where(ing, a, zero), axis=0)
    p1 = jnp.sum(jnp.where(jj == i1, probs3, 0.0), axis=1, keepdims=True)
    p2 = jnp.sum(jnp.where(jj == i2, probs3, 0.0), axis=1, keepdims=True)
    pa = pick(p1, 0.0)
    pb = pick(p2, 0.0)
    gbase = gidx[0] * EPG
    e0 = gbase + pick(i1, 0)
    e1 = gbase + pick(i2, 0)
    den = pa + pb
    gw0 = pa / den
    gw1 = pb / den

    ee = lax.broadcasted_iota(I32, (N_EXP, T), 0)
    oh0 = ee == e0
    oh1 = ee == e1
    amat = jnp.where(oh0 | oh1, 1.0, 0.0)
    rr = lax.broadcasted_iota(I32, (T, T), 0)
    cc = lax.broadcasted_iota(I32, (T, T), 1)
    upper = jnp.where(rr < cc, 1.0, 0.0).astype(BF16)
    before = jnp.dot(amat.astype(BF16), upper, preferred_element_type=F32) + cnt_scr[...]
    r0 = jnp.sum(jnp.where(oh0, before, 0.0), axis=0, keepdims=True)
    r1 = jnp.sum(jnp.where(oh1, before, 0.0), axis=0, keepdims=True)
    cnt_scr[...] = cnt_scr[...] + jnp.sum(amat, axis=1, keepdims=True)
    cnt_ref[...] = cnt_scr[...]

    rid = lax.broadcasted_iota(I32, (8, T), 0)
    rows = jnp.zeros((8, T), F32)
    for k, val in enumerate((gw0, gw1, e0.astype(F32), e1.astype(F32), r0, r1)):
        rows = jnp.where(rid == k, val, rows)
    rows = jnp.concatenate([rows, jnp.zeros((CHUNK - 8, T), F32)], axis=0)
    info_ref[...] = rows.T


def _mixer_out(yac, yb, x, mod, l, ln2_g, w_out_bf, router_wt, router_b, bsz):
    n_tok = x.shape[0]
    T = T_MIX
    per_b = n_tok // bsz // T
    row = lambda w: pl.BlockSpec((T, w), lambda i: (i, 0))
    return pl.pallas_call(
        _mixer_out_kernel,
        grid=(n_tok // T,),
        in_specs=[
            row(CONV_W + SGU_W), row(FOUR_W), row(D),
            pl.BlockSpec((1, 1, 8, D), lambda i: (l, i // per_b, 0, 0)),
            pl.BlockSpec((1, 1, D), lambda i: (l, 0, 0)),
            pl.BlockSpec((1, D, D), lambda i: (l, 0, 0)),
            pl.BlockSpec((N_EXP, D), lambda i: (0, 0)),
            pl.BlockSpec((N_EXP, 1), lambda i: (0, 0)),
        ],
        out_specs=[row(D), row(ROW_W), row(ROW_W), row(CHUNK), pl.BlockSpec((N_EXP, 1), lambda i: (0, 0))],
        out_shape=[
            jax.ShapeDtypeStruct((n_tok, D), F32),
            jax.ShapeDtypeStruct((n_tok, ROW_W), U32),
            jax.ShapeDtypeStruct((n_tok, ROW_W), U32),
            jax.ShapeDtypeStruct((n_tok, CHUNK), F32),
            jax.ShapeDtypeStruct((N_EXP, 1), F32),
        ],
        scratch_shapes=[pltpu.VMEM((N_EXP, 1), F32)],
        compiler_params=_cparams(("arbitrary",)),
        name="mixer_out",
    )(yac, yb, x, mod, ln2_g, w_out_bf, router_wt, router_b)


def _sc_mesh():
    return plsc.VectorSubcoreMesh(core_axis_name="c", subcore_axis_name="s")


def _sc_scatter2(src, idx0, idx1, n_rows):
    n = src.shape[0]

    @functools.partial(pl.kernel, out_type=jax.ShapeDtypeStruct((n_rows, ROW_W), src.dtype), mesh=_sc_mesh())
    def scatter(x_hbm, i0_hbm, i1_hbm, o_hbm):
        def body(x_vmem, i0_vmem, i1_vmem):
            pltpu.sync_copy(x_vmem, o_hbm.at[i0_vmem.at[0]])
            pltpu.sync_copy(x_vmem, o_hbm.at[i1_vmem.at[0]])

        pltpu.emit_pipeline(
            body, grid=(n // SC_WIN,),
            in_specs=[pl.BlockSpec((SC_WIN, ROW_W), index_map=lambda i: (i, 0)),
                      pl.BlockSpec((1, SC_WIN), index_map=lambda i: (0, i)),
                      pl.BlockSpec((1, SC_WIN), index_map=lambda i: (0, i))],
            out_specs=[],
            core_axis_name=("c", "s"), dimension_semantics=(pltpu.PARALLEL,),
        )(x_hbm, i0_hbm, i1_hbm)

    return scatter(src, idx0, idx1)


def _sc_gather(src, idx):
    m = idx.shape[1]

    @functools.partial(pl.kernel, out_type=jax.ShapeDtypeStruct((m, ROW_W), src.dtype), mesh=_sc_mesh())
    def gather(x_hbm, i_hbm, o_hbm):
        def body(i_vmem, o_vmem):
            pltpu.sync_copy(x_hbm.at[i_vmem.at[0]], o_vmem)

        pltpu.emit_pipeline(
            body, grid=(m // SC_WIN,),
            in_specs=[pl.BlockSpec((1, SC_WIN), index_map=lambda i: (0, i))],
            out_specs=[pl.BlockSpec((SC_WIN, ROW_W), index_map=lambda i: (i, 0))],
            core_axis_name=("c", "s"), dimension_semantics=(pltpu.PARALLEL,),
        )(i_hbm, o_hbm)

    return gather(src, idx)


def _experts_kernel(l, n_blk, start_ref, nchunk_ref, xpa_ref, xpb_ref, w1_ref, w3_ref, w2_ref, ypa_ref, ypb_ref,
                    w13_scr, w2_scr, wbuf13, wbuf2, xbuf, ybuf, wsem, xsem, ysem):
    e = pl.program_id(0)
    nc = nchunk_ref[e]
    chunk0 = start_ref[e]
    n_used = start_ref[N_EXP]

    def w_copies(ex, slot):
        cps = []
        for p in range(W_PIECES):
            r13 = pl.ds(p * (D // W_PIECES), D // W_PIECES)
            r2 = pl.ds(p * (D_FF // W_PIECES), D_FF // W_PIECES)
            cps.append(pltpu.make_async_copy(w1_ref.at[l, ex, r13], wbuf13.at[slot, 0, r13], wsem.at[slot]))
            cps.append(pltpu.make_async_copy(w3_ref.at[l, ex, r13], wbuf13.at[slot, 1, r13], wsem.at[slot]))
            cps.append(pltpu.make_async_copy(w2_ref.at[l, ex, r2], wbuf2.at[slot, r2], wsem.at[slot]))
        return cps

    def x_copies(g):
        rows = pl.ds(pl.multiple_of(g * BLK, BLK), BLK)
        slot = g % (X_AHEAD + 1)
        return [pltpu.make_async_copy(src.at[rows], xbuf.at[slot, h], xsem.at[slot])
                for h, src in enumerate((xpa_ref, xpb_ref))]

    def y_copies(g):
        rows = pl.ds(pl.multiple_of(g * BLK, BLK), BLK)
        slot = g % 2
        return [pltpu.make_async_copy(ybuf.at[slot, h], dst.at[rows], ysem.at[slot])
                for h, dst in enumerate((ypa_ref, ypb_ref))]

    def start(cps):
        for cp in cps:
            cp.start()

    def wait(cps):
        for cp in cps:
            cp.wait()

    @pl.when(e == 0)
    def _():
        for j in range(X_AHEAD):
            @pl.when(j < n_used)
            def _():
                start(x_copies(j))

        start(w_copies(0, 0))

    @pl.when(e < N_EXP)
    def _():
        wslot = e % 2
        wait(w_copies(e, wslot))

        @pl.when(e + 1 < N_EXP)
        def _():
            start(w_copies(e + 1, 1 - wslot))

        @pl.when(nc > 0)
        def _():
            w13_scr[:, 0:D_FF] = wbuf13[wslot, 0].astype(BF16)
            w13_scr[:, D_FF:2 * D_FF] = wbuf13[wslot, 1].astype(BF16)
            w2_scr[...] = wbuf2[wslot].astype(BF16)

            def chunk(c, carry):
                g = chunk0 + c
                slot = g % (X_AHEAD + 1)

                @pl.when(g + X_AHEAD < n_used)
                def _():
                    start(x_copies(g + X_AHEAD))

                wait(x_copies(g))

                @pl.when(g >= 2)
                def _():
                    wait(y_copies(g - 2))

                a, b = _unpack_bf16_pair(jnp.concatenate([xbuf[slot, 0], xbuf[slot, 1]], axis=1))
                x = jnp.concatenate([a.astype(BF16), b.astype(BF16)], axis=1)
                h13 = jnp.dot(x, w13_scr[...], preferred_element_type=F32)
                h1 = h13[:, 0:D_FF]
                act = (h1 * jax.nn.sigmoid(h1) * h13[:, D_FF:2 * D_FF]).astype(BF16)
                y = jnp.dot(act, w2_scr[...], preferred_element_type=F32)
                yp = _pack_bf16_pair(y[:, 0:D // 2], y[:, D // 2:D])
                ybuf[g % 2, 0] = yp[:, 0:ROW_W]
                ybuf[g % 2, 1] = yp[:, ROW_W:2 * ROW_W]
                start(y_copies(g))
                return carry

            lax.fori_loop(0, nc, chunk, 0)

    @pl.when(e == N_EXP)
    def _():
        @pl.when(n_used >= 2)
        def _():
            wait(y_copies(n_used - 2))

        wait(y_copies(n_used - 1))
        ybuf[0] = jnp.zeros((2, BLK, ROW_W), U32)

        def fill_one(g, carry):
            rows = pl.ds(pl.multiple_of(g * BLK, BLK), BLK)
            cps = [pltpu.make_async_copy(ybuf.at[0, h], dst.at[rows], ysem.at[0])
                   for h, dst in enumerate((ypa_ref, ypb_ref))]
            start(cps)
            wait(cps)
            return carry

        lax.fori_loop(n_used, n_blk, fill_one, 0)


def _experts(chunk_start, n_chunk, xpa, xpb, w1, w3, w2, l):
    n_rows = xpa.shape[0]
    n_blk = n_rows // BLK
    hbm = pl.BlockSpec(memory_space=pl.ANY)
    half = jax.ShapeDtypeStruct((n_rows, ROW_W), U32)
    return pl.pallas_call(
        functools.partial(_experts_kernel, l, n_blk),
        grid_spec=pltpu.PrefetchScalarGridSpec(
            num_scalar_prefetch=2,
            grid=(N_EXP + 1,),
            in_specs=[hbm, hbm, hbm, hbm, hbm],
            out_specs=[hbm, hbm],
            scratch_shapes=[pltpu.VMEM((D, 2 * D_FF), BF16), pltpu.VMEM((D_FF, D), BF16),
                            pltpu.VMEM((2, 2, D, D_FF), F32), pltpu.VMEM((2, D_FF, D), F32),
                            pltpu.VMEM((X_AHEAD + 1, 2, BLK, ROW_W), U32), pltpu.VMEM((2, 2, BLK, ROW_W), U32),
                            pltpu.SemaphoreType.DMA((2,)), pltpu.SemaphoreType.DMA((X_AHEAD + 1,)),
                            pltpu.SemaphoreType.DMA((2,))],
        ),
        out_shape=[half, half],
        compiler_params=_cparams(("arbitrary",)),
        name="experts",
    )(chunk_start, n_chunk, xpa, xpb, w1, w3, w2)


def _combine_kernel(final, ga0_ref, gb0_ref, ga1_ref, gb1_ref, x1_ref, info_ref, mod_ref, fg_ref, o_ref):
    info = info_ref[...]
    gw0 = info[:, 0:1]
    gw1 = info[:, 1:2]
    a0, b0 = _unpack_bf16_pair(jnp.concatenate([ga0_ref[...], gb0_ref[...]], axis=1))
    a1, b1 = _unpack_bf16_pair(jnp.concatenate([ga1_ref[...], gb1_ref[...]], axis=1))
    y = jnp.concatenate([gw0 * a0 + gw1 * a1, gw0 * b0 + gw1 * b1], axis=1)
    gate2 = mod_ref[0, 0][5:6, :]
    x2 = x1_ref[...] + gate2 * y
    if final:
        ms = jnp.mean(x2 * x2, axis=-1, keepdims=True)
        x2 = x2 * lax.rsqrt(ms + EPS) * fg_ref[...]
    o_ref[...] = x2


def _combine(ga, gb, x1, info, mod, l, final_g, bsz, final):
    n_tok = x1.shape[0]
    n_i = n_tok // T_CMB
    per_b = n_i // bsz
    first = pl.BlockSpec((T_CMB, ROW_W), lambda i: (i, 0))
    second = pl.BlockSpec((T_CMB, ROW_W), lambda i: (i + n_i, 0))
    return pl.pallas_call(
        functools.partial(_combine_kernel, final),
        grid=(n_i,),
        in_specs=[
            first, first, second, second,
            pl.BlockSpec((T_CMB, D), lambda i: (i, 0)),
            pl.BlockSpec((T_CMB, CHUNK), lambda i: (i, 0)),
            pl.BlockSpec((1, 1, 8, D), lambda i: (l, i // per_b, 0, 0)),
            pl.BlockSpec((1, D), lambda i: (0, 0)),
        ],
        out_specs=pl.BlockSpec((T_CMB, D), lambda i: (i, 0)),
        out_shape=jax.ShapeDtypeStruct((n_tok, D), F32),
        compiler_params=_cparams(("arbitrary",)),
        name="combine",
    )(ga, gb, ga, gb, x1, info, mod, final_g)


def _dft_table_kernel(t1_ref, t2_ref, o_ref):
    seq = t2_ref.shape[2]
    c1 = t1_ref[0, 0:1, :]
    s1 = t1_ref[0, 1:2, :]
    c2 = t2_ref[0]
    s2 = t2_ref[1]
    o_ref[:, 0:seq] = (c1 * c2 - s1 * s2).astype(BF16)
    o_ref[:, seq:2 * seq] = (-(s1 * c2 + c1 * s2)).astype(BF16)


def _dft_tables(seq):
    scale = 1.0 / (seq * HEAD) ** 0.5
    n_hi = seq // TAB_ROWS
    n = lax.broadcasted_iota(I32, (1, seq), 1)
    kh = lax.broadcasted_iota(I32, (n_hi, 1), 0)
    a1 = ((kh * n) % n_hi).astype(F32) * (2.0 * jnp.pi / n_hi)
    t1 = jnp.stack([jnp.cos(a1), jnp.sin(a1)], axis=1)
    kl = lax.broadcasted_iota(I32, (TAB_ROWS, 1), 0)
    a2 = ((kl * n) % seq).astype(F32) * (2.0 * jnp.pi / seq)
    t2 = jnp.stack([jnp.cos(a2) * scale, jnp.sin(a2) * scale], axis=0)
    cs = pl.pallas_call(
        _dft_table_kernel,
        grid=(n_hi,),
        in_specs=[pl.BlockSpec((1, 2, seq), lambda i: (i, 0, 0)),
                  pl.BlockSpec((2, TAB_ROWS, seq), lambda i: (0, 0, 0))],
        out_specs=pl.BlockSpec((TAB_ROWS, 2 * seq), lambda i: (i, 0)),
        out_shape=jax.ShapeDtypeStruct((seq, 2 * seq), BF16),
        compiler_params=_cparams(("arbitrary",)),
        name="dft_table",
    )(t1, t2)
    d = lax.broadcasted_iota(I32, (FOUR_W, FOUR_W), 0)
    q = lax.broadcasted_iota(I32, (FOUR_W, FOUR_W), 1)
    same = (d // HEAD) == (q // HEAD)
    ang64 = ((d * q) % HEAD).astype(F32) * (2.0 * jnp.pi / HEAD)
    dft64 = jnp.concatenate([jnp.where(same, jnp.cos(ang64), 0.0),
                             jnp.where(same, jnp.sin(ang64), 0.0)], axis=1).astype(BF16)
    return cs, dft64


def _routing_tables(info, counts_f):
    counts = counts_f[:, 0].astype(I32)
    pc = (counts + BLK - 1) // BLK * BLK
    pends = jnp.cumsum(pc)
    pstarts = pends - pc
    eid = lax.broadcasted_iota(I32, (1, N_EXP), 1)

    def dest_of(e, r):
        return jnp.sum(jnp.where(e[:, None] == eid, pstarts[None, :], 0), axis=1) + r

    e0 = info[:, 2].astype(I32)
    e1 = info[:, 3].astype(I32)
    r0 = info[:, 4].astype(I32)
    r1 = info[:, 5].astype(I32)
    dest = jnp.concatenate([dest_of(e0, r0), dest_of(e1, r1)])
    chunk_start = jnp.concatenate([pstarts, pends[-1:]]) // BLK
    n_chunk = jnp.concatenate([pc // BLK, jnp.zeros((1,), I32)])
    return dest, chunk_start, n_chunk


def kernel(x, c, ln1_g, ln2_g, w_ada, b_ada, w_in, w_out, conv_w, conv_b, conv_gn_g, conv_gn_b, sgu_ln_g,
           sgu_ln_b, sgu_w, sgu_b, router_w, router_b, exp_w1, exp_w3, exp_w2, final_g):
    bsz, seq, _ = x.shape
    depth = w_in.shape[0]
    n_tok = bsz * seq
    n_blk = (2 * n_tok + N_EXP * (BLK - 1) + BLK - 1) // BLK
    r3 = lambda a: a.reshape(depth, 1, a.shape[-1])

    mod = jnp.pad(_ada_mod(c, w_ada, b_ada).reshape(depth, bsz, 6, D), ((0, 0), (0, 0), (0, 2), (0, 0)))
    cs, dft64 = _dft_tables(seq)
    hd = lax.broadcasted_iota(I32, (CONV_W, CONV_W), 0) // HEAD
    gavg = jnp.where(hd == hd.T, 1.0 / HEAD, 0.0).astype(BF16)
    w_in_bf = w_in.astype(BF16)
    w_out_bf = w_out.astype(BF16)
    sgu_w_bf = sgu_w.astype(BF16)
    sgu_bias = jnp.repeat(jnp.swapaxes(sgu_b, 1, 2), HEAD, axis=2)
    conv_w_p = jnp.pad(conv_w, ((0, 0), (0, 1), (0, 0)))
    router_wt = router_w.T
    router_bc = router_b.reshape(N_EXP, 1)
    fg = final_g.reshape(1, D)

    for l in range(depth):
        yac, pq = _mixer_in(x, mod, l, r3(ln1_g), w_in_bf, conv_w_p, r3(conv_b), r3(conv_gn_g), r3(conv_gn_b),
                            r3(sgu_ln_g), r3(sgu_ln_b), sgu_w_bf, sgu_bias, dft64, gavg)
        yb = _seq_dft(cs, pq.reshape(bsz, 2 * seq, FOUR_W))
        x1, h2pa, h2pb, info, counts = _mixer_out(yac.reshape(n_tok, CONV_W + SGU_W), yb.reshape(n_tok, FOUR_W),
                                                  x.reshape(n_tok, D), mod, l, r3(ln2_g), w_out_bf, router_wt,
                                                  router_bc, bsz)
        dest, chunk_start, n_chunk = _routing_tables(info, counts)
        d0 = dest[:n_tok].reshape(1, n_tok)
        d1 = dest[n_tok:].reshape(1, n_tok)
        xpa = _sc_scatter2(h2pa, d0, d1, n_blk * BLK)
        xpb = _sc_scatter2(h2pb, d0, d1, n_blk * BLK)
        ypa, ypb = _experts(chunk_start, n_chunk, xpa, xpb, exp_w1, exp_w3, exp_w2, l)
        dall = dest.reshape(1, 2 * n_tok)
        x = _combine(_sc_gather(ypa, dall), _sc_gather(ypb, dall), x1, info, mod, l, fg, bsz,
                     l == depth - 1).reshape(bsz, seq, D)
    return x
```

```python
import functools

import jax
import jax.numpy as jnp
from jax import lax
from jax.experimental import pallas as pl
from jax.experimental.pallas import tpu as pltpu
from jax.experimental.pallas import tpu_sc as plsc

F32 = jnp.float32
BF16 = jnp.bfloat16
I32 = jnp.int32
U32 = jnp.uint32
HIGHEST = lax.Precision.HIGHEST

D = 1024
HEAD = 64
CONV_W = 384
FOUR_W = 256
SGU_W = 384
SGU_HEADS = SGU_W // HEAD
Z_COLS = 2 * CONV_W + FOUR_W + 2 * SGU_W
KSIZE = 31
HALO = 16
CHUNK = 128
N_EXP = 64
N_GRP = 8
EPG = N_EXP // N_GRP
D_FF = D // 2
EPS = 1e-6

T_MIX = 512
T_DFT = 512
T_CMB = 512
ROW_W = 256
SC_WIN = 128
BLK = 512
X_AHEAD = 3
CONV_ROWS = 64
TAB_ROWS = 64
W_PIECES = 4
VMEM_LIMIT = 56 * 1024 * 1024


def _cparams(sem):
    return pltpu.CompilerParams(dimension_semantics=sem, vmem_limit_bytes=VMEM_LIMIT)


def _pack_bf16_pair(a, b):
    ua = lax.bitcast_convert_type(a.astype(BF16).astype(F32), U32) >> 16
    ub = lax.bitcast_convert_type(b.astype(BF16).astype(F32), U32) & jnp.uint32(0xFFFF0000)
    return ua | ub


def _unpack_bf16_pair(p):
    a = lax.bitcast_convert_type(p << 16, F32)
    b = lax.bitcast_convert_type(p & jnp.uint32(0xFFFF0000), F32)
    return a, b


def _ada_kernel(c_ref, w_ref, b_ref, o_ref):
    c = c_ref[...]
    ca = c * jax.nn.sigmoid(c)
    o_ref[0] = jnp.dot(ca, w_ref[0], precision=HIGHEST, preferred_element_type=F32) + b_ref[0]


def _ada_mod(c, w_ada, b_ada):
    depth, _, ncol = w_ada.shape
    bsz = c.shape[0]
    tn = 1536
    return pl.pallas_call(
        _ada_kernel,
        grid=(depth, ncol // tn),
        in_specs=[pl.BlockSpec((bsz, D), lambda l, j: (0, 0)),
                  pl.BlockSpec((1, D, tn), lambda l, j: (l, 0, j)),
                  pl.BlockSpec((1, 1, tn), lambda l, j: (l, 0, j))],
        out_specs=pl.BlockSpec((1, bsz, tn), lambda l, j: (l, 0, j)),
        out_shape=jax.ShapeDtypeStruct((depth, bsz, ncol), F32),
        compiler_params=_cparams(("arbitrary", "arbitrary")),
        name="ada_mod",
    )(c, w_ada, b_ada.reshape(depth, 1, ncol))


def _mixer_in_kernel(xm_ref, xp_ref, xn_ref, mod_ref, g1_ref, win_ref, cw_ref, cb_ref, gng_ref, gnb_ref,
                     lng_ref, lnb_ref, sw_ref, sb_ref, dft_ref, gavg_ref,
                     yac_ref, pq_ref, glu_scr, sh_scr, conv_scr):
    i = pl.program_id(1)
    n_i = pl.num_programs(1)
    T = T_MIX
    mod = mod_ref[0, 0]
    shift1 = mod[0:1, :]
    gain1 = g1_ref[0] * (1.0 + mod[1:2, :])

    def norm_mod(x):
        ms = jnp.mean(x * x, axis=-1, keepdims=True)
        return x * lax.rsqrt(ms + EPS) * gain1 + shift1

    h = norm_mod(xm_ref[0]).astype(BF16)
    z = jnp.dot(h, win_ref[0], preferred_element_type=F32)

    hh = norm_mod(jnp.concatenate([xp_ref[0], xn_ref[0]], axis=0)).astype(BF16)
    zh = jnp.dot(hh, win_ref[0, :, 0:2 * CONV_W], preferred_element_type=F32)
    glu_h = zh[:, 0:CONV_W] * jax.nn.sigmoid(zh[:, CONV_W:2 * CONV_W])
    glu_scr[0:HALO, :] = jnp.where(i > 0, glu_h[0:HALO], 0.0)
    glu_scr[HALO + T:2 * HALO + T, :] = jnp.where(i < n_i - 1, glu_h[HALO:2 * HALO], 0.0)
    glu_scr[HALO:HALO + T, :] = z[:, 0:CONV_W] * jax.nn.sigmoid(z[:, CONV_W:2 * CONV_W])

    off = HALO - KSIZE // 2
    for b in range(8):
        sh_scr[b] = glu_scr[b:b + T + 3 * 8, :]

    for c in range(T // CONV_ROWS):
        r0 = c * CONV_ROWS
        acc = jnp.broadcast_to(cb_ref[0], (CONV_ROWS, CONV_W))
        for k in range(KSIZE):
            a, b = divmod(k + off, 8)
            acc = acc + sh_scr[b, r0 + 8 * a:r0 + 8 * a + CONV_ROWS, :] * cw_ref[0, k:k + 1, :]
        conv_scr[r0:r0 + CONV_ROWS, :] = acc
    hc = conv_scr[...]
    gavg = gavg_ref[...]
    mu = jnp.dot(hc.astype(BF16), gavg, preferred_element_type=F32)
    dc = hc - mu
    var = jnp.dot((dc * dc).astype(BF16), gavg, preferred_element_type=F32)
    hn = dc * lax.rsqrt(var + EPS) * gng_ref[0] + gnb_ref[0]
    ya = hn * jax.nn.sigmoid(hn)
    yac_ref[0, :, 0:CONV_W] = ya.astype(BF16)

    zb = z[:, 2 * CONV_W:2 * CONV_W + FOUR_W].astype(BF16)
    pq = jnp.dot(zb, dft_ref[...], preferred_element_type=F32)
    pq_ref[0, 0] = pq[:, 0:FOUR_W].astype(BF16)
    pq_ref[0, 1] = pq[:, FOUR_W:2 * FOUR_W].astype(BF16)

    c0 = 2 * CONV_W + FOUR_W
    zc = z[:, c0:c0 + 2 * SGU_W]
    zc = 0.5 * zc * (1.0 + jnp.tanh(0.7978845608028654 * (zc + 0.044715 * (zc * zc * zc))))
    u = zc[:, 0:SGU_W]
    v = zc[:, SGU_W:2 * SGU_W]
    vm = jnp.mean(v, axis=-1, keepdims=True)
    vd = v - vm
    vv = jnp.mean(vd * vd, axis=-1, keepdims=True)
    vn = (vd * lax.rsqrt(vv + EPS) * lng_ref[0] + lnb_ref[0]).astype(BF16)
    n_chunk = T // CHUNK
    lane = lax.broadcasted_iota(I32, (CHUNK, 2 * HEAD), 1)
    for pr in range(SGU_HEADS // 2):
        cols = slice(2 * HEAD * pr, 2 * HEAD * (pr + 1))
        rhs = jnp.concatenate([vn[n * CHUNK:(n + 1) * CHUNK, cols] for n in range(n_chunk)], axis=1)
        lo = jnp.dot(sw_ref[0, 2 * pr], rhs, preferred_element_type=F32)
        hi = jnp.dot(sw_ref[0, 2 * pr + 1], rhs, preferred_element_type=F32)
        for n in range(n_chunk):
            sl = slice(n * 2 * HEAD, (n + 1) * 2 * HEAD)
            vs = jnp.where(lane < HEAD, lo[:, sl], hi[:, sl]) + sb_ref[0, :, cols]
            rows = slice(n * CHUNK, (n + 1) * CHUNK)
            yac_ref[0, rows, CONV_W + 2 * HEAD * pr:CONV_W + 2 * HEAD * (pr + 1)] = (u[rows, cols] * vs).astype(BF16)


def _mixer_in(x, mod, l, ln1_g, w_in_bf, conv_w, conv_b, gn_g, gn_b, ln_g, ln_b, sgu_w_bf, sgu_bias, dft64, gavg):
    bsz, seq, _ = x.shape
    T = T_MIX
    n_i = seq // T
    hb = T // HALO
    n_h = seq // HALO
    vec = lambda w: pl.BlockSpec((1, 1, w), lambda b, i: (l, 0, 0))
    return pl.pallas_call(
        _mixer_in_kernel,
        grid=(bsz, n_i),
        in_specs=[
            pl.BlockSpec((1, T, D), lambda b, i: (b, i, 0)),
            pl.BlockSpec((1, HALO, D), lambda b, i: (b, jnp.maximum(i * hb - 1, 0), 0)),
            pl.BlockSpec((1, HALO, D), lambda b, i: (b, jnp.minimum((i + 1) * hb, n_h - 1), 0)),
            pl.BlockSpec((1, 1, 8, D), lambda b, i: (l, b, 0, 0)),
            vec(D),
            pl.BlockSpec((1, D, Z_COLS), lambda b, i: (l, 0, 0)),
            pl.BlockSpec((1, KSIZE + 1, CONV_W), lambda b, i: (l, 0, 0)),
            vec(CONV_W), vec(CONV_W), vec(CONV_W), vec(SGU_W), vec(SGU_W),
            pl.BlockSpec((1, SGU_HEADS, CHUNK, CHUNK), lambda b, i: (l, 0, 0, 0)),
            pl.BlockSpec((1, CHUNK, SGU_W), lambda b, i: (l, 0, 0)),
            pl.BlockSpec((FOUR_W, 2 * FOUR_W), lambda b, i: (0, 0)),
            pl.BlockSpec((CONV_W, CONV_W), lambda b, i: (0, 0)),
        ],
        out_specs=[
            pl.BlockSpec((1, T, CONV_W + SGU_W), lambda b, i: (b, i, 0)),
            pl.BlockSpec((1, 2, T, FOUR_W), lambda b, i: (b, 0, i, 0)),
        ],
        out_shape=[
            jax.ShapeDtypeStruct((bsz, seq, CONV_W + SGU_W), BF16),
            jax.ShapeDtypeStruct((bsz, 2, seq, FOUR_W), BF16),
        ],
        scratch_shapes=[
            pltpu.VMEM((T + 2 * HALO, CONV_W), F32),
            pltpu.VMEM((8, T + 3 * 8, CONV_W), F32),
            pltpu.VMEM((T, CONV_W), F32),
        ],
        compiler_params=_cparams(("arbitrary", "arbitrary")),
        name="mixer_in",
    )(x, x, x, mod, ln1_g, w_in_bf, conv_w, conv_b, gn_g, gn_b, ln_g, ln_b, sgu_w_bf, sgu_bias, dft64, gavg)


def _seq_dft_kernel(cs_ref, pq_ref, o_ref):
    o_ref[0] = jnp.dot(cs_ref[...], pq_ref[0], preferred_element_type=F32).astype(BF16)


def _seq_dft(cs, pq):
    bsz, two_s, _ = pq.shape
    seq = two_s // 2
    return pl.pallas_call(
        _seq_dft_kernel,
        grid=(seq // T_DFT, bsz),
        in_specs=[pl.BlockSpec((T_DFT, two_s), lambda k, b: (k, 0)),
                  pl.BlockSpec((1, two_s, FOUR_W), lambda k, b: (b, 0, 0))],
        out_specs=pl.BlockSpec((1, T_DFT, FOUR_W), lambda k, b: (b, k, 0)),
        out_shape=jax.ShapeDtypeStruct((bsz, seq, FOUR_W), BF16),
        compiler_params=_cparams(("arbitrary", "arbitrary")),
        name="seq_dft",
    )(cs, pq)


def _mixer_out_kernel(yac_ref, yb_ref, x_ref, mod_ref, g2_ref, wout_ref, rwt_ref, rb_ref,
                      x1_ref, h2pa_ref, h2pb_ref, info_ref, route_ref, cnt_ref, cnt_scr):
    i = pl.program_id(0)
    T = T_MIX

    @pl.when(i == 0)
    def _():
        cnt_scr[...] = jnp.zeros_like(cnt_scr)

    mod = mod_ref[0, 0]
    gate1 = mod[2:3, :]
    shift2 = mod[3:4, :]
    gain2 = g2_ref[0] * (1.0 + mod[4:5, :])
    yac = yac_ref[...]
    ycat = jnp.concatenate([yac[:, 0:CONV_W], yb_ref[...], yac[:, CONV_W:CONV_W + SGU_W]], axis=1)
    o = jnp.dot(ycat, wout_ref[0], preferred_element_type=F32)
    x1 = x_ref[...] + gate1 * o
    x1_ref[...] = x1
    ms = jnp.mean(x1 * x1, axis=-1, keepdims=True)
    h2 = x1 * lax.rsqrt(ms + EPS) * gain2 + shift2
    h2p = _pack_bf16_pair(h2[:, 0:D // 2], h2[:, D // 2:D])
    h2pa_ref[...] = h2p[:, 0:ROW_W]
    h2pb_ref[...] = h2p[:, ROW_W:2 * ROW_W]

    h_hi = h2.astype(BF16)
    h_lo = (h2 - h_hi.astype(F32)).astype(BF16)
    nt = (((1,), (1,)), ((), ()))
    part = lax.dot_general(rwt_ref[...], h_hi, nt, preferred_element_type=F32)
    logits = (part[0:N_EXP] + part[N_EXP:2 * N_EXP]
              + lax.dot_general(rwt_ref[0:N_EXP, :], h_lo, nt, preferred_element_type=F32))
    mx = jnp.max(logits, axis=0, keepdims=True)
    ex = jnp.exp(logits - mx)
    probs = ex / jnp.sum(ex, axis=0, keepdims=True)
    sel = probs + rb_ref[...]
    sel3 = sel.reshape(N_GRP, EPG, T)
    probs3 = probs.reshape(N_GRP, EPG, T)
    jj = lax.broadcasted_iota(I32, (N_GRP, EPG, T), 1)
    m1 = jnp.max(sel3, axis=1, keepdims=True)
    i1 = jnp.min(jnp.where(sel3 == m1, jj, EPG), axis=1, keepdims=True)
    rest = jnp.where(jj == i1, -jnp.inf, sel3)
    m2 = jnp.max(rest, axis=1, keepdims=True)
    i2 = jnp.min(jnp.where(rest == m2, jj, EPG), axis=1, keepdims=True)
    gscore = m1 + m2
    gg = lax.broadcasted_iota(I32, (N_GRP, 1, T), 0)
    gmax = jnp.max(gscore, axis=0, keepdims=True)
    gidx = jnp.min(jnp.where(gscore == gmax, gg, N_GRP), axis=0, keepdims=True)
    ing = gg == gidx
    pick = lambda a, zero: jnp.sum(jnp.where(ing, a, zero), axis=0)
    p1 = jnp.sum(jnp.where(jj == i1, probs3, 0.0), axis=1, keepdims=True)
    p2 = jnp.sum(jnp.where(jj == i2, probs3, 0.0), axis=1, keepdims=True)
    pa = pick(p1, 0.0)
    pb = pick(p2, 0.0)
    gbase = gidx[0] * EPG
    e0 = gbase + pick(i1, 0)
    e1 = gbase + pick(i2, 0)
    den = pa + pb
    gw0 = pa / den
    gw1 = pb / den

    ee = lax.broadcasted_iota(I32, (N_EXP, T), 0)
    oh0 = ee == e0
    oh1 = ee == e1
    amat = jnp.where(oh0 | oh1, 1.0, 0.0)
    rr = lax.broadcasted_iota(I32, (T, T), 0)
    cc = lax.broadcasted_iota(I32, (T, T), 1)
    upper = jnp.where(rr < cc, 1.0, 0.0).astype(BF16)
    before = jnp.dot(amat.astype(BF16), upper, preferred_element_type=F32) + cnt_scr[...]
    r0 = jnp.sum(jnp.where(oh0, before, 0.0), axis=0, keepdims=True)
    r1 = jnp.sum(jnp.where(oh1, before, 0.0), axis=0, keepdims=True)
    cnt_scr[...] = cnt_scr[...] + jnp.sum(amat, axis=1, keepdims=True)
    cnt_ref[...] = cnt_scr[...]

    rid = lax.broadcasted_iota(I32, (8, T), 0)
    route = jnp.zeros((8, T), I32)
    for k, val in enumerate((e0, e1, r0.astype(I32), r1.astype(I32))):
        route = jnp.where(rid == k, val, route)
    route_ref[...] = route
    rows = jnp.where(rid == 0, gw0, jnp.where(rid == 1, gw1, 0.0))
    rows = jnp.concatenate([rows, jnp.zeros((CHUNK - 8, T), F32)], axis=0)
    info_ref[...] = rows.T


def _mixer_out(yac, yb, x, mod, l, ln2_g, w_out_bf, router_wt, router_b, bsz):
    n_tok = x.shape[0]
    T = T_MIX
    per_b = n_tok // bsz // T
    row = lambda w: pl.BlockSpec((T, w), lambda i: (i, 0))
    return pl.pallas_call(
        _mixer_out_kernel,
        grid=(n_tok // T,),
        in_specs=[
            row(CONV_W + SGU_W), row(FOUR_W), row(D),
            pl.BlockSpec((1, 1, 8, D), lambda i: (l, i // per_b, 0, 0)),
            pl.BlockSpec((1, 1, D), lambda i: (l, 0, 0)),
            pl.BlockSpec((1, D, D), lambda i: (l, 0, 0)),
            pl.BlockSpec((2 * N_EXP, D), lambda i: (0, 0)),
            pl.BlockSpec((N_EXP, 1), lambda i: (0, 0)),
        ],
        out_specs=[row(D), row(ROW_W), row(ROW_W), row(CHUNK), pl.BlockSpec((8, T), lambda i: (0, i)),
                   pl.BlockSpec((N_EXP, 1), lambda i: (0, 0))],
        out_shape=[
            jax.ShapeDtypeStruct((n_tok, D), F32),
            jax.ShapeDtypeStruct((n_tok, ROW_W), U32),
            jax.ShapeDtypeStruct((n_tok, ROW_W), U32),
            jax.ShapeDtypeStruct((n_tok, CHUNK), F32),
            jax.ShapeDtypeStruct((8, n_tok), I32),
            jax.ShapeDtypeStruct((N_EXP, 1), F32),
        ],
        scratch_shapes=[pltpu.VMEM((N_EXP, 1), F32)],
        compiler_params=_cparams(("arbitrary",)),
        name="mixer_out",
    )(yac, yb, x, mod, ln2_g, w_out_bf, router_wt, router_b)


def _sc_mesh():
    return plsc.VectorSubcoreMesh(core_axis_name="c", subcore_axis_name="s")


def _sc_scatter2(src, idx0, idx1, n_rows):
    n = src.shape[0]

    @functools.partial(pl.kernel, out_type=jax.ShapeDtypeStruct((n_rows, ROW_W), src.dtype), mesh=_sc_mesh())
    def scatter(x_hbm, i0_hbm, i1_hbm, o_hbm):
        def body(x_vmem, i0_vmem, i1_vmem):
            pltpu.sync_copy(x_vmem, o_hbm.at[i0_vmem.at[0]])
            pltpu.sync_copy(x_vmem, o_hbm.at[i1_vmem.at[0]])

        pltpu.emit_pipeline(
            body, grid=(n // SC_WIN,),
            in_specs=[pl.BlockSpec((SC_WIN, ROW_W), index_map=lambda i: (i, 0)),
                      pl.BlockSpec((1, SC_WIN), index_map=lambda i: (0, i)),
                      pl.BlockSpec((1, SC_WIN), index_map=lambda i: (0, i))],
            out_specs=[],
            core_axis_name=("c", "s"), dimension_semantics=(pltpu.PARALLEL,),
        )(x_hbm, i0_hbm, i1_hbm)

    return scatter(src, idx0, idx1)


def _sc_gather(src, idx):
    m = idx.shape[1]

    @functools.partial(pl.kernel, out_type=jax.ShapeDtypeStruct((m, ROW_W), src.dtype), mesh=_sc_mesh())
    def gather(x_hbm, i_hbm, o_hbm):
        def body(i_vmem, o_vmem):
            pltpu.sync_copy(x_hbm.at[i_vmem.at[0]], o_vmem)

        pltpu.emit_pipeline(
            body, grid=(m // SC_WIN,),
            in_specs=[pl.BlockSpec((1, SC_WIN), index_map=lambda i: (0, i))],
            out_specs=[pl.BlockSpec((SC_WIN, ROW_W), index_map=lambda i: (i, 0))],
            core_axis_name=("c", "s"), dimension_semantics=(pltpu.PARALLEL,),
        )(i_hbm, o_hbm)

    return gather(src, idx)


def _experts_kernel(l, n_blk, start_ref, nchunk_ref, xpa_ref, xpb_ref, w1_ref, w3_ref, w2_ref, ypa_ref, ypb_ref,
                    w13_scr, w2_scr, wbuf13, wbuf2, xbuf, ybuf, wsem, xsem, ysem):
    e = pl.program_id(0)
    nc = nchunk_ref[e]
    chunk0 = start_ref[e]
    n_used = start_ref[N_EXP]

    def w_copies(ex, slot):
        cps = []
        for p in range(W_PIECES):
            r13 = pl.ds(p * (D // W_PIECES), D // W_PIECES)
            r2 = pl.ds(p * (D_FF // W_PIECES), D_FF // W_PIECES)
            cps.append(pltpu.make_async_copy(w1_ref.at[l, ex, r13], wbuf13.at[slot, 0, r13], wsem.at[slot]))
            cps.append(pltpu.make_async_copy(w3_ref.at[l, ex, r13], wbuf13.at[slot, 1, r13], wsem.at[slot]))
            cps.append(pltpu.make_async_copy(w2_ref.at[l, ex, r2], wbuf2.at[slot, r2], wsem.at[slot]))
        return cps

    def x_copies(g):
        rows = pl.ds(pl.multiple_of(g * BLK, BLK), BLK)
        slot = g % (X_AHEAD + 1)
        return [pltpu.make_async_copy(src.at[rows], xbuf.at[slot, h], xsem.at[slot])
                for h, src in enumerate((xpa_ref, xpb_ref))]

    def y_copies(g):
        rows = pl.ds(pl.multiple_of(g * BLK, BLK), BLK)
        slot = g % 2
        return [pltpu.make_async_copy(ybuf.at[slot, h], dst.at[rows], ysem.at[slot])
                for h, dst in enumerate((ypa_ref, ypb_ref))]

    def start(cps):
        for cp in cps:
            cp.start()

    def wait(cps):
        for cp in cps:
            cp.wait()

    @pl.when(e == 0)
    def _():
        for j in range(X_AHEAD):
            @pl.when(j < n_used)
            def _():
                start(x_copies(j))

        start(w_copies(0, 0))

    @pl.when(e < N_EXP)
    def _():
        wslot = e % 2
        wait(w_copies(e, wslot))

        @pl.when(e + 1 < N_EXP)
        def _():
            start(w_copies(e + 1, 1 - wslot))

        @pl.when(nc > 0)
        def _():
            w13_scr[:, 0:D_FF] = wbuf13[wslot, 0].astype(BF16)
            w13_scr[:, D_FF:2 * D_FF] = wbuf13[wslot, 1].astype(BF16)
            w2_scr[...] = wbuf2[wslot].astype(BF16)

            def chunk(c, carry):
                g = chunk0 + c
                slot = g % (X_AHEAD + 1)

                @pl.when(g + X_AHEAD < n_used)
                def _():
                    start(x_copies(g + X_AHEAD))

                wait(x_copies(g))

                @pl.when(g >= 2)
                def _():
                    wait(y_copies(g - 2))

                a, b = _unpack_bf16_pair(jnp.concatenate([xbuf[slot, 0], xbuf[slot, 1]], axis=1))
                x = jnp.concatenate([a.astype(BF16), b.astype(BF16)], axis=1)
                h13 = jnp.dot(x, w13_scr[...], preferred_element_type=F32)
                h1 = h13[:, 0:D_FF]
                act = (h1 * jax.nn.sigmoid(h1) * h13[:, D_FF:2 * D_FF]).astype(BF16)
                y = jnp.dot(act, w2_scr[...], preferred_element_type=F32)
                yp = _pack_bf16_pair(y[:, 0:D // 2], y[:, D // 2:D])
                ybuf[g % 2, 0] = yp[:, 0:ROW_W]
                ybuf[g % 2, 1] = yp[:, ROW_W:2 * ROW_W]
                start(y_copies(g))
                return carry

            lax.fori_loop(0, nc, chunk, 0)

    @pl.when(e == N_EXP)
    def _():
        @pl.when(n_used >= 2)
        def _():
            wait(y_copies(n_used - 2))

        wait(y_copies(n_used - 1))
        ybuf[0] = jnp.zeros((2, BLK, ROW_W), U32)

        def fill_one(g, carry):
            rows = pl.ds(pl.multiple_of(g * BLK, BLK), BLK)
            cps = [pltpu.make_async_copy(ybuf.at[0, h], dst.at[rows], ysem.at[0])
                   for h, dst in enumerate((ypa_ref, ypb_ref))]
            start(cps)
            wait(cps)
            return carry

        lax.fori_loop(n_used, n_blk, fill_one, 0)


def _experts(chunk_start, n_chunk, xpa, xpb, w1, w3, w2, l):
    n_rows = xpa.shape[0]
    n_blk = n_rows // BLK
    hbm = pl.BlockSpec(memory_space=pl.ANY)
    half = jax.ShapeDtypeStruct((n_rows, ROW_W), U32)
    return pl.pallas_call(
        functools.partial(_experts_kernel, l, n_blk),
        grid_spec=pltpu.PrefetchScalarGridSpec(
            num_scalar_prefetch=2,
            grid=(N_EXP + 1,),
            in_specs=[hbm, hbm, hbm, hbm, hbm],
            out_specs=[hbm, hbm],
            scratch_shapes=[pltpu.VMEM((D, 2 * D_FF), BF16), pltpu.VMEM((D_FF, D), BF16),
                            pltpu.VMEM((2, 2, D, D_FF), F32), pltpu.VMEM((2, D_FF, D), F32),
                            pltpu.VMEM((X_AHEAD + 1, 2, BLK, ROW_W), U32), pltpu.VMEM((2, 2, BLK, ROW_W), U32),
                            pltpu.SemaphoreType.DMA((2,)), pltpu.SemaphoreType.DMA((X_AHEAD + 1,)),
                            pltpu.SemaphoreType.DMA((2,))],
        ),
        out_shape=[half, half],
        compiler_params=_cparams(("arbitrary",)),
        name="experts",
    )(chunk_start, n_chunk, xpa, xpb, w1, w3, w2)


def _combine_kernel(final, ga0_ref, gb0_ref, ga1_ref, gb1_ref, x1_ref, info_ref, mod_ref, fg_ref, o_ref):
    info = info_ref[...]
    gw0 = info[:, 0:1]
    gw1 = info[:, 1:2]
    a0, b0 = _unpack_bf16_pair(jnp.concatenate([ga0_ref[...], gb0_ref[...]], axis=1))
    a1, b1 = _unpack_bf16_pair(jnp.concatenate([ga1_ref[...], gb1_ref[...]], axis=1))
    y = jnp.concatenate([gw0 * a0 + gw1 * a1, gw0 * b0 + gw1 * b1], axis=1)
    gate2 = mod_ref[0, 0][5:6, :]
    x2 = x1_ref[...] + gate2 * y
    if final:
        ms = jnp.mean(x2 * x2, axis=-1, keepdims=True)
        x2 = x2 * lax.rsqrt(ms + EPS) * fg_ref[...]
    o_ref[...] = x2


def _combine(ga, gb, x1, info, mod, l, final_g, bsz, final):
    n_tok = x1.shape[0]
    n_i = n_tok // T_CMB
    per_b = n_i // bsz
    first = pl.BlockSpec((T_CMB, ROW_W), lambda i: (i, 0))
    second = pl.BlockSpec((T_CMB, ROW_W), lambda i: (i + n_i, 0))
    return pl.pallas_call(
        functools.partial(_combine_kernel, final),
        grid=(n_i,),
        in_specs=[
            first, first, second, second,
            pl.BlockSpec((T_CMB, D), lambda i: (i, 0)),
            pl.BlockSpec((T_CMB, CHUNK), lambda i: (i, 0)),
            pl.BlockSpec((1, 1, 8, D), lambda i: (l, i // per_b, 0, 0)),
            pl.BlockSpec((1, D), lambda i: (0, 0)),
        ],
        out_specs=pl.BlockSpec((T_CMB, D), lambda i: (i, 0)),
        out_shape=jax.ShapeDtypeStruct((n_tok, D), F32),
        compiler_params=_cparams(("arbitrary",)),
        name="combine",
    )(ga, gb, ga, gb, x1, info, mod, final_g)


def _dft_table_kernel(t1_ref, t2_ref, o_ref):
    seq = t2_ref.shape[2]
    c1 = t1_ref[0, 0:1, :]
    s1 = t1_ref[0, 1:2, :]
    c2 = t2_ref[0]
    s2 = t2_ref[1]
    o_ref[:, 0:seq] = (c1 * c2 - s1 * s2).astype(BF16)
    o_ref[:, seq:2 * seq] = (-(s1 * c2 + c1 * s2)).astype(BF16)


def _dft_tables(seq):
    scale = 1.0 / (seq * HEAD) ** 0.5
    n_hi = seq // TAB_ROWS
    n = lax.broadcasted_iota(I32, (1, seq), 1)
    kh = lax.broadcasted_iota(I32, (n_hi, 1), 0)
    a1 = ((kh * n) % n_hi).astype(F32) * (2.0 * jnp.pi / n_hi)
    t1 = jnp.stack([jnp.cos(a1), jnp.sin(a1)], axis=1)
    kl = lax.broadcasted_iota(I32, (TAB_ROWS, 1), 0)
    a2 = ((kl * n) % seq).astype(F32) * (2.0 * jnp.pi / seq)
    t2 = jnp.stack([jnp.cos(a2) * scale, jnp.sin(a2) * scale], axis=0)
    cs = pl.pallas_call(
        _dft_table_kernel,
        grid=(n_hi,),
        in_specs=[pl.BlockSpec((1, 2, seq), lambda i: (i, 0, 0)),
                  pl.BlockSpec((2, TAB_ROWS, seq), lambda i: (0, 0, 0))],
        out_specs=pl.BlockSpec((TAB_ROWS, 2 * seq), lambda i: (i, 0)),
        out_shape=jax.ShapeDtypeStruct((seq, 2 * seq), BF16),
        compiler_params=_cparams(("arbitrary",)),
        name="dft_table",
    )(t1, t2)
    d = lax.broadcasted_iota(I32, (FOUR_W, FOUR_W), 0)
    q = lax.broadcasted_iota(I32, (FOUR_W, FOUR_W), 1)
    same = (d // HEAD) == (q // HEAD)
    ang64 = ((d * q) % HEAD).astype(F32) * (2.0 * jnp.pi / HEAD)
    dft64 = jnp.concatenate([jnp.where(same, jnp.cos(ang64), 0.0),
                             jnp.where(same, jnp.sin(ang64), 0.0)], axis=1).astype(BF16)
    return cs, dft64


def _routing_tables(route, counts_f):
    counts = counts_f[:, 0].astype(I32)
    pc = (counts + BLK - 1) // BLK * BLK
    pends = jnp.cumsum(pc)
    pstarts = pends - pc
    eid = lax.broadcasted_iota(I32, (N_EXP, 1), 0)

    def dest_of(e, r):
        return jnp.sum(jnp.where(e[None, :] == eid, pstarts[:, None], 0), axis=0) + r

    dest = jnp.concatenate([dest_of(route[0], route[2]), dest_of(route[1], route[3])])
    chunk_start = jnp.concatenate([pstarts, pends[-1:]]) // BLK
    n_chunk = jnp.concatenate([pc // BLK, jnp.zeros((1,), I32)])
    return dest, chunk_start, n_chunk


def kernel(x, c, ln1_g, ln2_g, w_ada, b_ada, w_in, w_out, conv_w, conv_b, conv_gn_g, conv_gn_b, sgu_ln_g,
           sgu_ln_b, sgu_w, sgu_b, router_w, router_b, exp_w1, exp_w3, exp_w2, final_g):
    bsz, seq, _ = x.shape
    depth = w_in.shape[0]
    n_tok = bsz * seq
    n_blk = (2 * n_tok + N_EXP * (BLK - 1) + BLK - 1) // BLK
    r3 = lambda a: a.reshape(depth, 1, a.shape[-1])

    mod = jnp.pad(_ada_mod(c, w_ada, b_ada).reshape(depth, bsz, 6, D), ((0, 0), (0, 0), (0, 2), (0, 0)))
    cs, dft64 = _dft_tables(seq)
    hd = lax.broadcasted_iota(I32, (CONV_W, CONV_W), 0) // HEAD
    gavg = jnp.where(hd == hd.T, 1.0 / HEAD, 0.0).astype(BF16)
    w_in_bf = w_in.astype(BF16)
    w_out_bf = w_out.astype(BF16)
    sgu_w_bf = sgu_w.astype(BF16)
    sgu_bias = jnp.repeat(jnp.swapaxes(sgu_b, 1, 2), HEAD, axis=2)
    conv_w_p = jnp.pad(conv_w, ((0, 0), (0, 1), (0, 0)))
    rwt_hi = router_w.T.astype(BF16)
    rwt_lo = (router_w.T - rwt_hi.astype(F32)).astype(BF16)
    router_wt = jnp.concatenate([rwt_hi, rwt_lo], axis=0)
    router_bc = router_b.reshape(N_EXP, 1)
    fg = final_g.reshape(1, D)

    for l in range(depth):
        yac, pq = _mixer_in(x, mod, l, r3(ln1_g), w_in_bf, conv_w_p, r3(conv_b), r3(conv_gn_g), r3(conv_gn_b),
                            r3(sgu_ln_g), r3(sgu_ln_b), sgu_w_bf, sgu_bias, dft64, gavg)
        yb = _seq_dft(cs, pq.reshape(bsz, 2 * seq, FOUR_W))
        x1, h2pa, h2pb, info, route, counts = _mixer_out(
            yac.reshape(n_tok, CONV_W + SGU_W), yb.reshape(n_tok, FOUR_W), x.reshape(n_tok, D),
            mod, l, r3(ln2_g), w_out_bf, router_wt, router_bc, bsz)
        dest, chunk_start, n_chunk = _routing_tables(route, counts)
        d0 = dest[:n_tok].reshape(1, n_tok)
        d1 = dest[n_tok:].reshape(1, n_tok)
        xpa = _sc_scatter2(h2pa, d0, d1, n_blk * BLK)
        xpb = _sc_scatter2(h2pb, d0, d1, n_blk * BLK)
        ypa, ypb = _experts(chunk_start, n_chunk, xpa, xpb, exp_w1, exp_w3, exp_w2, l)
        dall = dest.reshape(1, 2 * n_tok)
        x = _combine(_sc_gather(ypa, dall), _sc_gather(ypb, dall), x1, info, mod, l, fg, bsz,
                     l == depth - 1).reshape(bsz, seq, D)
    return x
```

```python
import functools

import jax
import jax.numpy as jnp
from jax import lax
from jax.experimental import pallas as pl
from jax.experimental.pallas import tpu as pltpu
from jax.experimental.pallas import tpu_sc as plsc

F32 = jnp.float32
BF16 = jnp.bfloat16
I32 = jnp.int32
U32 = jnp.uint32
HIGHEST = lax.Precision.HIGHEST

D = 1024
HEAD = 64
CONV_W = 384
FOUR_W = 256
SGU_W = 384
SGU_HEADS = SGU_W // HEAD
Z_COLS = 2 * CONV_W + FOUR_W + 2 * SGU_W
KSIZE = 31
HALO = 16
CHUNK = 128
N_EXP = 64
N_GRP = 8
EPG = N_EXP // N_GRP
D_FF = D // 2
EPS = 1e-6

T_MIX = 512
T_DFT = 512
T_CMB = 512
ROW_W = 256
SC_WIN = 128
BLK = 512
X_AHEAD = 3
CONV_ROWS = 64
TAB_ROWS = 64
W_PIECES = 4
VMEM_LIMIT = 56 * 1024 * 1024


def _cparams(sem):
    return pltpu.CompilerParams(dimension_semantics=sem, vmem_limit_bytes=VMEM_LIMIT)


def _pack_bf16_pair(a, b):
    ua = lax.bitcast_convert_type(a.astype(BF16).astype(F32), U32) >> 16
    ub = lax.bitcast_convert_type(b.astype(BF16).astype(F32), U32) & jnp.uint32(0xFFFF0000)
    return ua | ub


def _unpack_bf16_pair(p):
    a = lax.bitcast_convert_type(p << 16, F32)
    b = lax.bitcast_convert_type(p & jnp.uint32(0xFFFF0000), F32)
    return a, b


def _ada_kernel(c_ref, w_ref, b_ref, o_ref):
    c = c_ref[...]
    ca = c * jax.nn.sigmoid(c)
    o_ref[0] = jnp.dot(ca, w_ref[0], precision=HIGHEST, preferred_element_type=F32) + b_ref[0]


def _ada_mod(c, w_ada, b_ada):
    depth, _, ncol = w_ada.shape
    bsz = c.shape[0]
    tn = 1536
    return pl.pallas_call(
        _ada_kernel,
        grid=(depth, ncol // tn),
        in_specs=[pl.BlockSpec((bsz, D), lambda l, j: (0, 0)),
                  pl.BlockSpec((1, D, tn), lambda l, j: (l, 0, j)),
                  pl.BlockSpec((1, 1, tn), lambda l, j: (l, 0, j))],
        out_specs=pl.BlockSpec((1, bsz, tn), lambda l, j: (l, 0, j)),
        out_shape=jax.ShapeDtypeStruct((depth, bsz, ncol), F32),
        compiler_params=_cparams(("arbitrary", "arbitrary")),
        name="ada_mod",
    )(c, w_ada, b_ada.reshape(depth, 1, ncol))


def _mixer_in_kernel(xm_ref, xp_ref, xn_ref, mod_ref, g1_ref, win_ref, cw_ref, cb_ref, gng_ref, gnb_ref,
                     lng_ref, lnb_ref, sw_ref, sb_ref, dft_ref, gavg_ref,
                     yac_ref, pq_ref, glu_scr, sh_scr, conv_scr):
    i = pl.program_id(1)
    n_i = pl.num_programs(1)
    T = T_MIX
    mod = mod_ref[0, 0]
    shift1 = mod[0:1, :]
    gain1 = g1_ref[0] * (1.0 + mod[1:2, :])

    def norm_mod(x):
        ms = jnp.mean(x * x, axis=-1, keepdims=True)
        return x * lax.rsqrt(ms + EPS) * gain1 + shift1

    h = norm_mod(xm_ref[0]).astype(BF16)
    z = jnp.dot(h, win_ref[0], preferred_element_type=F32)

    hh = norm_mod(jnp.concatenate([xp_ref[0], xn_ref[0]], axis=0)).astype(BF16)
    zh = jnp.dot(hh, win_ref[0, :, 0:2 * CONV_W], preferred_element_type=F32)
    glu_h = zh[:, 0:CONV_W] * jax.nn.sigmoid(zh[:, CONV_W:2 * CONV_W])
    glu_scr[0:HALO, :] = jnp.where(i > 0, glu_h[0:HALO], 0.0)
    glu_scr[HALO + T:2 * HALO + T, :] = jnp.where(i < n_i - 1, glu_h[HALO:2 * HALO], 0.0)
    glu_scr[HALO:HALO + T, :] = z[:, 0:CONV_W] * jax.nn.sigmoid(z[:, CONV_W:2 * CONV_W])

    off = HALO - KSIZE // 2
    for b in range(8):
        sh_scr[b] = glu_scr[b:b + T + 3 * 8, :]

    for c in range(T // CONV_ROWS):
        r0 = c * CONV_ROWS
        acc = jnp.broadcast_to(cb_ref[0], (CONV_ROWS, CONV_W))
        for k in range(KSIZE):
            a, b = divmod(k + off, 8)
            acc = acc + sh_scr[b, r0 + 8 * a:r0 + 8 * a + CONV_ROWS, :] * cw_ref[0, k:k + 1, :]
        conv_scr[r0:r0 + CONV_ROWS, :] = acc
    hc = conv_scr[...]
    gavg = gavg_ref[...]
    mu = jnp.dot(hc.astype(BF16), gavg, preferred_element_type=F32)
    dc = hc - mu
    var = jnp.dot((dc * dc).astype(BF16), gavg, preferred_element_type=F32)
    hn = dc * lax.rsqrt(var + EPS) * gng_ref[0] + gnb_ref[0]
    ya = hn * jax.nn.sigmoid(hn)
    yac_ref[0, :, 0:CONV_W] = ya.astype(BF16)

    zb = z[:, 2 * CONV_W:2 * CONV_W + FOUR_W].astype(BF16)
    pq = jnp.dot(zb, dft_ref[...], preferred_element_type=F32)
    pq_ref[0, 0] = pq[:, 0:FOUR_W].astype(BF16)
    pq_ref[0, 1] = pq[:, FOUR_W:2 * FOUR_W].astype(BF16)

    c0 = 2 * CONV_W + FOUR_W
    zc = z[:, c0:c0 + 2 * SGU_W]
    zc = 0.5 * zc * (1.0 + jnp.tanh(0.7978845608028654 * (zc + 0.044715 * (zc * zc * zc))))
    u = zc[:, 0:SGU_W]
    v = zc[:, SGU_W:2 * SGU_W]
    vm = jnp.mean(v, axis=-1, keepdims=True)
    vd = v - vm
    vv = jnp.mean(vd * vd, axis=-1, keepdims=True)
    vn = (vd * lax.rsqrt(vv + EPS) * lng_ref[0] + lnb_ref[0]).astype(BF16)
    n_chunk = T // CHUNK
    lane = lax.broadcasted_iota(I32, (CHUNK, 2 * HEAD), 1)
    for pr in range(SGU_HEADS // 2):
        cols = slice(2 * HEAD * pr, 2 * HEAD * (pr + 1))
        rhs = jnp.concatenate([vn[n * CHUNK:(n + 1) * CHUNK, cols] for n in range(n_chunk)], axis=1)
        lo = jnp.dot(sw_ref[0, 2 * pr], rhs, preferred_element_type=F32)
        hi = jnp.dot(sw_ref[0, 2 * pr + 1], rhs, preferred_element_type=F32)
        for n in range(n_chunk):
            sl = slice(n * 2 * HEAD, (n + 1) * 2 * HEAD)
            vs = jnp.where(lane < HEAD, lo[:, sl], hi[:, sl]) + sb_ref[0, :, cols]
            rows = slice(n * CHUNK, (n + 1) * CHUNK)
            yac_ref[0, rows, CONV_W + 2 * HEAD * pr:CONV_W + 2 * HEAD * (pr + 1)] = (u[rows, cols] * vs).astype(BF16)


def _mixer_in(x, mod, l, ln1_g, w_in_bf, conv_w, conv_b, gn_g, gn_b, ln_g, ln_b, sgu_w_bf, sgu_bias, dft64, gavg):
    bsz, seq, _ = x.shape
    T = T_MIX
    n_i = seq // T
    hb = T // HALO
    n_h = seq // HALO
    vec = lambda w: pl.BlockSpec((1, 1, w), lambda b, i: (l, 0, 0))
    return pl.pallas_call(
        _mixer_in_kernel,
        grid=(bsz, n_i),
        in_specs=[
            pl.BlockSpec((1, T, D), lambda b, i: (b, i, 0)),
            pl.BlockSpec((1, HALO, D), lambda b, i: (b, jnp.maximum(i * hb - 1, 0), 0)),
            pl.BlockSpec((1, HALO, D), lambda b, i: (b, jnp.minimum((i + 1) * hb, n_h - 1), 0)),
            pl.BlockSpec((1, 1, 8, D), lambda b, i: (l, b, 0, 0)),
            vec(D),
            pl.BlockSpec((1, D, Z_COLS), lambda b, i: (l, 0, 0)),
            pl.BlockSpec((1, KSIZE + 1, CONV_W), lambda b, i: (l, 0, 0)),
            vec(CONV_W), vec(CONV_W), vec(CONV_W), vec(SGU_W), vec(SGU_W),
            pl.BlockSpec((1, SGU_HEADS, CHUNK, CHUNK), lambda b, i: (l, 0, 0, 0)),
            pl.BlockSpec((1, CHUNK, SGU_W), lambda b, i: (l, 0, 0)),
            pl.BlockSpec((FOUR_W, 2 * FOUR_W), lambda b, i: (0, 0)),
            pl.BlockSpec((CONV_W, CONV_W), lambda b, i: (0, 0)),
        ],
        out_specs=[
            pl.BlockSpec((1, T, CONV_W + SGU_W), lambda b, i: (b, i, 0)),
            pl.BlockSpec((1, 2, T, FOUR_W), lambda b, i: (b, 0, i, 0)),
        ],
        out_shape=[
            jax.ShapeDtypeStruct((bsz, seq, CONV_W + SGU_W), BF16),
            jax.ShapeDtypeStruct((bsz, 2, seq, FOUR_W), BF16),
        ],
        scratch_shapes=[
            pltpu.VMEM((T + 2 * HALO, CONV_W), F32),
            pltpu.VMEM((8, T + 3 * 8, CONV_W), F32),
            pltpu.VMEM((T, CONV_W), F32),
        ],
        compiler_params=_cparams(("arbitrary", "arbitrary")),
        name="mixer_in",
    )(x, x, x, mod, ln1_g, w_in_bf, conv_w, conv_b, gn_g, gn_b, ln_g, ln_b, sgu_w_bf, sgu_bias, dft64, gavg)


def _seq_dft_kernel(scale, cs_ref, pq_ref, o_ref, fold_scr, ph_scr):
    k = pl.program_id(1)
    seq = pq_ref.shape[1] // 2
    half = seq // 2
    nb = seq // CHUNK

    @pl.when(k == 0)
    def _():
        rr = lax.broadcasted_iota(I32, (CHUNK, CHUNK), 0)
        cc = lax.broadcasted_iota(I32, (CHUNK, CHUNK), 1)
        rev = jnp.where((rr >= 1) & (cc == CHUNK - rr), 1.0, 0.0).astype(BF16)
        row0 = lax.broadcasted_iota(I32, (CHUNK, FOUR_W), 0) == 0
        for part, sign in ((0, 1.0), (1, -1.0)):
            base = part * seq
            for m in range(half // CHUNK):
                lo = pq_ref[0, base + CHUNK * m:base + CHUNK * (m + 1), :].astype(F32)
                up = pq_ref[0, base + CHUNK * (nb - 1 - m):base + CHUNK * (nb - m), :]
                mirrored = jnp.dot(rev, up, preferred_element_type=F32)
                if m >= 1:
                    first = pq_ref[0, base + CHUNK * (nb - m):base + CHUNK * (nb - m) + 1, :].astype(F32)
                    mirrored = jnp.where(row0, first, mirrored)
                fold_scr[part * half + CHUNK * m:part * half + CHUNK * (m + 1), :] = (lo + sign * mirrored).astype(BF16)
        ph_scr[...] = pq_ref[0, half:half + 1, :].astype(F32) * scale

    alt = jnp.where(lax.broadcasted_iota(I32, (T_DFT, 1), 0) % 2 == 0, 1.0, -1.0)
    o = jnp.dot(cs_ref[...], fold_scr[...], preferred_element_type=F32) + alt * ph_scr[...]
    o_ref[0] = o.astype(BF16)


def _seq_dft(cs, pq):
    bsz, two_s, _ = pq.shape
    seq = two_s // 2
    scale = 1.0 / (seq * HEAD) ** 0.5
    return pl.pallas_call(
        functools.partial(_seq_dft_kernel, scale),
        grid=(bsz, seq // T_DFT),
        in_specs=[pl.BlockSpec((T_DFT, seq), lambda b, k: (k, 0)),
                  pl.BlockSpec((1, two_s, FOUR_W), lambda b, k: (b, 0, 0))],
        out_specs=pl.BlockSpec((1, T_DFT, FOUR_W), lambda b, k: (b, k, 0)),
        out_shape=jax.ShapeDtypeStruct((bsz, seq, FOUR_W), BF16),
        scratch_shapes=[pltpu.VMEM((seq, FOUR_W), BF16), pltpu.VMEM((1, FOUR_W), F32)],
        compiler_params=_cparams(("arbitrary", "arbitrary")),
        name="seq_dft",
    )(cs, pq)


def _mixer_out_kernel(yac_ref, yb_ref, x_ref, mod_ref, g2_ref, wout_ref, rwt_ref, rb_ref,
                      x1_ref, h2pa_ref, h2pb_ref, info_ref, route_ref, cnt_ref, cnt_scr):
    i = pl.program_id(0)
    T = T_MIX

    @pl.when(i == 0)
    def _():
        cnt_scr[...] = jnp.zeros_like(cnt_scr)

    mod = mod_ref[0, 0]
    gate1 = mod[2:3, :]
    shift2 = mod[3:4, :]
    gain2 = g2_ref[0] * (1.0 + mod[4:5, :])
    yac = yac_ref[...]
    ycat = jnp.concatenate([yac[:, 0:CONV_W], yb_ref[...], yac[:, CONV_W:CONV_W + SGU_W]], axis=1)
    o = jnp.dot(ycat, wout_ref[0], preferred_element_type=F32)
    x1 = x_ref[...] + gate1 * o
    x1_ref[...] = x1
    ms = jnp.mean(x1 * x1, axis=-1, keepdims=True)
    h2 = x1 * lax.rsqrt(ms + EPS) * gain2 + shift2
    h2p = _pack_bf16_pair(h2[:, 0:D // 2], h2[:, D // 2:D])
    h2pa_ref[...] = h2p[:, 0:ROW_W]
    h2pb_ref[...] = h2p[:, ROW_W:2 * ROW_W]

    h_hi = h2.astype(BF16)
    h_lo = (h2 - h_hi.astype(F32)).astype(BF16)
    nt = (((1,), (1,)), ((), ()))
    part = lax.dot_general(rwt_ref[...], h_hi, nt, preferred_element_type=F32)
    logits = (part[0:N_EXP] + part[N_EXP:2 * N_EXP]
              + lax.dot_general(rwt_ref[0:N_EXP, :], h_lo, nt, preferred_element_type=F32))
    mx = jnp.max(logits, axis=0, keepdims=True)
    ex = jnp.exp(logits - mx)
    probs = ex / jnp.sum(ex, axis=0, keepdims=True)
    sel = probs + rb_ref[...]
    sel3 = sel.reshape(N_GRP, EPG, T)
    probs3 = probs.reshape(N_GRP, EPG, T)
    jj = lax.broadcasted_iota(I32, (N_GRP, EPG, T), 1)
    m1 = jnp.max(sel3, axis=1, keepdims=True)
    i1 = jnp.min(jnp.where(sel3 == m1, jj, EPG), axis=1, keepdims=True)
    rest = jnp.where(jj == i1, -jnp.inf, sel3)
    m2 = jnp.max(rest, axis=1, keepdims=True)
    i2 = jnp.min(jnp.where(rest == m2, jj, EPG), axis=1, keepdims=True)
    gscore = m1 + m2
    gg = lax.broadcasted_iota(I32, (N_GRP, 1, T), 0)
    gmax = jnp.max(gscore, axis=0, keepdims=True)
    gidx = jnp.min(jnp.where(gscore == gmax, gg, N_GRP), axis=0, keepdims=True)
    ing = gg == gidx
    pick = lambda a, zero: jnp.sum(jnp.where(ing, a, zero), axis=0)
    p1 = jnp.sum(jnp.where(jj == i1, probs3, 0.0), axis=1, keepdims=True)
    p2 = jnp.sum(jnp.where(jj == i2, probs3, 0.0), axis=1, keepdims=True)
    pa = pick(p1, 0.0)
    pb = pick(p2, 0.0)
    gbase = gidx[0] * EPG
    e0 = gbase + pick(i1, 0)
    e1 = gbase + pick(i2, 0)
    den = pa + pb
    gw0 = pa / den
    gw1 = pb / den

    ee = lax.broadcasted_iota(I32, (N_EXP, T), 0)
    oh0 = ee == e0
    oh1 = ee == e1
    amat = jnp.where(oh0 | oh1, 1.0, 0.0)
    rr = lax.broadcasted_iota(I32, (T, T), 0)
    cc = lax.broadcasted_iota(I32, (T, T), 1)
    upper = jnp.where(rr < cc, 1.0, 0.0).astype(BF16)
    before = jnp.dot(amat.astype(BF16), upper, preferred_element_type=F32) + cnt_scr[...]
    r0 = jnp.sum(jnp.where(oh0, before, 0.0), axis=0, keepdims=True)
    r1 = jnp.sum(jnp.where(oh1, before, 0.0), axis=0, keepdims=True)
    cnt_scr[...] = cnt_scr[...] + jnp.sum(amat, axis=1, keepdims=True)
    cnt_ref[...] = cnt_scr[...]

    rid = lax.broadcasted_iota(I32, (8, T), 0)
    route = jnp.zeros((8, T), I32)
    for k, val in enumerate((e0, e1, r0.astype(I32), r1.astype(I32))):
        route = jnp.where(rid == k, val, route)
    route_ref[...] = route
    rows = jnp.where(rid == 0, gw0, jnp.where(rid == 1, gw1, 0.0))
    rows = jnp.concatenate([rows, jnp.zeros((CHUNK - 8, T), F32)], axis=0)
    info_ref[...] = rows.T


def _mixer_out(yac, yb, x, mod, l, ln2_g, w_out_bf, router_wt, router_b, bsz):
    n_tok = x.shape[0]
    T = T_MIX
    per_b = n_tok // bsz // T
    row = lambda w: pl.BlockSpec((T, w), lambda i: (i, 0))
    return pl.pallas_call(
        _mixer_out_kernel,
        grid=(n_tok // T,),
        in_specs=[
            row(CONV_W + SGU_W), row(FOUR_W), row(D),
            pl.BlockSpec((1, 1, 8, D), lambda i: (l, i // per_b, 0, 0)),
            pl.BlockSpec((1, 1, D), lambda i: (l, 0, 0)),
            pl.BlockSpec((1, D, D), lambda i: (l, 0, 0)),
            pl.BlockSpec((2 * N_EXP, D), lambda i: (0, 0)),
            pl.BlockSpec((N_EXP, 1), lambda i: (0, 0)),
        ],
        out_specs=[row(D), row(ROW_W), row(ROW_W), row(CHUNK), pl.BlockSpec((8, T), lambda i: (0, i)),
                   pl.BlockSpec((N_EXP, 1), lambda i: (0, 0))],
        out_shape=[
            jax.ShapeDtypeStruct((n_tok, D), F32),
            jax.ShapeDtypeStruct((n_tok, ROW_W), U32),
            jax.ShapeDtypeStruct((n_tok, ROW_W), U32),
            jax.ShapeDtypeStruct((n_tok, CHUNK), F32),
            jax.ShapeDtypeStruct((8, n_tok), I32),
            jax.ShapeDtypeStruct((N_EXP, 1), F32),
        ],
        scratch_shapes=[pltpu.VMEM((N_EXP, 1), F32)],
        compiler_params=_cparams(("arbitrary",)),
        name="mixer_out",
    )(yac, yb, x, mod, ln2_g, w_out_bf, router_wt, router_b)


def _sc_mesh():
    return plsc.VectorSubcoreMesh(core_axis_name="c", subcore_axis_name="s")


def _sc_scatter2(src_a, src_b, idx0, idx1, n_rows):
    n = src_a.shape[0]
    out = jax.ShapeDtypeStruct((n_rows, ROW_W), src_a.dtype)

    @functools.partial(pl.kernel, out_type=[out, out], mesh=_sc_mesh())
    def scatter(xa_hbm, xb_hbm, i0_hbm, i1_hbm, oa_hbm, ob_hbm):
        for x_hbm, o_hbm in ((xa_hbm, oa_hbm), (xb_hbm, ob_hbm)):
            def body(x_vmem, i0_vmem, i1_vmem, o_hbm=o_hbm):
                pltpu.sync_copy(x_vmem, o_hbm.at[i0_vmem.at[0]])
                pltpu.sync_copy(x_vmem, o_hbm.at[i1_vmem.at[0]])

            pltpu.emit_pipeline(
                body, grid=(n // SC_WIN,),
                in_specs=[pl.BlockSpec((SC_WIN, ROW_W), index_map=lambda i: (i, 0)),
                          pl.BlockSpec((1, SC_WIN), index_map=lambda i: (0, i)),
                          pl.BlockSpec((1, SC_WIN), index_map=lambda i: (0, i))],
                out_specs=[],
                core_axis_name=("c", "s"), dimension_semantics=(pltpu.PARALLEL,),
            )(x_hbm, i0_hbm, i1_hbm)

    return scatter(src_a, src_b, idx0, idx1)


def _sc_gather(src_a, src_b, idx):
    m = idx.shape[1]
    out = jax.ShapeDtypeStruct((m, ROW_W), src_a.dtype)

    @functools.partial(pl.kernel, out_type=[out, out], mesh=_sc_mesh())
    def gather(xa_hbm, xb_hbm, i_hbm, oa_hbm, ob_hbm):
        for x_hbm, o_hbm in ((xa_hbm, oa_hbm), (xb_hbm, ob_hbm)):
            def body(i_vmem, o_vmem, x_hbm=x_hbm):
                pltpu.sync_copy(x_hbm.at[i_vmem.at[0]], o_vmem)

            pltpu.emit_pipeline(
                body, grid=(m // SC_WIN,),
                in_specs=[pl.BlockSpec((1, SC_WIN), index_map=lambda i: (0, i))],
                out_specs=[pl.BlockSpec((SC_WIN, ROW_W), index_map=lambda i: (i, 0))],
                core_axis_name=("c", "s"), dimension_semantics=(pltpu.PARALLEL,),
            )(i_hbm, o_hbm)

    return gather(src_a, src_b, idx)


def _experts_kernel(l, n_blk, start_ref, nchunk_ref, xpa_ref, xpb_ref, w1_ref, w3_ref, w2_ref, ypa_ref, ypb_ref,
                    w13_scr, w2_scr, wbuf13, wbuf2, xbuf, ybuf, wsem, xsem, ysem):
    e = pl.program_id(0)
    nc = nchunk_ref[e]
    chunk0 = start_ref[e]
    n_used = start_ref[N_EXP]

    def w_copies(ex, slot):
        cps = []
        for p in range(W_PIECES):
            r13 = pl.ds(p * (D // W_PIECES), D // W_PIECES)
            r2 = pl.ds(p * (D_FF // W_PIECES), D_FF // W_PIECES)
            cps.append(pltpu.make_async_copy(w1_ref.at[l, ex, r13], wbuf13.at[slot, 0, r13], wsem.at[slot]))
            cps.append(pltpu.make_async_copy(w3_ref.at[l, ex, r13], wbuf13.at[slot, 1, r13], wsem.at[slot]))
            cps.append(pltpu.make_async_copy(w2_ref.at[l, ex, r2], wbuf2.at[slot, r2], wsem.at[slot]))
        return cps

    def x_copies(g):
        rows = pl.ds(pl.multiple_of(g * BLK, BLK), BLK)
        slot = g % (X_AHEAD + 1)
        return [pltpu.make_async_copy(src.at[rows], xbuf.at[slot, h], xsem.at[slot])
                for h, src in enumerate((xpa_ref, xpb_ref))]

    def y_copies(g):
        rows = pl.ds(pl.multiple_of(g * BLK, BLK), BLK)
        slot = g % 2
        return [pltpu.make_async_copy(ybuf.at[slot, h], dst.at[rows], ysem.at[slot])
                for h, dst in enumerate((ypa_ref, ypb_ref))]

    def start(cps):
        for cp in cps:
            cp.start()

    def wait(cps):
        for cp in cps:
            cp.wait()

    @pl.when(e == 0)
    def _():
        for j in range(X_AHEAD):
            @pl.when(j < n_used)
            def _():
                start(x_copies(j))

        start(w_copies(0, 0))

    @pl.when(e < N_EXP)
    def _():
        wslot = e % 2
        wait(w_copies(e, wslot))

        @pl.when(e + 1 < N_EXP)
        def _():
            start(w_copies(e + 1, 1 - wslot))

        @pl.when(nc > 0)
        def _():
            w13_scr[:, 0:D_FF] = wbuf13[wslot, 0].astype(BF16)
            w13_scr[:, D_FF:2 * D_FF] = wbuf13[wslot, 1].astype(BF16)
            w2_scr[...] = wbuf2[wslot].astype(BF16)

            def chunk(c, carry):
                g = chunk0 + c
                slot = g % (X_AHEAD + 1)

                @pl.when(g + X_AHEAD < n_used)
                def _():
                    start(x_copies(g + X_AHEAD))

                wait(x_copies(g))

                @pl.when(g >= 2)
                def _():
                    wait(y_copies(g - 2))

                a, b = _unpack_bf16_pair(jnp.concatenate([xbuf[slot, 0], xbuf[slot, 1]], axis=1))
                x = jnp.concatenate([a.astype(BF16), b.astype(BF16)], axis=1)
                h13 = jnp.dot(x, w13_scr[...], preferred_element_type=F32)
                h1 = h13[:, 0:D_FF]
                act = (h1 * jax.nn.sigmoid(h1) * h13[:, D_FF:2 * D_FF]).astype(BF16)
                y = jnp.dot(act, w2_scr[...], preferred_element_type=F32)
                yp = _pack_bf16_pair(y[:, 0:D // 2], y[:, D // 2:D])
                ybuf[g % 2, 0] = yp[:, 0:ROW_W]
                ybuf[g % 2, 1] = yp[:, ROW_W:2 * ROW_W]
                start(y_copies(g))
                return carry

            lax.fori_loop(0, nc, chunk, 0)

    @pl.when(e == N_EXP)
    def _():
        @pl.when(n_used >= 2)
        def _():
            wait(y_copies(n_used - 2))

        wait(y_copies(n_used - 1))
        ybuf[0] = jnp.zeros((2, BLK, ROW_W), U32)

        def fill_one(g, carry):
            rows = pl.ds(pl.multiple_of(g * BLK, BLK), BLK)
            cps = [pltpu.make_async_copy(ybuf.at[0, h], dst.at[rows], ysem.at[0])
                   for h, dst in enumerate((ypa_ref, ypb_ref))]
            start(cps)
            wait(cps)
            return carry

        lax.fori_loop(n_used, n_blk, fill_one, 0)


def _experts(chunk_start, n_chunk, xpa, xpb, w1, w3, w2, l):
    n_rows = xpa.shape[0]
    n_blk = n_rows // BLK
    hbm = pl.BlockSpec(memory_space=pl.ANY)
    half = jax.ShapeDtypeStruct((n_rows, ROW_W), U32)
    return pl.pallas_call(
        functools.partial(_experts_kernel, l, n_blk),
        grid_spec=pltpu.PrefetchScalarGridSpec(
            num_scalar_prefetch=2,
            grid=(N_EXP + 1,),
            in_specs=[hbm, hbm, hbm, hbm, hbm],
            out_specs=[hbm, hbm],
            scratch_shapes=[pltpu.VMEM((D, 2 * D_FF), BF16), pltpu.VMEM((D_FF, D), BF16),
                            pltpu.VMEM((2, 2, D, D_FF), F32), pltpu.VMEM((2, D_FF, D), F32),
                            pltpu.VMEM((X_AHEAD + 1, 2, BLK, ROW_W), U32), pltpu.VMEM((2, 2, BLK, ROW_W), U32),
                            pltpu.SemaphoreType.DMA((2,)), pltpu.SemaphoreType.DMA((X_AHEAD + 1,)),
                            pltpu.SemaphoreType.DMA((2,))],
        ),
        out_shape=[half, half],
        compiler_params=_cparams(("arbitrary",)),
        name="experts",
    )(chunk_start, n_chunk, xpa, xpb, w1, w3, w2)


def _combine_kernel(final, ga0_ref, gb0_ref, ga1_ref, gb1_ref, x1_ref, info_ref, mod_ref, fg_ref, o_ref):
    info = info_ref[...]
    gw0 = info[:, 0:1]
    gw1 = info[:, 1:2]
    a0, b0 = _unpack_bf16_pair(jnp.concatenate([ga0_ref[...], gb0_ref[...]], axis=1))
    a1, b1 = _unpack_bf16_pair(jnp.concatenate([ga1_ref[...], gb1_ref[...]], axis=1))
    y = jnp.concatenate([gw0 * a0 + gw1 * a1, gw0 * b0 + gw1 * b1], axis=1)
    gate2 = mod_ref[0, 0][5:6, :]
    x2 = x1_ref[...] + gate2 * y
    if final:
        ms = jnp.mean(x2 * x2, axis=-1, keepdims=True)
        x2 = x2 * lax.rsqrt(ms + EPS) * fg_ref[...]
    o_ref[...] = x2


def _combine(ga, gb, x1, info, mod, l, final_g, bsz, final):
    n_tok = x1.shape[0]
    n_i = n_tok // T_CMB
    per_b = n_i // bsz
    first = pl.BlockSpec((T_CMB, ROW_W), lambda i: (i, 0))
    second = pl.BlockSpec((T_CMB, ROW_W), lambda i: (i + n_i, 0))
    return pl.pallas_call(
        functools.partial(_combine_kernel, final),
        grid=(n_i,),
        in_specs=[
            first, first, second, second,
            pl.BlockSpec((T_CMB, D), lambda i: (i, 0)),
            pl.BlockSpec((T_CMB, CHUNK), lambda i: (i, 0)),
            pl.BlockSpec((1, 1, 8, D), lambda i: (l, i // per_b, 0, 0)),
            pl.BlockSpec((1, D), lambda i: (0, 0)),
        ],
        out_specs=pl.BlockSpec((T_CMB, D), lambda i: (i, 0)),
        out_shape=jax.ShapeDtypeStruct((n_tok, D), F32),
        compiler_params=_cparams(("arbitrary",)),
        name="combine",
    )(ga, gb, ga, gb, x1, info, mod, final_g)


def _dft_table_kernel(t1_ref, t2_ref, o_ref):
    half = t2_ref.shape[2]
    c1 = t1_ref[0, 0:1, :]
    s1 = t1_ref[0, 1:2, :]
    c2 = t2_ref[0]
    s2 = t2_ref[1]
    o_ref[:, 0:half] = (c1 * c2 - s1 * s2).astype(BF16)
    o_ref[:, half:2 * half] = (-(s1 * c2 + c1 * s2)).astype(BF16)


def _dft_tables(seq):
    scale = 1.0 / (seq * HEAD) ** 0.5
    n_hi = seq // TAB_ROWS
    n = lax.broadcasted_iota(I32, (1, seq // 2), 1)
    kh = lax.broadcasted_iota(I32, (n_hi, 1), 0)
    a1 = ((kh * n) % n_hi).astype(F32) * (2.0 * jnp.pi / n_hi)
    t1 = jnp.stack([jnp.cos(a1), jnp.sin(a1)], axis=1)
    kl = lax.broadcasted_iota(I32, (TAB_ROWS, 1), 0)
    a2 = ((kl * n) % seq).astype(F32) * (2.0 * jnp.pi / seq)
    t2 = jnp.stack([jnp.cos(a2) * scale, jnp.sin(a2) * scale], axis=0)
    cs = pl.pallas_call(
        _dft_table_kernel,
        grid=(n_hi,),
        in_specs=[pl.BlockSpec((1, 2, seq // 2), lambda i: (i, 0, 0)),
                  pl.BlockSpec((2, TAB_ROWS, seq // 2), lambda i: (0, 0, 0))],
        out_specs=pl.BlockSpec((TAB_ROWS, seq), lambda i: (i, 0)),
        out_shape=jax.ShapeDtypeStruct((seq, seq), BF16),
        compiler_params=_cparams(("arbitrary",)),
        name="dft_table",
    )(t1, t2)
    d = lax.broadcasted_iota(I32, (FOUR_W, FOUR_W), 0)
    q = lax.broadcasted_iota(I32, (FOUR_W, FOUR_W), 1)
    same = (d // HEAD) == (q // HEAD)
    ang64 = ((d * q) % HEAD).astype(F32) * (2.0 * jnp.pi / HEAD)
    dft64 = jnp.concatenate([jnp.where(same, jnp.cos(ang64), 0.0),
                             jnp.where(same, jnp.sin(ang64), 0.0)], axis=1).astype(BF16)
    return cs, dft64


def _routing_tables(route, counts_f):
    counts = counts_f[:, 0].astype(I32)
    pc = (counts + BLK - 1) // BLK * BLK
    pends = jnp.cumsum(pc)
    pstarts = pends - pc
    eid = lax.broadcasted_iota(I32, (N_EXP, 1), 0)

    def dest_of(e, r):
        return jnp.sum(jnp.where(e[None, :] == eid, pstarts[:, None], 0), axis=0) + r

    dest = jnp.concatenate([dest_of(route[0], route[2]), dest_of(route[1], route[3])])
    chunk_start = jnp.concatenate([pstarts, pends[-1:]]) // BLK
    n_chunk = jnp.concatenate([pc // BLK, jnp.zeros((1,), I32)])
    return dest, chunk_start, n_chunk


def kernel(x, c, ln1_g, ln2_g, w_ada, b_ada, w_in, w_out, conv_w, conv_b, conv_gn_g, conv_gn_b, sgu_ln_g,
           sgu_ln_b, sgu_w, sgu_b, router_w, router_b, exp_w1, exp_w3, exp_w2, final_g):
    bsz, seq, _ = x.shape
    depth = w_in.shape[0]
    n_tok = bsz * seq
    n_blk = (2 * n_tok + N_EXP * (BLK - 1) + BLK - 1) // BLK
    r3 = lambda a: a.reshape(depth, 1, a.shape[-1])

    mod = jnp.pad(_ada_mod(c, w_ada, b_ada).reshape(depth, bsz, 6, D), ((0, 0), (0, 0), (0, 2), (0, 0)))
    cs, dft64 = _dft_tables(seq)
    hd = lax.broadcasted_iota(I32, (CONV_W, CONV_W), 0) // HEAD
    gavg = jnp.where(hd == hd.T, 1.0 / HEAD, 0.0).astype(BF16)
    w_in_bf = w_in.astype(BF16)
    w_out_bf = w_out.astype(BF16)
    sgu_w_bf = sgu_w.astype(BF16)
    sgu_bias = jnp.repeat(jnp.swapaxes(sgu_b, 1, 2), HEAD, axis=2)
    conv_w_p = jnp.pad(conv_w, ((0, 0), (0, 1), (0, 0)))
    rwt_hi = router_w.T.astype(BF16)
    rwt_lo = (router_w.T - rwt_hi.astype(F32)).astype(BF16)
    router_wt = jnp.concatenate([rwt_hi, rwt_lo], axis=0)
    router_bc = router_b.reshape(N_EXP, 1)
    fg = final_g.reshape(1, D)

    for l in range(depth):
        yac, pq = _mixer_in(x, mod, l, r3(ln1_g), w_in_bf, conv_w_p, r3(conv_b), r3(conv_gn_g), r3(conv_gn_b),
                            r3(sgu_ln_g), r3(sgu_ln_b), sgu_w_bf, sgu_bias, dft64, gavg)
        yb = _seq_dft(cs, pq.reshape(bsz, 2 * seq, FOUR_W))
        x1, h2pa, h2pb, info, route, counts = _mixer_out(
            yac.reshape(n_tok, CONV_W + SGU_W), yb.reshape(n_tok, FOUR_W), x.reshape(n_tok, D),
            mod, l, r3(ln2_g), w_out_bf, router_wt, router_bc, bsz)
        dest, chunk_start, n_chunk = _routing_tables(route, counts)
        d0 = dest[:n_tok].reshape(1, n_tok)
        d1 = dest[n_tok:].reshape(1, n_tok)
        xpa, xpb = _sc_scatter2(h2pa, h2pb, d0, d1, n_blk * BLK)
        ypa, ypb = _experts(chunk_start, n_chunk, xpa, xpb, exp_w1, exp_w3, exp_w2, l)
        ga, gb = _sc_gather(ypa, ypb, dest.reshape(1, 2 * n_tok))
        x = _combine(ga, gb, x1, info, mod, l, fg, bsz, l == depth - 1).reshape(bsz, seq, D)
    return x
```

```python
import functools

import jax
import jax.numpy as jnp
from jax import lax
from jax.experimental import pallas as pl
from jax.experimental.pallas import tpu as pltpu
from jax.experimental.pallas import tpu_sc as plsc

F32 = jnp.float32
BF16 = jnp.bfloat16
I32 = jnp.int32
U32 = jnp.uint32
HIGHEST = lax.Precision.HIGHEST

D = 1024
HEAD = 64
CONV_W = 384
FOUR_W = 256
SGU_W = 384
SGU_HEADS = SGU_W // HEAD
Z_COLS = 2 * CONV_W + FOUR_W + 2 * SGU_W
KSIZE = 31
HALO = 16
CHUNK = 128
N_EXP = 64
N_GRP = 8
EPG = N_EXP // N_GRP
D_FF = D // 2
EPS = 1e-6

T_MIX = 512
T_DFT = 512
CS_SPLIT = 4
T_CMB = 512
N_PIECES = 4
ROW_W = 256
SC_WIN = 128
BLK = 512
X_AHEAD = 3
CONV_ROWS = 64
TAB_ROWS = 64
W_PIECES = 4
VMEM_LIMIT = 56 * 1024 * 1024


def _cparams(sem):
    return pltpu.CompilerParams(dimension_semantics=sem, vmem_limit_bytes=VMEM_LIMIT)


def _pack_bf16_pair(a, b):
    ua = lax.bitcast_convert_type(a.astype(BF16).astype(F32), U32) >> 16
    ub = lax.bitcast_convert_type(b.astype(BF16).astype(F32), U32) & jnp.uint32(0xFFFF0000)
    return ua | ub


def _unpack_bf16_pair(p):
    a = lax.bitcast_convert_type(p << 16, F32)
    b = lax.bitcast_convert_type(p & jnp.uint32(0xFFFF0000), F32)
    return a, b


def _ada_kernel(c_ref, w_ref, b_ref, o_ref):
    c = c_ref[...]
    ca = c * jax.nn.sigmoid(c)
    o_ref[0] = jnp.dot(ca, w_ref[0], precision=HIGHEST, preferred_element_type=F32) + b_ref[0]


def _ada_mod(c, w_ada, b_ada):
    depth, _, ncol = w_ada.shape
    bsz = c.shape[0]
    tn = 1536
    return pl.pallas_call(
        _ada_kernel,
        grid=(depth, ncol // tn),
        in_specs=[pl.BlockSpec((bsz, D), lambda l, j: (0, 0)),
                  pl.BlockSpec((1, D, tn), lambda l, j: (l, 0, j)),
                  pl.BlockSpec((1, 1, tn), lambda l, j: (l, 0, j))],
        out_specs=pl.BlockSpec((1, bsz, tn), lambda l, j: (l, 0, j)),
        out_shape=jax.ShapeDtypeStruct((depth, bsz, ncol), F32),
        compiler_params=_cparams(("arbitrary", "arbitrary")),
        name="ada_mod",
    )(c, w_ada, b_ada.reshape(depth, 1, ncol))


def _mixer_in_kernel(xm_ref, xp_ref, xn_ref, mod_ref, g1_ref, win_ref, cw_ref, cb_ref, gng_ref, gnb_ref,
                     lng_ref, lnb_ref, sw_ref, sb_ref, dft_ref, gavg_ref,
                     yac_ref, pq_ref, glu_scr, sh_scr, conv_scr):
    i = pl.program_id(1)
    n_i = pl.num_programs(1)
    T = T_MIX
    mod = mod_ref[0, 0]
    shift1 = mod[0:1, :]
    gain1 = g1_ref[0] * (1.0 + mod[1:2, :])

    def norm_mod(x):
        ms = jnp.mean(x * x, axis=-1, keepdims=True)
        return x * lax.rsqrt(ms + EPS) * gain1 + shift1

    h = norm_mod(xm_ref[0]).astype(BF16)
    z = jnp.dot(h, win_ref[0], preferred_element_type=F32)

    hh = norm_mod(jnp.concatenate([xp_ref[0], xn_ref[0]], axis=0)).astype(BF16)
    zh = jnp.dot(hh, win_ref[0, :, 0:2 * CONV_W], preferred_element_type=F32)
    glu_h = zh[:, 0:CONV_W] * jax.nn.sigmoid(zh[:, CONV_W:2 * CONV_W])
    glu_scr[0:HALO, :] = jnp.where(i > 0, glu_h[0:HALO], 0.0)
    glu_scr[HALO + T:2 * HALO + T, :] = jnp.where(i < n_i - 1, glu_h[HALO:2 * HALO], 0.0)
    glu_scr[HALO:HALO + T, :] = z[:, 0:CONV_W] * jax.nn.sigmoid(z[:, CONV_W:2 * CONV_W])

    off = HALO - KSIZE // 2
    for b in range(8):
        sh_scr[b] = glu_scr[b:b + T + 3 * 8, :]

    for c in range(T // CONV_ROWS):
        r0 = c * CONV_ROWS
        acc = jnp.broadcast_to(cb_ref[0], (CONV_ROWS, CONV_W))
        for k in range(KSIZE):
            a, b = divmod(k + off, 8)
            acc = acc + sh_scr[b, r0 + 8 * a:r0 + 8 * a + CONV_ROWS, :] * cw_ref[0, k:k + 1, :]
        conv_scr[r0:r0 + CONV_ROWS, :] = acc
    hc = conv_scr[...]
    gavg = gavg_ref[...]
    mu = jnp.dot(hc.astype(BF16), gavg, preferred_element_type=F32)
    dc = hc - mu
    var = jnp.dot((dc * dc).astype(BF16), gavg, preferred_element_type=F32)
    hn = dc * lax.rsqrt(var + EPS) * gng_ref[0] + gnb_ref[0]
    ya = hn * jax.nn.sigmoid(hn)
    yac_ref[0, :, 0:CONV_W] = ya.astype(BF16)

    zb = z[:, 2 * CONV_W:2 * CONV_W + FOUR_W].astype(BF16)
    pq = jnp.dot(zb, dft_ref[...], preferred_element_type=F32)
    pq_ref[0, 0] = pq[:, 0:FOUR_W].astype(BF16)
    pq_ref[0, 1] = pq[:, FOUR_W:2 * FOUR_W].astype(BF16)

    c0 = 2 * CONV_W + FOUR_W
    zc = z[:, c0:c0 + 2 * SGU_W]
    zc = 0.5 * zc * (1.0 + jnp.tanh(0.7978845608028654 * (zc + 0.044715 * (zc * zc * zc))))
    u = zc[:, 0:SGU_W]
    v = zc[:, SGU_W:2 * SGU_W]
    vm = jnp.mean(v, axis=-1, keepdims=True)
    vd = v - vm
    vv = jnp.mean(vd * vd, axis=-1, keepdims=True)
    vn = (vd * lax.rsqrt(vv + EPS) * lng_ref[0] + lnb_ref[0]).astype(BF16)
    n_chunk = T // CHUNK
    lane = lax.broadcasted_iota(I32, (CHUNK, 2 * HEAD), 1)
    for pr in range(SGU_HEADS // 2):
        cols = slice(2 * HEAD * pr, 2 * HEAD * (pr + 1))
        rhs = jnp.concatenate([vn[n * CHUNK:(n + 1) * CHUNK, cols] for n in range(n_chunk)], axis=1)
        lo = jnp.dot(sw_ref[0, 2 * pr], rhs, preferred_element_type=F32)
        hi = jnp.dot(sw_ref[0, 2 * pr + 1], rhs, preferred_element_type=F32)
        for n in range(n_chunk):
            sl = slice(n * 2 * HEAD, (n + 1) * 2 * HEAD)
            vs = jnp.where(lane < HEAD, lo[:, sl], hi[:, sl]) + sb_ref[0, :, cols]
            rows = slice(n * CHUNK, (n + 1) * CHUNK)
            yac_ref[0, rows, CONV_W + 2 * HEAD * pr:CONV_W + 2 * HEAD * (pr + 1)] = (u[rows, cols] * vs).astype(BF16)


def _mixer_in(x, mod, l, ln1_g, w_in_bf, conv_w, conv_b, gn_g, gn_b, ln_g, ln_b, sgu_w_bf, sgu_bias, dft64, gavg):
    bsz, seq, _ = x.shape
    T = T_MIX
    n_i = seq // T
    hb = T // HALO
    n_h = seq // HALO
    vec = lambda w: pl.BlockSpec((1, 1, w), lambda b, i: (l, 0, 0))
    return pl.pallas_call(
        _mixer_in_kernel,
        grid=(bsz, n_i),
        in_specs=[
            pl.BlockSpec((1, T, D), lambda b, i: (b, i, 0)),
            pl.BlockSpec((1, HALO, D), lambda b, i: (b, jnp.maximum(i * hb - 1, 0), 0)),
            pl.BlockSpec((1, HALO, D), lambda b, i: (b, jnp.minimum((i + 1) * hb, n_h - 1), 0)),
            pl.BlockSpec((1, 1, 8, D), lambda b, i: (l, b, 0, 0)),
            vec(D),
            pl.BlockSpec((1, D, Z_COLS), lambda b, i: (l, 0, 0)),
            pl.BlockSpec((1, KSIZE + 1, CONV_W), lambda b, i: (l, 0, 0)),
            vec(CONV_W), vec(CONV_W), vec(CONV_W), vec(SGU_W), vec(SGU_W),
            pl.BlockSpec((1, SGU_HEADS, CHUNK, CHUNK), lambda b, i: (l, 0, 0, 0)),
            pl.BlockSpec((1, CHUNK, SGU_W), lambda b, i: (l, 0, 0)),
            pl.BlockSpec((FOUR_W, 2 * FOUR_W), lambda b, i: (0, 0)),
            pl.BlockSpec((CONV_W, CONV_W), lambda b, i: (0, 0)),
        ],
        out_specs=[
            pl.BlockSpec((1, T, CONV_W + SGU_W), lambda b, i: (b, i, 0)),
            pl.BlockSpec((1, 2, T, FOUR_W), lambda b, i: (b, 0, i, 0)),
        ],
        out_shape=[
            jax.ShapeDtypeStruct((bsz, seq, CONV_W + SGU_W), BF16),
            jax.ShapeDtypeStruct((bsz, 2, seq, FOUR_W), BF16),
        ],
        scratch_shapes=[
            pltpu.VMEM((T + 2 * HALO, CONV_W), F32),
            pltpu.VMEM((8, T + 3 * 8, CONV_W), F32),
            pltpu.VMEM((T, CONV_W), F32),
        ],
        compiler_params=_cparams(("arbitrary", "arbitrary")),
        name="mixer_in",
    )(x, x, x, mod, ln1_g, w_in_bf, conv_w, conv_b, gn_g, gn_b, ln_g, ln_b, sgu_w_bf, sgu_bias, dft64, gavg)


def _seq_dft_kernel(scale, *refs):
    cs_refs = refs[:CS_SPLIT]
    pq_ref, o_ref, fold_scr, ph_scr = refs[CS_SPLIT:]
    k = pl.program_id(1)
    seq = pq_ref.shape[1] // 2
    half = seq // 2
    nb = seq // CHUNK

    @pl.when(k == 0)
    def _():
        rr = lax.broadcasted_iota(I32, (CHUNK, CHUNK), 0)
        cc = lax.broadcasted_iota(I32, (CHUNK, CHUNK), 1)
        rev = jnp.where((rr >= 1) & (cc == CHUNK - rr), 1.0, 0.0).astype(BF16)
        row0 = lax.broadcasted_iota(I32, (CHUNK, FOUR_W), 0) == 0
        for part, sign in ((0, 1.0), (1, -1.0)):
            base = part * seq
            for m in range(half // CHUNK):
                lo = pq_ref[0, base + CHUNK * m:base + CHUNK * (m + 1), :].astype(F32)
                up = pq_ref[0, base + CHUNK * (nb - 1 - m):base + CHUNK * (nb - m), :]
                mirrored = jnp.dot(rev, up, preferred_element_type=F32)
                if m >= 1:
                    first = pq_ref[0, base + CHUNK * (nb - m):base + CHUNK * (nb - m) + 1, :].astype(F32)
                    mirrored = jnp.where(row0, first, mirrored)
                fold_scr[part * half + CHUNK * m:part * half + CHUNK * (m + 1), :] = (lo + sign * mirrored).astype(BF16)
        ph_scr[...] = pq_ref[0, half:half + 1, :].astype(F32) * scale

    alt = jnp.where(lax.broadcasted_iota(I32, (T_DFT, 1), 0) % 2 == 0, 1.0, -1.0)
    o = alt * ph_scr[...]
    slab = seq // CS_SPLIT
    for j, cs_ref in enumerate(cs_refs):
        o = o + jnp.dot(cs_ref[...], fold_scr[j * slab:(j + 1) * slab, :], preferred_element_type=F32)
    o_ref[0] = o.astype(BF16)


def _seq_dft(cs, pq):
    bsz, two_s, _ = pq.shape
    seq = two_s // 2
    scale = 1.0 / (seq * HEAD) ** 0.5
    return pl.pallas_call(
        functools.partial(_seq_dft_kernel, scale),
        grid=(bsz, seq // T_DFT),
        in_specs=[pl.BlockSpec((T_DFT, seq // CS_SPLIT), functools.partial(lambda j, b, k: (k, j), j))
                  for j in range(CS_SPLIT)]
        + [pl.BlockSpec((1, two_s, FOUR_W), lambda b, k: (b, 0, 0))],
        out_specs=pl.BlockSpec((1, T_DFT, FOUR_W), lambda b, k: (b, k, 0)),
        out_shape=jax.ShapeDtypeStruct((bsz, seq, FOUR_W), BF16),
        scratch_shapes=[pltpu.VMEM((seq, FOUR_W), BF16), pltpu.VMEM((1, FOUR_W), F32)],
        compiler_params=_cparams(("arbitrary", "arbitrary")),
        name="seq_dft",
    )(*([cs] * CS_SPLIT), pq)


def _mixer_out_kernel(yac_ref, yb_ref, x_ref, mod_ref, g2_ref, wout_ref, rwt_ref, rb_ref,
                      x1_ref, h2pa_ref, h2pb_ref, info_ref, route_ref, cnt_ref, cnt_scr):
    i = pl.program_id(0)
    T = T_MIX

    @pl.when(i == 0)
    def _():
        cnt_scr[...] = jnp.zeros_like(cnt_scr)

    mod = mod_ref[0, 0]
    gate1 = mod[2:3, :]
    shift2 = mod[3:4, :]
    gain2 = g2_ref[0] * (1.0 + mod[4:5, :])
    yac = yac_ref[...]
    ycat = jnp.concatenate([yac[:, 0:CONV_W], yb_ref[...], yac[:, CONV_W:CONV_W + SGU_W]], axis=1)
    o = jnp.dot(ycat, wout_ref[0], preferred_element_type=F32)
    x1 = x_ref[...] + gate1 * o
    x1_ref[...] = x1
    ms = jnp.mean(x1 * x1, axis=-1, keepdims=True)
    h2 = x1 * lax.rsqrt(ms + EPS) * gain2 + shift2
    h2p = _pack_bf16_pair(h2[:, 0:D // 2], h2[:, D // 2:D])
    h2pa_ref[...] = h2p[:, 0:ROW_W]
    h2pb_ref[...] = h2p[:, ROW_W:2 * ROW_W]

    h_hi = h2.astype(BF16)
    h_lo = (h2 - h_hi.astype(F32)).astype(BF16)
    nt = (((1,), (1,)), ((), ()))
    part = lax.dot_general(rwt_ref[...], h_hi, nt, preferred_element_type=F32)
    logits = (part[0:N_EXP] + part[N_EXP:2 * N_EXP]
              + lax.dot_general(rwt_ref[0:N_EXP, :], h_lo, nt, preferred_element_type=F32))
    mx = jnp.max(logits, axis=0, keepdims=True)
    ex = jnp.exp(logits - mx)
    probs = ex / jnp.sum(ex, axis=0, keepdims=True)
    sel = probs + rb_ref[...]
    sel3 = sel.reshape(N_GRP, EPG, T)
    probs3 = probs.reshape(N_GRP, EPG, T)
    jj = lax.broadcasted_iota(I32, (N_GRP, EPG, T), 1)
    m1 = jnp.max(sel3, axis=1, keepdims=True)
    i1 = jnp.min(jnp.where(sel3 == m1, jj, EPG), axis=1, keepdims=True)
    rest = jnp.where(jj == i1, -jnp.inf, sel3)
    m2 = jnp.max(rest, axis=1, keepdims=True)
    i2 = jnp.min(jnp.where(rest == m2, jj, EPG), axis=1, keepdims=True)
    gscore = m1 + m2
    gg = lax.broadcasted_iota(I32, (N_GRP, 1, T), 0)
    gmax = jnp.max(gscore, axis=0, keepdims=True)
    gidx = jnp.min(jnp.where(gscore == gmax, gg, N_GRP), axis=0, keepdims=True)
    ing = gg == gidx
    pick = lambda a, zero: jnp.sum(jnp.where(ing, a, zero), axis=0)
    p1 = jnp.sum(jnp.where(jj == i1, probs3, 0.0), axis=1, keepdims=True)
    p2 = jnp.sum(jnp.where(jj == i2, probs3, 0.0), axis=1, keepdims=True)
    pa = pick(p1, 0.0)
    pb = pick(p2, 0.0)
    gbase = gidx[0] * EPG
    e0 = gbase + pick(i1, 0)
    e1 = gbase + pick(i2, 0)
    den = pa + pb
    gw0 = pa / den
    gw1 = pb / den

    ee = lax.broadcasted_iota(I32, (N_EXP, T), 0)
    oh0 = ee == e0
    oh1 = ee == e1
    amat = jnp.where(oh0 | oh1, 1.0, 0.0)
    rr = lax.broadcasted_iota(I32, (T, T), 0)
    cc = lax.broadcasted_iota(I32, (T, T), 1)
    upper = jnp.where(rr < cc, 1.0, 0.0).astype(BF16)
    before = jnp.dot(amat.astype(BF16), upper, preferred_element_type=F32) + cnt_scr[...]
    r0 = jnp.sum(jnp.where(oh0, before, 0.0), axis=0, keepdims=True)
    r1 = jnp.sum(jnp.where(oh1, before, 0.0), axis=0, keepdims=True)
    cnt_scr[...] = cnt_scr[...] + jnp.sum(amat, axis=1, keepdims=True)
    cnt_ref[...] = cnt_scr[...]

    rid = lax.broadcasted_iota(I32, (8, T), 0)
    route = jnp.zeros((8, T), I32)
    for k, val in enumerate((e0, e1, r0.astype(I32), r1.astype(I32))):
        route = jnp.where(rid == k, val, route)
    route_ref[...] = route
    rows = jnp.where(rid == 0, gw0, jnp.where(rid == 1, gw1, 0.0))
    rows = jnp.concatenate([rows, jnp.zeros((CHUNK - 8, T), F32)], axis=0)
    info_ref[...] = rows.T


def _mixer_out(yac, yb, x, mod, l, ln2_g, w_out_bf, router_wt, router_b, bsz):
    n_tok = x.shape[0]
    T = T_MIX
    per_b = n_tok // bsz // T
    row = lambda w: pl.BlockSpec((T, w), lambda i: (i, 0))
    return pl.pallas_call(
        _mixer_out_kernel,
        grid=(n_tok // T,),
        in_specs=[
            row(CONV_W + SGU_W), row(FOUR_W), row(D),
            pl.BlockSpec((1, 1, 8, D), lambda i: (l, i // per_b, 0, 0)),
            pl.BlockSpec((1, 1, D), lambda i: (l, 0, 0)),
            pl.BlockSpec((1, D, D), lambda i: (l, 0, 0)),
            pl.BlockSpec((2 * N_EXP, D), lambda i: (0, 0)),
            pl.BlockSpec((N_EXP, 1), lambda i: (0, 0)),
        ],
        out_specs=[row(D), row(ROW_W), row(ROW_W), row(CHUNK), pl.BlockSpec((8, T), lambda i: (0, i)),
                   pl.BlockSpec((N_EXP, 1), lambda i: (0, 0))],
        out_shape=[
            jax.ShapeDtypeStruct((n_tok, D), F32),
            jax.ShapeDtypeStruct((n_tok, ROW_W), U32),
            jax.ShapeDtypeStruct((n_tok, ROW_W), U32),
            jax.ShapeDtypeStruct((n_tok, CHUNK), F32),
            jax.ShapeDtypeStruct((8, n_tok), I32),
            jax.ShapeDtypeStruct((N_EXP, 1), F32),
        ],
        scratch_shapes=[pltpu.VMEM((N_EXP, 1), F32)],
        compiler_params=_cparams(("arbitrary",)),
        name="mixer_out",
    )(yac, yb, x, mod, ln2_g, w_out_bf, router_wt, router_b)


def _sc_mesh():
    return plsc.VectorSubcoreMesh(core_axis_name="c", subcore_axis_name="s")


def _sc_scatter2(src_a, src_b, idx0, idx1, n_rows):
    n = src_a.shape[0]
    out = jax.ShapeDtypeStruct((n_rows, ROW_W), src_a.dtype)

    @functools.partial(pl.kernel, out_type=[out, out], mesh=_sc_mesh())
    def scatter(xa_hbm, xb_hbm, i0_hbm, i1_hbm, oa_hbm, ob_hbm):
        for x_hbm, o_hbm in ((xa_hbm, oa_hbm), (xb_hbm, ob_hbm)):
            def body(x_vmem, i0_vmem, i1_vmem, o_hbm=o_hbm):
                pltpu.sync_copy(x_vmem, o_hbm.at[i0_vmem.at[0]])
                pltpu.sync_copy(x_vmem, o_hbm.at[i1_vmem.at[0]])

            pltpu.emit_pipeline(
                body, grid=(n // SC_WIN,),
                in_specs=[pl.BlockSpec((SC_WIN, ROW_W), index_map=lambda i: (i, 0)),
                          pl.BlockSpec((1, SC_WIN), index_map=lambda i: (0, i)),
                          pl.BlockSpec((1, SC_WIN), index_map=lambda i: (0, i))],
                out_specs=[],
                core_axis_name=("c", "s"), dimension_semantics=(pltpu.PARALLEL,),
            )(x_hbm, i0_hbm, i1_hbm)

    return scatter(src_a, src_b, idx0, idx1)


def _sc_gather(src_a, src_b, idx):
    m = idx.shape[1]
    out = jax.ShapeDtypeStruct((m, ROW_W), src_a.dtype)

    @functools.partial(pl.kernel, out_type=[out, out], mesh=_sc_mesh())
    def gather(xa_hbm, xb_hbm, i_hbm, oa_hbm, ob_hbm):
        for x_hbm, o_hbm in ((xa_hbm, oa_hbm), (xb_hbm, ob_hbm)):
            def body(i_vmem, o_vmem, x_hbm=x_hbm):
                pltpu.sync_copy(x_hbm.at[i_vmem.at[0]], o_vmem)

            pltpu.emit_pipeline(
                body, grid=(m // SC_WIN,),
                in_specs=[pl.BlockSpec((1, SC_WIN), index_map=lambda i: (0, i))],
                out_specs=[pl.BlockSpec((SC_WIN, ROW_W), index_map=lambda i: (i, 0))],
                core_axis_name=("c", "s"), dimension_semantics=(pltpu.PARALLEL,),
            )(i_hbm, o_hbm)

    return gather(src_a, src_b, idx)


def _experts_kernel(l, n_blk, start_ref, nchunk_ref, xpa_ref, xpb_ref, w1_ref, w3_ref, w2_ref, ypa_ref, ypb_ref,
                    w13_scr, w2_scr, wbuf13, wbuf2, xbuf, ybuf, wsem, xsem, ysem):
    e = pl.program_id(0)
    nc = nchunk_ref[e]
    chunk0 = start_ref[e]
    n_used = start_ref[N_EXP]

    def w_copies(ex, slot):
        cps = []
        for p in range(W_PIECES):
            r13 = pl.ds(p * (D // W_PIECES), D // W_PIECES)
            r2 = pl.ds(p * (D_FF // W_PIECES), D_FF // W_PIECES)
            cps.append(pltpu.make_async_copy(w1_ref.at[l, ex, r13], wbuf13.at[slot, 0, r13], wsem.at[slot]))
            cps.append(pltpu.make_async_copy(w3_ref.at[l, ex, r13], wbuf13.at[slot, 1, r13], wsem.at[slot]))
            cps.append(pltpu.make_async_copy(w2_ref.at[l, ex, r2], wbuf2.at[slot, r2], wsem.at[slot]))
        return cps

    def x_copies(g):
        rows = pl.ds(pl.multiple_of(g * BLK, BLK), BLK)
        slot = g % (X_AHEAD + 1)
        return [pltpu.make_async_copy(src.at[rows], xbuf.at[slot, h], xsem.at[slot])
                for h, src in enumerate((xpa_ref, xpb_ref))]

    def y_copies(g):
        rows = pl.ds(pl.multiple_of(g * BLK, BLK), BLK)
        slot = g % 2
        return [pltpu.make_async_copy(ybuf.at[slot, h], dst.at[rows], ysem.at[slot])
                for h, dst in enumerate((ypa_ref, ypb_ref))]

    def start(cps):
        for cp in cps:
            cp.start()

    def wait(cps):
        for cp in cps:
            cp.wait()

    @pl.when(e == 0)
    def _():
        for j in range(X_AHEAD):
            @pl.when(j < n_used)
            def _():
                start(x_copies(j))

        start(w_copies(0, 0))

    @pl.when(e < N_EXP)
    def _():
        wslot = e % 2
        wait(w_copies(e, wslot))

        @pl.when(e + 1 < N_EXP)
        def _():
            start(w_copies(e + 1, 1 - wslot))

        @pl.when(nc > 0)
        def _():
            w13_scr[:, 0:D_FF] = wbuf13[wslot, 0].astype(BF16)
            w13_scr[:, D_FF:2 * D_FF] = wbuf13[wslot, 1].astype(BF16)
            w2_scr[...] = wbuf2[wslot].astype(BF16)

            def chunk(c, carry):
                g = chunk0 + c
                slot = g % (X_AHEAD + 1)

                @pl.when(g + X_AHEAD < n_used)
                def _():
                    start(x_copies(g + X_AHEAD))

                wait(x_copies(g))

                @pl.when(g >= 2)
                def _():
                    wait(y_copies(g - 2))

                a, b = _unpack_bf16_pair(jnp.concatenate([xbuf[slot, 0], xbuf[slot, 1]], axis=1))
                x = jnp.concatenate([a.astype(BF16), b.astype(BF16)], axis=1)
                h13 = jnp.dot(x, w13_scr[...], preferred_element_type=F32)
                h1 = h13[:, 0:D_FF]
                act = (h1 * jax.nn.sigmoid(h1) * h13[:, D_FF:2 * D_FF]).astype(BF16)
                y = jnp.dot(act, w2_scr[...], preferred_element_type=F32)
                yp = _pack_bf16_pair(y[:, 0:D // 2], y[:, D // 2:D])
                ybuf[g % 2, 0] = yp[:, 0:ROW_W]
                ybuf[g % 2, 1] = yp[:, ROW_W:2 * ROW_W]
                start(y_copies(g))
                return carry

            lax.fori_loop(0, nc, chunk, 0)

    @pl.when(e == N_EXP)
    def _():
        @pl.when(n_used >= 2)
        def _():
            wait(y_copies(n_used - 2))

        wait(y_copies(n_used - 1))
        ybuf[0] = jnp.zeros((2, BLK, ROW_W), U32)

        def fill_one(g, carry):
            rows = pl.ds(pl.multiple_of(g * BLK, BLK), BLK)
            cps = [pltpu.make_async_copy(ybuf.at[0, h], dst.at[rows], ysem.at[0])
                   for h, dst in enumerate((ypa_ref, ypb_ref))]
            start(cps)
            wait(cps)
            return carry

        lax.fori_loop(n_used, n_blk, fill_one, 0)


def _experts(chunk_start, n_chunk, xpa, xpb, w1, w3, w2, l):
    n_rows = xpa.shape[0]
    n_blk = n_rows // BLK
    hbm = pl.BlockSpec(memory_space=pl.ANY)
    half = jax.ShapeDtypeStruct((n_rows, ROW_W), U32)
    return pl.pallas_call(
        functools.partial(_experts_kernel, l, n_blk),
        grid_spec=pltpu.PrefetchScalarGridSpec(
            num_scalar_prefetch=2,
            grid=(N_EXP + 1,),
            in_specs=[hbm, hbm, hbm, hbm, hbm],
            out_specs=[hbm, hbm],
            scratch_shapes=[pltpu.VMEM((D, 2 * D_FF), BF16), pltpu.VMEM((D_FF, D), BF16),
                            pltpu.VMEM((2, 2, D, D_FF), F32), pltpu.VMEM((2, D_FF, D), F32),
                            pltpu.VMEM((X_AHEAD + 1, 2, BLK, ROW_W), U32), pltpu.VMEM((2, 2, BLK, ROW_W), U32),
                            pltpu.SemaphoreType.DMA((2,)), pltpu.SemaphoreType.DMA((X_AHEAD + 1,)),
                            pltpu.SemaphoreType.DMA((2,))],
        ),
        out_shape=[half, half],
        compiler_params=_cparams(("arbitrary",)),
        name="experts",
    )(chunk_start, n_chunk, xpa, xpb, w1, w3, w2)


def _combine_kernel(final, ga0_ref, gb0_ref, ga1_ref, gb1_ref, x1_ref, info_ref, mod_ref, fg_ref, *rest):
    o_ref = rest[-1]
    info = info_ref[...]
    gw0 = info[:, 0:1]
    gw1 = info[:, 1:2]
    a0, b0 = _unpack_bf16_pair(jnp.concatenate([ga0_ref[...], gb0_ref[...]], axis=1))
    a1, b1 = _unpack_bf16_pair(jnp.concatenate([ga1_ref[...], gb1_ref[...]], axis=1))
    y = jnp.concatenate([gw0 * a0 + gw1 * a1, gw0 * b0 + gw1 * b1], axis=1)
    gate2 = mod_ref[0, 0][5:6, :]
    x2 = x1_ref[...] + gate2 * y
    if final:
        ms = jnp.mean(x2 * x2, axis=-1, keepdims=True)
        x2 = x2 * lax.rsqrt(ms + EPS) * fg_ref[...]
    o_ref[...] = x2


def _combine(ga, gb, x1, info, mod, l, final_g, bsz, final, piece, prev):
    n_tok = x1.shape[0]
    n_i = n_tok // T_CMB // N_PIECES
    i0 = piece * n_i
    per_b = n_tok // T_CMB // bsz
    first = pl.BlockSpec((T_CMB, ROW_W), lambda i: (i, 0))
    second = pl.BlockSpec((T_CMB, ROW_W), lambda i: (i + n_i, 0))
    in_specs = [
        first, first, second, second,
        pl.BlockSpec((T_CMB, D), lambda i: (i + i0, 0)),
        pl.BlockSpec((T_CMB, CHUNK), lambda i: (i + i0, 0)),
        pl.BlockSpec((1, 1, 8, D), lambda i: (l, (i + i0) // per_b, 0, 0)),
        pl.BlockSpec((1, D), lambda i: (0, 0)),
    ]
    args = [ga, gb, ga, gb, x1, info, mod, final_g]
    aliases = {}
    if prev is not None:
        in_specs.append(pl.BlockSpec(memory_space=pl.ANY))
        args.append(prev)
        aliases = {len(args) - 1: 0}
    return pl.pallas_call(
        functools.partial(_combine_kernel, final),
        grid=(n_i,),
        in_specs=in_specs,
        out_specs=pl.BlockSpec((T_CMB, D), lambda i: (i + i0, 0)),
        out_shape=jax.ShapeDtypeStruct((n_tok, D), F32),
        input_output_aliases=aliases,
        compiler_params=_cparams(("arbitrary",)),
        name="combine",
    )(*args)


def _dft_table_kernel(t1_ref, t2_ref, o_ref):
    half = t2_ref.shape[2]
    c1 = t1_ref[0, 0:1, :]
    s1 = t1_ref[0, 1:2, :]
    c2 = t2_ref[0]
    s2 = t2_ref[1]
    o_ref[:, 0:half] = (c1 * c2 - s1 * s2).astype(BF16)
    o_ref[:, half:2 * half] = (-(s1 * c2 + c1 * s2)).astype(BF16)


def _dft_tables(seq):
    scale = 1.0 / (seq * HEAD) ** 0.5
    n_hi = seq // TAB_ROWS
    n = lax.broadcasted_iota(I32, (1, seq // 2), 1)
    kh = lax.broadcasted_iota(I32, (n_hi, 1), 0)
    a1 = ((kh * n) % n_hi).astype(F32) * (2.0 * jnp.pi / n_hi)
    t1 = jnp.stack([jnp.cos(a1), jnp.sin(a1)], axis=1)
    kl = lax.broadcasted_iota(I32, (TAB_ROWS, 1), 0)
    a2 = ((kl * n) % seq).astype(F32) * (2.0 * jnp.pi / seq)
    t2 = jnp.stack([jnp.cos(a2) * scale, jnp.sin(a2) * scale], axis=0)
    cs = pl.pallas_call(
        _dft_table_kernel,
        grid=(n_hi,),
        in_specs=[pl.BlockSpec((1, 2, seq // 2), lambda i: (i, 0, 0)),
                  pl.BlockSpec((2, TAB_ROWS, seq // 2), lambda i: (0, 0, 0))],
        out_specs=pl.BlockSpec((TAB_ROWS, seq), lambda i: (i, 0)),
        out_shape=jax.ShapeDtypeStruct((seq, seq), BF16),
        compiler_params=_cparams(("arbitrary",)),
        name="dft_table",
    )(t1, t2)
    d = lax.broadcasted_iota(I32, (FOUR_W, FOUR_W), 0)
    q = lax.broadcasted_iota(I32, (FOUR_W, FOUR_W), 1)
    same = (d // HEAD) == (q // HEAD)
    ang64 = ((d * q) % HEAD).astype(F32) * (2.0 * jnp.pi / HEAD)
    dft64 = jnp.concatenate([jnp.where(same, jnp.cos(ang64), 0.0),
                             jnp.where(same, jnp.sin(ang64), 0.0)], axis=1).astype(BF16)
    return cs, dft64


def _routing_tables(route, counts_f):
    counts = counts_f[:, 0].astype(I32)
    pc = (counts + BLK - 1) // BLK * BLK
    pends = jnp.cumsum(pc)
    pstarts = pends - pc
    eid = lax.broadcasted_iota(I32, (N_EXP, 1), 0)

    def dest_of(e, r):
        return jnp.sum(jnp.where(e[None, :] == eid, pstarts[:, None], 0), axis=0) + r

    dest = jnp.concatenate([dest_of(route[0], route[2]), dest_of(route[1], route[3])])
    chunk_start = jnp.concatenate([pstarts, pends[-1:]]) // BLK
    n_chunk = jnp.concatenate([pc // BLK, jnp.zeros((1,), I32)])
    return dest, chunk_start, n_chunk


def kernel(x, c, ln1_g, ln2_g, w_ada, b_ada, w_in, w_out, conv_w, conv_b, conv_gn_g, conv_gn_b, sgu_ln_g,
           sgu_ln_b, sgu_w, sgu_b, router_w, router_b, exp_w1, exp_w3, exp_w2, final_g):
    bsz, seq, _ = x.shape
    depth = w_in.shape[0]
    n_tok = bsz * seq
    n_blk = (2 * n_tok + N_EXP * (BLK - 1) + BLK - 1) // BLK
    r3 = lambda a: a.reshape(depth, 1, a.shape[-1])

    mod = jnp.pad(_ada_mod(c, w_ada, b_ada).reshape(depth, bsz, 6, D), ((0, 0), (0, 0), (0, 2), (0, 0)))
    cs, dft64 = _dft_tables(seq)
    hd = lax.broadcasted_iota(I32, (CONV_W, CONV_W), 0) // HEAD
    gavg = jnp.where(hd == hd.T, 1.0 / HEAD, 0.0).astype(BF16)
    w_in_bf = w_in.astype(BF16)
    w_out_bf = w_out.astype(BF16)
    sgu_w_bf = sgu_w.astype(BF16)
    sgu_bias = jnp.repeat(jnp.swapaxes(sgu_b, 1, 2), HEAD, axis=2)
    conv_w_p = jnp.pad(conv_w, ((0, 0), (0, 1), (0, 0)))
    rwt_hi = router_w.T.astype(BF16)
    rwt_lo = (router_w.T - rwt_hi.astype(F32)).astype(BF16)
    router_wt = jnp.concatenate([rwt_hi, rwt_lo], axis=0)
    router_bc = router_b.reshape(N_EXP, 1)
    fg = final_g.reshape(1, D)

    for l in range(depth):
        yac, pq = _mixer_in(x, mod, l, r3(ln1_g), w_in_bf, conv_w_p, r3(conv_b), r3(conv_gn_g), r3(conv_gn_b),
                            r3(sgu_ln_g), r3(sgu_ln_b), sgu_w_bf, sgu_bias, dft64, gavg)
        yb = _seq_dft(cs, pq.reshape(bsz, 2 * seq, FOUR_W))
        x1, h2pa, h2pb, info, route, counts = _mixer_out(
            yac.reshape(n_tok, CONV_W + SGU_W), yb.reshape(n_tok, FOUR_W), x.reshape(n_tok, D),
            mod, l, r3(ln2_g), w_out_bf, router_wt, router_bc, bsz)
        dest, chunk_start, n_chunk = _routing_tables(route, counts)
        d0 = dest[:n_tok].reshape(1, n_tok)
        d1 = dest[n_tok:].reshape(1, n_tok)
        xpa, xpb = _sc_scatter2(h2pa, h2pb, d0, d1, n_blk * BLK)
        ypa, ypb = _experts(chunk_start, n_chunk, xpa, xpb, exp_w1, exp_w3, exp_w2, l)
        x2 = None
        pt = n_tok // N_PIECES
        for p in range(N_PIECES):
            idx = jnp.concatenate([dest[p * pt:(p + 1) * pt], dest[n_tok + p * pt:n_tok + (p + 1) * pt]])
            ga, gb = _sc_gather(ypa, ypb, idx.reshape(1, 2 * pt))
            x2 = _combine(ga, gb, x1, info, mod, l, fg, bsz, l == depth - 1, p, x2)
        x = x2.reshape(bsz, seq, D)
    return x
```

```python
import functools

import jax
import jax.numpy as jnp
from jax import lax
from jax.experimental import pallas as pl
from jax.experimental.pallas import tpu as pltpu
from jax.experimental.pallas import tpu_sc as plsc

F32 = jnp.float32
BF16 = jnp.bfloat16
I32 = jnp.int32
U32 = jnp.uint32
HIGHEST = lax.Precision.HIGHEST

D = 1024
HEAD = 64
CONV_W = 384
FOUR_W = 256
SGU_W = 384
SGU_HEADS = SGU_W // HEAD
Z_COLS = 2 * CONV_W + FOUR_W + 2 * SGU_W
KSIZE = 31
HALO = 16
CHUNK = 128
N_EXP = 64
N_GRP = 8
EPG = N_EXP // N_GRP
D_FF = D // 2
EPS = 1e-6

T_MIX = 512
T_DFT = 512
T_CMB = 512
ROW_W = 256
SC_WIN = 128
BLK = 512
X_AHEAD = 3
CONV_ROWS = 64
TAB_ROWS = 64
TAB_GROUP = 4
W_PIECES = 4
VMEM_LIMIT = 56 * 1024 * 1024


def _cparams(sem):
    return pltpu.CompilerParams(dimension_semantics=sem, vmem_limit_bytes=VMEM_LIMIT)


def _pack_bf16_pair(a, b):
    ua = lax.bitcast_convert_type(a.astype(BF16).astype(F32), U32) >> 16
    ub = lax.bitcast_convert_type(b.astype(BF16).astype(F32), U32) & jnp.uint32(0xFFFF0000)
    return ua | ub


def _unpack_bf16_pair(p):
    a = lax.bitcast_convert_type(p << 16, F32)
    b = lax.bitcast_convert_type(p & jnp.uint32(0xFFFF0000), F32)
    return a, b


def _ada_kernel(c_ref, *refs):
    w_refs, b_ref, o_ref = refs[:-2], refs[-2], refs[-1]
    c = c_ref[...]
    ca = c * jax.nn.sigmoid(c)
    tn = w_refs[0].shape[2]
    for j, w_ref in enumerate(w_refs):
        cols = slice(j * tn, (j + 1) * tn)
        o_ref[0, :, cols] = jnp.dot(ca, w_ref[0], precision=HIGHEST, preferred_element_type=F32) + b_ref[0, :, cols]


def _ada_mod(c, w_ada, b_ada):
    depth, _, ncol = w_ada.shape
    bsz = c.shape[0]
    n_slab = 4
    n_half = 2
    tn = ncol // (n_slab * n_half)
    return pl.pallas_call(
        _ada_kernel,
        grid=(depth, n_half),
        in_specs=[pl.BlockSpec((bsz, D), lambda l, h: (0, 0))]
        + [pl.BlockSpec((1, D, tn), functools.partial(lambda j, l, h: (l, 0, n_slab * h + j), j))
           for j in range(n_slab)]
        + [pl.BlockSpec((1, 1, n_slab * tn), lambda l, h: (l, 0, h))],
        out_specs=pl.BlockSpec((1, bsz, n_slab * tn), lambda l, h: (l, 0, h)),
        out_shape=jax.ShapeDtypeStruct((depth, bsz, ncol), F32),
        compiler_params=_cparams(("arbitrary", "arbitrary")),
        name="ada_mod",
    )(c, *([w_ada] * n_slab), b_ada.reshape(depth, 1, ncol))


def _mixer_in_kernel(xm_ref, xp_ref, xn_ref, mod_ref, g1_ref, win_ref, cw_ref, cb_ref, gng_ref, gnb_ref,
                     lng_ref, lnb_ref, sw_ref, sb_ref, dft_ref, gavg_ref,
                     yac_ref, pq_ref, glu_scr, sh_scr, conv_scr):
    i = pl.program_id(1)
    n_i = pl.num_programs(1)
    T = T_MIX
    mod = mod_ref[0, 0]
    shift1 = mod[0:1, :]
    gain1 = g1_ref[0] * (1.0 + mod[1:2, :])

    def norm_mod(x):
        ms = jnp.mean(x * x, axis=-1, keepdims=True)
        return x * lax.rsqrt(ms + EPS) * gain1 + shift1

    h = norm_mod(xm_ref[0]).astype(BF16)
    z = jnp.dot(h, win_ref[0], preferred_element_type=F32)

    hh = norm_mod(jnp.concatenate([xp_ref[0], xn_ref[0]], axis=0)).astype(BF16)
    zh = jnp.dot(hh, win_ref[0, :, 0:2 * CONV_W], preferred_element_type=F32)
    glu_h = zh[:, 0:CONV_W] * jax.nn.sigmoid(zh[:, CONV_W:2 * CONV_W])
    glu_scr[0:HALO, :] = jnp.where(i > 0, glu_h[0:HALO], 0.0)
    glu_scr[HALO + T:2 * HALO + T, :] = jnp.where(i < n_i - 1, glu_h[HALO:2 * HALO], 0.0)
    glu_scr[HALO:HALO + T, :] = z[:, 0:CONV_W] * jax.nn.sigmoid(z[:, CONV_W:2 * CONV_W])

    off = HALO - KSIZE // 2
    for b in range(8):
        sh_scr[b] = glu_scr[b:b + T + 3 * 8, :]

    for c in range(T // CONV_ROWS):
        r0 = c * CONV_ROWS
        acc = jnp.broadcast_to(cb_ref[0], (CONV_ROWS, CONV_W))
        for k in range(KSIZE):
            a, b = divmod(k + off, 8)
            acc = acc + sh_scr[b, r0 + 8 * a:r0 + 8 * a + CONV_ROWS, :] * cw_ref[0, k:k + 1, :]
        conv_scr[r0:r0 + CONV_ROWS, :] = acc
    hc = conv_scr[...]
    gavg = gavg_ref[...]
    mu = jnp.dot(hc.astype(BF16), gavg, preferred_element_type=F32)
    dc = hc - mu
    var = jnp.dot((dc * dc).astype(BF16), gavg, preferred_element_type=F32)
    hn = dc * lax.rsqrt(var + EPS) * gng_ref[0] + gnb_ref[0]
    ya = hn * jax.nn.sigmoid(hn)
    yac_ref[0, :, 0:CONV_W] = ya.astype(BF16)

    zb = z[:, 2 * CONV_W:2 * CONV_W + FOUR_W].astype(BF16)
    pq = jnp.dot(zb, dft_ref[...], preferred_element_type=F32)
    pq_ref[0, 0] = pq[:, 0:FOUR_W].astype(BF16)
    pq_ref[0, 1] = pq[:, FOUR_W:2 * FOUR_W].astype(BF16)

    c0 = 2 * CONV_W + FOUR_W
    zc = z[:, c0:c0 + 2 * SGU_W]
    zc = 0.5 * zc * (1.0 + jnp.tanh(0.7978845608028654 * (zc + 0.044715 * (zc * zc * zc))))
    u = zc[:, 0:SGU_W]
    v = zc[:, SGU_W:2 * SGU_W]
    vm = jnp.mean(v, axis=-1, keepdims=True)
    vd = v - vm
    vv = jnp.mean(vd * vd, axis=-1, keepdims=True)
    vn = (vd * lax.rsqrt(vv + EPS) * lng_ref[0] + lnb_ref[0]).astype(BF16)
    n_chunk = T // CHUNK
    lane = lax.broadcasted_iota(I32, (CHUNK, 2 * HEAD), 1)
    for pr in range(SGU_HEADS // 2):
        cols = slice(2 * HEAD * pr, 2 * HEAD * (pr + 1))
        rhs = jnp.concatenate([vn[n * CHUNK:(n + 1) * CHUNK, cols] for n in range(n_chunk)], axis=1)
        lo = jnp.dot(sw_ref[0, 2 * pr], rhs, preferred_element_type=F32)
        hi = jnp.dot(sw_ref[0, 2 * pr + 1], rhs, preferred_element_type=F32)
        for n in range(n_chunk):
            sl = slice(n * 2 * HEAD, (n + 1) * 2 * HEAD)
            vs = jnp.where(lane < HEAD, lo[:, sl], hi[:, sl]) + sb_ref[0, :, cols]
            rows = slice(n * CHUNK, (n + 1) * CHUNK)
            yac_ref[0, rows, CONV_W + 2 * HEAD * pr:CONV_W + 2 * HEAD * (pr + 1)] = (u[rows, cols] * vs).astype(BF16)


def _mixer_in(x, mod, l, ln1_g, w_in_bf, conv_w, conv_b, gn_g, gn_b, ln_g, ln_b, sgu_w_bf, sgu_bias, dft64, gavg):
    bsz, seq, _ = x.shape
    T = T_MIX
    n_i = seq // T
    hb = T // HALO
    n_h = seq // HALO
    vec = lambda w: pl.BlockSpec((1, 1, w), lambda b, i: (l, 0, 0))
    return pl.pallas_call(
        _mixer_in_kernel,
        grid=(bsz, n_i),
        in_specs=[
            pl.BlockSpec((1, T, D), lambda b, i: (b, i, 0)),
            pl.BlockSpec((1, HALO, D), lambda b, i: (b, jnp.maximum(i * hb - 1, 0), 0)),
            pl.BlockSpec((1, HALO, D), lambda b, i: (b, jnp.minimum((i + 1) * hb, n_h - 1), 0)),
            pl.BlockSpec((1, 1, 8, D), lambda b, i: (l, b, 0, 0)),
            vec(D),
            pl.BlockSpec((1, D, Z_COLS), lambda b, i: (l, 0, 0)),
            pl.BlockSpec((1, KSIZE + 1, CONV_W), lambda b, i: (l, 0, 0)),
            vec(CONV_W), vec(CONV_W), vec(CONV_W), vec(SGU_W), vec(SGU_W),
            pl.BlockSpec((1, SGU_HEADS, CHUNK, CHUNK), lambda b, i: (l, 0, 0, 0)),
            pl.BlockSpec((1, CHUNK, SGU_W), lambda b, i: (l, 0, 0)),
            pl.BlockSpec((FOUR_W, 2 * FOUR_W), lambda b, i: (0, 0)),
            pl.BlockSpec((CONV_W, CONV_W), lambda b, i: (0, 0)),
        ],
        out_specs=[
            pl.BlockSpec((1, T, CONV_W + SGU_W), lambda b, i: (b, i, 0)),
            pl.BlockSpec((1, 2, T, FOUR_W), lambda b, i: (b, 0, i, 0)),
        ],
        out_shape=[
            jax.ShapeDtypeStruct((bsz, seq, CONV_W + SGU_W), BF16),
            jax.ShapeDtypeStruct((bsz, 2, seq, FOUR_W), BF16),
        ],
        scratch_shapes=[
            pltpu.VMEM((T + 2 * HALO, CONV_W), F32),
            pltpu.VMEM((8, T + 3 * 8, CONV_W), F32),
            pltpu.VMEM((T, CONV_W), F32),
        ],
        compiler_params=_cparams(("arbitrary", "arbitrary")),
        name="mixer_in",
    )(x, x, x, mod, ln1_g, w_in_bf, conv_w, conv_b, gn_g, gn_b, ln_g, ln_b, sgu_w_bf, sgu_bias, dft64, gavg)


def _dft_fold_kernel(scale, pq_ref, fold_ref, ph_ref):
    seq = pq_ref.shape[1] // 2
    half = seq // 2
    nb = seq // CHUNK
    rr = lax.broadcasted_iota(I32, (CHUNK, CHUNK), 0)
    cc = lax.broadcasted_iota(I32, (CHUNK, CHUNK), 1)
    rev = jnp.where((rr >= 1) & (cc == CHUNK - rr), 1.0, 0.0).astype(BF16)
    row0 = lax.broadcasted_iota(I32, (CHUNK, FOUR_W), 0) == 0
    for part, sign in ((0, 1.0), (1, -1.0)):
        base = part * seq
        for m in range(half // CHUNK):
            lo = pq_ref[0, base + CHUNK * m:base + CHUNK * (m + 1), :].astype(F32)
            up = pq_ref[0, base + CHUNK * (nb - 1 - m):base + CHUNK * (nb - m), :]
            mirrored = jnp.dot(rev, up, preferred_element_type=F32)
            if m >= 1:
                first = pq_ref[0, base + CHUNK * (nb - m):base + CHUNK * (nb - m) + 1, :].astype(F32)
                mirrored = jnp.where(row0, first, mirrored)
            fold_ref[0, part * half + CHUNK * m:part * half + CHUNK * (m + 1), :] = (lo + sign * mirrored).astype(BF16)
    ph_ref[0] = jnp.broadcast_to(pq_ref[0, half:half + 1, :].astype(F32) * scale, (8, FOUR_W))


def _seq_dft_kernel(cs_ref, fold_ref, ph_ref, o_ref):
    alt = jnp.where(lax.broadcasted_iota(I32, (T_DFT, 1), 0) % 2 == 0, 1.0, -1.0)
    o = jnp.dot(cs_ref[...], fold_ref[0], preferred_element_type=F32) + alt * ph_ref[0, 0:1, :]
    o_ref[0] = o.astype(BF16)


def _seq_dft(cs, pq):
    bsz, two_s, _ = pq.shape
    seq = two_s // 2
    scale = 1.0 / (seq * HEAD) ** 0.5
    fold, ph = pl.pallas_call(
        functools.partial(_dft_fold_kernel, scale),
        grid=(bsz,),
        in_specs=[pl.BlockSpec((1, two_s, FOUR_W), lambda b: (b, 0, 0))],
        out_specs=[pl.BlockSpec((1, seq, FOUR_W), lambda b: (b, 0, 0)),
                   pl.BlockSpec((1, 8, FOUR_W), lambda b: (b, 0, 0))],
        out_shape=[jax.ShapeDtypeStruct((bsz, seq, FOUR_W), BF16),
                   jax.ShapeDtypeStruct((bsz, 8, FOUR_W), F32)],
        compiler_params=_cparams(("arbitrary",)),
        name="dft_fold",
    )(pq)
    return pl.pallas_call(
        _seq_dft_kernel,
        grid=(seq // T_DFT, bsz),
        in_specs=[pl.BlockSpec((T_DFT, seq), lambda k, b: (k, 0)),
                  pl.BlockSpec((1, seq, FOUR_W), lambda k, b: (b, 0, 0)),
                  pl.BlockSpec((1, 8, FOUR_W), lambda k, b: (b, 0, 0))],
        out_specs=pl.BlockSpec((1, T_DFT, FOUR_W), lambda k, b: (b, k, 0)),
        out_shape=jax.ShapeDtypeStruct((bsz, seq, FOUR_W), BF16),
        compiler_params=_cparams(("arbitrary", "arbitrary")),
        name="seq_dft",
    )(cs, fold, ph)


def _mixer_out_kernel(yac_ref, yb_ref, x_ref, mod_ref, g2_ref, wout_ref, rwt_ref, rb_ref,
                      x1_ref, h2pa_ref, h2pb_ref, info_ref, route_ref, cnt_ref, cnt_scr):
    i = pl.program_id(0)
    T = T_MIX

    @pl.when(i == 0)
    def _():
        cnt_scr[...] = jnp.zeros_like(cnt_scr)

    mod = mod_ref[0, 0]
    gate1 = mod[2:3, :]
    shift2 = mod[3:4, :]
    gain2 = g2_ref[0] * (1.0 + mod[4:5, :])
    yac = yac_ref[...]
    ycat = jnp.concatenate([yac[:, 0:CONV_W], yb_ref[...], yac[:, CONV_W:CONV_W + SGU_W]], axis=1)
    o = jnp.dot(ycat, wout_ref[0], preferred_element_type=F32)
    x1 = x_ref[...] + gate1 * o
    x1_ref[...] = x1
    ms = jnp.mean(x1 * x1, axis=-1, keepdims=True)
    h2 = x1 * lax.rsqrt(ms + EPS) * gain2 + shift2
    h2p = _pack_bf16_pair(h2[:, 0:D // 2], h2[:, D // 2:D])
    h2pa_ref[...] = h2p[:, 0:ROW_W]
    h2pb_ref[...] = h2p[:, ROW_W:2 * ROW_W]

    h_hi = h2.astype(BF16)
    h_lo = (h2 - h_hi.astype(F32)).astype(BF16)
    nt = (((1,), (1,)), ((), ()))
    part = lax.dot_general(rwt_ref[...], h_hi, nt, preferred_element_type=F32)
    logits = (part[0:N_EXP] + part[N_EXP:2 * N_EXP]
              + lax.dot_general(rwt_ref[0:N_EXP, :], h_lo, nt, preferred_element_type=F32))
    mx = jnp.max(logits, axis=0, keepdims=True)
    ex = jnp.exp(logits - mx)
    probs = ex / jnp.sum(ex, axis=0, keepdims=True)
    sel = probs + rb_ref[...]
    sel3 = sel.reshape(N_GRP, EPG, T)
    probs3 = probs.reshape(N_GRP, EPG, T)
    jj = lax.broadcasted_iota(I32, (N_GRP, EPG, T), 1)
    m1 = jnp.max(sel3, axis=1, keepdims=True)
    i1 = jnp.min(jnp.where(sel3 == m1, jj, EPG), axis=1, keepdims=True)
    rest = jnp.where(jj == i1, -jnp.inf, sel3)
    m2 = jnp.max(rest, axis=1, keepdims=True)
    i2 = jnp.min(jnp.where(rest == m2, jj, EPG), axis=1, keepdims=True)
    gscore = m1 + m2
    gg = lax.broadcasted_iota(I32, (N_GRP, 1, T), 0)
    gmax = jnp.max(gscore, axis=0, keepdims=True)
    gidx = jnp.min(jnp.where(gscore == gmax, gg, N_GRP), axis=0, keepdims=True)
    ing = gg == gidx
    pick = lambda a, zero: jnp.sum(jnp.where(ing, a, zero), axis=0)
    p1 = jnp.sum(jnp.where(jj == i1, probs3, 0.0), axis=1, keepdims=True)
    p2 = jnp.sum(jnp.where(jj == i2, probs3, 0.0), axis=1, keepdims=True)
    pa = pick(p1, 0.0)
    pb = pick(p2, 0.0)
    gbase = gidx[0] * EPG
    e0 = gbase + pick(i1, 0)
    e1 = gbase + pick(i2, 0)
    den = pa + pb
    gw0 = pa / den
    gw1 = pb / den

    ee = lax.broadcasted_iota(I32, (N_EXP, T), 0)
    oh0 = ee == e0
    oh1 = ee == e1
    amat = jnp.where(oh0 | oh1, 1.0, 0.0)
    rr = lax.broadcasted_iota(I32, (T, T), 0)
    cc = lax.broadcasted_iota(I32, (T, T), 1)
    upper = jnp.where(rr < cc, 1.0, 0.0).astype(BF16)
    before = jnp.dot(amat.astype(BF16), upper, preferred_element_type=F32) + cnt_scr[...]
    r0 = jnp.sum(jnp.where(oh0, before, 0.0), axis=0, keepdims=True)
    r1 = jnp.sum(jnp.where(oh1, before, 0.0), axis=0, keepdims=True)
    cnt_scr[...] = cnt_scr[...] + jnp.sum(amat, axis=1, keepdims=True)
    cnt_ref[...] = cnt_scr[...]

    rid = lax.broadcasted_iota(I32, (8, T), 0)
    route = jnp.zeros((8, T), I32)
    for k, val in enumerate((e0, e1, r0.astype(I32), r1.astype(I32))):
        route = jnp.where(rid == k, val, route)
    route_ref[...] = route
    rows = jnp.where(rid == 0, gw0, jnp.where(rid == 1, gw1, 0.0))
    rows = jnp.concatenate([rows, jnp.zeros((CHUNK - 8, T), F32)], axis=0)
    info_ref[...] = rows.T


def _mixer_out(yac, yb, x, mod, l, ln2_g, w_out_bf, router_wt, router_b, bsz):
    n_tok = x.shape[0]
    T = T_MIX
    per_b = n_tok // bsz // T
    row = lambda w: pl.BlockSpec((T, w), lambda i: (i, 0))
    return pl.pallas_call(
        _mixer_out_kernel,
        grid=(n_tok // T,),
        in_specs=[
            row(CONV_W + SGU_W), row(FOUR_W), row(D),
            pl.BlockSpec((1, 1, 8, D), lambda i: (l, i // per_b, 0, 0)),
            pl.BlockSpec((1, 1, D), lambda i: (l, 0, 0)),
            pl.BlockSpec((1, D, D), lambda i: (l, 0, 0)),
            pl.BlockSpec((2 * N_EXP, D), lambda i: (0, 0)),
            pl.BlockSpec((N_EXP, 1), lambda i: (0, 0)),
        ],
        out_specs=[row(D), row(ROW_W), row(ROW_W), row(CHUNK), pl.BlockSpec((8, T), lambda i: (0, i)),
                   pl.BlockSpec((N_EXP, 1), lambda i: (0, 0))],
        out_shape=[
            jax.ShapeDtypeStruct((n_tok, D), F32),
            jax.ShapeDtypeStruct((n_tok, ROW_W), U32),
            jax.ShapeDtypeStruct((n_tok, ROW_W), U32),
            jax.ShapeDtypeStruct((n_tok, CHUNK), F32),
            jax.ShapeDtypeStruct((8, n_tok), I32),
            jax.ShapeDtypeStruct((N_EXP, 1), F32),
        ],
        scratch_shapes=[pltpu.VMEM((N_EXP, 1), F32)],
        compiler_params=_cparams(("arbitrary",)),
        name="mixer_out",
    )(yac, yb, x, mod, ln2_g, w_out_bf, router_wt, router_b)


def _sc_mesh():
    return plsc.VectorSubcoreMesh(core_axis_name="c", subcore_axis_name="s")


def _sc_scatter2(src_a, src_b, idx0, idx1, n_rows):
    n = src_a.shape[0]
    out = jax.ShapeDtypeStruct((n_rows, ROW_W), src_a.dtype)

    @functools.partial(pl.kernel, out_type=[out, out], mesh=_sc_mesh())
    def scatter(xa_hbm, xb_hbm, i0_hbm, i1_hbm, oa_hbm, ob_hbm):
        for x_hbm, o_hbm in ((xa_hbm, oa_hbm), (xb_hbm, ob_hbm)):
            def body(x_vmem, i0_vmem, i1_vmem, o_hbm=o_hbm):
                pltpu.sync_copy(x_vmem, o_hbm.at[i0_vmem.at[0]])
                pltpu.sync_copy(x_vmem, o_hbm.at[i1_vmem.at[0]])

            pltpu.emit_pipeline(
                body, grid=(n // SC_WIN,),
                in_specs=[pl.BlockSpec((SC_WIN, ROW_W), index_map=lambda i: (i, 0)),
                          pl.BlockSpec((1, SC_WIN), index_map=lambda i: (0, i)),
                          pl.BlockSpec((1, SC_WIN), index_map=lambda i: (0, i))],
                out_specs=[],
                core_axis_name=("c", "s"), dimension_semantics=(pltpu.PARALLEL,),
            )(x_hbm, i0_hbm, i1_hbm)

    return scatter(src_a, src_b, idx0, idx1)


def _sc_gather(src_a, src_b, idx):
    m = idx.shape[1]
    out = jax.ShapeDtypeStruct((m, ROW_W), src_a.dtype)

    @functools.partial(pl.kernel, out_type=[out, out], mesh=_sc_mesh())
    def gather(xa_hbm, xb_hbm, i_hbm, oa_hbm, ob_hbm):
        for x_hbm, o_hbm in ((xa_hbm, oa_hbm), (xb_hbm, ob_hbm)):
            def body(i_vmem, o_vmem, x_hbm=x_hbm):
                pltpu.sync_copy(x_hbm.at[i_vmem.at[0]], o_vmem)

            pltpu.emit_pipeline(
                body, grid=(m // SC_WIN,),
                in_specs=[pl.BlockSpec((1, SC_WIN), index_map=lambda i: (0, i))],
                out_specs=[pl.BlockSpec((SC_WIN, ROW_W), index_map=lambda i: (i, 0))],
                core_axis_name=("c", "s"), dimension_semantics=(pltpu.PARALLEL,),
            )(i_hbm, o_hbm)

    return gather(src_a, src_b, idx)


def _experts_kernel(l, n_blk, start_ref, nchunk_ref, xpa_ref, xpb_ref, w1_ref, w3_ref, w2_ref, ypa_ref, ypb_ref,
                    w13_scr, w2_scr, wbuf13, wbuf2, xbuf, ybuf, wsem, xsem, ysem):
    e = pl.program_id(0)
    nc = nchunk_ref[e]
    chunk0 = start_ref[e]
    n_used = start_ref[N_EXP]

    def w_copies(ex, slot):
        cps = []
        for p in range(W_PIECES):
            r13 = pl.ds(p * (D // W_PIECES), D // W_PIECES)
            r2 = pl.ds(p * (D_FF // W_PIECES), D_FF // W_PIECES)
            cps.append(pltpu.make_async_copy(w1_ref.at[l, ex, r13], wbuf13.at[slot, 0, r13], wsem.at[slot]))
            cps.append(pltpu.make_async_copy(w3_ref.at[l, ex, r13], wbuf13.at[slot, 1, r13], wsem.at[slot]))
            cps.append(pltpu.make_async_copy(w2_ref.at[l, ex, r2], wbuf2.at[slot, r2], wsem.at[slot]))
        return cps

    def x_copies(g):
        rows = pl.ds(pl.multiple_of(g * BLK, BLK), BLK)
        slot = g % (X_AHEAD + 1)
        return [pltpu.make_async_copy(src.at[rows], xbuf.at[slot, h], xsem.at[slot])
                for h, src in enumerate((xpa_ref, xpb_ref))]

    def y_copies(g):
        rows = pl.ds(pl.multiple_of(g * BLK, BLK), BLK)
        slot = g % 2
        return [pltpu.make_async_copy(ybuf.at[slot, h], dst.at[rows], ysem.at[slot])
                for h, dst in enumerate((ypa_ref, ypb_ref))]

    def start(cps):
        for cp in cps:
            cp.start()

    def wait(cps):
        for cp in cps:
            cp.wait()

    @pl.when(e == 0)
    def _():
        for j in range(X_AHEAD):
            @pl.when(j < n_used)
            def _():
                start(x_copies(j))

        start(w_copies(0, 0))

    @pl.when(e < N_EXP)
    def _():
        wslot = e % 2
        wait(w_copies(e, wslot))

        @pl.when(e + 1 < N_EXP)
        def _():
            start(w_copies(e + 1, 1 - wslot))

        @pl.when(nc > 0)
        def _():
            w13_scr[:, 0:D_FF] = wbuf13[wslot, 0].astype(BF16)
            w13_scr[:, D_FF:2 * D_FF] = wbuf13[wslot, 1].astype(BF16)
            w2_scr[...] = wbuf2[wslot].astype(BF16)

            def chunk(c, carry):
                g = chunk0 + c
                slot = g % (X_AHEAD + 1)

                @pl.when(g + X_AHEAD < n_used)
                def _():
                    start(x_copies(g + X_AHEAD))

                wait(x_copies(g))

                @pl.when(g >= 2)
                def _():
                    wait(y_copies(g - 2))

                a, b = _unpack_bf16_pair(jnp.concatenate([xbuf[slot, 0], xbuf[slot, 1]], axis=1))
                x = jnp.concatenate([a.astype(BF16), b.astype(BF16)], axis=1)
                h13 = jnp.dot(x, w13_scr[...], preferred_element_type=F32)
                h1 = h13[:, 0:D_FF]
                act = (h1 * jax.nn.sigmoid(h1) * h13[:, D_FF:2 * D_FF]).astype(BF16)
                y = jnp.dot(act, w2_scr[...], preferred_element_type=F32)
                yp = _pack_bf16_pair(y[:, 0:D // 2], y[:, D // 2:D])
                ybuf[g % 2, 0] = yp[:, 0:ROW_W]
                ybuf[g % 2, 1] = yp[:, ROW_W:2 * ROW_W]
                start(y_copies(g))
                return carry

            lax.fori_loop(0, nc, chunk, 0)

    @pl.when(e == N_EXP)
    def _():
        @pl.when(n_used >= 2)
        def _():
            wait(y_copies(n_used - 2))

        wait(y_copies(n_used - 1))
        ybuf[0] = jnp.zeros((2, BLK, ROW_W), U32)

        def fill_one(g, carry):
            rows = pl.ds(pl.multiple_of(g * BLK, BLK), BLK)
            cps = [pltpu.make_async_copy(ybuf.at[0, h], dst.at[rows], ysem.at[0])
                   for h, dst in enumerate((ypa_ref, ypb_ref))]
            start(cps)
            wait(cps)
            return carry

        lax.fori_loop(n_used, n_blk, fill_one, 0)


def _experts(chunk_start, n_chunk, xpa, xpb, w1, w3, w2, l):
    n_rows = xpa.shape[0]
    n_blk = n_rows // BLK
    hbm = pl.BlockSpec(memory_space=pl.ANY)
    half = jax.ShapeDtypeStruct((n_rows, ROW_W), U32)
    return pl.pallas_call(
        functools.partial(_experts_kernel, l, n_blk),
        grid_spec=pltpu.PrefetchScalarGridSpec(
            num_scalar_prefetch=2,
            grid=(N_EXP + 1,),
            in_specs=[hbm, hbm, hbm, hbm, hbm],
            out_specs=[hbm, hbm],
            scratch_shapes=[pltpu.VMEM((D, 2 * D_FF), BF16), pltpu.VMEM((D_FF, D), BF16),
                            pltpu.VMEM((2, 2, D, D_FF), F32), pltpu.VMEM((2, D_FF, D), F32),
                            pltpu.VMEM((X_AHEAD + 1, 2, BLK, ROW_W), U32), pltpu.VMEM((2, 2, BLK, ROW_W), U32),
                            pltpu.SemaphoreType.DMA((2,)), pltpu.SemaphoreType.DMA((X_AHEAD + 1,)),
                            pltpu.SemaphoreType.DMA((2,))],
        ),
        out_shape=[half, half],
        compiler_params=_cparams(("arbitrary",)),
        name="experts",
    )(chunk_start, n_chunk, xpa, xpb, w1, w3, w2)


def _combine_kernel(final, ga0_ref, gb0_ref, ga1_ref, gb1_ref, x1_ref, info_ref, mod_ref, fg_ref, o_ref):
    info = info_ref[...]
    gw0 = info[:, 0:1]
    gw1 = info[:, 1:2]
    a0, b0 = _unpack_bf16_pair(jnp.concatenate([ga0_ref[...], gb0_ref[...]], axis=1))
    a1, b1 = _unpack_bf16_pair(jnp.concatenate([ga1_ref[...], gb1_ref[...]], axis=1))
    y = jnp.concatenate([gw0 * a0 + gw1 * a1, gw0 * b0 + gw1 * b1], axis=1)
    gate2 = mod_ref[0, 0][5:6, :]
    x2 = x1_ref[...] + gate2 * y
    if final:
        ms = jnp.mean(x2 * x2, axis=-1, keepdims=True)
        x2 = x2 * lax.rsqrt(ms + EPS) * fg_ref[...]
    o_ref[...] = x2


def _combine(ga, gb, x1, info, mod, l, final_g, bsz, final):
    n_tok = x1.shape[0]
    n_i = n_tok // T_CMB
    per_b = n_i // bsz
    first = pl.BlockSpec((T_CMB, ROW_W), lambda i: (i, 0))
    second = pl.BlockSpec((T_CMB, ROW_W), lambda i: (i + n_i, 0))
    return pl.pallas_call(
        functools.partial(_combine_kernel, final),
        grid=(n_i,),
        in_specs=[
            first, first, second, second,
            pl.BlockSpec((T_CMB, D), lambda i: (i, 0)),
            pl.BlockSpec((T_CMB, CHUNK), lambda i: (i, 0)),
            pl.BlockSpec((1, 1, 8, D), lambda i: (l, i // per_b, 0, 0)),
            pl.BlockSpec((1, D), lambda i: (0, 0)),
        ],
        out_specs=pl.BlockSpec((T_CMB, D), lambda i: (i, 0)),
        out_shape=jax.ShapeDtypeStruct((n_tok, D), F32),
        compiler_params=_cparams(("arbitrary",)),
        name="combine",
    )(ga, gb, ga, gb, x1, info, mod, final_g)


def _dft_table_kernel(t1_ref, t2_ref, o_ref):
    half = t2_ref.shape[2]
    c2 = t2_ref[0]
    s2 = t2_ref[1]
    for j in range(TAB_GROUP):
        c1 = t1_ref[j, 0:1, :]
        s1 = t1_ref[j, 1:2, :]
        rows = slice(j * TAB_ROWS, (j + 1) * TAB_ROWS)
        o_ref[rows, 0:half] = (c1 * c2 - s1 * s2).astype(BF16)
        o_ref[rows, half:2 * half] = (-(s1 * c2 + c1 * s2)).astype(BF16)


def _dft_tables(seq):
    scale = 1.0 / (seq * HEAD) ** 0.5
    n_hi = seq // TAB_ROWS
    n = lax.broadcasted_iota(I32, (1, seq // 2), 1)
    kh = lax.broadcasted_iota(I32, (n_hi, 1), 0)
    a1 = ((kh * n) % n_hi).astype(F32) * (2.0 * jnp.pi / n_hi)
    t1 = jnp.stack([jnp.cos(a1), jnp.sin(a1)], axis=1)
    kl = lax.broadcasted_iota(I32, (TAB_ROWS, 1), 0)
    a2 = ((kl * n) % seq).astype(F32) * (2.0 * jnp.pi / seq)
    t2 = jnp.stack([jnp.cos(a2) * scale, jnp.sin(a2) * scale], axis=0)
    cs = pl.pallas_call(
        _dft_table_kernel,
        grid=(n_hi // TAB_GROUP,),
        in_specs=[pl.BlockSpec((TAB_GROUP, 2, seq // 2), lambda i: (i, 0, 0)),
                  pl.BlockSpec((2, TAB_ROWS, seq // 2), lambda i: (0, 0, 0))],
        out_specs=pl.BlockSpec((TAB_GROUP * TAB_ROWS, seq), lambda i: (i, 0)),
        out_shape=jax.ShapeDtypeStruct((seq, seq), BF16),
        compiler_params=_cparams(("arbitrary",)),
        name="dft_table",
    )(t1, t2)
    d = lax.broadcasted_iota(I32, (FOUR_W, FOUR_W), 0)
    q = lax.broadcasted_iota(I32, (FOUR_W, FOUR_W), 1)
    same = (d // HEAD) == (q // HEAD)
    ang64 = ((d * q) % HEAD).astype(F32) * (2.0 * jnp.pi / HEAD)
    dft64 = jnp.concatenate([jnp.where(same, jnp.cos(ang64), 0.0),
                             jnp.where(same, jnp.sin(ang64), 0.0)], axis=1).astype(BF16)
    return cs, dft64


def _routing_tables(route, counts_f):
    counts = counts_f[:, 0].astype(I32)
    pc = (counts + BLK - 1) // BLK * BLK
    pends = jnp.cumsum(pc)
    pstarts = pends - pc
    eid = lax.broadcasted_iota(I32, (N_EXP, 1), 0)

    def dest_of(e, r):
        return jnp.sum(jnp.where(e[None, :] == eid, pstarts[:, None], 0), axis=0) + r

    dest = jnp.concatenate([dest_of(route[0], route[2]), dest_of(route[1], route[3])])
    chunk_start = jnp.concatenate([pstarts, pends[-1:]]) // BLK
    n_chunk = jnp.concatenate([pc // BLK, jnp.zeros((1,), I32)])
    return dest, chunk_start, n_chunk


def kernel(x, c, ln1_g, ln2_g, w_ada, b_ada, w_in, w_out, conv_w, conv_b, conv_gn_g, conv_gn_b, sgu_ln_g,
           sgu_ln_b, sgu_w, sgu_b, router_w, router_b, exp_w1, exp_w3, exp_w2, final_g):
    bsz, seq, _ = x.shape
    depth = w_in.shape[0]
    n_tok = bsz * seq
    n_blk = (2 * n_tok + N_EXP * (BLK - 1) + BLK - 1) // BLK
    r3 = lambda a: a.reshape(depth, 1, a.shape[-1])

    mod = jnp.pad(_ada_mod(c, w_ada, b_ada).reshape(depth, bsz, 6, D), ((0, 0), (0, 0), (0, 2), (0, 0)))
    cs, dft64 = _dft_tables(seq)
    hd = lax.broadcasted_iota(I32, (CONV_W, CONV_W), 0) // HEAD
    gavg = jnp.where(hd == hd.T, 1.0 / HEAD, 0.0).astype(BF16)
    w_in_bf = w_in.astype(BF16)
    w_out_bf = w_out.astype(BF16)
    sgu_w_bf = sgu_w.astype(BF16)
    sgu_bias = jnp.repeat(jnp.swapaxes(sgu_b, 1, 2), HEAD, axis=2)
    conv_w_p = jnp.pad(conv_w, ((0, 0), (0, 1), (0, 0)))
    rwt_hi = router_w.T.astype(BF16)
    rwt_lo = (router_w.T - rwt_hi.astype(F32)).astype(BF16)
    router_wt = jnp.concatenate([rwt_hi, rwt_lo], axis=0)
    router_bc = router_b.reshape(N_EXP, 1)
    fg = final_g.reshape(1, D)

    for l in range(depth):
        yac, pq = _mixer_in(x, mod, l, r3(ln1_g), w_in_bf, conv_w_p, r3(conv_b), r3(conv_gn_g), r3(conv_gn_b),
                            r3(sgu_ln_g), r3(sgu_ln_b), sgu_w_bf, sgu_bias, dft64, gavg)
        yb = _seq_dft(cs, pq.reshape(bsz, 2 * seq, FOUR_W))
        x1, h2pa, h2pb, info, route, counts = _mixer_out(
            yac.reshape(n_tok, CONV_W + SGU_W), yb.reshape(n_tok, FOUR_W), x.reshape(n_tok, D),
            mod, l, r3(ln2_g), w_out_bf, router_wt, router_bc, bsz)
        dest, chunk_start, n_chunk = _routing_tables(route, counts)
        d0 = dest[:n_tok].reshape(1, n_tok)
        d1 = dest[n_tok:].reshape(1, n_tok)
        xpa, xpb = _sc_scatter2(h2pa, h2pb, d0, d1, n_blk * BLK)
        ypa, ypb = _experts(chunk_start, n_chunk, xpa, xpb, exp_w1, exp_w3, exp_w2, l)
        ga, gb = _sc_gather(ypa, ypb, dest.reshape(1, 2 * n_tok))
        x = _combine(ga, gb, x1, info, mod, l, fg, bsz, l == depth - 1).reshape(bsz, seq, D)
    return x
```

```python
import functools

import jax
import jax.numpy as jnp
from jax import lax
from jax.experimental import pallas as pl
from jax.experimental.pallas import tpu as pltpu
from jax.experimental.pallas import tpu_sc as plsc

F32 = jnp.float32
BF16 = jnp.bfloat16
I32 = jnp.int32
U32 = jnp.uint32
HIGHEST = lax.Precision.HIGHEST

D = 1024
HEAD = 64
CONV_W = 384
FOUR_W = 256
SGU_W = 384
SGU_HEADS = SGU_W // HEAD
Z_COLS = 2 * CONV_W + FOUR_W + 2 * SGU_W
KSIZE = 31
HALO = 16
CHUNK = 128
N_EXP = 64
N_GRP = 8
EPG = N_EXP // N_GRP
D_FF = D // 2
EPS = 1e-6

T_MIX = 512
T_DFT = 512
T_CMB = 512
ROW_W = 256
SC_WIN = 128
BLK = 384
X_AHEAD = 3
W_AHEAD = 1
CONV_ROWS = 64
TAB_ROWS = 64
TAB_GROUP = 4
W_PIECES = 4
VMEM_LIMIT = 56 * 1024 * 1024


def _cparams(sem):
    return pltpu.CompilerParams(dimension_semantics=sem, vmem_limit_bytes=VMEM_LIMIT)


def _pack_bf16_pair(a, b):
    ua = lax.bitcast_convert_type(a.astype(BF16).astype(F32), U32) >> 16
    ub = lax.bitcast_convert_type(b.astype(BF16).astype(F32), U32) & jnp.uint32(0xFFFF0000)
    return ua | ub


def _unpack_bf16_pair(p):
    a = lax.bitcast_convert_type(p << 16, F32)
    b = lax.bitcast_convert_type(p & jnp.uint32(0xFFFF0000), F32)
    return a, b


def _ada_kernel(c_ref, *refs):
    w_refs, b_ref, o_ref = refs[:-2], refs[-2], refs[-1]
    c = c_ref[...]
    ca = c * jax.nn.sigmoid(c)
    tn = w_refs[0].shape[2]
    for j, w_ref in enumerate(w_refs):
        cols = slice(j * tn, (j + 1) * tn)
        o_ref[0, :, cols] = jnp.dot(ca, w_ref[0], precision=HIGHEST, preferred_element_type=F32) + b_ref[0, :, cols]


def _ada_mod(c, w_ada, b_ada):
    depth, _, ncol = w_ada.shape
    bsz = c.shape[0]
    n_slab = 4
    n_half = 2
    tn = ncol // (n_slab * n_half)
    return pl.pallas_call(
        _ada_kernel,
        grid=(depth, n_half),
        in_specs=[pl.BlockSpec((bsz, D), lambda l, h: (0, 0))]
        + [pl.BlockSpec((1, D, tn), functools.partial(lambda j, l, h: (l, 0, n_slab * h + j), j))
           for j in range(n_slab)]
        + [pl.BlockSpec((1, 1, n_slab * tn), lambda l, h: (l, 0, h))],
        out_specs=pl.BlockSpec((1, bsz, n_slab * tn), lambda l, h: (l, 0, h)),
        out_shape=jax.ShapeDtypeStruct((depth, bsz, ncol), F32),
        compiler_params=_cparams(("arbitrary", "arbitrary")),
        name="ada_mod",
    )(c, *([w_ada] * n_slab), b_ada.reshape(depth, 1, ncol))


def _mixer_in_kernel(xm_ref, xp_ref, xn_ref, mod_ref, g1_ref, win_ref, cw_ref, cb_ref, gng_ref, gnb_ref,
                     lng_ref, lnb_ref, sw_ref, sb_ref, dft_ref, gavg_ref,
                     yac_ref, pq_ref, glu_scr, sh_scr, conv_scr):
    i = pl.program_id(1)
    n_i = pl.num_programs(1)
    T = T_MIX
    mod = mod_ref[0, 0]
    shift1 = mod[0:1, :]
    gain1 = g1_ref[0] * (1.0 + mod[1:2, :])

    def norm_mod(x):
        ms = jnp.mean(x * x, axis=-1, keepdims=True)
        return x * lax.rsqrt(ms + EPS) * gain1 + shift1

    h = norm_mod(xm_ref[0]).astype(BF16)
    z = jnp.dot(h, win_ref[0], preferred_element_type=F32)

    hh = norm_mod(jnp.concatenate([xp_ref[0], xn_ref[0]], axis=0)).astype(BF16)
    zh = jnp.dot(hh, win_ref[0, :, 0:2 * CONV_W], preferred_element_type=F32)
    glu_h = zh[:, 0:CONV_W] * jax.nn.sigmoid(zh[:, CONV_W:2 * CONV_W])
    glu_scr[0:HALO, :] = jnp.where(i > 0, glu_h[0:HALO], 0.0)
    glu_scr[HALO + T:2 * HALO + T, :] = jnp.where(i < n_i - 1, glu_h[HALO:2 * HALO], 0.0)
    glu_scr[HALO:HALO + T, :] = z[:, 0:CONV_W] * jax.nn.sigmoid(z[:, CONV_W:2 * CONV_W])

    off = HALO - KSIZE // 2
    for b in range(8):
        sh_scr[b] = glu_scr[b:b + T + 3 * 8, :]

    for c in range(T // CONV_ROWS):
        r0 = c * CONV_ROWS
        acc = jnp.broadcast_to(cb_ref[0], (CONV_ROWS, CONV_W))
        for k in range(KSIZE):
            a, b = divmod(k + off, 8)
            acc = acc + sh_scr[b, r0 + 8 * a:r0 + 8 * a + CONV_ROWS, :] * cw_ref[0, k:k + 1, :]
        conv_scr[r0:r0 + CONV_ROWS, :] = acc
    hc = conv_scr[...]
    gavg = gavg_ref[...]
    mu = jnp.dot(hc.astype(BF16), gavg, preferred_element_type=F32)
    dc = hc - mu
    var = jnp.dot((dc * dc).astype(BF16), gavg, preferred_element_type=F32)
    hn = dc * lax.rsqrt(var + EPS) * gng_ref[0] + gnb_ref[0]
    ya = hn * jax.nn.sigmoid(hn)
    yac_ref[0, :, 0:CONV_W] = ya.astype(BF16)

    zb = z[:, 2 * CONV_W:2 * CONV_W + FOUR_W].astype(BF16)
    pq = jnp.dot(zb, dft_ref[...], preferred_element_type=F32)
    pq_ref[0, 0] = pq[:, 0:FOUR_W].astype(BF16)
    pq_ref[0, 1] = pq[:, FOUR_W:2 * FOUR_W].astype(BF16)

    c0 = 2 * CONV_W + FOUR_W
    zc = z[:, c0:c0 + 2 * SGU_W]
    zc = 0.5 * zc * (1.0 + jnp.tanh(0.7978845608028654 * (zc + 0.044715 * (zc * zc * zc))))
    u = zc[:, 0:SGU_W]
    v = zc[:, SGU_W:2 * SGU_W]
    vm = jnp.mean(v, axis=-1, keepdims=True)
    vd = v - vm
    vv = jnp.mean(vd * vd, axis=-1, keepdims=True)
    vn = (vd * lax.rsqrt(vv + EPS) * lng_ref[0] + lnb_ref[0]).astype(BF16)
    n_chunk = T // CHUNK
    lane = lax.broadcasted_iota(I32, (CHUNK, 2 * HEAD), 1)
    for pr in range(SGU_HEADS // 2):
        cols = slice(2 * HEAD * pr, 2 * HEAD * (pr + 1))
        rhs = jnp.concatenate([vn[n * CHUNK:(n + 1) * CHUNK, cols] for n in range(n_chunk)], axis=1)
        lo = jnp.dot(sw_ref[0, 2 * pr], rhs, preferred_element_type=F32)
        hi = jnp.dot(sw_ref[0, 2 * pr + 1], rhs, preferred_element_type=F32)
        for n in range(n_chunk):
            sl = slice(n * 2 * HEAD, (n + 1) * 2 * HEAD)
            vs = jnp.where(lane < HEAD, lo[:, sl], hi[:, sl]) + sb_ref[0, :, cols]
            rows = slice(n * CHUNK, (n + 1) * CHUNK)
            yac_ref[0, rows, CONV_W + 2 * HEAD * pr:CONV_W + 2 * HEAD * (pr + 1)] = (u[rows, cols] * vs).astype(BF16)


def _mixer_in(x, mod, l, ln1_g, w_in_bf, conv_w, conv_b, gn_g, gn_b, ln_g, ln_b, sgu_w_bf, sgu_bias, dft64, gavg):
    bsz, seq, _ = x.shape
    T = T_MIX
    n_i = seq // T
    hb = T // HALO
    n_h = seq // HALO
    vec = lambda w: pl.BlockSpec((1, 1, w), lambda b, i: (l, 0, 0))
    return pl.pallas_call(
        _mixer_in_kernel,
        grid=(bsz, n_i),
        in_specs=[
            pl.BlockSpec((1, T, D), lambda b, i: (b, i, 0)),
            pl.BlockSpec((1, HALO, D), lambda b, i: (b, jnp.maximum(i * hb - 1, 0), 0)),
            pl.BlockSpec((1, HALO, D), lambda b, i: (b, jnp.minimum((i + 1) * hb, n_h - 1), 0)),
            pl.BlockSpec((1, 1, 8, D), lambda b, i: (l, b, 0, 0)),
            vec(D),
            pl.BlockSpec((1, D, Z_COLS), lambda b, i: (l, 0, 0)),
            pl.BlockSpec((1, KSIZE + 1, CONV_W), lambda b, i: (l, 0, 0)),
            vec(CONV_W), vec(CONV_W), vec(CONV_W), vec(SGU_W), vec(SGU_W),
            pl.BlockSpec((1, SGU_HEADS, CHUNK, CHUNK), lambda b, i: (l, 0, 0, 0)),
            pl.BlockSpec((1, CHUNK, SGU_W), lambda b, i: (l, 0, 0)),
            pl.BlockSpec((FOUR_W, 2 * FOUR_W), lambda b, i: (0, 0)),
            pl.BlockSpec((CONV_W, CONV_W), lambda b, i: (0, 0)),
        ],
        out_specs=[
            pl.BlockSpec((1, T, CONV_W + SGU_W), lambda b, i: (b, i, 0)),
            pl.BlockSpec((1, 2, T, FOUR_W), lambda b, i: (b, 0, i, 0)),
        ],
        out_shape=[
            jax.ShapeDtypeStruct((bsz, seq, CONV_W + SGU_W), BF16),
            jax.ShapeDtypeStruct((bsz, 2, seq, FOUR_W), BF16),
        ],
        scratch_shapes=[
            pltpu.VMEM((T + 2 * HALO, CONV_W), F32),
            pltpu.VMEM((8, T + 3 * 8, CONV_W), F32),
            pltpu.VMEM((T, CONV_W), F32),
        ],
        compiler_params=_cparams(("arbitrary", "arbitrary")),
        name="mixer_in",
    )(x, x, x, mod, ln1_g, w_in_bf, conv_w, conv_b, gn_g, gn_b, ln_g, ln_b, sgu_w_bf, sgu_bias, dft64, gavg)


def _dft_fold_kernel(scale, pq_ref, fold_ref, ph_ref):
    seq = pq_ref.shape[1] // 2
    half = seq // 2
    nb = seq // CHUNK
    rr = lax.broadcasted_iota(I32, (CHUNK, CHUNK), 0)
    cc = lax.broadcasted_iota(I32, (CHUNK, CHUNK), 1)
    rev = jnp.where((rr >= 1) & (cc == CHUNK - rr), 1.0, 0.0).astype(BF16)
    row0 = lax.broadcasted_iota(I32, (CHUNK, FOUR_W), 0) == 0
    for part, sign in ((0, 1.0), (1, -1.0)):
        base = part * seq
        for m in range(half // CHUNK):
            lo = pq_ref[0, base + CHUNK * m:base + CHUNK * (m + 1), :].astype(F32)
            up = pq_ref[0, base + CHUNK * (nb - 1 - m):base + CHUNK * (nb - m), :]
            mirrored = jnp.dot(rev, up, preferred_element_type=F32)
            if m >= 1:
                first = pq_ref[0, base + CHUNK * (nb - m):base + CHUNK * (nb - m) + 1, :].astype(F32)
                mirrored = jnp.where(row0, first, mirrored)
            fold_ref[0, part * half + CHUNK * m:part * half + CHUNK * (m + 1), :] = (lo + sign * mirrored).astype(BF16)
    ph_ref[0] = jnp.broadcast_to(pq_ref[0, half:half + 1, :].astype(F32) * scale, (8, FOUR_W))


def _seq_dft_kernel(cs_ref, fold_ref, ph_ref, o_ref):
    alt = jnp.where(lax.broadcasted_iota(I32, (T_DFT, 1), 0) % 2 == 0, 1.0, -1.0)
    o = jnp.dot(cs_ref[...], fold_ref[0], preferred_element_type=F32) + alt * ph_ref[0, 0:1, :]
    o_ref[0] = o.astype(BF16)


def _seq_dft(cs, pq):
    bsz, two_s, _ = pq.shape
    seq = two_s // 2
    scale = 1.0 / (seq * HEAD) ** 0.5
    fold, ph = pl.pallas_call(
        functools.partial(_dft_fold_kernel, scale),
        grid=(bsz,),
        in_specs=[pl.BlockSpec((1, two_s, FOUR_W), lambda b: (b, 0, 0))],
        out_specs=[pl.BlockSpec((1, seq, FOUR_W), lambda b: (b, 0, 0)),
                   pl.BlockSpec((1, 8, FOUR_W), lambda b: (b, 0, 0))],
        out_shape=[jax.ShapeDtypeStruct((bsz, seq, FOUR_W), BF16),
                   jax.ShapeDtypeStruct((bsz, 8, FOUR_W), F32)],
        compiler_params=_cparams(("arbitrary",)),
        name="dft_fold",
    )(pq)
    return pl.pallas_call(
        _seq_dft_kernel,
        grid=(seq // T_DFT, bsz),
        in_specs=[pl.BlockSpec((T_DFT, seq), lambda k, b: (k, 0)),
                  pl.BlockSpec((1, seq, FOUR_W), lambda k, b: (b, 0, 0)),
                  pl.BlockSpec((1, 8, FOUR_W), lambda k, b: (b, 0, 0))],
        out_specs=pl.BlockSpec((1, T_DFT, FOUR_W), lambda k, b: (b, k, 0)),
        out_shape=jax.ShapeDtypeStruct((bsz, seq, FOUR_W), BF16),
        compiler_params=_cparams(("arbitrary", "arbitrary")),
        name="seq_dft",
    )(cs, fold, ph)


def _mixer_out_kernel(yac_ref, yb_ref, x_ref, mod_ref, g2_ref, wout_ref, rwt_ref, rb_ref, upper_ref,
                      x1_ref, h2pa_ref, h2pb_ref, info_ref, route_ref, cnt_ref, cnt_scr):
    i = pl.program_id(0)
    T = T_MIX

    @pl.when(i == 0)
    def _():
        cnt_scr[...] = jnp.zeros_like(cnt_scr)

    mod = mod_ref[0, 0]
    gate1 = mod[2:3, :]
    shift2 = mod[3:4, :]
    gain2 = g2_ref[0] * (1.0 + mod[4:5, :])
    yac = yac_ref[...]
    ycat = jnp.concatenate([yac[:, 0:CONV_W], yb_ref[...], yac[:, CONV_W:CONV_W + SGU_W]], axis=1)
    o = jnp.dot(ycat, wout_ref[0], preferred_element_type=F32)
    x1 = x_ref[...] + gate1 * o
    x1_ref[...] = x1
    ms = jnp.mean(x1 * x1, axis=-1, keepdims=True)
    h2 = x1 * lax.rsqrt(ms + EPS) * gain2 + shift2
    h2p = _pack_bf16_pair(h2[:, 0:D // 2], h2[:, D // 2:D])
    h2pa_ref[...] = h2p[:, 0:ROW_W]
    h2pb_ref[...] = h2p[:, ROW_W:2 * ROW_W]

    h_hi = h2.astype(BF16)
    h_lo = (h2 - h_hi.astype(F32)).astype(BF16)
    nt = (((1,), (1,)), ((), ()))
    part = lax.dot_general(rwt_ref[...], h_hi, nt, preferred_element_type=F32)
    logits = (part[0:N_EXP] + part[N_EXP:2 * N_EXP]
              + lax.dot_general(rwt_ref[0:N_EXP, :], h_lo, nt, preferred_element_type=F32))
    mx = jnp.max(logits, axis=0, keepdims=True)
    ex = jnp.exp(logits - mx)
    probs = ex / jnp.sum(ex, axis=0, keepdims=True)
    sel = probs + rb_ref[...]
    sel3 = sel.reshape(N_GRP, EPG, T)
    probs3 = probs.reshape(N_GRP, EPG, T)
    jj = lax.broadcasted_iota(I32, (N_GRP, EPG, T), 1)
    m1 = jnp.max(sel3, axis=1, keepdims=True)
    i1 = jnp.min(jnp.where(sel3 == m1, jj, EPG), axis=1, keepdims=True)
    rest = jnp.where(jj == i1, -jnp.inf, sel3)
    m2 = jnp.max(rest, axis=1, keepdims=True)
    i2 = jnp.min(jnp.where(rest == m2, jj, EPG), axis=1, keepdims=True)
    gscore = m1 + m2
    gg = lax.broadcasted_iota(I32, (N_GRP, 1, T), 0)
    gmax = jnp.max(gscore, axis=0, keepdims=True)
    gidx = jnp.min(jnp.where(gscore == gmax, gg, N_GRP), axis=0, keepdims=True)
    ing = gg == gidx
    pick = lambda a, zero: jnp.sum(jnp.where(ing, a, zero), axis=0)
    p1 = jnp.sum(jnp.where(jj == i1, probs3, 0.0), axis=1, keepdims=True)
    p2 = jnp.sum(jnp.where(jj == i2, probs3, 0.0), axis=1, keepdims=True)
    pa = pick(p1, 0.0)
    pb = pick(p2, 0.0)
    gbase = gidx[0] * EPG
    e0 = gbase + pick(i1, 0)
    e1 = gbase + pick(i2, 0)
    den = pa + pb
    gw0 = pa / den
    gw1 = pb / den

    ee = lax.broadcasted_iota(I32, (N_EXP, T), 0)
    oh0 = ee == e0
    oh1 = ee == e1
    amat = jnp.where(oh0 | oh1, 1.0, 0.0)
    before = jnp.dot(amat.astype(BF16), upper_ref[...], preferred_element_type=F32) + cnt_scr[...]
    r0 = jnp.sum(jnp.where(oh0, before, 0.0), axis=0, keepdims=True)
    r1 = jnp.sum(jnp.where(oh1, before, 0.0), axis=0, keepdims=True)
    cnt_scr[...] = cnt_scr[...] + jnp.sum(amat, axis=1, keepdims=True)
    cnt_ref[...] = cnt_scr[...]

    rid = lax.broadcasted_iota(I32, (8, T), 0)
    route = jnp.zeros((8, T), I32)
    for k, val in enumerate((e0, e1, r0.astype(I32), r1.astype(I32))):
        route = jnp.where(rid == k, val, route)
    route_ref[...] = route
    gws = jnp.where(rid == 0, gw0, jnp.where(rid == 1, gw1, 0.0))
    gws = jnp.concatenate([gws, jnp.zeros((CHUNK - 8, T), F32)], axis=0)
    info_ref[...] = gws.T


def _mixer_out(yac, yb, x, mod, l, ln2_g, w_out_bf, router_wt, router_b, bsz):
    n_tok = x.shape[0]
    T = T_MIX
    per_b = n_tok // bsz // T
    row = lambda w: pl.BlockSpec((T, w), lambda i: (i, 0))
    upper = (lax.broadcasted_iota(I32, (T, T), 0) < lax.broadcasted_iota(I32, (T, T), 1)).astype(BF16)
    return pl.pallas_call(
        _mixer_out_kernel,
        grid=(n_tok // T,),
        in_specs=[
            row(CONV_W + SGU_W), row(FOUR_W), row(D),
            pl.BlockSpec((1, 1, 8, D), lambda i: (l, i // per_b, 0, 0)),
            pl.BlockSpec((1, 1, D), lambda i: (l, 0, 0)),
            pl.BlockSpec((1, D, D), lambda i: (l, 0, 0)),
            pl.BlockSpec((2 * N_EXP, D), lambda i: (0, 0)),
            pl.BlockSpec((N_EXP, 1), lambda i: (0, 0)),
            pl.BlockSpec((T, T), lambda i: (0, 0)),
        ],
        out_specs=[row(D), row(ROW_W), row(ROW_W), row(CHUNK), pl.BlockSpec((8, T), lambda i: (0, i)),
                   pl.BlockSpec((N_EXP, 1), lambda i: (0, 0))],
        out_shape=[
            jax.ShapeDtypeStruct((n_tok, D), F32),
            jax.ShapeDtypeStruct((n_tok, ROW_W), U32),
            jax.ShapeDtypeStruct((n_tok, ROW_W), U32),
            jax.ShapeDtypeStruct((n_tok, CHUNK), F32),
            jax.ShapeDtypeStruct((8, n_tok), I32),
            jax.ShapeDtypeStruct((N_EXP, 1), F32),
        ],
        scratch_shapes=[pltpu.VMEM((N_EXP, 1), F32)],
        compiler_params=_cparams(("arbitrary",)),
        name="mixer_out",
    )(yac, yb, x, mod, ln2_g, w_out_bf, router_wt, router_b, upper)


def _sc_mesh():
    return plsc.VectorSubcoreMesh(core_axis_name="c", subcore_axis_name="s")


def _sc_scatter2(src_a, src_b, idx0, idx1, n_rows):
    n = src_a.shape[0]
    out = jax.ShapeDtypeStruct((n_rows, ROW_W), src_a.dtype)

    @functools.partial(pl.kernel, out_type=[out, out], mesh=_sc_mesh())
    def scatter(xa_hbm, xb_hbm, i0_hbm, i1_hbm, oa_hbm, ob_hbm):
        for x_hbm, o_hbm in ((xa_hbm, oa_hbm), (xb_hbm, ob_hbm)):
            def body(x_vmem, i0_vmem, i1_vmem, o_hbm=o_hbm):
                pltpu.sync_copy(x_vmem, o_hbm.at[i0_vmem.at[0]])
                pltpu.sync_copy(x_vmem, o_hbm.at[i1_vmem.at[0]])

            pltpu.emit_pipeline(
                body, grid=(n // SC_WIN,),
                in_specs=[pl.BlockSpec((SC_WIN, ROW_W), index_map=lambda i: (i, 0)),
                          pl.BlockSpec((1, SC_WIN), index_map=lambda i: (0, i)),
                          pl.BlockSpec((1, SC_WIN), index_map=lambda i: (0, i))],
                out_specs=[],
                core_axis_name=("c", "s"), dimension_semantics=(pltpu.PARALLEL,),
            )(x_hbm, i0_hbm, i1_hbm)

    return scatter(src_a, src_b, idx0, idx1)


def _sc_gather(src_a, src_b, idx):
    m = idx.shape[1]
    out = jax.ShapeDtypeStruct((m, ROW_W), src_a.dtype)

    @functools.partial(pl.kernel, out_type=[out, out], mesh=_sc_mesh())
    def gather(xa_hbm, xb_hbm, i_hbm, oa_hbm, ob_hbm):
        for x_hbm, o_hbm in ((xa_hbm, oa_hbm), (xb_hbm, ob_hbm)):
            def body(i_vmem, o_vmem, x_hbm=x_hbm):
                pltpu.sync_copy(x_hbm.at[i_vmem.at[0]], o_vmem)

            pltpu.emit_pipeline(
                body, grid=(m // SC_WIN,),
                in_specs=[pl.BlockSpec((1, SC_WIN), index_map=lambda i: (0, i))],
                out_specs=[pl.BlockSpec((SC_WIN, ROW_W), index_map=lambda i: (i, 0))],
                core_axis_name=("c", "s"), dimension_semantics=(pltpu.PARALLEL,),
            )(i_hbm, o_hbm)

    return gather(src_a, src_b, idx)


def _experts_kernel(l, n_blk, start_ref, nchunk_ref, xpa_ref, xpb_ref, w1_ref, w3_ref, w2_ref, ypa_ref, ypb_ref,
                    w13_scr, w2_scr, wbuf13, wbuf2, xbuf, ybuf, wsem, xsem, ysem):
    e = pl.program_id(0)
    nc = nchunk_ref[e]
    chunk0 = start_ref[e]
    n_used = start_ref[N_EXP]

    def w_copies(ex):
        slot = ex % (W_AHEAD + 1)
        cps = []
        for p in range(W_PIECES):
            r13 = pl.ds(p * (D // W_PIECES), D // W_PIECES)
            r2 = pl.ds(p * (D_FF // W_PIECES), D_FF // W_PIECES)
            cps.append(pltpu.make_async_copy(w1_ref.at[l, ex, r13], wbuf13.at[slot, 0, r13], wsem.at[slot]))
            cps.append(pltpu.make_async_copy(w3_ref.at[l, ex, r13], wbuf13.at[slot, 1, r13], wsem.at[slot]))
            cps.append(pltpu.make_async_copy(w2_ref.at[l, ex, r2], wbuf2.at[slot, r2], wsem.at[slot]))
        return cps

    def x_copies(g):
        rows = pl.ds(pl.multiple_of(g * BLK, BLK), BLK)
        slot = g % (X_AHEAD + 1)
        return [pltpu.make_async_copy(src.at[rows], xbuf.at[slot, h], xsem.at[slot])
                for h, src in enumerate((xpa_ref, xpb_ref))]

    def y_copies(g):
        rows = pl.ds(pl.multiple_of(g * BLK, BLK), BLK)
        slot = g % 2
        return [pltpu.make_async_copy(ybuf.at[slot, h], dst.at[rows], ysem.at[slot])
                for h, dst in enumerate((ypa_ref, ypb_ref))]

    def start(cps):
        for cp in cps:
            cp.start()

    def wait(cps):
        for cp in cps:
            cp.wait()

    @pl.when(e == 0)
    def _():
        for j in range(X_AHEAD):
            @pl.when(j < n_used)
            def _():
                start(x_copies(j))

        for j in range(W_AHEAD):
            start(w_copies(j))

    @pl.when(e < N_EXP)
    def _():
        wslot = e % (W_AHEAD + 1)
        wait(w_copies(e))

        @pl.when(e + W_AHEAD < N_EXP)
        def _():
            start(w_copies(e + W_AHEAD))

        @pl.when(nc > 0)
        def _():
            w13_scr[:, 0:D_FF] = wbuf13[wslot, 0].astype(BF16)
            w13_scr[:, D_FF:2 * D_FF] = wbuf13[wslot, 1].astype(BF16)
            w2_scr[...] = wbuf2[wslot].astype(BF16)

            def chunk(c, carry):
                g = chunk0 + c
                slot = g % (X_AHEAD + 1)

                @pl.when(g + X_AHEAD < n_used)
                def _():
                    start(x_copies(g + X_AHEAD))

                wait(x_copies(g))

                @pl.when(g >= 2)
                def _():
                    wait(y_copies(g - 2))

                a, b = _unpack_bf16_pair(jnp.concatenate([xbuf[slot, 0], xbuf[slot, 1]], axis=1))
                x = jnp.concatenate([a.astype(BF16), b.astype(BF16)], axis=1)
                h13 = jnp.dot(x, w13_scr[...], preferred_element_type=F32)
                h1 = h13[:, 0:D_FF]
                act = (h1 * jax.nn.sigmoid(h1) * h13[:, D_FF:2 * D_FF]).astype(BF16)
                y = jnp.dot(act, w2_scr[...], preferred_element_type=F32)
                yp = _pack_bf16_pair(y[:, 0:D // 2], y[:, D // 2:D])
                ybuf[g % 2, 0] = yp[:, 0:ROW_W]
                ybuf[g % 2, 1] = yp[:, ROW_W:2 * ROW_W]
                start(y_copies(g))
                return carry

            lax.fori_loop(0, nc, chunk, 0)

    @pl.when(e == N_EXP)
    def _():
        @pl.when(n_used >= 2)
        def _():
            wait(y_copies(n_used - 2))

        wait(y_copies(n_used - 1))
        ybuf[0] = jnp.zeros((2, BLK, ROW_W), U32)

        def fill_one(g, carry):
            rows = pl.ds(pl.multiple_of(g * BLK, BLK), BLK)
            cps = [pltpu.make_async_copy(ybuf.at[0, h], dst.at[rows], ysem.at[0])
                   for h, dst in enumerate((ypa_ref, ypb_ref))]
            start(cps)
            wait(cps)
            return carry

        lax.fori_loop(n_used, n_blk, fill_one, 0)


def _experts(chunk_start, n_chunk, xpa, xpb, w1, w3, w2, l):
    n_rows = xpa.shape[0]
    n_blk = n_rows // BLK
    hbm = pl.BlockSpec(memory_space=pl.ANY)
    half = jax.ShapeDtypeStruct((n_rows, ROW_W), U32)
    return pl.pallas_call(
        functools.partial(_experts_kernel, l, n_blk),
        grid_spec=pltpu.PrefetchScalarGridSpec(
            num_scalar_prefetch=2,
            grid=(N_EXP + 1,),
            in_specs=[hbm, hbm, hbm, hbm, hbm],
            out_specs=[hbm, hbm],
            scratch_shapes=[pltpu.VMEM((D, 2 * D_FF), BF16), pltpu.VMEM((D_FF, D), BF16),
                            pltpu.VMEM((W_AHEAD + 1, 2, D, D_FF), F32), pltpu.VMEM((W_AHEAD + 1, D_FF, D), F32),
                            pltpu.VMEM((X_AHEAD + 1, 2, BLK, ROW_W), U32), pltpu.VMEM((2, 2, BLK, ROW_W), U32),
                            pltpu.SemaphoreType.DMA((W_AHEAD + 1,)), pltpu.SemaphoreType.DMA((X_AHEAD + 1,)),
                            pltpu.SemaphoreType.DMA((2,))],
        ),
        out_shape=[half, half],
        compiler_params=_cparams(("arbitrary",)),
        name="experts",
    )(chunk_start, n_chunk, xpa, xpb, w1, w3, w2)


def _combine_kernel(final, ga0_ref, gb0_ref, ga1_ref, gb1_ref, x1_ref, info_ref, mod_ref, fg_ref, o_ref):
    info = info_ref[...]
    gw0 = info[:, 0:1]
    gw1 = info[:, 1:2]
    a0, b0 = _unpack_bf16_pair(jnp.concatenate([ga0_ref[...], gb0_ref[...]], axis=1))
    a1, b1 = _unpack_bf16_pair(jnp.concatenate([ga1_ref[...], gb1_ref[...]], axis=1))
    y = jnp.concatenate([gw0 * a0 + gw1 * a1, gw0 * b0 + gw1 * b1], axis=1)
    gate2 = mod_ref[0, 0][5:6, :]
    x2 = x1_ref[...] + gate2 * y
    if final:
        ms = jnp.mean(x2 * x2, axis=-1, keepdims=True)
        x2 = x2 * lax.rsqrt(ms + EPS) * fg_ref[...]
    o_ref[...] = x2


def _combine(ga, gb, x1, info, mod, l, final_g, bsz, final):
    n_tok = x1.shape[0]
    n_i = n_tok // T_CMB
    per_b = n_i // bsz
    first = pl.BlockSpec((T_CMB, ROW_W), lambda i: (i, 0))
    second = pl.BlockSpec((T_CMB, ROW_W), lambda i: (i + n_i, 0))
    return pl.pallas_call(
        functools.partial(_combine_kernel, final),
        grid=(n_i,),
        in_specs=[
            first, first, second, second,
            pl.BlockSpec((T_CMB, D), lambda i: (i, 0)),
            pl.BlockSpec((T_CMB, CHUNK), lambda i: (i, 0)),
            pl.BlockSpec((1, 1, 8, D), lambda i: (l, i // per_b, 0, 0)),
            pl.BlockSpec((1, D), lambda i: (0, 0)),
        ],
        out_specs=pl.BlockSpec((T_CMB, D), lambda i: (i, 0)),
        out_shape=jax.ShapeDtypeStruct((n_tok, D), F32),
        compiler_params=_cparams(("arbitrary",)),
        name="combine",
    )(ga, gb, ga, gb, x1, info, mod, final_g)


def _dft_table_kernel(t1_ref, t2_ref, o_ref):
    half = t2_ref.shape[2]
    c2 = t2_ref[0]
    s2 = t2_ref[1]
    for j in range(TAB_GROUP):
        c1 = t1_ref[j, 0:1, :]
        s1 = t1_ref[j, 1:2, :]
        rows = slice(j * TAB_ROWS, (j + 1) * TAB_ROWS)
        o_ref[rows, 0:half] = (c1 * c2 - s1 * s2).astype(BF16)
        o_ref[rows, half:2 * half] = (-(s1 * c2 + c1 * s2)).astype(BF16)


def _dft_tables(seq):
    scale = 1.0 / (seq * HEAD) ** 0.5
    n_hi = seq // TAB_ROWS
    n = lax.broadcasted_iota(I32, (1, seq // 2), 1)
    kh = lax.broadcasted_iota(I32, (n_hi, 1), 0)
    a1 = ((kh * n) % n_hi).astype(F32) * (2.0 * jnp.pi / n_hi)
    t1 = jnp.stack([jnp.cos(a1), jnp.sin(a1)], axis=1)
    kl = lax.broadcasted_iota(I32, (TAB_ROWS, 1), 0)
    a2 = ((kl * n) % seq).astype(F32) * (2.0 * jnp.pi / seq)
    t2 = jnp.stack([jnp.cos(a2) * scale, jnp.sin(a2) * scale], axis=0)
    cs = pl.pallas_call(
        _dft_table_kernel,
        grid=(n_hi // TAB_GROUP,),
        in_specs=[pl.BlockSpec((TAB_GROUP, 2, seq // 2), lambda i: (i, 0, 0)),
                  pl.BlockSpec((2, TAB_ROWS, seq // 2), lambda i: (0, 0, 0))],
        out_specs=pl.BlockSpec((TAB_GROUP * TAB_ROWS, seq), lambda i: (i, 0)),
        out_shape=jax.ShapeDtypeStruct((seq, seq), BF16),
        compiler_params=_cparams(("arbitrary",)),
        name="dft_table",
    )(t1, t2)
    d = lax.broadcasted_iota(I32, (FOUR_W, FOUR_W), 0)
    q = lax.broadcasted_iota(I32, (FOUR_W, FOUR_W), 1)
    same = (d // HEAD) == (q // HEAD)
    ang64 = ((d * q) % HEAD).astype(F32) * (2.0 * jnp.pi / HEAD)
    dft64 = jnp.concatenate([jnp.where(same, jnp.cos(ang64), 0.0),
                             jnp.where(same, jnp.sin(ang64), 0.0)], axis=1).astype(BF16)
    return cs, dft64


def _routing_tables(route, counts_f):
    counts = counts_f[:, 0].astype(I32)
    pc = (counts + BLK - 1) // BLK * BLK
    pends = jnp.cumsum(pc)
    pstarts = pends - pc
    eid = lax.broadcasted_iota(I32, (N_EXP, 1), 0)

    def dest_of(e, r):
        return jnp.sum(jnp.where(e[None, :] == eid, pstarts[:, None], 0), axis=0) + r

    dest = jnp.concatenate([dest_of(route[0], route[2]), dest_of(route[1], route[3])])
    chunk_start = jnp.concatenate([pstarts, pends[-1:]]) // BLK
    n_chunk = jnp.concatenate([pc // BLK, jnp.zeros((1,), I32)])
    return dest, chunk_start, n_chunk


def kernel(x, c, ln1_g, ln2_g, w_ada, b_ada, w_in, w_out, conv_w, conv_b, conv_gn_g, conv_gn_b, sgu_ln_g,
           sgu_ln_b, sgu_w, sgu_b, router_w, router_b, exp_w1, exp_w3, exp_w2, final_g):
    bsz, seq, _ = x.shape
    depth = w_in.shape[0]
    n_tok = bsz * seq
    n_blk = (2 * n_tok + N_EXP * (BLK - 1) + BLK - 1) // BLK
    r3 = lambda a: a.reshape(depth, 1, a.shape[-1])

    mod = jnp.pad(_ada_mod(c, w_ada, b_ada).reshape(depth, bsz, 6, D), ((0, 0), (0, 0), (0, 2), (0, 0)))
    cs, dft64 = _dft_tables(seq)
    hd = lax.broadcasted_iota(I32, (CONV_W, CONV_W), 0) // HEAD
    gavg = jnp.where(hd == hd.T, 1.0 / HEAD, 0.0).astype(BF16)
    w_in_bf = w_in.astype(BF16)
    w_out_bf = w_out.astype(BF16)
    sgu_w_bf = sgu_w.astype(BF16)
    sgu_bias = jnp.repeat(jnp.swapaxes(sgu_b, 1, 2), HEAD, axis=2)
    conv_w_p = jnp.pad(conv_w, ((0, 0), (0, 1), (0, 0)))
    rwt_hi = router_w.T.astype(BF16)
    rwt_lo = (router_w.T - rwt_hi.astype(F32)).astype(BF16)
    router_wt = jnp.concatenate([rwt_hi, rwt_lo], axis=0)
    router_bc = router_b.reshape(N_EXP, 1)
    fg = final_g.reshape(1, D)

    for l in range(depth):
        yac, pq = _mixer_in(x, mod, l, r3(ln1_g), w_in_bf, conv_w_p, r3(conv_b), r3(conv_gn_g), r3(conv_gn_b),
                            r3(sgu_ln_g), r3(sgu_ln_b), sgu_w_bf, sgu_bias, dft64, gavg)
        yb = _seq_dft(cs, pq.reshape(bsz, 2 * seq, FOUR_W))
        x1, h2pa, h2pb, info, route, counts = _mixer_out(
            yac.reshape(n_tok, CONV_W + SGU_W), yb.reshape(n_tok, FOUR_W), x.reshape(n_tok, D),
            mod, l, r3(ln2_g), w_out_bf, router_wt, router_bc, bsz)
        dest, chunk_start, n_chunk = _routing_tables(route, counts)
        d0 = dest[:n_tok].reshape(1, n_tok)
        d1 = dest[n_tok:].reshape(1, n_tok)
        xpa, xpb = _sc_scatter2(h2pa, h2pb, d0, d1, n_blk * BLK)
        ypa, ypb = _experts(chunk_start, n_chunk, xpa, xpb, exp_w1, exp_w3, exp_w2, l)
        ga, gb = _sc_gather(ypa, ypb, dest.reshape(1, 2 * n_tok))
        x = _combine(ga, gb, x1, info, mod, l, fg, bsz, l == depth - 1).reshape(bsz, seq, D)
    return x
```

```python
import functools

import jax
import jax.numpy as jnp
from jax import lax
from jax.experimental import pallas as pl
from jax.experimental.pallas import tpu as pltpu
from jax.experimental.pallas import tpu_sc as plsc

F32 = jnp.float32
BF16 = jnp.bfloat16
I32 = jnp.int32
U32 = jnp.uint32
HIGHEST = lax.Precision.HIGHEST

D = 1024
HEAD = 64
CONV_W = 384
FOUR_W = 256
SGU_W = 384
SGU_HEADS = SGU_W // HEAD
Z_COLS = 2 * CONV_W + FOUR_W + 2 * SGU_W
KSIZE = 31
HALO = 16
CHUNK = 128
N_EXP = 64
N_GRP = 8
EPG = N_EXP // N_GRP
D_FF = D // 2
EPS = 1e-6
GELU_C = 0.7978845608028654
GELU_A = 0.044715

T_MIX = 1024
T_DFT = 512
T_CMB = 512
ROW_W = 256
SC_WIN = 128
BLK = 384
X_AHEAD = 3
W_AHEAD = 1
CONV_ROWS = 64
TAB_ROWS = 64
TAB_GROUP = 4
W_PIECES = 4
VMEM_LIMIT = 56 * 1024 * 1024


def _cparams(sem):
    return pltpu.CompilerParams(dimension_semantics=sem, vmem_limit_bytes=VMEM_LIMIT)


def _pack_bf16_pair(a, b):
    ua = lax.bitcast_convert_type(a.astype(BF16).astype(F32), U32) >> 16
    ub = lax.bitcast_convert_type(b.astype(BF16).astype(F32), U32) & jnp.uint32(0xFFFF0000)
    return ua | ub


def _unpack_bf16_pair(p):
    a = lax.bitcast_convert_type(p << 16, F32)
    b = lax.bitcast_convert_type(p & jnp.uint32(0xFFFF0000), F32)
    return a, b


def _ada_kernel(c_ref, *refs):
    w_refs, b_ref, o_ref = refs[:-2], refs[-2], refs[-1]
    c = c_ref[...]
    ca = c * jax.nn.sigmoid(c)
    tn = w_refs[0].shape[2]
    for j, w_ref in enumerate(w_refs):
        cols = slice(j * tn, (j + 1) * tn)
        o_ref[0, :, cols] = jnp.dot(ca, w_ref[0], precision=HIGHEST, preferred_element_type=F32) + b_ref[0, :, cols]


def _ada_mod(c, w_ada, b_ada):
    depth, _, ncol = w_ada.shape
    bsz = c.shape[0]
    n_slab = 4
    n_half = 2
    tn = ncol // (n_slab * n_half)
    return pl.pallas_call(
        _ada_kernel,
        grid=(depth, n_half),
        in_specs=[pl.BlockSpec((bsz, D), lambda l, h: (0, 0))]
        + [pl.BlockSpec((1, D, tn), functools.partial(lambda j, l, h: (l, 0, n_slab * h + j), j))
           for j in range(n_slab)]
        + [pl.BlockSpec((1, 1, n_slab * tn), lambda l, h: (l, 0, h))],
        out_specs=pl.BlockSpec((1, bsz, n_slab * tn), lambda l, h: (l, 0, h)),
        out_shape=jax.ShapeDtypeStruct((depth, bsz, ncol), F32),
        compiler_params=_cparams(("arbitrary", "arbitrary")),
        name="ada_mod",
    )(c, *([w_ada] * n_slab), b_ada.reshape(depth, 1, ncol))


def _mixer_in_kernel(xm_ref, xp_ref, xn_ref, mod_ref, g1_ref, win_ref, cw_ref, cb_ref, gng_ref, gnb_ref,
                     lng_ref, lnb_ref, sw_ref, sb_ref, dft_ref, gavg_ref,
                     yac_ref, pq_ref, glu_scr, sh_scr, conv_scr):
    i = pl.program_id(1)
    n_i = pl.num_programs(1)
    T = T_MIX
    mod = mod_ref[0, 0]
    shift1 = mod[0:1, :]
    gain1 = g1_ref[0] * (1.0 + mod[1:2, :])

    def norm_mod(x):
        ms = jnp.mean(x * x, axis=-1, keepdims=True)
        return x * lax.rsqrt(ms + EPS) * gain1 + shift1

    h = norm_mod(xm_ref[0]).astype(BF16)
    z = jnp.dot(h, win_ref[0], preferred_element_type=F32)

    hh = norm_mod(jnp.concatenate([xp_ref[0], xn_ref[0]], axis=0)).astype(BF16)
    zh = jnp.dot(hh, win_ref[0, :, 0:2 * CONV_W], preferred_element_type=F32)
    glu_h = zh[:, 0:CONV_W] * jax.nn.sigmoid(zh[:, CONV_W:2 * CONV_W])
    glu_scr[0:HALO, :] = jnp.where(i > 0, glu_h[0:HALO], 0.0)
    glu_scr[HALO + T:2 * HALO + T, :] = jnp.where(i < n_i - 1, glu_h[HALO:2 * HALO], 0.0)
    glu_scr[HALO:HALO + T, :] = z[:, 0:CONV_W] * jax.nn.sigmoid(z[:, CONV_W:2 * CONV_W])

    off = HALO - KSIZE // 2
    for b in range(8):
        sh_scr[b] = glu_scr[b:b + T + 3 * 8, :]

    for c in range(T // CONV_ROWS):
        r0 = c * CONV_ROWS
        acc = jnp.broadcast_to(cb_ref[0], (CONV_ROWS, CONV_W))
        for k in range(KSIZE):
            a, b = divmod(k + off, 8)
            acc = acc + sh_scr[b, r0 + 8 * a:r0 + 8 * a + CONV_ROWS, :] * cw_ref[0, k:k + 1, :]
        conv_scr[r0:r0 + CONV_ROWS, :] = acc
    hc = conv_scr[...]
    gavg = gavg_ref[...]
    mu = jnp.dot(hc.astype(BF16), gavg, preferred_element_type=F32)
    dc = hc - mu
    var = jnp.dot((dc * dc).astype(BF16), gavg, preferred_element_type=F32)
    hn = dc * lax.rsqrt(var + EPS) * gng_ref[0] + gnb_ref[0]
    ya = hn * jax.nn.sigmoid(hn)
    yac_ref[0, :, 0:CONV_W] = ya.astype(BF16)

    zb = z[:, 2 * CONV_W:2 * CONV_W + FOUR_W].astype(BF16)
    pq = jnp.dot(zb, dft_ref[...], preferred_element_type=F32)
    pq_ref[0, 0] = pq[:, 0:FOUR_W].astype(BF16)
    pq_ref[0, 1] = pq[:, FOUR_W:2 * FOUR_W].astype(BF16)

    c0 = 2 * CONV_W + FOUR_W
    zc = z[:, c0:c0 + 2 * SGU_W]
    zc = 0.5 * zc * (1.0 + jnp.tanh(GELU_C * (zc + GELU_A * (zc * zc * zc))))
    u = zc[:, 0:SGU_W]
    v = zc[:, SGU_W:2 * SGU_W]
    vm = jnp.mean(v, axis=-1, keepdims=True)
    vd = v - vm
    vv = jnp.mean(vd * vd, axis=-1, keepdims=True)
    vn = (vd * lax.rsqrt(vv + EPS) * lng_ref[0] + lnb_ref[0]).astype(BF16)
    n_chunk = T // CHUNK
    lane = lax.broadcasted_iota(I32, (CHUNK, 2 * HEAD), 1)
    for pr in range(SGU_HEADS // 2):
        cols = slice(2 * HEAD * pr, 2 * HEAD * (pr + 1))
        rhs = jnp.concatenate([vn[n * CHUNK:(n + 1) * CHUNK, cols] for n in range(n_chunk)], axis=1)
        lo = jnp.dot(sw_ref[0, 2 * pr], rhs, preferred_element_type=F32)
        hi = jnp.dot(sw_ref[0, 2 * pr + 1], rhs, preferred_element_type=F32)
        for n in range(n_chunk):
            sl = slice(n * 2 * HEAD, (n + 1) * 2 * HEAD)
            vs = jnp.where(lane < HEAD, lo[:, sl], hi[:, sl]) + sb_ref[0, :, cols]
            rows = slice(n * CHUNK, (n + 1) * CHUNK)
            yac_ref[0, rows, CONV_W + 2 * HEAD * pr:CONV_W + 2 * HEAD * (pr + 1)] = (u[rows, cols] * vs).astype(BF16)


def _mixer_in(x, mod, l, ln1_g, w_in_bf, conv_w, conv_b, gn_g, gn_b, ln_g, ln_b, sgu_w_bf, sgu_bias, dft64, gavg):
    bsz, seq, _ = x.shape
    T = T_MIX
    n_i = seq // T
    hb = T // HALO
    n_h = seq // HALO
    vec = lambda w: pl.BlockSpec((1, 1, w), lambda b, i: (l, 0, 0))
    return pl.pallas_call(
        _mixer_in_kernel,
        grid=(bsz, n_i),
        in_specs=[
            pl.BlockSpec((1, T, D), lambda b, i: (b, i, 0)),
            pl.BlockSpec((1, HALO, D), lambda b, i: (b, jnp.maximum(i * hb - 1, 0), 0)),
            pl.BlockSpec((1, HALO, D), lambda b, i: (b, jnp.minimum((i + 1) * hb, n_h - 1), 0)),
            pl.BlockSpec((1, 1, 8, D), lambda b, i: (l, b, 0, 0)),
            vec(D),
            pl.BlockSpec((1, D, Z_COLS), lambda b, i: (l, 0, 0)),
            pl.BlockSpec((1, KSIZE + 1, CONV_W), lambda b, i: (l, 0, 0)),
            vec(CONV_W), vec(CONV_W), vec(CONV_W), vec(SGU_W), vec(SGU_W),
            pl.BlockSpec((1, SGU_HEADS, CHUNK, CHUNK), lambda b, i: (l, 0, 0, 0)),
            pl.BlockSpec((1, CHUNK, SGU_W), lambda b, i: (l, 0, 0)),
            pl.BlockSpec((FOUR_W, 2 * FOUR_W), lambda b, i: (0, 0)),
            pl.BlockSpec((CONV_W, CONV_W), lambda b, i: (0, 0)),
        ],
        out_specs=[
            pl.BlockSpec((1, T, CONV_W + SGU_W), lambda b, i: (b, i, 0)),
            pl.BlockSpec((1, 2, T, FOUR_W), lambda b, i: (b, 0, i, 0)),
        ],
        out_shape=[
            jax.ShapeDtypeStruct((bsz, seq, CONV_W + SGU_W), BF16),
            jax.ShapeDtypeStruct((bsz, 2, seq, FOUR_W), BF16),
        ],
        scratch_shapes=[
            pltpu.VMEM((T + 2 * HALO, CONV_W), F32),
            pltpu.VMEM((8, T + 3 * 8, CONV_W), F32),
            pltpu.VMEM((T, CONV_W), F32),
        ],
        compiler_params=_cparams(("arbitrary", "arbitrary")),
        name="mixer_in",
    )(x, x, x, mod, ln1_g, w_in_bf, conv_w, conv_b, gn_g, gn_b, ln_g, ln_b, sgu_w_bf, sgu_bias, dft64, gavg)


def _dft_fold_kernel(scale, pq_ref, fold_ref, ph_ref):
    seq = pq_ref.shape[1] // 2
    half = seq // 2
    nb = seq // CHUNK
    rr = lax.broadcasted_iota(I32, (CHUNK, CHUNK), 0)
    cc = lax.broadcasted_iota(I32, (CHUNK, CHUNK), 1)
    rev = jnp.where((rr >= 1) & (cc == CHUNK - rr), 1.0, 0.0).astype(BF16)
    row0 = lax.broadcasted_iota(I32, (CHUNK, FOUR_W), 0) == 0
    for part, sign in ((0, 1.0), (1, -1.0)):
        base = part * seq
        for m in range(half // CHUNK):
            lo = pq_ref[0, base + CHUNK * m:base + CHUNK * (m + 1), :].astype(F32)
            up = pq_ref[0, base + CHUNK * (nb - 1 - m):base + CHUNK * (nb - m), :]
            mirrored = jnp.dot(rev, up, preferred_element_type=F32)
            if m >= 1:
                first = pq_ref[0, base + CHUNK * (nb - m):base + CHUNK * (nb - m) + 1, :].astype(F32)
                mirrored = jnp.where(row0, first, mirrored)
            fold_ref[0, part * half + CHUNK * m:part * half + CHUNK * (m + 1), :] = (lo + sign * mirrored).astype(BF16)
    ph_ref[0] = jnp.broadcast_to(pq_ref[0, half:half + 1, :].astype(F32) * scale, (8, FOUR_W))


def _seq_dft_kernel(cs_ref, fold_ref, ph_ref, o_ref):
    alt = jnp.where(lax.broadcasted_iota(I32, (T_DFT, 1), 0) % 2 == 0, 1.0, -1.0)
    o = jnp.dot(cs_ref[...], fold_ref[0], preferred_element_type=F32) + alt * ph_ref[0, 0:1, :]
    o_ref[0] = o.astype(BF16)


def _seq_dft(cs, pq):
    bsz, two_s, _ = pq.shape
    seq = two_s // 2
    scale = 1.0 / (seq * HEAD) ** 0.5
    fold, ph = pl.pallas_call(
        functools.partial(_dft_fold_kernel, scale),
        grid=(bsz,),
        in_specs=[pl.BlockSpec((1, two_s, FOUR_W), lambda b: (b, 0, 0))],
        out_specs=[pl.BlockSpec((1, seq, FOUR_W), lambda b: (b, 0, 0)),
                   pl.BlockSpec((1, 8, FOUR_W), lambda b: (b, 0, 0))],
        out_shape=[jax.ShapeDtypeStruct((bsz, seq, FOUR_W), BF16),
                   jax.ShapeDtypeStruct((bsz, 8, FOUR_W), F32)],
        compiler_params=_cparams(("arbitrary",)),
        name="dft_fold",
    )(pq)
    return pl.pallas_call(
        _seq_dft_kernel,
        grid=(seq // T_DFT, bsz),
        in_specs=[pl.BlockSpec((T_DFT, seq), lambda k, b: (k, 0)),
                  pl.BlockSpec((1, seq, FOUR_W), lambda k, b: (b, 0, 0)),
                  pl.BlockSpec((1, 8, FOUR_W), lambda k, b: (b, 0, 0))],
        out_specs=pl.BlockSpec((1, T_DFT, FOUR_W), lambda k, b: (b, k, 0)),
        out_shape=jax.ShapeDtypeStruct((bsz, seq, FOUR_W), BF16),
        compiler_params=_cparams(("arbitrary", "arbitrary")),
        name="seq_dft",
    )(cs, fold, ph)


def _mixer_out_kernel(yac_ref, yb_ref, x_ref, mod_ref, g2_ref, wout_ref, rwt_ref, rb_ref, upper_ref,
                      x1_ref, h2pa_ref, h2pb_ref, info_ref, route_ref, cnt_ref, cnt_scr):
    i = pl.program_id(0)
    T = T_MIX

    @pl.when(i == 0)
    def _():
        cnt_scr[...] = jnp.zeros_like(cnt_scr)

    mod = mod_ref[0, 0]
    gate1 = mod[2:3, :]
    shift2 = mod[3:4, :]
    gain2 = g2_ref[0] * (1.0 + mod[4:5, :])
    yac = yac_ref[...]
    ycat = jnp.concatenate([yac[:, 0:CONV_W], yb_ref[...], yac[:, CONV_W:CONV_W + SGU_W]], axis=1)
    o = jnp.dot(ycat, wout_ref[0], preferred_element_type=F32)
    x1 = x_ref[...] + gate1 * o
    x1_ref[...] = x1
    ms = jnp.mean(x1 * x1, axis=-1, keepdims=True)
    h2 = x1 * lax.rsqrt(ms + EPS) * gain2 + shift2
    h2p = _pack_bf16_pair(h2[:, 0:D // 2], h2[:, D // 2:D])
    h2pa_ref[...] = h2p[:, 0:ROW_W]
    h2pb_ref[...] = h2p[:, ROW_W:2 * ROW_W]

    h_hi = h2.astype(BF16)
    h_lo = (h2 - h_hi.astype(F32)).astype(BF16)
    nt = (((1,), (1,)), ((), ()))
    part = lax.dot_general(rwt_ref[...], h_hi, nt, preferred_element_type=F32)
    logits = (part[0:N_EXP] + part[N_EXP:2 * N_EXP]
              + lax.dot_general(rwt_ref[0:N_EXP, :], h_lo, nt, preferred_element_type=F32))
    mx = jnp.max(logits, axis=0, keepdims=True)
    ex = jnp.exp(logits - mx)
    probs = ex / jnp.sum(ex, axis=0, keepdims=True)
    sel = probs + rb_ref[...]
    sel3 = sel.reshape(N_GRP, EPG, T)
    probs3 = probs.reshape(N_GRP, EPG, T)
    jj = lax.broadcasted_iota(I32, (N_GRP, EPG, T), 1)
    m1 = jnp.max(sel3, axis=1, keepdims=True)
    i1 = jnp.min(jnp.where(sel3 == m1, jj, EPG), axis=1, keepdims=True)
    rest = jnp.where(jj == i1, -jnp.inf, sel3)
    m2 = jnp.max(rest, axis=1, keepdims=True)
    i2 = jnp.min(jnp.where(rest == m2, jj, EPG), axis=1, keepdims=True)
    gscore = m1 + m2
    gg = lax.broadcasted_iota(I32, (N_GRP, 1, T), 0)
    gmax = jnp.max(gscore, axis=0, keepdims=True)
    gidx = jnp.min(jnp.where(gscore == gmax, gg, N_GRP), axis=0, keepdims=True)
    ing = gg == gidx
    pick = lambda a, zero: jnp.sum(jnp.where(ing, a, zero), axis=0)
    p1 = jnp.sum(jnp.where(jj == i1, probs3, 0.0), axis=1, keepdims=True)
    p2 = jnp.sum(jnp.where(jj == i2, probs3, 0.0), axis=1, keepdims=True)
    pa = pick(p1, 0.0)
    pb = pick(p2, 0.0)
    gbase = gidx[0] * EPG
    e0 = gbase + pick(i1, 0)
    e1 = gbase + pick(i2, 0)
    den = pa + pb
    gw0 = pa / den
    gw1 = pb / den

    ee = lax.broadcasted_iota(I32, (N_EXP, T), 0)
    oh0 = ee == e0
    oh1 = ee == e1
    amat = jnp.where(oh0 | oh1, 1.0, 0.0)
    before = jnp.dot(amat.astype(BF16), upper_ref[...], preferred_element_type=F32) + cnt_scr[...]
    r0 = jnp.sum(jnp.where(oh0, before, 0.0), axis=0, keepdims=True)
    r1 = jnp.sum(jnp.where(oh1, before, 0.0), axis=0, keepdims=True)
    cnt_scr[...] = cnt_scr[...] + jnp.sum(amat, axis=1, keepdims=True)
    cnt_ref[...] = cnt_scr[...]

    rid = lax.broadcasted_iota(I32, (8, T), 0)
    route = jnp.zeros((8, T), I32)
    for k, val in enumerate((e0, e1, r0.astype(I32), r1.astype(I32))):
        route = jnp.where(rid == k, val, route)
    route_ref[...] = route
    gws = jnp.where(rid == 0, gw0, jnp.where(rid == 1, gw1, 0.0))
    gws = jnp.concatenate([gws, jnp.zeros((CHUNK - 8, T), F32)], axis=0)
    info_ref[...] = gws.T


def _mixer_out(yac, yb, x, mod, l, ln2_g, w_out_bf, router_wt, router_b, bsz):
    n_tok = x.shape[0]
    T = T_MIX
    per_b = n_tok // bsz // T
    row = lambda w: pl.BlockSpec((T, w), lambda i: (i, 0))
    upper = (lax.broadcasted_iota(I32, (T, T), 0) < lax.broadcasted_iota(I32, (T, T), 1)).astype(BF16)
    return pl.pallas_call(
        _mixer_out_kernel,
        grid=(n_tok // T,),
        in_specs=[
            row(CONV_W + SGU_W), row(FOUR_W), row(D),
            pl.BlockSpec((1, 1, 8, D), lambda i: (l, i // per_b, 0, 0)),
            pl.BlockSpec((1, 1, D), lambda i: (l, 0, 0)),
            pl.BlockSpec((1, D, D), lambda i: (l, 0, 0)),
            pl.BlockSpec((2 * N_EXP, D), lambda i: (0, 0)),
            pl.BlockSpec((N_EXP, 1), lambda i: (0, 0)),
            pl.BlockSpec((T, T), lambda i: (0, 0)),
        ],
        out_specs=[row(D), row(ROW_W), row(ROW_W), row(CHUNK), pl.BlockSpec((8, T), lambda i: (0, i)),
                   pl.BlockSpec((N_EXP, 1), lambda i: (0, 0))],
        out_shape=[
            jax.ShapeDtypeStruct((n_tok, D), F32),
            jax.ShapeDtypeStruct((n_tok, ROW_W), U32),
            jax.ShapeDtypeStruct((n_tok, ROW_W), U32),
            jax.ShapeDtypeStruct((n_tok, CHUNK), F32),
            jax.ShapeDtypeStruct((8, n_tok), I32),
            jax.ShapeDtypeStruct((N_EXP, 1), F32),
        ],
        scratch_shapes=[pltpu.VMEM((N_EXP, 1), F32)],
        compiler_params=_cparams(("arbitrary",)),
        name="mixer_out",
    )(yac, yb, x, mod, ln2_g, w_out_bf, router_wt, router_b, upper)


def _sc_mesh():
    return plsc.VectorSubcoreMesh(core_axis_name="c", subcore_axis_name="s")


def _sc_scatter2(src_a, src_b, idx0, idx1, n_rows):
    n = src_a.shape[0]
    out = jax.ShapeDtypeStruct((n_rows, ROW_W), src_a.dtype)

    @functools.partial(pl.kernel, out_type=[out, out], mesh=_sc_mesh())
    def scatter(xa_hbm, xb_hbm, i0_hbm, i1_hbm, oa_hbm, ob_hbm):
        for x_hbm, o_hbm in ((xa_hbm, oa_hbm), (xb_hbm, ob_hbm)):
            def body(x_vmem, i0_vmem, i1_vmem, o_hbm=o_hbm):
                pltpu.sync_copy(x_vmem, o_hbm.at[i0_vmem.at[0]])
                pltpu.sync_copy(x_vmem, o_hbm.at[i1_vmem.at[0]])

            pltpu.emit_pipeline(
                body, grid=(n // SC_WIN,),
                in_specs=[pl.BlockSpec((SC_WIN, ROW_W), index_map=lambda i: (i, 0)),
                          pl.BlockSpec((1, SC_WIN), index_map=lambda i: (0, i)),
                          pl.BlockSpec((1, SC_WIN), index_map=lambda i: (0, i))],
                out_specs=[],
                core_axis_name=("c", "s"), dimension_semantics=(pltpu.PARALLEL,),
            )(x_hbm, i0_hbm, i1_hbm)

    return scatter(src_a, src_b, idx0, idx1)


def _sc_gather(src_a, src_b, idx):
    m = idx.shape[1]
    out = jax.ShapeDtypeStruct((m, ROW_W), src_a.dtype)

    @functools.partial(pl.kernel, out_type=[out, out], mesh=_sc_mesh())
    def gather(xa_hbm, xb_hbm, i_hbm, oa_hbm, ob_hbm):
        for x_hbm, o_hbm in ((xa_hbm, oa_hbm), (xb_hbm, ob_hbm)):
            def body(i_vmem, o_vmem, x_hbm=x_hbm):
                pltpu.sync_copy(x_hbm.at[i_vmem.at[0]], o_vmem)

            pltpu.emit_pipeline(
                body, grid=(m // SC_WIN,),
                in_specs=[pl.BlockSpec((1, SC_WIN), index_map=lambda i: (0, i))],
                out_specs=[pl.BlockSpec((SC_WIN, ROW_W), index_map=lambda i: (i, 0))],
                core_axis_name=("c", "s"), dimension_semantics=(pltpu.PARALLEL,),
            )(i_hbm, o_hbm)

    return gather(src_a, src_b, idx)


def _experts_kernel(l, n_blk, start_ref, nchunk_ref, xpa_ref, xpb_ref, w1_ref, w3_ref, w2_ref, ypa_ref, ypb_ref,
                    w13_scr, w2_scr, wbuf13, wbuf2, xbuf, ybuf, wsem, xsem, ysem):
    e = pl.program_id(0)
    nc = nchunk_ref[e]
    chunk0 = start_ref[e]
    n_used = start_ref[N_EXP]

    def w_copies(ex):
        slot = ex % (W_AHEAD + 1)
        cps = []
        for p in range(W_PIECES):
            r13 = pl.ds(p * (D // W_PIECES), D // W_PIECES)
            r2 = pl.ds(p * (D_FF // W_PIECES), D_FF // W_PIECES)
            cps.append(pltpu.make_async_copy(w1_ref.at[l, ex, r13], wbuf13.at[slot, 0, r13], wsem.at[slot]))
            cps.append(pltpu.make_async_copy(w3_ref.at[l, ex, r13], wbuf13.at[slot, 1, r13], wsem.at[slot]))
            cps.append(pltpu.make_async_copy(w2_ref.at[l, ex, r2], wbuf2.at[slot, r2], wsem.at[slot]))
        return cps

    def x_copies(g):
        rows = pl.ds(pl.multiple_of(g * BLK, BLK), BLK)
        slot = g % (X_AHEAD + 1)
        return [pltpu.make_async_copy(src.at[rows], xbuf.at[slot, h], xsem.at[slot])
                for h, src in enumerate((xpa_ref, xpb_ref))]

    def y_copies(g):
        rows = pl.ds(pl.multiple_of(g * BLK, BLK), BLK)
        slot = g % 2
        return [pltpu.make_async_copy(ybuf.at[slot, h], dst.at[rows], ysem.at[slot])
                for h, dst in enumerate((ypa_ref, ypb_ref))]

    def start(cps):
        for cp in cps:
            cp.start()

    def wait(cps):
        for cp in cps:
            cp.wait()

    @pl.when(e == 0)
    def _():
        for j in range(X_AHEAD):
            @pl.when(j < n_used)
            def _():
                start(x_copies(j))

        for j in range(W_AHEAD):
            start(w_copies(j))

    @pl.when(e < N_EXP)
    def _():
        wslot = e % (W_AHEAD + 1)
        wait(w_copies(e))

        @pl.when(e + W_AHEAD < N_EXP)
        def _():
            start(w_copies(e + W_AHEAD))

        @pl.when(nc > 0)
        def _():
            w13_scr[:, 0:D_FF] = wbuf13[wslot, 0].astype(BF16)
            w13_scr[:, D_FF:2 * D_FF] = wbuf13[wslot, 1].astype(BF16)
            w2_scr[...] = wbuf2[wslot].astype(BF16)

            def chunk(c, carry):
                g = chunk0 + c
                slot = g % (X_AHEAD + 1)

                @pl.when(g + X_AHEAD < n_used)
                def _():
                    start(x_copies(g + X_AHEAD))

                wait(x_copies(g))

                @pl.when(g >= 2)
                def _():
                    wait(y_copies(g - 2))

                a, b = _unpack_bf16_pair(jnp.concatenate([xbuf[slot, 0], xbuf[slot, 1]], axis=1))
                x = jnp.concatenate([a.astype(BF16), b.astype(BF16)], axis=1)
                h13 = jnp.dot(x, w13_scr[...], preferred_element_type=F32)
                h1 = h13[:, 0:D_FF]
                act = (h1 * jax.nn.sigmoid(h1) * h13[:, D_FF:2 * D_FF]).astype(BF16)
                y = jnp.dot(act, w2_scr[...], preferred_element_type=F32)
                yp = _pack_bf16_pair(y[:, 0:D // 2], y[:, D // 2:D])
                ybuf[g % 2, 0] = yp[:, 0:ROW_W]
                ybuf[g % 2, 1] = yp[:, ROW_W:2 * ROW_W]
                start(y_copies(g))
                return carry

            lax.fori_loop(0, nc, chunk, 0)

    @pl.when(e == N_EXP)
    def _():
        @pl.when(n_used >= 2)
        def _():
            wait(y_copies(n_used - 2))

        wait(y_copies(n_used - 1))
        ybuf[0] = jnp.zeros((2, BLK, ROW_W), U32)

        def fill_one(g, carry):
            rows = pl.ds(pl.multiple_of(g * BLK, BLK), BLK)
            cps = [pltpu.make_async_copy(ybuf.at[0, h], dst.at[rows], ysem.at[0])
                   for h, dst in enumerate((ypa_ref, ypb_ref))]
            start(cps)
            wait(cps)
            return carry

        lax.fori_loop(n_used, n_blk, fill_one, 0)


def _experts(chunk_start, n_chunk, xpa, xpb, w1, w3, w2, l):
    n_rows = xpa.shape[0]
    n_blk = n_rows // BLK
    hbm = pl.BlockSpec(memory_space=pl.ANY)
    half = jax.ShapeDtypeStruct((n_rows, ROW_W), U32)
    return pl.pallas_call(
        functools.partial(_experts_kernel, l, n_blk),
        grid_spec=pltpu.PrefetchScalarGridSpec(
            num_scalar_prefetch=2,
            grid=(N_EXP + 1,),
            in_specs=[hbm, hbm, hbm, hbm, hbm],
            out_specs=[hbm, hbm],
            scratch_shapes=[pltpu.VMEM((D, 2 * D_FF), BF16), pltpu.VMEM((D_FF, D), BF16),
                            pltpu.VMEM((W_AHEAD + 1, 2, D, D_FF), F32), pltpu.VMEM((W_AHEAD + 1, D_FF, D), F32),
                            pltpu.VMEM((X_AHEAD + 1, 2, BLK, ROW_W), U32), pltpu.VMEM((2, 2, BLK, ROW_W), U32),
                            pltpu.SemaphoreType.DMA((W_AHEAD + 1,)), pltpu.SemaphoreType.DMA((X_AHEAD + 1,)),
                            pltpu.SemaphoreType.DMA((2,))],
        ),
        out_shape=[half, half],
        compiler_params=_cparams(("arbitrary",)),
        name="experts",
    )(chunk_start, n_chunk, xpa, xpb, w1, w3, w2)


def _combine_kernel(final, ga0_ref, gb0_ref, ga1_ref, gb1_ref, x1_ref, info_ref, mod_ref, fg_ref, o_ref):
    info = info_ref[...]
    gw0 = info[:, 0:1]
    gw1 = info[:, 1:2]
    a0, b0 = _unpack_bf16_pair(jnp.concatenate([ga0_ref[...], gb0_ref[...]], axis=1))
    a1, b1 = _unpack_bf16_pair(jnp.concatenate([ga1_ref[...], gb1_ref[...]], axis=1))
    y = jnp.concatenate([gw0 * a0 + gw1 * a1, gw0 * b0 + gw1 * b1], axis=1)
    gate2 = mod_ref[0, 0][5:6, :]
    x2 = x1_ref[...] + gate2 * y
    if final:
        ms = jnp.mean(x2 * x2, axis=-1, keepdims=True)
        x2 = x2 * lax.rsqrt(ms + EPS) * fg_ref[...]
    o_ref[...] = x2


def _combine(ga, gb, x1, info, mod, l, final_g, bsz, final):
    n_tok = x1.shape[0]
    n_i = n_tok // T_CMB
    per_b = n_i // bsz
    first = pl.BlockSpec((T_CMB, ROW_W), lambda i: (i, 0))
    second = pl.BlockSpec((T_CMB, ROW_W), lambda i: (i + n_i, 0))
    return pl.pallas_call(
        functools.partial(_combine_kernel, final),
        grid=(n_i,),
        in_specs=[
            first, first, second, second,
            pl.BlockSpec((T_CMB, D), lambda i: (i, 0)),
            pl.BlockSpec((T_CMB, CHUNK), lambda i: (i, 0)),
            pl.BlockSpec((1, 1, 8, D), lambda i: (l, i // per_b, 0, 0)),
            pl.BlockSpec((1, D), lambda i: (0, 0)),
        ],
        out_specs=pl.BlockSpec((T_CMB, D), lambda i: (i, 0)),
        out_shape=jax.ShapeDtypeStruct((n_tok, D), F32),
        compiler_params=_cparams(("arbitrary",)),
        name="combine",
    )(ga, gb, ga, gb, x1, info, mod, final_g)


def _dft_table_kernel(t1_ref, t2_ref, o_ref):
    half = t2_ref.shape[2]
    c2 = t2_ref[0]
    s2 = t2_ref[1]
    for j in range(TAB_GROUP):
        c1 = t1_ref[j, 0:1, :]
        s1 = t1_ref[j, 1:2, :]
        rows = slice(j * TAB_ROWS, (j + 1) * TAB_ROWS)
        o_ref[rows, 0:half] = (c1 * c2 - s1 * s2).astype(BF16)
        o_ref[rows, half:2 * half] = (-(s1 * c2 + c1 * s2)).astype(BF16)


def _dft_tables(seq):
    scale = 1.0 / (seq * HEAD) ** 0.5
    n_hi = seq // TAB_ROWS
    n = lax.broadcasted_iota(I32, (1, seq // 2), 1)
    kh = lax.broadcasted_iota(I32, (n_hi, 1), 0)
    a1 = ((kh * n) % n_hi).astype(F32) * (2.0 * jnp.pi / n_hi)
    t1 = jnp.stack([jnp.cos(a1), jnp.sin(a1)], axis=1)
    kl = lax.broadcasted_iota(I32, (TAB_ROWS, 1), 0)
    a2 = ((kl * n) % seq).astype(F32) * (2.0 * jnp.pi / seq)
    t2 = jnp.stack([jnp.cos(a2) * scale, jnp.sin(a2) * scale], axis=0)
    cs = pl.pallas_call(
        _dft_table_kernel,
        grid=(n_hi // TAB_GROUP,),
        in_specs=[pl.BlockSpec((TAB_GROUP, 2, seq // 2), lambda i: (i, 0, 0)),
                  pl.BlockSpec((2, TAB_ROWS, seq // 2), lambda i: (0, 0, 0))],
        out_specs=pl.BlockSpec((TAB_GROUP * TAB_ROWS, seq), lambda i: (i, 0)),
        out_shape=jax.ShapeDtypeStruct((seq, seq), BF16),
        compiler_params=_cparams(("arbitrary",)),
        name="dft_table",
    )(t1, t2)
    d = lax.broadcasted_iota(I32, (FOUR_W, FOUR_W), 0)
    q = lax.broadcasted_iota(I32, (FOUR_W, FOUR_W), 1)
    same = (d // HEAD) == (q // HEAD)
    ang64 = ((d * q) % HEAD).astype(F32) * (2.0 * jnp.pi / HEAD)
    dft64 = jnp.concatenate([jnp.where(same, jnp.cos(ang64), 0.0),
                             jnp.where(same, jnp.sin(ang64), 0.0)], axis=1).astype(BF16)
    return cs, dft64


def _routing_tables(route, counts_f):
    counts = counts_f[:, 0].astype(I32)
    pc = (counts + BLK - 1) // BLK * BLK
    pends = jnp.cumsum(pc)
    pstarts = pends - pc
    eid = lax.broadcasted_iota(I32, (N_EXP, 1), 0)

    def dest_of(e, r):
        return jnp.sum(jnp.where(e[None, :] == eid, pstarts[:, None], 0), axis=0) + r

    dest = jnp.concatenate([dest_of(route[0], route[2]), dest_of(route[1], route[3])])
    chunk_start = jnp.concatenate([pstarts, pends[-1:]]) // BLK
    n_chunk = jnp.concatenate([pc // BLK, jnp.zeros((1,), I32)])
    return dest, chunk_start, n_chunk


def kernel(x, c, ln1_g, ln2_g, w_ada, b_ada, w_in, w_out, conv_w, conv_b, conv_gn_g, conv_gn_b, sgu_ln_g,
           sgu_ln_b, sgu_w, sgu_b, router_w, router_b, exp_w1, exp_w3, exp_w2, final_g):
    bsz, seq, _ = x.shape
    depth = w_in.shape[0]
    n_tok = bsz * seq
    n_blk = (2 * n_tok + N_EXP * (BLK - 1) + BLK - 1) // BLK
    r3 = lambda a: a.reshape(depth, 1, a.shape[-1])

    mod = jnp.pad(_ada_mod(c, w_ada, b_ada).reshape(depth, bsz, 6, D), ((0, 0), (0, 0), (0, 2), (0, 0)))
    cs, dft64 = _dft_tables(seq)
    hd = lax.broadcasted_iota(I32, (CONV_W, CONV_W), 0) // HEAD
    gavg = jnp.where(hd == hd.T, 1.0 / HEAD, 0.0).astype(BF16)
    w_in_bf = w_in.astype(BF16)
    w_out_bf = w_out.astype(BF16)
    sgu_w_bf = sgu_w.astype(BF16)
    sgu_bias = jnp.repeat(jnp.swapaxes(sgu_b, 1, 2), HEAD, axis=2)
    conv_w_p = jnp.pad(conv_w, ((0, 0), (0, 1), (0, 0)))
    rwt_hi = router_w.T.astype(BF16)
    rwt_lo = (router_w.T - rwt_hi.astype(F32)).astype(BF16)
    router_wt = jnp.concatenate([rwt_hi, rwt_lo], axis=0)
    router_bc = router_b.reshape(N_EXP, 1)
    fg = final_g.reshape(1, D)

    for l in range(depth):
        yac, pq = _mixer_in(x, mod, l, r3(ln1_g), w_in_bf, conv_w_p, r3(conv_b), r3(conv_gn_g), r3(conv_gn_b),
                            r3(sgu_ln_g), r3(sgu_ln_b), sgu_w_bf, sgu_bias, dft64, gavg)
        yb = _seq_dft(cs, pq.reshape(bsz, 2 * seq, FOUR_W))
        x1, h2pa, h2pb, info, route, counts = _mixer_out(
            yac.reshape(n_tok, CONV_W + SGU_W), yb.reshape(n_tok, FOUR_W), x.reshape(n_tok, D),
            mod, l, r3(ln2_g), w_out_bf, router_wt, router_bc, bsz)
        dest, chunk_start, n_chunk = _routing_tables(route, counts)
        d0 = dest[:n_tok].reshape(1, n_tok)
        d1 = dest[n_tok:].reshape(1, n_tok)
        xpa, xpb = _sc_scatter2(h2pa, h2pb, d0, d1, n_blk * BLK)
        ypa, ypb = _experts(chunk_start, n_chunk, xpa, xpb, exp_w1, exp_w3, exp_w2, l)
        ga, gb = _sc_gather(ypa, ypb, dest.reshape(1, 2 * n_tok))
        x = _combine(ga, gb, x1, info, mod, l, fg, bsz, l == depth - 1).reshape(bsz, seq, D)
    return x
```

```python
import functools

import jax
import jax.numpy as jnp
from jax import lax
from jax.experimental import pallas as pl
from jax.experimental.pallas import tpu as pltpu
from jax.experimental.pallas import tpu_sc as plsc

F32 = jnp.float32
BF16 = jnp.bfloat16
I32 = jnp.int32
U32 = jnp.uint32
HIGHEST = lax.Precision.HIGHEST

D = 1024
HEAD = 64
CONV_W = 384
FOUR_W = 256
SGU_W = 384
SGU_HEADS = SGU_W // HEAD
Z_COLS = 2 * CONV_W + FOUR_W + 2 * SGU_W
KSIZE = 31
HALO = 16
CHUNK = 128
N_EXP = 64
N_GRP = 8
EPG = N_EXP // N_GRP
D_FF = D // 2
EPS = 1e-6
GELU_C = 0.7978845608028654
GELU_A = 0.044715

T_MIX = 1024
T_DFT = 512
T_CMB = 512
ROW_W = 256
SC_WIN = 128
BLK = 384
X_AHEAD = 3
W_AHEAD = 1
CONV_ROWS = 64
TAB_ROWS = 64
TAB_GROUP = 4
W_PIECES = 4
VMEM_LIMIT = 56 * 1024 * 1024


def _cparams(sem):
    return pltpu.CompilerParams(dimension_semantics=sem, vmem_limit_bytes=VMEM_LIMIT)


def _pack_bf16_pair(a, b):
    ua = lax.bitcast_convert_type(a.astype(BF16).astype(F32), U32) >> 16
    ub = lax.bitcast_convert_type(b.astype(BF16).astype(F32), U32) & jnp.uint32(0xFFFF0000)
    return ua | ub


def _unpack_bf16_pair(p):
    a = lax.bitcast_convert_type(p << 16, F32)
    b = lax.bitcast_convert_type(p & jnp.uint32(0xFFFF0000), F32)
    return a, b


def _ada_kernel(c_ref, *refs):
    w_refs, b_ref, o_ref = refs[:-2], refs[-2], refs[-1]
    c = c_ref[...]
    ca = c * jax.nn.sigmoid(c)
    tn = w_refs[0].shape[2]
    for j, w_ref in enumerate(w_refs):
        cols = slice(j * tn, (j + 1) * tn)
        o_ref[0, :, cols] = jnp.dot(ca, w_ref[0], precision=HIGHEST, preferred_element_type=F32) + b_ref[0, :, cols]


def _ada_mod(c, w_ada, b_ada):
    depth, _, ncol = w_ada.shape
    bsz = c.shape[0]
    n_slab = 4
    n_half = 2
    tn = ncol // (n_slab * n_half)
    return pl.pallas_call(
        _ada_kernel,
        grid=(depth, n_half),
        in_specs=[pl.BlockSpec((bsz, D), lambda l, h: (0, 0))]
        + [pl.BlockSpec((1, D, tn), functools.partial(lambda j, l, h: (l, 0, n_slab * h + j), j))
           for j in range(n_slab)]
        + [pl.BlockSpec((1, 1, n_slab * tn), lambda l, h: (l, 0, h))],
        out_specs=pl.BlockSpec((1, bsz, n_slab * tn), lambda l, h: (l, 0, h)),
        out_shape=jax.ShapeDtypeStruct((depth, bsz, ncol), F32),
        compiler_params=_cparams(("arbitrary", "arbitrary")),
        name="ada_mod",
    )(c, *([w_ada] * n_slab), b_ada.reshape(depth, 1, ncol))


def _mixer_in_kernel(xm_ref, xp_ref, xn_ref, mod_ref, g1_ref, win_ref, cw_ref, cb_ref, gng_ref, gnb_ref,
                     lng_ref, lnb_ref, sw_ref, sb_ref, dft_ref, gavg_ref,
                     yac_ref, pq_ref, glu_scr, sh_scr, conv_scr):
    i = pl.program_id(1)
    n_i = pl.num_programs(1)
    T = T_MIX
    mod = mod_ref[0, 0]
    shift1 = mod[0:1, :]
    gain1 = g1_ref[0] * (1.0 + mod[1:2, :])

    def norm_mod(x):
        ms = jnp.mean(x * x, axis=-1, keepdims=True)
        return x * lax.rsqrt(ms + EPS) * gain1 + shift1

    h = norm_mod(xm_ref[0]).astype(BF16)
    z = jnp.dot(h, win_ref[0], preferred_element_type=F32)

    hh = norm_mod(jnp.concatenate([xp_ref[0], xn_ref[0]], axis=0)).astype(BF16)
    zh = jnp.dot(hh, win_ref[0, :, 0:2 * CONV_W], preferred_element_type=F32)
    glu_h = zh[:, 0:CONV_W] * jax.nn.sigmoid(zh[:, CONV_W:2 * CONV_W])
    glu_scr[0:HALO, :] = jnp.where(i > 0, glu_h[0:HALO], 0.0)
    glu_scr[HALO + T:2 * HALO + T, :] = jnp.where(i < n_i - 1, glu_h[HALO:2 * HALO], 0.0)
    glu_scr[HALO:HALO + T, :] = z[:, 0:CONV_W] * jax.nn.sigmoid(z[:, CONV_W:2 * CONV_W])

    off = HALO - KSIZE // 2
    for b in range(8):
        sh_scr[b] = glu_scr[b:b + T + 3 * 8, :]

    for c in range(T // CONV_ROWS):
        r0 = c * CONV_ROWS
        acc = jnp.broadcast_to(cb_ref[0], (CONV_ROWS, CONV_W))
        for k in range(KSIZE):
            a, b = divmod(k + off, 8)
            acc = acc + sh_scr[b, r0 + 8 * a:r0 + 8 * a + CONV_ROWS, :] * cw_ref[0, k:k + 1, :]
        conv_scr[r0:r0 + CONV_ROWS, :] = acc
    hc = conv_scr[...]
    gavg = gavg_ref[...]
    mu = jnp.dot(hc.astype(BF16), gavg, preferred_element_type=F32)
    dc = hc - mu
    var = jnp.dot((dc * dc).astype(BF16), gavg, preferred_element_type=F32)
    hn = dc * lax.rsqrt(var + EPS) * gng_ref[0] + gnb_ref[0]
    ya = hn * jax.nn.sigmoid(hn)
    yac_ref[0, :, 0:CONV_W] = ya.astype(BF16)

    zb = z[:, 2 * CONV_W:2 * CONV_W + FOUR_W].astype(BF16)
    pq = jnp.dot(zb, dft_ref[...], preferred_element_type=F32)
    pq_ref[0, 0] = pq[:, 0:FOUR_W].astype(BF16)
    pq_ref[0, 1] = pq[:, FOUR_W:2 * FOUR_W].astype(BF16)

    c0 = 2 * CONV_W + FOUR_W
    zc = z[:, c0:c0 + 2 * SGU_W]
    zc = 0.5 * zc * (1.0 + jnp.tanh(GELU_C * (zc + GELU_A * (zc * zc * zc))))
    u = zc[:, 0:SGU_W]
    v = zc[:, SGU_W:2 * SGU_W]
    vm = jnp.mean(v, axis=-1, keepdims=True)
    vd = v - vm
    vv = jnp.mean(vd * vd, axis=-1, keepdims=True)
    vn = (vd * lax.rsqrt(vv + EPS) * lng_ref[0] + lnb_ref[0]).astype(BF16)
    n_chunk = T // CHUNK
    lane = lax.broadcasted_iota(I32, (CHUNK, 2 * HEAD), 1)
    for pr in range(SGU_HEADS // 2):
        cols = slice(2 * HEAD * pr, 2 * HEAD * (pr + 1))
        rhs = jnp.concatenate([vn[n * CHUNK:(n + 1) * CHUNK, cols] for n in range(n_chunk)], axis=1)
        lo = jnp.dot(sw_ref[0, 2 * pr], rhs, preferred_element_type=F32)
        hi = jnp.dot(sw_ref[0, 2 * pr + 1], rhs, preferred_element_type=F32)
        for n in range(n_chunk):
            sl = slice(n * 2 * HEAD, (n + 1) * 2 * HEAD)
            vs = jnp.where(lane < HEAD, lo[:, sl], hi[:, sl]) + sb_ref[0, :, cols]
            rows = slice(n * CHUNK, (n + 1) * CHUNK)
            yac_ref[0, rows, CONV_W + 2 * HEAD * pr:CONV_W + 2 * HEAD * (pr + 1)] = (u[rows, cols] * vs).astype(BF16)


def _mixer_in(x, mod, l, ln1_g, w_in_bf, conv_w, conv_b, gn_g, gn_b, ln_g, ln_b, sgu_w_bf, sgu_bias, dft64, gavg):
    bsz, seq, _ = x.shape
    T = T_MIX
    n_i = seq // T
    hb = T // HALO
    n_h = seq // HALO
    vec = lambda w: pl.BlockSpec((1, 1, w), lambda b, i: (l, 0, 0))
    return pl.pallas_call(
        _mixer_in_kernel,
        grid=(bsz, n_i),
        in_specs=[
            pl.BlockSpec((1, T, D), lambda b, i: (b, i, 0)),
            pl.BlockSpec((1, HALO, D), lambda b, i: (b, jnp.maximum(i * hb - 1, 0), 0)),
            pl.BlockSpec((1, HALO, D), lambda b, i: (b, jnp.minimum((i + 1) * hb, n_h - 1), 0)),
            pl.BlockSpec((1, 1, 8, D), lambda b, i: (l, b, 0, 0)),
            vec(D),
            pl.BlockSpec((1, D, Z_COLS), lambda b, i: (l, 0, 0)),
            pl.BlockSpec((1, KSIZE + 1, CONV_W), lambda b, i: (l, 0, 0)),
            vec(CONV_W), vec(CONV_W), vec(CONV_W), vec(SGU_W), vec(SGU_W),
            pl.BlockSpec((1, SGU_HEADS, CHUNK, CHUNK), lambda b, i: (l, 0, 0, 0)),
            pl.BlockSpec((1, CHUNK, SGU_W), lambda b, i: (l, 0, 0)),
            pl.BlockSpec((FOUR_W, 2 * FOUR_W), lambda b, i: (0, 0)),
            pl.BlockSpec((CONV_W, CONV_W), lambda b, i: (0, 0)),
        ],
        out_specs=[
            pl.BlockSpec((1, T, CONV_W + SGU_W), lambda b, i: (b, i, 0)),
            pl.BlockSpec((1, 2, T, FOUR_W), lambda b, i: (b, 0, i, 0)),
        ],
        out_shape=[
            jax.ShapeDtypeStruct((bsz, seq, CONV_W + SGU_W), BF16),
            jax.ShapeDtypeStruct((bsz, 2, seq, FOUR_W), BF16),
        ],
        scratch_shapes=[
            pltpu.VMEM((T + 2 * HALO, CONV_W), F32),
            pltpu.VMEM((8, T + 3 * 8, CONV_W), F32),
            pltpu.VMEM((T, CONV_W), F32),
        ],
        compiler_params=_cparams(("arbitrary", "arbitrary")),
        name="mixer_in",
    )(x, x, x, mod, ln1_g, w_in_bf, conv_w, conv_b, gn_g, gn_b, ln_g, ln_b, sgu_w_bf, sgu_bias, dft64, gavg)


def _dft_fold_kernel(scale, pq_ref, fold_ref, ph_ref):
    seq = pq_ref.shape[1] // 2
    half = seq // 2
    nb = seq // CHUNK
    rev = _block_reversal()
    row0 = lax.broadcasted_iota(I32, (CHUNK, FOUR_W), 0) == 0
    alt = jnp.where(lax.broadcasted_iota(I32, (CHUNK, 1), 0) % 2 == 0, 1.0, -1.0)
    alt_sum = jnp.zeros((1, FOUR_W), F32)
    for part, sign in ((0, 1.0), (1, -1.0)):
        base = part * seq
        for m in range(half // CHUNK):
            lo = pq_ref[0, base + CHUNK * m:base + CHUNK * (m + 1), :].astype(F32)
            up = pq_ref[0, base + CHUNK * (nb - 1 - m):base + CHUNK * (nb - m), :]
            mirrored = jnp.dot(rev, up, preferred_element_type=F32)
            if m >= 1:
                first = pq_ref[0, base + CHUNK * (nb - m):base + CHUNK * (nb - m) + 1, :].astype(F32)
                mirrored = jnp.where(row0, first, mirrored)
            folded = (lo + sign * mirrored).astype(BF16)
            fold_ref[0, part * half + CHUNK * m:part * half + CHUNK * (m + 1), :] = folded
            if part == 0:
                alt_sum = alt_sum + jnp.sum(folded.astype(F32) * alt, axis=0, keepdims=True)
    ph = pq_ref[0, half:half + 1, :].astype(F32) * scale
    rid = lax.broadcasted_iota(I32, (8, FOUR_W), 0)
    ph_ref[0] = jnp.where(rid == 1, alt_sum * scale + ph, ph)


def _block_reversal():
    rr = lax.broadcasted_iota(I32, (CHUNK, CHUNK), 0)
    cc = lax.broadcasted_iota(I32, (CHUNK, CHUNK), 1)
    return jnp.where((rr >= 1) & (cc == CHUNK - rr), 1.0, 0.0).astype(BF16)


def _seq_dft_kernel(cs_ref, fold_ref, ph_ref, lo_ref, hi_ref, carry_scr):
    jj = pl.program_id(0)
    b = pl.program_id(1)
    half = fold_ref.shape[1] // 2
    ph = ph_ref[0, 0:1, :]

    @pl.when(jj == 0)
    def _():
        carry_scr[b] = jnp.broadcast_to(ph_ref[0, 1:2, :], (8, FOUR_W))

    alt = jnp.where(lax.broadcasted_iota(I32, (T_DFT, 1), 0) % 2 == 0, 1.0, -1.0)
    a = jnp.dot(cs_ref[:, 0:half], fold_ref[0, 0:half, :], preferred_element_type=F32) + alt * ph
    minus_b = jnp.dot(cs_ref[:, half:2 * half], fold_ref[0, half:2 * half, :], preferred_element_type=F32)
    lo_ref[0] = (a + minus_b).astype(BF16)
    mirror_src = (a - minus_b).astype(BF16)
    rev = _block_reversal()
    row0 = lax.broadcasted_iota(I32, (CHUNK, FOUR_W), 0) == 0
    nbt = T_DFT // CHUNK
    for m in range(nbt):
        blk = mirror_src[CHUNK * (nbt - 1 - m):CHUNK * (nbt - m), :]
        mirrored = jnp.dot(rev, blk, preferred_element_type=F32)
        if m >= 1:
            first = mirror_src[CHUNK * (nbt - m):CHUNK * (nbt - m) + 1, :].astype(F32)
        else:
            first = carry_scr[b][0:1, :]
        hi_ref[0, CHUNK * m:CHUNK * (m + 1), :] = jnp.where(row0, first, mirrored).astype(BF16)
    carry_scr[b] = jnp.broadcast_to(mirror_src[0:1, :].astype(F32), (8, FOUR_W))


def _seq_dft(cs, pq):
    bsz, two_s, _ = pq.shape
    seq = two_s // 2
    scale = 1.0 / (seq * HEAD) ** 0.5
    n_t = seq // 2 // T_DFT
    fold, ph = pl.pallas_call(
        functools.partial(_dft_fold_kernel, scale),
        grid=(bsz,),
        in_specs=[pl.BlockSpec((1, two_s, FOUR_W), lambda b: (b, 0, 0))],
        out_specs=[pl.BlockSpec((1, seq, FOUR_W), lambda b: (b, 0, 0)),
                   pl.BlockSpec((1, 8, FOUR_W), lambda b: (b, 0, 0))],
        out_shape=[jax.ShapeDtypeStruct((bsz, seq, FOUR_W), BF16),
                   jax.ShapeDtypeStruct((bsz, 8, FOUR_W), F32)],
        compiler_params=_cparams(("arbitrary",)),
        name="dft_fold",
    )(pq)
    half_out = jax.ShapeDtypeStruct((bsz, seq // 2, FOUR_W), BF16)
    return pl.pallas_call(
        _seq_dft_kernel,
        grid=(n_t, bsz),
        in_specs=[pl.BlockSpec((T_DFT, seq), lambda jj, b: (n_t - 1 - jj, 0)),
                  pl.BlockSpec((1, seq, FOUR_W), lambda jj, b: (b, 0, 0)),
                  pl.BlockSpec((1, 8, FOUR_W), lambda jj, b: (b, 0, 0))],
        out_specs=[pl.BlockSpec((1, T_DFT, FOUR_W), lambda jj, b: (b, n_t - 1 - jj, 0)),
                   pl.BlockSpec((1, T_DFT, FOUR_W), lambda jj, b: (b, jj, 0))],
        out_shape=[half_out, half_out],
        scratch_shapes=[pltpu.VMEM((bsz, 8, FOUR_W), F32)],
        compiler_params=_cparams(("arbitrary", "arbitrary")),
        name="seq_dft",
    )(cs, fold, ph)


def _mixer_out_kernel(per_b, yac_ref, yblo_ref, ybhi_ref, x_ref, mod_ref, g2_ref, wout_ref, rwt_ref, rb_ref, upper_ref,
                      x1_ref, h2pa_ref, h2pb_ref, info_ref, route_ref, cnt_ref, cnt_scr):
    i = pl.program_id(0)
    T = T_MIX
    yb = jnp.where(i % per_b < per_b // 2, yblo_ref[...], ybhi_ref[...])

    @pl.when(i == 0)
    def _():
        cnt_scr[...] = jnp.zeros_like(cnt_scr)

    mod = mod_ref[0, 0]
    gate1 = mod[2:3, :]
    shift2 = mod[3:4, :]
    gain2 = g2_ref[0] * (1.0 + mod[4:5, :])
    yac = yac_ref[...]
    ycat = jnp.concatenate([yac[:, 0:CONV_W], yb, yac[:, CONV_W:CONV_W + SGU_W]], axis=1)
    o = jnp.dot(ycat, wout_ref[0], preferred_element_type=F32)
    x1 = x_ref[...] + gate1 * o
    x1_ref[...] = x1
    ms = jnp.mean(x1 * x1, axis=-1, keepdims=True)
    h2 = x1 * lax.rsqrt(ms + EPS) * gain2 + shift2
    h2p = _pack_bf16_pair(h2[:, 0:D // 2], h2[:, D // 2:D])
    h2pa_ref[...] = h2p[:, 0:ROW_W]
    h2pb_ref[...] = h2p[:, ROW_W:2 * ROW_W]

    h_hi = h2.astype(BF16)
    h_lo = (h2 - h_hi.astype(F32)).astype(BF16)
    nt = (((1,), (1,)), ((), ()))
    part = lax.dot_general(rwt_ref[...], h_hi, nt, preferred_element_type=F32)
    logits = (part[0:N_EXP] + part[N_EXP:2 * N_EXP]
              + lax.dot_general(rwt_ref[0:N_EXP, :], h_lo, nt, preferred_element_type=F32))
    mx = jnp.max(logits, axis=0, keepdims=True)
    ex = jnp.exp(logits - mx)
    probs = ex / jnp.sum(ex, axis=0, keepdims=True)
    sel = probs + rb_ref[...]
    sel3 = sel.reshape(N_GRP, EPG, T)
    probs3 = probs.reshape(N_GRP, EPG, T)
    jj = lax.broadcasted_iota(I32, (N_GRP, EPG, T), 1)
    m1 = jnp.max(sel3, axis=1, keepdims=True)
    i1 = jnp.min(jnp.where(sel3 == m1, jj, EPG), axis=1, keepdims=True)
    rest = jnp.where(jj == i1, -jnp.inf, sel3)
    m2 = jnp.max(rest, axis=1, keepdims=True)
    i2 = jnp.min(jnp.where(rest == m2, jj, EPG), axis=1, keepdims=True)
    gscore = m1 + m2
    gg = lax.broadcasted_iota(I32, (N_GRP, 1, T), 0)
    gmax = jnp.max(gscore, axis=0, keepdims=True)
    gidx = jnp.min(jnp.where(gscore == gmax, gg, N_GRP), axis=0, keepdims=True)
    ing = gg == gidx
    pick = lambda a, zero: jnp.sum(jnp.where(ing, a, zero), axis=0)
    p1 = jnp.sum(jnp.where(jj == i1, probs3, 0.0), axis=1, keepdims=True)
    p2 = jnp.sum(jnp.where(jj == i2, probs3, 0.0), axis=1, keepdims=True)
    pa = pick(p1, 0.0)
    pb = pick(p2, 0.0)
    gbase = gidx[0] * EPG
    e0 = gbase + pick(i1, 0)
    e1 = gbase + pick(i2, 0)
    den = pa + pb
    gw0 = pa / den
    gw1 = pb / den

    ee = lax.broadcasted_iota(I32, (N_EXP, T), 0)
    oh0 = ee == e0
    oh1 = ee == e1
    amat = jnp.where(oh0 | oh1, 1.0, 0.0)
    before = jnp.dot(amat.astype(BF16), upper_ref[...], preferred_element_type=F32) + cnt_scr[...]
    r0 = jnp.sum(jnp.where(oh0, before, 0.0), axis=0, keepdims=True)
    r1 = jnp.sum(jnp.where(oh1, before, 0.0), axis=0, keepdims=True)
    cnt_scr[...] = cnt_scr[...] + jnp.sum(amat, axis=1, keepdims=True)
    cnt_ref[...] = cnt_scr[...]

    rid = lax.broadcasted_iota(I32, (8, T), 0)
    route = jnp.zeros((8, T), I32)
    for k, val in enumerate((e0, e1, r0.astype(I32), r1.astype(I32))):
        route = jnp.where(rid == k, val, route)
    route_ref[...] = route
    gws = jnp.where(rid == 0, gw0, jnp.where(rid == 1, gw1, 0.0))
    gws = jnp.concatenate([gws, jnp.zeros((CHUNK - 8, T), F32)], axis=0)
    info_ref[...] = gws.T


def _mixer_out(yac, yb_lo, yb_hi, x, mod, l, ln2_g, w_out_bf, router_wt, router_b, bsz):
    n_tok = x.shape[0]
    T = T_MIX
    per_b = n_tok // bsz // T
    hb = per_b // 2
    row = lambda w: pl.BlockSpec((T, w), lambda i: (i, 0))
    lo_spec = pl.BlockSpec((T, FOUR_W), lambda i: ((i // per_b) * hb + jnp.minimum(i % per_b, hb - 1), 0))
    hi_spec = pl.BlockSpec((T, FOUR_W), lambda i: ((i // per_b) * hb + jnp.maximum(i % per_b - hb, 0), 0))
    upper = (lax.broadcasted_iota(I32, (T, T), 0) < lax.broadcasted_iota(I32, (T, T), 1)).astype(BF16)
    return pl.pallas_call(
        functools.partial(_mixer_out_kernel, per_b),
        grid=(n_tok // T,),
        in_specs=[
            row(CONV_W + SGU_W), lo_spec, hi_spec, row(D),
            pl.BlockSpec((1, 1, 8, D), lambda i: (l, i // per_b, 0, 0)),
            pl.BlockSpec((1, 1, D), lambda i: (l, 0, 0)),
            pl.BlockSpec((1, D, D), lambda i: (l, 0, 0)),
            pl.BlockSpec((2 * N_EXP, D), lambda i: (0, 0)),
            pl.BlockSpec((N_EXP, 1), lambda i: (0, 0)),
            pl.BlockSpec((T, T), lambda i: (0, 0)),
        ],
        out_specs=[row(D), row(ROW_W), row(ROW_W), row(CHUNK), pl.BlockSpec((8, T), lambda i: (0, i)),
                   pl.BlockSpec((N_EXP, 1), lambda i: (0, 0))],
        out_shape=[
            jax.ShapeDtypeStruct((n_tok, D), F32),
            jax.ShapeDtypeStruct((n_tok, ROW_W), U32),
            jax.ShapeDtypeStruct((n_tok, ROW_W), U32),
            jax.ShapeDtypeStruct((n_tok, CHUNK), F32),
            jax.ShapeDtypeStruct((8, n_tok), I32),
            jax.ShapeDtypeStruct((N_EXP, 1), F32),
        ],
        scratch_shapes=[pltpu.VMEM((N_EXP, 1), F32)],
        compiler_params=_cparams(("arbitrary",)),
        name="mixer_out",
    )(yac, yb_lo, yb_hi, x, mod, ln2_g, w_out_bf, router_wt, router_b, upper)


def _sc_mesh():
    return plsc.VectorSubcoreMesh(core_axis_name="c", subcore_axis_name="s")


def _sc_scatter2(src_a, src_b, idx0, idx1, n_rows):
    n = src_a.shape[0]
    out = jax.ShapeDtypeStruct((n_rows, ROW_W), src_a.dtype)

    @functools.partial(pl.kernel, out_type=[out, out], mesh=_sc_mesh())
    def scatter(xa_hbm, xb_hbm, i0_hbm, i1_hbm, oa_hbm, ob_hbm):
        for x_hbm, o_hbm in ((xa_hbm, oa_hbm), (xb_hbm, ob_hbm)):
            def body(x_vmem, i0_vmem, i1_vmem, o_hbm=o_hbm):
                pltpu.sync_copy(x_vmem, o_hbm.at[i0_vmem.at[0]])
                pltpu.sync_copy(x_vmem, o_hbm.at[i1_vmem.at[0]])

            pltpu.emit_pipeline(
                body, grid=(n // SC_WIN,),
                in_specs=[pl.BlockSpec((SC_WIN, ROW_W), index_map=lambda i: (i, 0)),
                          pl.BlockSpec((1, SC_WIN), index_map=lambda i: (0, i)),
                          pl.BlockSpec((1, SC_WIN), index_map=lambda i: (0, i))],
                out_specs=[],
                core_axis_name=("c", "s"), dimension_semantics=(pltpu.PARALLEL,),
            )(x_hbm, i0_hbm, i1_hbm)

    return scatter(src_a, src_b, idx0, idx1)


def _sc_gather(src_a, src_b, idx):
    m = idx.shape[1]
    out = jax.ShapeDtypeStruct((m, ROW_W), src_a.dtype)

    @functools.partial(pl.kernel, out_type=[out, out], mesh=_sc_mesh())
    def gather(xa_hbm, xb_hbm, i_hbm, oa_hbm, ob_hbm):
        for x_hbm, o_hbm in ((xa_hbm, oa_hbm), (xb_hbm, ob_hbm)):
            def body(i_vmem, o_vmem, x_hbm=x_hbm):
                pltpu.sync_copy(x_hbm.at[i_vmem.at[0]], o_vmem)

            pltpu.emit_pipeline(
                body, grid=(m // SC_WIN,),
                in_specs=[pl.BlockSpec((1, SC_WIN), index_map=lambda i: (0, i))],
                out_specs=[pl.BlockSpec((SC_WIN, ROW_W), index_map=lambda i: (i, 0))],
                core_axis_name=("c", "s"), dimension_semantics=(pltpu.PARALLEL,),
            )(i_hbm, o_hbm)

    return gather(src_a, src_b, idx)


def _experts_kernel(l, n_blk, start_ref, nchunk_ref, xpa_ref, xpb_ref, w1_ref, w3_ref, w2_ref, ypa_ref, ypb_ref,
                    w13_scr, w2_scr, wbuf13, wbuf2, xbuf, ybuf, wsem, xsem, ysem):
    e = pl.program_id(0)
    nc = nchunk_ref[e]
    chunk0 = start_ref[e]
    n_used = start_ref[N_EXP]

    def w_copies(ex):
        slot = ex % (W_AHEAD + 1)
        cps = []
        for p in range(W_PIECES):
            r13 = pl.ds(p * (D // W_PIECES), D // W_PIECES)
            r2 = pl.ds(p * (D_FF // W_PIECES), D_FF // W_PIECES)
            cps.append(pltpu.make_async_copy(w1_ref.at[l, ex, r13], wbuf13.at[slot, 0, r13], wsem.at[slot]))
            cps.append(pltpu.make_async_copy(w3_ref.at[l, ex, r13], wbuf13.at[slot, 1, r13], wsem.at[slot]))
            cps.append(pltpu.make_async_copy(w2_ref.at[l, ex, r2], wbuf2.at[slot, r2], wsem.at[slot]))
        return cps

    def x_copies(g):
        rows = pl.ds(pl.multiple_of(g * BLK, BLK), BLK)
        slot = g % (X_AHEAD + 1)
        return [pltpu.make_async_copy(src.at[rows], xbuf.at[slot, h], xsem.at[slot])
                for h, src in enumerate((xpa_ref, xpb_ref))]

    def y_copies(g):
        rows = pl.ds(pl.multiple_of(g * BLK, BLK), BLK)
        slot = g % 2
        return [pltpu.make_async_copy(ybuf.at[slot, h], dst.at[rows], ysem.at[slot])
                for h, dst in enumerate((ypa_ref, ypb_ref))]

    def start(cps):
        for cp in cps:
            cp.start()

    def wait(cps):
        for cp in cps:
            cp.wait()

    @pl.when(e == 0)
    def _():
        for j in range(X_AHEAD):
            @pl.when(j < n_used)
            def _():
                start(x_copies(j))

        for j in range(W_AHEAD):
            start(w_copies(j))

    @pl.when(e < N_EXP)
    def _():
        wslot = e % (W_AHEAD + 1)
        wait(w_copies(e))

        @pl.when(e + W_AHEAD < N_EXP)
        def _():
            start(w_copies(e + W_AHEAD))

        @pl.when(nc > 0)
        def _():
            w13_scr[:, 0:D_FF] = wbuf13[wslot, 0].astype(BF16)
            w13_scr[:, D_FF:2 * D_FF] = wbuf13[wslot, 1].astype(BF16)
            w2_scr[...] = wbuf2[wslot].astype(BF16)

            def chunk(c, carry):
                g = chunk0 + c
                slot = g % (X_AHEAD + 1)

                @pl.when(g + X_AHEAD < n_used)
                def _():
                    start(x_copies(g + X_AHEAD))

                wait(x_copies(g))

                @pl.when(g >= 2)
                def _():
                    wait(y_copies(g - 2))

                a, b = _unpack_bf16_pair(jnp.concatenate([xbuf[slot, 0], xbuf[slot, 1]], axis=1))
                x = jnp.concatenate([a.astype(BF16), b.astype(BF16)], axis=1)
                h13 = jnp.dot(x, w13_scr[...], preferred_element_type=F32)
                h1 = h13[:, 0:D_FF]
                act = (h1 * jax.nn.sigmoid(h1) * h13[:, D_FF:2 * D_FF]).astype(BF16)
                y = jnp.dot(act, w2_scr[...], preferred_element_type=F32)
                yp = _pack_bf16_pair(y[:, 0:D // 2], y[:, D // 2:D])
                ybuf[g % 2, 0] = yp[:, 0:ROW_W]
                ybuf[g % 2, 1] = yp[:, ROW_W:2 * ROW_W]
                start(y_copies(g))
                return carry

            lax.fori_loop(0, nc, chunk, 0)

    @pl.when(e == N_EXP)
    def _():
        @pl.when(n_used >= 2)
        def _():
            wait(y_copies(n_used - 2))

        wait(y_copies(n_used - 1))
        ybuf[0] = jnp.zeros((2, BLK, ROW_W), U32)

        def fill_one(g, carry):
            rows = pl.ds(pl.multiple_of(g * BLK, BLK), BLK)
            cps = [pltpu.make_async_copy(ybuf.at[0, h], dst.at[rows], ysem.at[0])
                   for h, dst in enumerate((ypa_ref, ypb_ref))]
            start(cps)
            wait(cps)
            return carry

        lax.fori_loop(n_used, n_blk, fill_one, 0)


def _experts(chunk_start, n_chunk, xpa, xpb, w1, w3, w2, l):
    n_rows = xpa.shape[0]
    n_blk = n_rows // BLK
    hbm = pl.BlockSpec(memory_space=pl.ANY)
    half = jax.ShapeDtypeStruct((n_rows, ROW_W), U32)
    return pl.pallas_call(
        functools.partial(_experts_kernel, l, n_blk),
        grid_spec=pltpu.PrefetchScalarGridSpec(
            num_scalar_prefetch=2,
            grid=(N_EXP + 1,),
            in_specs=[hbm, hbm, hbm, hbm, hbm],
            out_specs=[hbm, hbm],
            scratch_shapes=[pltpu.VMEM((D, 2 * D_FF), BF16), pltpu.VMEM((D_FF, D), BF16),
                            pltpu.VMEM((W_AHEAD + 1, 2, D, D_FF), F32), pltpu.VMEM((W_AHEAD + 1, D_FF, D), F32),
                            pltpu.VMEM((X_AHEAD + 1, 2, BLK, ROW_W), U32), pltpu.VMEM((2, 2, BLK, ROW_W), U32),
                            pltpu.SemaphoreType.DMA((W_AHEAD + 1,)), pltpu.SemaphoreType.DMA((X_AHEAD + 1,)),
                            pltpu.SemaphoreType.DMA((2,))],
        ),
        out_shape=[half, half],
        compiler_params=_cparams(("arbitrary",)),
        name="experts",
    )(chunk_start, n_chunk, xpa, xpb, w1, w3, w2)


def _combine_kernel(final, ga0_ref, gb0_ref, ga1_ref, gb1_ref, x1_ref, info_ref, mod_ref, fg_ref, o_ref):
    info = info_ref[...]
    gw0 = info[:, 0:1]
    gw1 = info[:, 1:2]
    a0, b0 = _unpack_bf16_pair(jnp.concatenate([ga0_ref[...], gb0_ref[...]], axis=1))
    a1, b1 = _unpack_bf16_pair(jnp.concatenate([ga1_ref[...], gb1_ref[...]], axis=1))
    y = jnp.concatenate([gw0 * a0 + gw1 * a1, gw0 * b0 + gw1 * b1], axis=1)
    gate2 = mod_ref[0, 0][5:6, :]
    x2 = x1_ref[...] + gate2 * y
    if final:
        ms = jnp.mean(x2 * x2, axis=-1, keepdims=True)
        x2 = x2 * lax.rsqrt(ms + EPS) * fg_ref[...]
    o_ref[...] = x2


def _combine(ga, gb, x1, info, mod, l, final_g, bsz, final):
    n_tok = x1.shape[0]
    n_i = n_tok // T_CMB
    per_b = n_i // bsz
    first = pl.BlockSpec((T_CMB, ROW_W), lambda i: (i, 0))
    second = pl.BlockSpec((T_CMB, ROW_W), lambda i: (i + n_i, 0))
    return pl.pallas_call(
        functools.partial(_combine_kernel, final),
        grid=(n_i,),
        in_specs=[
            first, first, second, second,
            pl.BlockSpec((T_CMB, D), lambda i: (i, 0)),
            pl.BlockSpec((T_CMB, CHUNK), lambda i: (i, 0)),
            pl.BlockSpec((1, 1, 8, D), lambda i: (l, i // per_b, 0, 0)),
            pl.BlockSpec((1, D), lambda i: (0, 0)),
        ],
        out_specs=pl.BlockSpec((T_CMB, D), lambda i: (i, 0)),
        out_shape=jax.ShapeDtypeStruct((n_tok, D), F32),
        compiler_params=_cparams(("arbitrary",)),
        name="combine",
    )(ga, gb, ga, gb, x1, info, mod, final_g)


def _dft_table_kernel(t1_ref, t2_ref, o_ref):
    half = t2_ref.shape[2]
    c2 = t2_ref[0]
    s2 = t2_ref[1]
    for j in range(TAB_GROUP):
        c1 = t1_ref[j, 0:1, :]
        s1 = t1_ref[j, 1:2, :]
        rows = slice(j * TAB_ROWS, (j + 1) * TAB_ROWS)
        o_ref[rows, 0:half] = (c1 * c2 - s1 * s2).astype(BF16)
        o_ref[rows, half:2 * half] = (-(s1 * c2 + c1 * s2)).astype(BF16)


def _dft_tables(seq):
    scale = 1.0 / (seq * HEAD) ** 0.5
    n_hi = seq // TAB_ROWS
    n = lax.broadcasted_iota(I32, (1, seq // 2), 1)
    kh = lax.broadcasted_iota(I32, (n_hi // 2, 1), 0)
    a1 = ((kh * n) % n_hi).astype(F32) * (2.0 * jnp.pi / n_hi)
    t1 = jnp.stack([jnp.cos(a1), jnp.sin(a1)], axis=1)
    kl = lax.broadcasted_iota(I32, (TAB_ROWS, 1), 0)
    a2 = ((kl * n) % seq).astype(F32) * (2.0 * jnp.pi / seq)
    t2 = jnp.stack([jnp.cos(a2) * scale, jnp.sin(a2) * scale], axis=0)
    cs = pl.pallas_call(
        _dft_table_kernel,
        grid=(n_hi // 2 // TAB_GROUP,),
        in_specs=[pl.BlockSpec((TAB_GROUP, 2, seq // 2), lambda i: (i, 0, 0)),
                  pl.BlockSpec((2, TAB_ROWS, seq // 2), lambda i: (0, 0, 0))],
        out_specs=pl.BlockSpec((TAB_GROUP * TAB_ROWS, seq), lambda i: (i, 0)),
        out_shape=jax.ShapeDtypeStruct((seq // 2, seq), BF16),
        compiler_params=_cparams(("arbitrary",)),
        name="dft_table",
    )(t1, t2)
    d = lax.broadcasted_iota(I32, (FOUR_W, FOUR_W), 0)
    q = lax.broadcasted_iota(I32, (FOUR_W, FOUR_W), 1)
    same = (d // HEAD) == (q // HEAD)
    ang64 = ((d * q) % HEAD).astype(F32) * (2.0 * jnp.pi / HEAD)
    dft64 = jnp.concatenate([jnp.where(same, jnp.cos(ang64), 0.0),
                             jnp.where(same, jnp.sin(ang64), 0.0)], axis=1).astype(BF16)
    return cs, dft64


def _routing_tables(route, counts_f):
    counts = counts_f[:, 0].astype(I32)
    pc = (counts + BLK - 1) // BLK * BLK
    pends = jnp.cumsum(pc)
    pstarts = pends - pc
    eid = lax.broadcasted_iota(I32, (N_EXP, 1), 0)

    def dest_of(e, r):
        return jnp.sum(jnp.where(e[None, :] == eid, pstarts[:, None], 0), axis=0) + r

    dest = jnp.concatenate([dest_of(route[0], route[2]), dest_of(route[1], route[3])])
    chunk_start = jnp.concatenate([pstarts, pends[-1:]]) // BLK
    n_chunk = jnp.concatenate([pc // BLK, jnp.zeros((1,), I32)])
    return dest, chunk_start, n_chunk


def kernel(x, c, ln1_g, ln2_g, w_ada, b_ada, w_in, w_out, conv_w, conv_b, conv_gn_g, conv_gn_b, sgu_ln_g,
           sgu_ln_b, sgu_w, sgu_b, router_w, router_b, exp_w1, exp_w3, exp_w2, final_g):
    bsz, seq, _ = x.shape
    depth = w_in.shape[0]
    n_tok = bsz * seq
    n_blk = (2 * n_tok + N_EXP * (BLK - 1) + BLK - 1) // BLK
    r3 = lambda a: a.reshape(depth, 1, a.shape[-1])

    mod = jnp.pad(_ada_mod(c, w_ada, b_ada).reshape(depth, bsz, 6, D), ((0, 0), (0, 0), (0, 2), (0, 0)))
    cs, dft64 = _dft_tables(seq)
    hd = lax.broadcasted_iota(I32, (CONV_W, CONV_W), 0) // HEAD
    gavg = jnp.where(hd == hd.T, 1.0 / HEAD, 0.0).astype(BF16)
    w_in_bf = w_in.astype(BF16)
    w_out_bf = w_out.astype(BF16)
    sgu_w_bf = sgu_w.astype(BF16)
    sgu_bias = jnp.repeat(jnp.swapaxes(sgu_b, 1, 2), HEAD, axis=2)
    conv_w_p = jnp.pad(conv_w, ((0, 0), (0, 1), (0, 0)))
    rwt_hi = router_w.T.astype(BF16)
    rwt_lo = (router_w.T - rwt_hi.astype(F32)).astype(BF16)
    router_wt = jnp.concatenate([rwt_hi, rwt_lo], axis=0)
    router_bc = router_b.reshape(N_EXP, 1)
    fg = final_g.reshape(1, D)

    for l in range(depth):
        yac, pq = _mixer_in(x, mod, l, r3(ln1_g), w_in_bf, conv_w_p, r3(conv_b), r3(conv_gn_g), r3(conv_gn_b),
                            r3(sgu_ln_g), r3(sgu_ln_b), sgu_w_bf, sgu_bias, dft64, gavg)
        yb_lo, yb_hi = _seq_dft(cs, pq.reshape(bsz, 2 * seq, FOUR_W))
        x1, h2pa, h2pb, info, route, counts = _mixer_out(
            yac.reshape(n_tok, CONV_W + SGU_W), yb_lo.reshape(n_tok // 2, FOUR_W),
            yb_hi.reshape(n_tok // 2, FOUR_W), x.reshape(n_tok, D),
            mod, l, r3(ln2_g), w_out_bf, router_wt, router_bc, bsz)
        dest, chunk_start, n_chunk = _routing_tables(route, counts)
        d0 = dest[:n_tok].reshape(1, n_tok)
        d1 = dest[n_tok:].reshape(1, n_tok)
        xpa, xpb = _sc_scatter2(h2pa, h2pb, d0, d1, n_blk * BLK)
        ypa, ypb = _experts(chunk_start, n_chunk, xpa, xpb, exp_w1, exp_w3, exp_w2, l)
        ga, gb = _sc_gather(ypa, ypb, dest.reshape(1, 2 * n_tok))
        x = _combine(ga, gb, x1, info, mod, l, fg, bsz, l == depth - 1).reshape(bsz, seq, D)
    return x
```

```python
import functools

import jax
import jax.numpy as jnp
from jax import lax
from jax.experimental import pallas as pl
from jax.experimental.pallas import tpu as pltpu
from jax.experimental.pallas import tpu_sc as plsc

F32 = jnp.float32
BF16 = jnp.bfloat16
I32 = jnp.int32
U32 = jnp.uint32
HIGHEST = lax.Precision.HIGHEST

D = 1024
HEAD = 64
CONV_W = 384
FOUR_W = 256
SGU_W = 384
SGU_HEADS = SGU_W // HEAD
Z_COLS = 2 * CONV_W + FOUR_W + 2 * SGU_W
KSIZE = 31
HALO = 16
CHUNK = 128
N_EXP = 64
N_GRP = 8
EPG = N_EXP // N_GRP
D_FF = D // 2
EPS = 1e-6
GELU_C = 0.7978845608028654
GELU_A = 0.044715

T_MIX = 1024
T_DFT = 512
T_CMB = 512
ROW_W = 256
SC_WIN = 128
BLK = 384
X_AHEAD = 3
W_AHEAD = 1
CONV_ROWS = 64
TAB_ROWS = 64
TAB_GROUP = 4
W_PIECES = 4
VMEM_LIMIT = 56 * 1024 * 1024


def _cparams(sem):
    return pltpu.CompilerParams(dimension_semantics=sem, vmem_limit_bytes=VMEM_LIMIT)


def _pack_bf16_pair(a, b):
    ua = lax.bitcast_convert_type(a.astype(BF16).astype(F32), U32) >> 16
    ub = lax.bitcast_convert_type(b.astype(BF16).astype(F32), U32) & jnp.uint32(0xFFFF0000)
    return ua | ub


def _unpack_bf16_pair(p):
    a = lax.bitcast_convert_type(p << 16, F32)
    b = lax.bitcast_convert_type(p & jnp.uint32(0xFFFF0000), F32)
    return a, b


def _ada_kernel(c_ref, *refs):
    w_refs, b_ref, o_ref = refs[:-2], refs[-2], refs[-1]
    c = c_ref[...]
    ca = c * jax.nn.sigmoid(c)
    tn = w_refs[0].shape[2]
    for j, w_ref in enumerate(w_refs):
        cols = slice(j * tn, (j + 1) * tn)
        o_ref[0, :, cols] = jnp.dot(ca, w_ref[0], precision=HIGHEST, preferred_element_type=F32) + b_ref[0, :, cols]


def _ada_mod(c, w_ada, b_ada):
    depth, _, ncol = w_ada.shape
    bsz = c.shape[0]
    n_slab = 4
    n_half = 2
    tn = ncol // (n_slab * n_half)
    return pl.pallas_call(
        _ada_kernel,
        grid=(depth, n_half),
        in_specs=[pl.BlockSpec((bsz, D), lambda l, h: (0, 0))]
        + [pl.BlockSpec((1, D, tn), functools.partial(lambda j, l, h: (l, 0, n_slab * h + j), j))
           for j in range(n_slab)]
        + [pl.BlockSpec((1, 1, n_slab * tn), lambda l, h: (l, 0, h))],
        out_specs=pl.BlockSpec((1, bsz, n_slab * tn), lambda l, h: (l, 0, h)),
        out_shape=jax.ShapeDtypeStruct((depth, bsz, ncol), F32),
        compiler_params=_cparams(("arbitrary", "arbitrary")),
        name="ada_mod",
    )(c, *([w_ada] * n_slab), b_ada.reshape(depth, 1, ncol))


def _mixer_in_kernel(xm_ref, xp_ref, xn_ref, mod_ref, g1_ref, win_ref, cw_ref, cb_ref, gng_ref, gnb_ref,
                     lng_ref, lnb_ref, sw_ref, sb_ref, dft_ref, gavg_ref,
                     yac_ref, pq_ref, glu_scr, sh_scr, conv_scr):
    i = pl.program_id(1)
    n_i = pl.num_programs(1)
    T = T_MIX
    mod = mod_ref[0, 0]
    shift1 = mod[0:1, :]
    gain1 = g1_ref[0] * (1.0 + mod[1:2, :])

    def norm_mod(x):
        ms = jnp.mean(x * x, axis=-1, keepdims=True)
        return x * lax.rsqrt(ms + EPS) * gain1 + shift1

    h = norm_mod(xm_ref[0]).astype(BF16)
    z = jnp.dot(h, win_ref[0], preferred_element_type=F32)

    hh = norm_mod(jnp.concatenate([xp_ref[0], xn_ref[0]], axis=0)).astype(BF16)
    zh = jnp.dot(hh, win_ref[0, :, 0:2 * CONV_W], preferred_element_type=F32)
    glu_h = zh[:, 0:CONV_W] * jax.nn.sigmoid(zh[:, CONV_W:2 * CONV_W])
    glu_scr[0:HALO, :] = jnp.where(i > 0, glu_h[0:HALO], 0.0)
    glu_scr[HALO + T:2 * HALO + T, :] = jnp.where(i < n_i - 1, glu_h[HALO:2 * HALO], 0.0)
    glu_scr[HALO:HALO + T, :] = z[:, 0:CONV_W] * jax.nn.sigmoid(z[:, CONV_W:2 * CONV_W])

    off = HALO - KSIZE // 2
    for b in range(8):
        sh_scr[b] = glu_scr[b:b + T + 3 * 8, :]

    for c in range(T // CONV_ROWS):
        r0 = c * CONV_ROWS
        acc = jnp.broadcast_to(cb_ref[0], (CONV_ROWS, CONV_W))
        for k in range(KSIZE):
            a, b = divmod(k + off, 8)
            acc = acc + sh_scr[b, r0 + 8 * a:r0 + 8 * a + CONV_ROWS, :] * cw_ref[0, k:k + 1, :]
        conv_scr[r0:r0 + CONV_ROWS, :] = acc
    hc = conv_scr[...]
    gavg = gavg_ref[...]
    mu = jnp.dot(hc.astype(BF16), gavg, preferred_element_type=F32)
    dc = hc - mu
    var = jnp.dot((dc * dc).astype(BF16), gavg, preferred_element_type=F32)
    hn = dc * lax.rsqrt(var + EPS) * gng_ref[0] + gnb_ref[0]
    ya = hn * jax.nn.sigmoid(hn)
    yac_ref[0, :, 0:CONV_W] = ya.astype(BF16)

    zb = z[:, 2 * CONV_W:2 * CONV_W + FOUR_W].astype(BF16)
    pq = jnp.dot(zb, dft_ref[...], preferred_element_type=F32)
    pq_ref[0, 0] = pq[:, 0:FOUR_W].astype(BF16)
    pq_ref[0, 1] = pq[:, FOUR_W:2 * FOUR_W].astype(BF16)

    c0 = 2 * CONV_W + FOUR_W
    zc = z[:, c0:c0 + 2 * SGU_W]
    zc = 0.5 * zc * (1.0 + jnp.tanh(GELU_C * (zc + GELU_A * (zc * zc * zc))))
    u = zc[:, 0:SGU_W]
    v = zc[:, SGU_W:2 * SGU_W]
    vm = jnp.mean(v, axis=-1, keepdims=True)
    vd = v - vm
    vv = jnp.mean(vd * vd, axis=-1, keepdims=True)
    vn = (vd * lax.rsqrt(vv + EPS) * lng_ref[0] + lnb_ref[0]).astype(BF16)
    n_chunk = T // CHUNK
    lane = lax.broadcasted_iota(I32, (CHUNK, 2 * HEAD), 1)
    for pr in range(SGU_HEADS // 2):
        cols = slice(2 * HEAD * pr, 2 * HEAD * (pr + 1))
        rhs = jnp.concatenate([vn[n * CHUNK:(n + 1) * CHUNK, cols] for n in range(n_chunk)], axis=1)
        lo = jnp.dot(sw_ref[0, 2 * pr], rhs, preferred_element_type=F32)
        hi = jnp.dot(sw_ref[0, 2 * pr + 1], rhs, preferred_element_type=F32)
        for n in range(n_chunk):
            sl = slice(n * 2 * HEAD, (n + 1) * 2 * HEAD)
            vs = jnp.where(lane < HEAD, lo[:, sl], hi[:, sl]) + sb_ref[0, :, cols]
            rows = slice(n * CHUNK, (n + 1) * CHUNK)
            yac_ref[0, rows, CONV_W + 2 * HEAD * pr:CONV_W + 2 * HEAD * (pr + 1)] = (u[rows, cols] * vs).astype(BF16)


def _mixer_in(x, mod, l, ln1_g, w_in_bf, conv_w, conv_b, gn_g, gn_b, ln_g, ln_b, sgu_w_bf, sgu_bias, dft64, gavg):
    bsz, seq, _ = x.shape
    T = T_MIX
    n_i = seq // T
    hb = T // HALO
    n_h = seq // HALO
    vec = lambda w: pl.BlockSpec((1, 1, w), lambda b, i: (l, 0, 0))
    return pl.pallas_call(
        _mixer_in_kernel,
        grid=(bsz, n_i),
        in_specs=[
            pl.BlockSpec((1, T, D), lambda b, i: (b, i, 0)),
            pl.BlockSpec((1, HALO, D), lambda b, i: (b, jnp.maximum(i * hb - 1, 0), 0)),
            pl.BlockSpec((1, HALO, D), lambda b, i: (b, jnp.minimum((i + 1) * hb, n_h - 1), 0)),
            pl.BlockSpec((1, 1, 8, D), lambda b, i: (l, b, 0, 0)),
            vec(D),
            pl.BlockSpec((1, D, Z_COLS), lambda b, i: (l, 0, 0)),
            pl.BlockSpec((1, KSIZE + 1, CONV_W), lambda b, i: (l, 0, 0)),
            vec(CONV_W), vec(CONV_W), vec(CONV_W), vec(SGU_W), vec(SGU_W),
            pl.BlockSpec((1, SGU_HEADS, CHUNK, CHUNK), lambda b, i: (l, 0, 0, 0)),
            pl.BlockSpec((1, CHUNK, SGU_W), lambda b, i: (l, 0, 0)),
            pl.BlockSpec((FOUR_W, 2 * FOUR_W), lambda b, i: (0, 0)),
            pl.BlockSpec((CONV_W, CONV_W), lambda b, i: (0, 0)),
        ],
        out_specs=[
            pl.BlockSpec((1, T, CONV_W + SGU_W), lambda b, i: (b, i, 0)),
            pl.BlockSpec((1, 2, T, FOUR_W), lambda b, i: (b, 0, i, 0)),
        ],
        out_shape=[
            jax.ShapeDtypeStruct((bsz, seq, CONV_W + SGU_W), BF16),
            jax.ShapeDtypeStruct((bsz, 2, seq, FOUR_W), BF16),
        ],
        scratch_shapes=[
            pltpu.VMEM((T + 2 * HALO, CONV_W), F32),
            pltpu.VMEM((8, T + 3 * 8, CONV_W), F32),
            pltpu.VMEM((T, CONV_W), F32),
        ],
        compiler_params=_cparams(("arbitrary", "arbitrary")),
        name="mixer_in",
    )(x, x, x, mod, ln1_g, w_in_bf, conv_w, conv_b, gn_g, gn_b, ln_g, ln_b, sgu_w_bf, sgu_bias, dft64, gavg)


def _dft_fold_kernel(scale, pq_ref, fold_ref, ph_ref):
    seq = pq_ref.shape[1] // 2
    half = seq // 2
    nb = seq // CHUNK
    rev = _block_reversal()
    row0 = lax.broadcasted_iota(I32, (CHUNK, FOUR_W), 0) == 0
    alt = jnp.where(lax.broadcasted_iota(I32, (CHUNK, 1), 0) % 2 == 0, 1.0, -1.0)
    alt_sum = jnp.zeros((1, FOUR_W), F32)
    for part, sign in ((0, 1.0), (1, -1.0)):
        base = part * seq
        for m in range(half // CHUNK):
            lo = pq_ref[0, base + CHUNK * m:base + CHUNK * (m + 1), :].astype(F32)
            up = pq_ref[0, base + CHUNK * (nb - 1 - m):base + CHUNK * (nb - m), :]
            mirrored = jnp.dot(rev, up, preferred_element_type=F32)
            if m >= 1:
                first = pq_ref[0, base + CHUNK * (nb - m):base + CHUNK * (nb - m) + 1, :].astype(F32)
                mirrored = jnp.where(row0, first, mirrored)
            folded = (lo + sign * mirrored).astype(BF16)
            fold_ref[0, part * half + CHUNK * m:part * half + CHUNK * (m + 1), :] = folded
            if part == 0:
                alt_sum = alt_sum + jnp.sum(folded.astype(F32) * alt, axis=0, keepdims=True)
    ph = pq_ref[0, half:half + 1, :].astype(F32) * scale
    rid = lax.broadcasted_iota(I32, (8, FOUR_W), 0)
    ph_ref[0] = jnp.where(rid == 1, alt_sum * scale + ph, ph)


def _block_reversal():
    rr = lax.broadcasted_iota(I32, (CHUNK, CHUNK), 0)
    cc = lax.broadcasted_iota(I32, (CHUNK, CHUNK), 1)
    return jnp.where((rr >= 1) & (cc == CHUNK - rr), 1.0, 0.0).astype(BF16)


def _seq_dft_kernel(cs_ref, fold_ref, ph_ref, lo_ref, hi_ref, carry_scr):
    jj = pl.program_id(0)
    b = pl.program_id(1)
    half = fold_ref.shape[1] // 2
    ph = ph_ref[0, 0:1, :]

    @pl.when(jj == 0)
    def _():
        carry_scr[b] = jnp.broadcast_to(ph_ref[0, 1:2, :], (8, FOUR_W))

    alt = jnp.where(lax.broadcasted_iota(I32, (T_DFT, 1), 0) % 2 == 0, 1.0, -1.0)
    a = jnp.dot(cs_ref[:, 0:half], fold_ref[0, 0:half, :], preferred_element_type=F32) + alt * ph
    minus_b = jnp.dot(cs_ref[:, half:2 * half], fold_ref[0, half:2 * half, :], preferred_element_type=F32)
    lo_ref[0] = (a + minus_b).astype(BF16)
    mirror_src = (a - minus_b).astype(BF16)
    rev = _block_reversal()
    row0 = lax.broadcasted_iota(I32, (CHUNK, FOUR_W), 0) == 0
    nbt = T_DFT // CHUNK
    for m in range(nbt):
        blk = mirror_src[CHUNK * (nbt - 1 - m):CHUNK * (nbt - m), :]
        mirrored = jnp.dot(rev, blk, preferred_element_type=F32)
        if m >= 1:
            first = mirror_src[CHUNK * (nbt - m):CHUNK * (nbt - m) + 1, :].astype(F32)
        else:
            first = carry_scr[b][0:1, :]
        hi_ref[0, CHUNK * m:CHUNK * (m + 1), :] = jnp.where(row0, first, mirrored).astype(BF16)
    carry_scr[b] = jnp.broadcast_to(mirror_src[0:1, :].astype(F32), (8, FOUR_W))


def _seq_dft(cs, pq):
    bsz, two_s, _ = pq.shape
    seq = two_s // 2
    scale = 1.0 / (seq * HEAD) ** 0.5
    n_t = seq // 2 // T_DFT
    fold, ph = pl.pallas_call(
        functools.partial(_dft_fold_kernel, scale),
        grid=(bsz,),
        in_specs=[pl.BlockSpec((1, two_s, FOUR_W), lambda b: (b, 0, 0))],
        out_specs=[pl.BlockSpec((1, seq, FOUR_W), lambda b: (b, 0, 0)),
                   pl.BlockSpec((1, 8, FOUR_W), lambda b: (b, 0, 0))],
        out_shape=[jax.ShapeDtypeStruct((bsz, seq, FOUR_W), BF16),
                   jax.ShapeDtypeStruct((bsz, 8, FOUR_W), F32)],
        compiler_params=_cparams(("arbitrary",)),
        name="dft_fold",
    )(pq)
    half_out = jax.ShapeDtypeStruct((bsz, seq // 2, FOUR_W), BF16)
    return pl.pallas_call(
        _seq_dft_kernel,
        grid=(n_t, bsz),
        in_specs=[pl.BlockSpec((T_DFT, seq), lambda jj, b: (n_t - 1 - jj, 0)),
                  pl.BlockSpec((1, seq, FOUR_W), lambda jj, b: (b, 0, 0)),
                  pl.BlockSpec((1, 8, FOUR_W), lambda jj, b: (b, 0, 0))],
        out_specs=[pl.BlockSpec((1, T_DFT, FOUR_W), lambda jj, b: (b, n_t - 1 - jj, 0)),
                   pl.BlockSpec((1, T_DFT, FOUR_W), lambda jj, b: (b, jj, 0))],
        out_shape=[half_out, half_out],
        scratch_shapes=[pltpu.VMEM((bsz, 8, FOUR_W), F32)],
        compiler_params=_cparams(("arbitrary", "arbitrary")),
        name="seq_dft",
    )(cs, fold, ph)


def _mixer_out_kernel(per_b, yac_ref, yblo_ref, ybhi_ref, x_ref, mod_ref, g2_ref, wout_ref, rwt_ref, rb_ref, upper_ref,
                      x1_ref, h2pa_ref, h2pb_ref, route_ref, cnt_ref, cnt_scr):
    i = pl.program_id(0)
    T = T_MIX
    yb = jnp.where(i % per_b < per_b // 2, yblo_ref[...], ybhi_ref[...])

    @pl.when(i == 0)
    def _():
        cnt_scr[...] = jnp.zeros_like(cnt_scr)

    mod = mod_ref[0, 0]
    gate1 = mod[2:3, :]
    shift2 = mod[3:4, :]
    gain2 = g2_ref[0] * (1.0 + mod[4:5, :])
    yac = yac_ref[...]
    ycat = jnp.concatenate([yac[:, 0:CONV_W], yb, yac[:, CONV_W:CONV_W + SGU_W]], axis=1)
    o = jnp.dot(ycat, wout_ref[0], preferred_element_type=F32)
    x1 = x_ref[...] + gate1 * o
    x1_ref[...] = x1
    ms = jnp.mean(x1 * x1, axis=-1, keepdims=True)
    h2 = x1 * lax.rsqrt(ms + EPS) * gain2 + shift2
    h2p = _pack_bf16_pair(h2[:, 0:D // 2], h2[:, D // 2:D])
    h2pa_ref[...] = h2p[:, 0:ROW_W]
    h2pb_ref[...] = h2p[:, ROW_W:2 * ROW_W]

    h_hi = h2.astype(BF16)
    h_lo = (h2 - h_hi.astype(F32)).astype(BF16)
    nt = (((1,), (1,)), ((), ()))
    part = lax.dot_general(rwt_ref[...], h_hi, nt, preferred_element_type=F32)
    logits = (part[0:N_EXP] + part[N_EXP:2 * N_EXP]
              + lax.dot_general(rwt_ref[0:N_EXP, :], h_lo, nt, preferred_element_type=F32))
    mx = jnp.max(logits, axis=0, keepdims=True)
    ex = jnp.exp(logits - mx)
    probs = ex / jnp.sum(ex, axis=0, keepdims=True)
    sel = probs + rb_ref[...]
    sel3 = sel.reshape(N_GRP, EPG, T)
    probs3 = probs.reshape(N_GRP, EPG, T)
    jj = lax.broadcasted_iota(I32, (N_GRP, EPG, T), 1)
    m1 = jnp.max(sel3, axis=1, keepdims=True)
    i1 = jnp.min(jnp.where(sel3 == m1, jj, EPG), axis=1, keepdims=True)
    rest = jnp.where(jj == i1, -jnp.inf, sel3)
    m2 = jnp.max(rest, axis=1, keepdims=True)
    i2 = jnp.min(jnp.where(rest == m2, jj, EPG), axis=1, keepdims=True)
    gscore = m1 + m2
    gg = lax.broadcasted_iota(I32, (N_GRP, 1, T), 0)
    gmax = jnp.max(gscore, axis=0, keepdims=True)
    gidx = jnp.min(jnp.where(gscore == gmax, gg, N_GRP), axis=0, keepdims=True)
    ing = gg == gidx
    pick = lambda a, zero: jnp.sum(jnp.where(ing, a, zero), axis=0)
    p1 = jnp.sum(jnp.where(jj == i1, probs3, 0.0), axis=1, keepdims=True)
    p2 = jnp.sum(jnp.where(jj == i2, probs3, 0.0), axis=1, keepdims=True)
    pa = pick(p1, 0.0)
    pb = pick(p2, 0.0)
    gbase = gidx[0] * EPG
    e0 = gbase + pick(i1, 0)
    e1 = gbase + pick(i2, 0)
    den = pa + pb
    gw0 = pa / den
    gw1 = pb / den

    ee = lax.broadcasted_iota(I32, (N_EXP, T), 0)
    oh0 = ee == e0
    oh1 = ee == e1
    amat = jnp.where(oh0 | oh1, 1.0, 0.0)
    before = jnp.dot(amat.astype(BF16), upper_ref[...], preferred_element_type=F32) + cnt_scr[...]
    r0 = jnp.sum(jnp.where(oh0, before, 0.0), axis=0, keepdims=True)
    r1 = jnp.sum(jnp.where(oh1, before, 0.0), axis=0, keepdims=True)
    cnt_scr[...] = cnt_scr[...] + jnp.sum(amat, axis=1, keepdims=True)
    cnt_ref[...] = cnt_scr[...]

    rid = lax.broadcasted_iota(I32, (8, T), 0)
    route = jnp.zeros((8, T), I32)
    bits = lambda w: lax.bitcast_convert_type(w, I32)
    for k, val in enumerate((e0, e1, r0.astype(I32), r1.astype(I32), bits(gw0), bits(gw1))):
        route = jnp.where(rid == k, val, route)
    route_ref[...] = route


def _mixer_out(yac, yb_lo, yb_hi, x, mod, l, ln2_g, w_out_bf, router_wt, router_b, bsz):
    n_tok = x.shape[0]
    T = T_MIX
    per_b = n_tok // bsz // T
    hb = per_b // 2
    row = lambda w: pl.BlockSpec((T, w), lambda i: (i, 0))
    lo_spec = pl.BlockSpec((T, FOUR_W), lambda i: ((i // per_b) * hb + jnp.minimum(i % per_b, hb - 1), 0))
    hi_spec = pl.BlockSpec((T, FOUR_W), lambda i: ((i // per_b) * hb + jnp.maximum(i % per_b - hb, 0), 0))
    upper = (lax.broadcasted_iota(I32, (T, T), 0) < lax.broadcasted_iota(I32, (T, T), 1)).astype(BF16)
    return pl.pallas_call(
        functools.partial(_mixer_out_kernel, per_b),
        grid=(n_tok // T,),
        in_specs=[
            row(CONV_W + SGU_W), lo_spec, hi_spec, row(D),
            pl.BlockSpec((1, 1, 8, D), lambda i: (l, i // per_b, 0, 0)),
            pl.BlockSpec((1, 1, D), lambda i: (l, 0, 0)),
            pl.BlockSpec((1, D, D), lambda i: (l, 0, 0)),
            pl.BlockSpec((2 * N_EXP, D), lambda i: (0, 0)),
            pl.BlockSpec((N_EXP, 1), lambda i: (0, 0)),
            pl.BlockSpec((T, T), lambda i: (0, 0)),
        ],
        out_specs=[row(D), row(ROW_W), row(ROW_W), pl.BlockSpec((8, T), lambda i: (0, i)),
                   pl.BlockSpec((N_EXP, 1), lambda i: (0, 0))],
        out_shape=[
            jax.ShapeDtypeStruct((n_tok, D), F32),
            jax.ShapeDtypeStruct((n_tok, ROW_W), U32),
            jax.ShapeDtypeStruct((n_tok, ROW_W), U32),
            jax.ShapeDtypeStruct((8, n_tok), I32),
            jax.ShapeDtypeStruct((N_EXP, 1), F32),
        ],
        scratch_shapes=[pltpu.VMEM((N_EXP, 1), F32)],
        compiler_params=_cparams(("arbitrary",)),
        name="mixer_out",
    )(yac, yb_lo, yb_hi, x, mod, ln2_g, w_out_bf, router_wt, router_b, upper)


def _sc_mesh():
    return plsc.VectorSubcoreMesh(core_axis_name="c", subcore_axis_name="s")


def _sc_scatter2(src_a, src_b, idx0, idx1, n_rows):
    n = src_a.shape[0]
    out = jax.ShapeDtypeStruct((n_rows, ROW_W), src_a.dtype)

    @functools.partial(pl.kernel, out_type=[out, out], mesh=_sc_mesh())
    def scatter(xa_hbm, xb_hbm, i0_hbm, i1_hbm, oa_hbm, ob_hbm):
        for x_hbm, o_hbm in ((xa_hbm, oa_hbm), (xb_hbm, ob_hbm)):
            def body(x_vmem, i0_vmem, i1_vmem, o_hbm=o_hbm):
                pltpu.sync_copy(x_vmem, o_hbm.at[i0_vmem.at[0]])
                pltpu.sync_copy(x_vmem, o_hbm.at[i1_vmem.at[0]])

            pltpu.emit_pipeline(
                body, grid=(n // SC_WIN,),
                in_specs=[pl.BlockSpec((SC_WIN, ROW_W), index_map=lambda i: (i, 0)),
                          pl.BlockSpec((1, SC_WIN), index_map=lambda i: (0, i)),
                          pl.BlockSpec((1, SC_WIN), index_map=lambda i: (0, i))],
                out_specs=[],
                core_axis_name=("c", "s"), dimension_semantics=(pltpu.PARALLEL,),
            )(x_hbm, i0_hbm, i1_hbm)

    return scatter(src_a, src_b, idx0, idx1)


def _sc_gather(src_a, src_b, idx):
    m = idx.shape[1]
    out = jax.ShapeDtypeStruct((m, ROW_W), src_a.dtype)

    @functools.partial(pl.kernel, out_type=[out, out], mesh=_sc_mesh())
    def gather(xa_hbm, xb_hbm, i_hbm, oa_hbm, ob_hbm):
        for x_hbm, o_hbm in ((xa_hbm, oa_hbm), (xb_hbm, ob_hbm)):
            def body(i_vmem, o_vmem, x_hbm=x_hbm):
                pltpu.sync_copy(x_hbm.at[i_vmem.at[0]], o_vmem)

            pltpu.emit_pipeline(
                body, grid=(m // SC_WIN,),
                in_specs=[pl.BlockSpec((1, SC_WIN), index_map=lambda i: (0, i))],
                out_specs=[pl.BlockSpec((SC_WIN, ROW_W), index_map=lambda i: (i, 0))],
                core_axis_name=("c", "s"), dimension_semantics=(pltpu.PARALLEL,),
            )(i_hbm, o_hbm)

    return gather(src_a, src_b, idx)


def _experts_kernel(l, n_blk, start_ref, nchunk_ref, xpa_ref, xpb_ref, w1_ref, w3_ref, w2_ref, ypa_ref, ypb_ref,
                    w13_scr, w2_scr, wbuf13, wbuf2, xbuf, ybuf, wsem, xsem, ysem):
    e = pl.program_id(0)
    nc = nchunk_ref[e]
    chunk0 = start_ref[e]
    n_used = start_ref[N_EXP]

    def w_copies(ex):
        slot = ex % (W_AHEAD + 1)
        cps = []
        for p in range(W_PIECES):
            r13 = pl.ds(p * (D // W_PIECES), D // W_PIECES)
            r2 = pl.ds(p * (D_FF // W_PIECES), D_FF // W_PIECES)
            cps.append(pltpu.make_async_copy(w1_ref.at[l, ex, r13], wbuf13.at[slot, 0, r13], wsem.at[slot]))
            cps.append(pltpu.make_async_copy(w3_ref.at[l, ex, r13], wbuf13.at[slot, 1, r13], wsem.at[slot]))
            cps.append(pltpu.make_async_copy(w2_ref.at[l, ex, r2], wbuf2.at[slot, r2], wsem.at[slot]))
        return cps

    def x_copies(g):
        rows = pl.ds(pl.multiple_of(g * BLK, BLK), BLK)
        slot = g % (X_AHEAD + 1)
        return [pltpu.make_async_copy(src.at[rows], xbuf.at[slot, h], xsem.at[slot])
                for h, src in enumerate((xpa_ref, xpb_ref))]

    def y_copies(g):
        rows = pl.ds(pl.multiple_of(g * BLK, BLK), BLK)
        slot = g % 2
        return [pltpu.make_async_copy(ybuf.at[slot, h], dst.at[rows], ysem.at[slot])
                for h, dst in enumerate((ypa_ref, ypb_ref))]

    def start(cps):
        for cp in cps:
            cp.start()

    def wait(cps):
        for cp in cps:
            cp.wait()

    @pl.when(e == 0)
    def _():
        for j in range(X_AHEAD):
            @pl.when(j < n_used)
            def _():
                start(x_copies(j))

        for j in range(W_AHEAD):
            start(w_copies(j))

    @pl.when(e < N_EXP)
    def _():
        wslot = e % (W_AHEAD + 1)
        wait(w_copies(e))

        @pl.when(e + W_AHEAD < N_EXP)
        def _():
            start(w_copies(e + W_AHEAD))

        @pl.when(nc > 0)
        def _():
            w13_scr[:, 0:D_FF] = wbuf13[wslot, 0].astype(BF16)
            w13_scr[:, D_FF:2 * D_FF] = wbuf13[wslot, 1].astype(BF16)
            w2_scr[...] = wbuf2[wslot].astype(BF16)

            def chunk(c, carry):
                g = chunk0 + c
                slot = g % (X_AHEAD + 1)

                @pl.when(g + X_AHEAD < n_used)
                def _():
                    start(x_copies(g + X_AHEAD))

                wait(x_copies(g))

                @pl.when(g >= 2)
                def _():
                    wait(y_copies(g - 2))

                a, b = _unpack_bf16_pair(jnp.concatenate([xbuf[slot, 0], xbuf[slot, 1]], axis=1))
                x = jnp.concatenate([a.astype(BF16), b.astype(BF16)], axis=1)
                h13 = jnp.dot(x, w13_scr[...], preferred_element_type=F32)
                h1 = h13[:, 0:D_FF]
                act = (h1 * jax.nn.sigmoid(h1) * h13[:, D_FF:2 * D_FF]).astype(BF16)
                y = jnp.dot(act, w2_scr[...], preferred_element_type=F32)
                yp = _pack_bf16_pair(y[:, 0:D // 2], y[:, D // 2:D])
                ybuf[g % 2, 0] = yp[:, 0:ROW_W]
                ybuf[g % 2, 1] = yp[:, ROW_W:2 * ROW_W]
                start(y_copies(g))
                return carry

            lax.fori_loop(0, nc, chunk, 0)

    @pl.when(e == N_EXP)
    def _():
        @pl.when(n_used >= 2)
        def _():
            wait(y_copies(n_used - 2))

        wait(y_copies(n_used - 1))
        ybuf[0] = jnp.zeros((2, BLK, ROW_W), U32)

        def fill_one(g, carry):
            rows = pl.ds(pl.multiple_of(g * BLK, BLK), BLK)
            cps = [pltpu.make_async_copy(ybuf.at[0, h], dst.at[rows], ysem.at[0])
                   for h, dst in enumerate((ypa_ref, ypb_ref))]
            start(cps)
            wait(cps)
            return carry

        lax.fori_loop(n_used, n_blk, fill_one, 0)


def _experts(chunk_start, n_chunk, xpa, xpb, w1, w3, w2, l):
    n_rows = xpa.shape[0]
    n_blk = n_rows // BLK
    hbm = pl.BlockSpec(memory_space=pl.ANY)
    half = jax.ShapeDtypeStruct((n_rows, ROW_W), U32)
    return pl.pallas_call(
        functools.partial(_experts_kernel, l, n_blk),
        grid_spec=pltpu.PrefetchScalarGridSpec(
            num_scalar_prefetch=2,
            grid=(N_EXP + 1,),
            in_specs=[hbm, hbm, hbm, hbm, hbm],
            out_specs=[hbm, hbm],
            scratch_shapes=[pltpu.VMEM((D, 2 * D_FF), BF16), pltpu.VMEM((D_FF, D), BF16),
                            pltpu.VMEM((W_AHEAD + 1, 2, D, D_FF), F32), pltpu.VMEM((W_AHEAD + 1, D_FF, D), F32),
                            pltpu.VMEM((X_AHEAD + 1, 2, BLK, ROW_W), U32), pltpu.VMEM((2, 2, BLK, ROW_W), U32),
                            pltpu.SemaphoreType.DMA((W_AHEAD + 1,)), pltpu.SemaphoreType.DMA((X_AHEAD + 1,)),
                            pltpu.SemaphoreType.DMA((2,))],
        ),
        out_shape=[half, half],
        compiler_params=_cparams(("arbitrary",)),
        name="experts",
    )(chunk_start, n_chunk, xpa, xpb, w1, w3, w2)


def _combine_kernel(final, ga0_ref, gb0_ref, ga1_ref, gb1_ref, x1_ref, route_ref, mod_ref, fg_ref, o_ref):
    gws = lax.bitcast_convert_type(route_ref[...], F32)
    gws = jnp.concatenate([gws, jnp.zeros((CHUNK - 8, T_CMB), F32)], axis=0).T
    gw0 = gws[:, 4:5]
    gw1 = gws[:, 5:6]
    a0, b0 = _unpack_bf16_pair(jnp.concatenate([ga0_ref[...], gb0_ref[...]], axis=1))
    a1, b1 = _unpack_bf16_pair(jnp.concatenate([ga1_ref[...], gb1_ref[...]], axis=1))
    y = jnp.concatenate([gw0 * a0 + gw1 * a1, gw0 * b0 + gw1 * b1], axis=1)
    gate2 = mod_ref[0, 0][5:6, :]
    x2 = x1_ref[...] + gate2 * y
    if final:
        ms = jnp.mean(x2 * x2, axis=-1, keepdims=True)
        x2 = x2 * lax.rsqrt(ms + EPS) * fg_ref[...]
    o_ref[...] = x2


def _combine(ga, gb, x1, route, mod, l, final_g, bsz, final):
    n_tok = x1.shape[0]
    n_i = n_tok // T_CMB
    per_b = n_i // bsz
    first = pl.BlockSpec((T_CMB, ROW_W), lambda i: (i, 0))
    second = pl.BlockSpec((T_CMB, ROW_W), lambda i: (i + n_i, 0))
    return pl.pallas_call(
        functools.partial(_combine_kernel, final),
        grid=(n_i,),
        in_specs=[
            first, first, second, second,
            pl.BlockSpec((T_CMB, D), lambda i: (i, 0)),
            pl.BlockSpec((8, T_CMB), lambda i: (0, i)),
            pl.BlockSpec((1, 1, 8, D), lambda i: (l, i // per_b, 0, 0)),
            pl.BlockSpec((1, D), lambda i: (0, 0)),
        ],
        out_specs=pl.BlockSpec((T_CMB, D), lambda i: (i, 0)),
        out_shape=jax.ShapeDtypeStruct((n_tok, D), F32),
        compiler_params=_cparams(("arbitrary",)),
        name="combine",
    )(ga, gb, ga, gb, x1, route, mod, final_g)


def _dft_table_kernel(t1_ref, t2_ref, o_ref):
    half = t2_ref.shape[2]
    c2 = t2_ref[0]
    s2 = t2_ref[1]
    for j in range(TAB_GROUP):
        c1 = t1_ref[j, 0:1, :]
        s1 = t1_ref[j, 1:2, :]
        rows = slice(j * TAB_ROWS, (j + 1) * TAB_ROWS)
        o_ref[rows, 0:half] = (c1 * c2 - s1 * s2).astype(BF16)
        o_ref[rows, half:2 * half] = (-(s1 * c2 + c1 * s2)).astype(BF16)


def _dft_tables(seq):
    scale = 1.0 / (seq * HEAD) ** 0.5
    n_hi = seq // TAB_ROWS
    n = lax.broadcasted_iota(I32, (1, seq // 2), 1)
    kh = lax.broadcasted_iota(I32, (n_hi // 2, 1), 0)
    a1 = ((kh * n) % n_hi).astype(F32) * (2.0 * jnp.pi / n_hi)
    t1 = jnp.stack([jnp.cos(a1), jnp.sin(a1)], axis=1)
    kl = lax.broadcasted_iota(I32, (TAB_ROWS, 1), 0)
    a2 = ((kl * n) % seq).astype(F32) * (2.0 * jnp.pi / seq)
    t2 = jnp.stack([jnp.cos(a2) * scale, jnp.sin(a2) * scale], axis=0)
    cs = pl.pallas_call(
        _dft_table_kernel,
        grid=(n_hi // 2 // TAB_GROUP,),
        in_specs=[pl.BlockSpec((TAB_GROUP, 2, seq // 2), lambda i: (i, 0, 0)),
                  pl.BlockSpec((2, TAB_ROWS, seq // 2), lambda i: (0, 0, 0))],
        out_specs=pl.BlockSpec((TAB_GROUP * TAB_ROWS, seq), lambda i: (i, 0)),
        out_shape=jax.ShapeDtypeStruct((seq // 2, seq), BF16),
        compiler_params=_cparams(("arbitrary",)),
        name="dft_table",
    )(t1, t2)
    d = lax.broadcasted_iota(I32, (FOUR_W, FOUR_W), 0)
    q = lax.broadcasted_iota(I32, (FOUR_W, FOUR_W), 1)
    same = (d // HEAD) == (q // HEAD)
    ang64 = ((d * q) % HEAD).astype(F32) * (2.0 * jnp.pi / HEAD)
    dft64 = jnp.concatenate([jnp.where(same, jnp.cos(ang64), 0.0),
                             jnp.where(same, jnp.sin(ang64), 0.0)], axis=1).astype(BF16)
    return cs, dft64


def _routing_tables(route, counts_f):
    counts = counts_f[:, 0].astype(I32)
    pc = (counts + BLK - 1) // BLK * BLK
    pends = jnp.cumsum(pc)
    pstarts = pends - pc
    eid = lax.broadcasted_iota(I32, (N_EXP, 1), 0)

    def dest_of(e, r):
        return jnp.sum(jnp.where(e[None, :] == eid, pstarts[:, None], 0), axis=0) + r

    dest = jnp.concatenate([dest_of(route[0], route[2]), dest_of(route[1], route[3])])
    chunk_start = jnp.concatenate([pstarts, pends[-1:]]) // BLK
    n_chunk = jnp.concatenate([pc // BLK, jnp.zeros((1,), I32)])
    return dest, chunk_start, n_chunk


def kernel(x, c, ln1_g, ln2_g, w_ada, b_ada, w_in, w_out, conv_w, conv_b, conv_gn_g, conv_gn_b, sgu_ln_g,
           sgu_ln_b, sgu_w, sgu_b, router_w, router_b, exp_w1, exp_w3, exp_w2, final_g):
    bsz, seq, _ = x.shape
    depth = w_in.shape[0]
    n_tok = bsz * seq
    n_blk = (2 * n_tok + N_EXP * (BLK - 1) + BLK - 1) // BLK
    r3 = lambda a: a.reshape(depth, 1, a.shape[-1])

    mod = jnp.pad(_ada_mod(c, w_ada, b_ada).reshape(depth, bsz, 6, D), ((0, 0), (0, 0), (0, 2), (0, 0)))
    cs, dft64 = _dft_tables(seq)
    hd = lax.broadcasted_iota(I32, (CONV_W, CONV_W), 0) // HEAD
    gavg = jnp.where(hd == hd.T, 1.0 / HEAD, 0.0).astype(BF16)
    w_in_bf = w_in.astype(BF16)
    w_out_bf = w_out.astype(BF16)
    sgu_w_bf = sgu_w.astype(BF16)
    sgu_bias = jnp.repeat(jnp.swapaxes(sgu_b, 1, 2), HEAD, axis=2)
    conv_w_p = jnp.pad(conv_w, ((0, 0), (0, 1), (0, 0)))
    rwt_hi = router_w.T.astype(BF16)
    rwt_lo = (router_w.T - rwt_hi.astype(F32)).astype(BF16)
    router_wt = jnp.concatenate([rwt_hi, rwt_lo], axis=0)
    router_bc = router_b.reshape(N_EXP, 1)
    fg = final_g.reshape(1, D)

    for l in range(depth):
        yac, pq = _mixer_in(x, mod, l, r3(ln1_g), w_in_bf, conv_w_p, r3(conv_b), r3(conv_gn_g), r3(conv_gn_b),
                            r3(sgu_ln_g), r3(sgu_ln_b), sgu_w_bf, sgu_bias, dft64, gavg)
        yb_lo, yb_hi = _seq_dft(cs, pq.reshape(bsz, 2 * seq, FOUR_W))
        x1, h2pa, h2pb, route, counts = _mixer_out(
            yac.reshape(n_tok, CONV_W + SGU_W), yb_lo.reshape(n_tok // 2, FOUR_W),
            yb_hi.reshape(n_tok // 2, FOUR_W), x.reshape(n_tok, D),
            mod, l, r3(ln2_g), w_out_bf, router_wt, router_bc, bsz)
        dest, chunk_start, n_chunk = _routing_tables(route, counts)
        d0 = dest[:n_tok].reshape(1, n_tok)
        d1 = dest[n_tok:].reshape(1, n_tok)
        xpa, xpb = _sc_scatter2(h2pa, h2pb, d0, d1, n_blk * BLK)
        ypa, ypb = _experts(chunk_start, n_chunk, xpa, xpb, exp_w1, exp_w3, exp_w2, l)
        ga, gb = _sc_gather(ypa, ypb, dest.reshape(1, 2 * n_tok))
        x = _combine(ga, gb, x1, route, mod, l, fg, bsz, l == depth - 1).reshape(bsz, seq, D)
    return x
```

```python
import functools

import jax
import jax.numpy as jnp
from jax import lax
from jax.experimental import pallas as pl
from jax.experimental.pallas import tpu as pltpu
from jax.experimental.pallas import tpu_sc as plsc

F32 = jnp.float32
BF16 = jnp.bfloat16
I32 = jnp.int32
U32 = jnp.uint32
HIGHEST = lax.Precision.HIGHEST

D = 1024
HEAD = 64
CONV_W = 384
FOUR_W = 256
SGU_W = 384
SGU_HEADS = SGU_W // HEAD
Z_COLS = 2 * CONV_W + FOUR_W + 2 * SGU_W
KSIZE = 31
HALO = 16
CHUNK = 128
N_EXP = 64
N_GRP = 8
EPG = N_EXP // N_GRP
D_FF = D // 2
EPS = 1e-6
GELU_C = 0.7978845608028654
GELU_A = 0.044715

T_MIX = 1024
T_DFT = 512
T_CMB = 512
ROW_W = 256
SC_WIN = 128
BLK = 384
X_AHEAD = 3
W_AHEAD = 1
CONV_ROWS = 64
TAB_ROWS = 64
TAB_GROUP = 4
W_PIECES = 4
VMEM_LIMIT = 56 * 1024 * 1024


def _cparams(sem):
    return pltpu.CompilerParams(dimension_semantics=sem, vmem_limit_bytes=VMEM_LIMIT)


def _pack_bf16_pair(a, b):
    ua = lax.bitcast_convert_type(a.astype(BF16).astype(F32), U32) >> 16
    ub = lax.bitcast_convert_type(b.astype(BF16).astype(F32), U32) & jnp.uint32(0xFFFF0000)
    return ua | ub


def _unpack_bf16_pair(p):
    a = lax.bitcast_convert_type(p << 16, F32)
    b = lax.bitcast_convert_type(p & jnp.uint32(0xFFFF0000), F32)
    return a, b


def _ada_kernel(c_ref, *refs):
    w_refs, b_ref, o_ref = refs[:-2], refs[-2], refs[-1]
    c = c_ref[...]
    ca = c * jax.nn.sigmoid(c)
    tn = w_refs[0].shape[2]
    for j, w_ref in enumerate(w_refs):
        cols = slice(j * tn, (j + 1) * tn)
        o_ref[0, :, cols] = jnp.dot(ca, w_ref[0], precision=HIGHEST, preferred_element_type=F32) + b_ref[0, :, cols]


def _ada_mod(c, w_ada, b_ada):
    depth, _, ncol = w_ada.shape
    bsz = c.shape[0]
    n_slab = 4
    n_half = 2
    tn = ncol // (n_slab * n_half)
    return pl.pallas_call(
        _ada_kernel,
        grid=(depth, n_half),
        in_specs=[pl.BlockSpec((bsz, D), lambda l, h: (0, 0))]
        + [pl.BlockSpec((1, D, tn), functools.partial(lambda j, l, h: (l, 0, n_slab * h + j), j))
           for j in range(n_slab)]
        + [pl.BlockSpec((1, 1, n_slab * tn), lambda l, h: (l, 0, h))],
        out_specs=pl.BlockSpec((1, bsz, n_slab * tn), lambda l, h: (l, 0, h)),
        out_shape=jax.ShapeDtypeStruct((depth, bsz, ncol), F32),
        compiler_params=_cparams(("arbitrary", "arbitrary")),
        name="ada_mod",
    )(c, *([w_ada] * n_slab), b_ada.reshape(depth, 1, ncol))


def _mixer_in_kernel(xm_ref, xp_ref, xn_ref, mod_ref, g1_ref, win_ref, cw_ref, cb_ref, gng_ref, gnb_ref,
                     lng_ref, lnb_ref, sw_ref, sb_ref, dft_ref, gavg_ref,
                     yac_ref, pq_ref, glu_scr, sh_scr, conv_scr):
    i = pl.program_id(1)
    n_i = pl.num_programs(1)
    T = T_MIX
    mod = mod_ref[0, 0]
    shift1 = mod[0:1, :]
    gain1 = g1_ref[0] * (1.0 + mod[1:2, :])

    def norm_mod(x):
        ms = jnp.mean(x * x, axis=-1, keepdims=True)
        return x * lax.rsqrt(ms + EPS) * gain1 + shift1

    h = norm_mod(xm_ref[0]).astype(BF16)
    z = jnp.dot(h, win_ref[0], preferred_element_type=F32)

    hh = norm_mod(jnp.concatenate([xp_ref[0], xn_ref[0]], axis=0)).astype(BF16)
    zh = jnp.dot(hh, win_ref[0, :, 0:2 * CONV_W], preferred_element_type=F32)
    glu_h = zh[:, 0:CONV_W] * jax.nn.sigmoid(zh[:, CONV_W:2 * CONV_W])
    glu_scr[0:HALO, :] = jnp.where(i > 0, glu_h[0:HALO], 0.0)
    glu_scr[HALO + T:2 * HALO + T, :] = jnp.where(i < n_i - 1, glu_h[HALO:2 * HALO], 0.0)
    glu_scr[HALO:HALO + T, :] = z[:, 0:CONV_W] * jax.nn.sigmoid(z[:, CONV_W:2 * CONV_W])

    off = HALO - KSIZE // 2
    for b in range(8):
        shifted = glu_scr[b:b + T + 3 * 8, :]
        sh_scr[b] = shifted[0:T + 16, :].astype(BF16)
        sh_scr[b + 8] = shifted[8:T + 24, :].astype(BF16)
    cw = cw_ref[0].astype(BF16)
    halves = (range(0, KSIZE // 2 + 1), range(KSIZE // 2 + 1, KSIZE))
    for c in range(T // CONV_ROWS):
        r0 = c * CONV_ROWS
        acc = jnp.broadcast_to(cb_ref[0], (CONV_ROWS, CONV_W))
        for taps in halves:
            part = None
            for k in taps:
                a16, b16 = divmod(k + off, 16)
                term = sh_scr[b16, r0 + 16 * a16:r0 + 16 * a16 + CONV_ROWS, :] * cw[k:k + 1, :]
                part = term if part is None else part + term
            acc = acc + part.astype(F32)
        conv_scr[r0:r0 + CONV_ROWS, :] = acc
    hc = conv_scr[...]
    gavg = gavg_ref[...]
    mu = jnp.dot(hc.astype(BF16), gavg, preferred_element_type=F32)
    dc = hc - mu
    var = jnp.dot((dc * dc).astype(BF16), gavg, preferred_element_type=F32)
    hn = dc * lax.rsqrt(var + EPS) * gng_ref[0] + gnb_ref[0]
    ya = hn * jax.nn.sigmoid(hn)
    yac_ref[0, :, 0:CONV_W] = ya.astype(BF16)

    zb = z[:, 2 * CONV_W:2 * CONV_W + FOUR_W].astype(BF16)
    pq = jnp.dot(zb, dft_ref[...], preferred_element_type=F32)
    pq_ref[0, 0] = pq[:, 0:FOUR_W].astype(BF16)
    pq_ref[0, 1] = pq[:, FOUR_W:2 * FOUR_W].astype(BF16)

    c0 = 2 * CONV_W + FOUR_W
    zc = z[:, c0:c0 + 2 * SGU_W]
    zc = 0.5 * zc * (1.0 + jnp.tanh(GELU_C * (zc + GELU_A * (zc * zc * zc))))
    u = zc[:, 0:SGU_W]
    v = zc[:, SGU_W:2 * SGU_W]
    vm = jnp.mean(v, axis=-1, keepdims=True)
    vd = v - vm
    vv = jnp.mean(vd * vd, axis=-1, keepdims=True)
    vn = (vd * lax.rsqrt(vv + EPS) * lng_ref[0] + lnb_ref[0]).astype(BF16)
    n_chunk = T // CHUNK
    lane = lax.broadcasted_iota(I32, (CHUNK, 2 * HEAD), 1)
    for pr in range(SGU_HEADS // 2):
        cols = slice(2 * HEAD * pr, 2 * HEAD * (pr + 1))
        rhs = jnp.concatenate([vn[n * CHUNK:(n + 1) * CHUNK, cols] for n in range(n_chunk)], axis=1)
        lo = jnp.dot(sw_ref[0, 2 * pr], rhs, preferred_element_type=F32)
        hi = jnp.dot(sw_ref[0, 2 * pr + 1], rhs, preferred_element_type=F32)
        for n in range(n_chunk):
            sl = slice(n * 2 * HEAD, (n + 1) * 2 * HEAD)
            vs = jnp.where(lane < HEAD, lo[:, sl], hi[:, sl]) + sb_ref[0, :, cols]
            rows = slice(n * CHUNK, (n + 1) * CHUNK)
            yac_ref[0, rows, CONV_W + 2 * HEAD * pr:CONV_W + 2 * HEAD * (pr + 1)] = (u[rows, cols] * vs).astype(BF16)


def _mixer_in(x, mod, l, ln1_g, w_in_bf, conv_w, conv_b, gn_g, gn_b, ln_g, ln_b, sgu_w_bf, sgu_bias, dft64, gavg):
    bsz, seq, _ = x.shape
    T = T_MIX
    n_i = seq // T
    hb = T // HALO
    n_h = seq // HALO
    vec = lambda w: pl.BlockSpec((1, 1, w), lambda b, i: (l, 0, 0))
    return pl.pallas_call(
        _mixer_in_kernel,
        grid=(bsz, n_i),
        in_specs=[
            pl.BlockSpec((1, T, D), lambda b, i: (b, i, 0)),
            pl.BlockSpec((1, HALO, D), lambda b, i: (b, jnp.maximum(i * hb - 1, 0), 0)),
            pl.BlockSpec((1, HALO, D), lambda b, i: (b, jnp.minimum((i + 1) * hb, n_h - 1), 0)),
            pl.BlockSpec((1, 1, 8, D), lambda b, i: (l, b, 0, 0)),
            vec(D),
            pl.BlockSpec((1, D, Z_COLS), lambda b, i: (l, 0, 0)),
            pl.BlockSpec((1, KSIZE + 1, CONV_W), lambda b, i: (l, 0, 0)),
            vec(CONV_W), vec(CONV_W), vec(CONV_W), vec(SGU_W), vec(SGU_W),
            pl.BlockSpec((1, SGU_HEADS, CHUNK, CHUNK), lambda b, i: (l, 0, 0, 0)),
            pl.BlockSpec((1, CHUNK, SGU_W), lambda b, i: (l, 0, 0)),
            pl.BlockSpec((FOUR_W, 2 * FOUR_W), lambda b, i: (0, 0)),
            pl.BlockSpec((CONV_W, CONV_W), lambda b, i: (0, 0)),
        ],
        out_specs=[
            pl.BlockSpec((1, T, CONV_W + SGU_W), lambda b, i: (b, i, 0)),
            pl.BlockSpec((1, 2, T, FOUR_W), lambda b, i: (b, 0, i, 0)),
        ],
        out_shape=[
            jax.ShapeDtypeStruct((bsz, seq, CONV_W + SGU_W), BF16),
            jax.ShapeDtypeStruct((bsz, 2, seq, FOUR_W), BF16),
        ],
        scratch_shapes=[
            pltpu.VMEM((T + 2 * HALO, CONV_W), F32),
            pltpu.VMEM((16, T + 16, CONV_W), BF16),
            pltpu.VMEM((T, CONV_W), F32),
        ],
        compiler_params=_cparams(("arbitrary", "arbitrary")),
        name="mixer_in",
    )(x, x, x, mod, ln1_g, w_in_bf, conv_w, conv_b, gn_g, gn_b, ln_g, ln_b, sgu_w_bf, sgu_bias, dft64, gavg)


def _dft_fold_kernel(scale, pq_ref, fold_ref, ph_ref):
    seq = pq_ref.shape[1] // 2
    half = seq // 2
    nb = seq // CHUNK
    rev = _block_reversal()
    row0 = lax.broadcasted_iota(I32, (CHUNK, FOUR_W), 0) == 0
    alt = jnp.where(lax.broadcasted_iota(I32, (CHUNK, 1), 0) % 2 == 0, 1.0, -1.0)
    alt_sum = jnp.zeros((1, FOUR_W), F32)
    for part, sign in ((0, 1.0), (1, -1.0)):
        base = part * seq
        for m in range(half // CHUNK):
            lo = pq_ref[0, base + CHUNK * m:base + CHUNK * (m + 1), :].astype(F32)
            up = pq_ref[0, base + CHUNK * (nb - 1 - m):base + CHUNK * (nb - m), :]
            mirrored = jnp.dot(rev, up, preferred_element_type=F32)
            if m >= 1:
                first = pq_ref[0, base + CHUNK * (nb - m):base + CHUNK * (nb - m) + 1, :].astype(F32)
                mirrored = jnp.where(row0, first, mirrored)
            folded = (lo + sign * mirrored).astype(BF16)
            fold_ref[0, part * half + CHUNK * m:part * half + CHUNK * (m + 1), :] = folded
            if part == 0:
                alt_sum = alt_sum + jnp.sum(folded.astype(F32) * alt, axis=0, keepdims=True)
    ph = pq_ref[0, half:half + 1, :].astype(F32) * scale
    rid = lax.broadcasted_iota(I32, (8, FOUR_W), 0)
    ph_ref[0] = jnp.where(rid == 1, alt_sum * scale + ph, ph)


def _block_reversal():
    rr = lax.broadcasted_iota(I32, (CHUNK, CHUNK), 0)
    cc = lax.broadcasted_iota(I32, (CHUNK, CHUNK), 1)
    return jnp.where((rr >= 1) & (cc == CHUNK - rr), 1.0, 0.0).astype(BF16)


def _seq_dft_kernel(cs_ref, fold_ref, ph_ref, lo_ref, hi_ref, carry_scr):
    jj = pl.program_id(0)
    b = pl.program_id(1)
    half = fold_ref.shape[1] // 2
    ph = ph_ref[0, 0:1, :]

    @pl.when(jj == 0)
    def _():
        carry_scr[b] = jnp.broadcast_to(ph_ref[0, 1:2, :], (8, FOUR_W))

    alt = jnp.where(lax.broadcasted_iota(I32, (T_DFT, 1), 0) % 2 == 0, 1.0, -1.0)
    a = jnp.dot(cs_ref[:, 0:half], fold_ref[0, 0:half, :], preferred_element_type=F32) + alt * ph
    minus_b = jnp.dot(cs_ref[:, half:2 * half], fold_ref[0, half:2 * half, :], preferred_element_type=F32)
    lo_ref[0] = (a + minus_b).astype(BF16)
    mirror_src = (a - minus_b).astype(BF16)
    rev = _block_reversal()
    row0 = lax.broadcasted_iota(I32, (CHUNK, FOUR_W), 0) == 0
    nbt = T_DFT // CHUNK
    for m in range(nbt):
        blk = mirror_src[CHUNK * (nbt - 1 - m):CHUNK * (nbt - m), :]
        mirrored = jnp.dot(rev, blk, preferred_element_type=F32)
        if m >= 1:
            first = mirror_src[CHUNK * (nbt - m):CHUNK * (nbt - m) + 1, :].astype(F32)
        else:
            first = carry_scr[b][0:1, :]
        hi_ref[0, CHUNK * m:CHUNK * (m + 1), :] = jnp.where(row0, first, mirrored).astype(BF16)
    carry_scr[b] = jnp.broadcast_to(mirror_src[0:1, :].astype(F32), (8, FOUR_W))


def _seq_dft(cs, pq):
    bsz, two_s, _ = pq.shape
    seq = two_s // 2
    scale = 1.0 / (seq * HEAD) ** 0.5
    n_t = seq // 2 // T_DFT
    fold, ph = pl.pallas_call(
        functools.partial(_dft_fold_kernel, scale),
        grid=(bsz,),
        in_specs=[pl.BlockSpec((1, two_s, FOUR_W), lambda b: (b, 0, 0))],
        out_specs=[pl.BlockSpec((1, seq, FOUR_W), lambda b: (b, 0, 0)),
                   pl.BlockSpec((1, 8, FOUR_W), lambda b: (b, 0, 0))],
        out_shape=[jax.ShapeDtypeStruct((bsz, seq, FOUR_W), BF16),
                   jax.ShapeDtypeStruct((bsz, 8, FOUR_W), F32)],
        compiler_params=_cparams(("arbitrary",)),
        name="dft_fold",
    )(pq)
    half_out = jax.ShapeDtypeStruct((bsz, seq // 2, FOUR_W), BF16)
    return pl.pallas_call(
        _seq_dft_kernel,
        grid=(n_t, bsz),
        in_specs=[pl.BlockSpec((T_DFT, seq), lambda jj, b: (n_t - 1 - jj, 0)),
                  pl.BlockSpec((1, seq, FOUR_W), lambda jj, b: (b, 0, 0)),
                  pl.BlockSpec((1, 8, FOUR_W), lambda jj, b: (b, 0, 0))],
        out_specs=[pl.BlockSpec((1, T_DFT, FOUR_W), lambda jj, b: (b, n_t - 1 - jj, 0)),
                   pl.BlockSpec((1, T_DFT, FOUR_W), lambda jj, b: (b, jj, 0))],
        out_shape=[half_out, half_out],
        scratch_shapes=[pltpu.VMEM((bsz, 8, FOUR_W), F32)],
        compiler_params=_cparams(("arbitrary", "arbitrary")),
        name="seq_dft",
    )(cs, fold, ph)


def _mixer_out_kernel(per_b, yac_ref, yblo_ref, ybhi_ref, x_ref, mod_ref, g2_ref, wout_ref, rwt_ref, rb_ref, upper_ref,
                      x1_ref, h2pa_ref, h2pb_ref, route_ref, cnt_ref, cnt_scr):
    i = pl.program_id(0)
    T = T_MIX
    yb = jnp.where(i % per_b < per_b // 2, yblo_ref[...], ybhi_ref[...])

    @pl.when(i == 0)
    def _():
        cnt_scr[...] = jnp.zeros_like(cnt_scr)

    mod = mod_ref[0, 0]
    gate1 = mod[2:3, :]
    shift2 = mod[3:4, :]
    gain2 = g2_ref[0] * (1.0 + mod[4:5, :])
    yac = yac_ref[...]
    ycat = jnp.concatenate([yac[:, 0:CONV_W], yb, yac[:, CONV_W:CONV_W + SGU_W]], axis=1)
    o = jnp.dot(ycat, wout_ref[0], preferred_element_type=F32)
    x1 = x_ref[...] + gate1 * o
    x1_ref[...] = x1
    ms = jnp.mean(x1 * x1, axis=-1, keepdims=True)
    h2 = x1 * lax.rsqrt(ms + EPS) * gain2 + shift2
    h2p = _pack_bf16_pair(h2[:, 0:D // 2], h2[:, D // 2:D])
    h2pa_ref[...] = h2p[:, 0:ROW_W]
    h2pb_ref[...] = h2p[:, ROW_W:2 * ROW_W]

    h_hi = h2.astype(BF16)
    h_lo = (h2 - h_hi.astype(F32)).astype(BF16)
    nt = (((1,), (1,)), ((), ()))
    part = lax.dot_general(rwt_ref[...], h_hi, nt, preferred_element_type=F32)
    logits = (part[0:N_EXP] + part[N_EXP:2 * N_EXP]
              + lax.dot_general(rwt_ref[0:N_EXP, :], h_lo, nt, preferred_element_type=F32))
    mx = jnp.max(logits, axis=0, keepdims=True)
    ex = jnp.exp(logits - mx)
    probs = ex / jnp.sum(ex, axis=0, keepdims=True)
    sel = probs + rb_ref[...]
    sel3 = sel.reshape(N_GRP, EPG, T)
    probs3 = probs.reshape(N_GRP, EPG, T)
    jj = lax.broadcasted_iota(I32, (N_GRP, EPG, T), 1)
    m1 = jnp.max(sel3, axis=1, keepdims=True)
    i1 = jnp.min(jnp.where(sel3 == m1, jj, EPG), axis=1, keepdims=True)
    rest = jnp.where(jj == i1, -jnp.inf, sel3)
    m2 = jnp.max(rest, axis=1, keepdims=True)
    i2 = jnp.min(jnp.where(rest == m2, jj, EPG), axis=1, keepdims=True)
    gscore = m1 + m2
    gg = lax.broadcasted_iota(I32, (N_GRP, 1, T), 0)
    gmax = jnp.max(gscore, axis=0, keepdims=True)
    gidx = jnp.min(jnp.where(gscore == gmax, gg, N_GRP), axis=0, keepdims=True)
    ing = gg == gidx
    pick = lambda a, zero: jnp.sum(jnp.where(ing, a, zero), axis=0)
    p1 = jnp.sum(jnp.where(jj == i1, probs3, 0.0), axis=1, keepdims=True)
    p2 = jnp.sum(jnp.where(jj == i2, probs3, 0.0), axis=1, keepdims=True)
    pa = pick(p1, 0.0)
    pb = pick(p2, 0.0)
    gbase = gidx[0] * EPG
    e0 = gbase + pick(i1, 0)
    e1 = gbase + pick(i2, 0)
    den = pa + pb
    gw0 = pa / den
    gw1 = pb / den

    ee = lax.broadcasted_iota(I32, (N_EXP, T), 0)
    oh0 = ee == e0
    oh1 = ee == e1
    amat = jnp.where(oh0 | oh1, 1.0, 0.0)
    before = jnp.dot(amat.astype(BF16), upper_ref[...], preferred_element_type=F32) + cnt_scr[...]
    r0 = jnp.sum(jnp.where(oh0, before, 0.0), axis=0, keepdims=True)
    r1 = jnp.sum(jnp.where(oh1, before, 0.0), axis=0, keepdims=True)
    cnt_scr[...] = cnt_scr[...] + jnp.sum(amat, axis=1, keepdims=True)
    cnt_ref[...] = cnt_scr[...]

    rid = lax.broadcasted_iota(I32, (8, T), 0)
    route = jnp.zeros((8, T), I32)
    bits = lambda w: lax.bitcast_convert_type(w, I32)
    for k, val in enumerate((e0, e1, r0.astype(I32), r1.astype(I32), bits(gw0), bits(gw1))):
        route = jnp.where(rid == k, val, route)
    route_ref[...] = route


def _mixer_out(yac, yb_lo, yb_hi, x, mod, l, ln2_g, w_out_bf, router_wt, router_b, bsz):
    n_tok = x.shape[0]
    T = T_MIX
    per_b = n_tok // bsz // T
    hb = per_b // 2
    row = lambda w: pl.BlockSpec((T, w), lambda i: (i, 0))
    lo_spec = pl.BlockSpec((T, FOUR_W), lambda i: ((i // per_b) * hb + jnp.minimum(i % per_b, hb - 1), 0))
    hi_spec = pl.BlockSpec((T, FOUR_W), lambda i: ((i // per_b) * hb + jnp.maximum(i % per_b - hb, 0), 0))
    upper = (lax.broadcasted_iota(I32, (T, T), 0) < lax.broadcasted_iota(I32, (T, T), 1)).astype(BF16)
    return pl.pallas_call(
        functools.partial(_mixer_out_kernel, per_b),
        grid=(n_tok // T,),
        in_specs=[
            row(CONV_W + SGU_W), lo_spec, hi_spec, row(D),
            pl.BlockSpec((1, 1, 8, D), lambda i: (l, i // per_b, 0, 0)),
            pl.BlockSpec((1, 1, D), lambda i: (l, 0, 0)),
            pl.BlockSpec((1, D, D), lambda i: (l, 0, 0)),
            pl.BlockSpec((2 * N_EXP, D), lambda i: (0, 0)),
            pl.BlockSpec((N_EXP, 1), lambda i: (0, 0)),
            pl.BlockSpec((T, T), lambda i: (0, 0)),
        ],
        out_specs=[row(D), row(ROW_W), row(ROW_W), pl.BlockSpec((8, T), lambda i: (0, i)),
                   pl.BlockSpec((N_EXP, 1), lambda i: (0, 0))],
        out_shape=[
            jax.ShapeDtypeStruct((n_tok, D), F32),
            jax.ShapeDtypeStruct((n_tok, ROW_W), U32),
            jax.ShapeDtypeStruct((n_tok, ROW_W), U32),
            jax.ShapeDtypeStruct((8, n_tok), I32),
            jax.ShapeDtypeStruct((N_EXP, 1), F32),
        ],
        scratch_shapes=[pltpu.VMEM((N_EXP, 1), F32)],
        compiler_params=_cparams(("arbitrary",)),
        name="mixer_out",
    )(yac, yb_lo, yb_hi, x, mod, ln2_g, w_out_bf, router_wt, router_b, upper)


def _sc_mesh():
    return plsc.VectorSubcoreMesh(core_axis_name="c", subcore_axis_name="s")


def _sc_scatter2(src_a, src_b, idx0, idx1, n_rows):
    n = src_a.shape[0]
    out = jax.ShapeDtypeStruct((n_rows, ROW_W), src_a.dtype)

    @functools.partial(pl.kernel, out_type=[out, out], mesh=_sc_mesh())
    def scatter(xa_hbm, xb_hbm, i0_hbm, i1_hbm, oa_hbm, ob_hbm):
        for x_hbm, o_hbm in ((xa_hbm, oa_hbm), (xb_hbm, ob_hbm)):
            def body(x_vmem, i0_vmem, i1_vmem, o_hbm=o_hbm):
                pltpu.sync_copy(x_vmem, o_hbm.at[i0_vmem.at[0]])
                pltpu.sync_copy(x_vmem, o_hbm.at[i1_vmem.at[0]])

            pltpu.emit_pipeline(
                body, grid=(n // SC_WIN,),
                in_specs=[pl.BlockSpec((SC_WIN, ROW_W), index_map=lambda i: (i, 0)),
                          pl.BlockSpec((1, SC_WIN), index_map=lambda i: (0, i)),
                          pl.BlockSpec((1, SC_WIN), index_map=lambda i: (0, i))],
                out_specs=[],
                core_axis_name=("c", "s"), dimension_semantics=(pltpu.PARALLEL,),
            )(x_hbm, i0_hbm, i1_hbm)

    return scatter(src_a, src_b, idx0, idx1)


def _sc_gather(src_a, src_b, idx):
    m = idx.shape[1]
    out = jax.ShapeDtypeStruct((m, ROW_W), src_a.dtype)

    @functools.partial(pl.kernel, out_type=[out, out], mesh=_sc_mesh())
    def gather(xa_hbm, xb_hbm, i_hbm, oa_hbm, ob_hbm):
        for x_hbm, o_hbm in ((xa_hbm, oa_hbm), (xb_hbm, ob_hbm)):
            def body(i_vmem, o_vmem, x_hbm=x_hbm):
                pltpu.sync_copy(x_hbm.at[i_vmem.at[0]], o_vmem)

            pltpu.emit_pipeline(
                body, grid=(m // SC_WIN,),
                in_specs=[pl.BlockSpec((1, SC_WIN), index_map=lambda i: (0, i))],
                out_specs=[pl.BlockSpec((SC_WIN, ROW_W), index_map=lambda i: (i, 0))],
                core_axis_name=("c", "s"), dimension_semantics=(pltpu.PARALLEL,),
            )(i_hbm, o_hbm)

    return gather(src_a, src_b, idx)


def _experts_kernel(l, n_blk, start_ref, nchunk_ref, xpa_ref, xpb_ref, w1_ref, w3_ref, w2_ref, ypa_ref, ypb_ref,
                    w13_scr, w2_scr, wbuf13, wbuf2, xbuf, ybuf, wsem, xsem, ysem):
    e = pl.program_id(0)
    nc = nchunk_ref[e]
    chunk0 = start_ref[e]
    n_used = start_ref[N_EXP]

    def w_copies(ex):
        slot = ex % (W_AHEAD + 1)
        cps = []
        for p in range(W_PIECES):
            r13 = pl.ds(p * (D // W_PIECES), D // W_PIECES)
            r2 = pl.ds(p * (D_FF // W_PIECES), D_FF // W_PIECES)
            cps.append(pltpu.make_async_copy(w1_ref.at[l, ex, r13], wbuf13.at[slot, 0, r13], wsem.at[slot]))
            cps.append(pltpu.make_async_copy(w3_ref.at[l, ex, r13], wbuf13.at[slot, 1, r13], wsem.at[slot]))
            cps.append(pltpu.make_async_copy(w2_ref.at[l, ex, r2], wbuf2.at[slot, r2], wsem.at[slot]))
        return cps

    def x_copies(g):
        rows = pl.ds(pl.multiple_of(g * BLK, BLK), BLK)
        slot = g % (X_AHEAD + 1)
        return [pltpu.make_async_copy(src.at[rows], xbuf.at[slot, h], xsem.at[slot])
                for h, src in enumerate((xpa_ref, xpb_ref))]

    def y_copies(g):
        rows = pl.ds(pl.multiple_of(g * BLK, BLK), BLK)
        slot = g % 2
        return [pltpu.make_async_copy(ybuf.at[slot, h], dst.at[rows], ysem.at[slot])
                for h, dst in enumerate((ypa_ref, ypb_ref))]

    def start(cps):
        for cp in cps:
            cp.start()

    def wait(cps):
        for cp in cps:
            cp.wait()

    @pl.when(e == 0)
    def _():
        for j in range(X_AHEAD):
            @pl.when(j < n_used)
            def _():
                start(x_copies(j))

        for j in range(W_AHEAD):
            start(w_copies(j))

    @pl.when(e < N_EXP)
    def _():
        wslot = e % (W_AHEAD + 1)
        wait(w_copies(e))

        @pl.when(e + W_AHEAD < N_EXP)
        def _():
            start(w_copies(e + W_AHEAD))

        @pl.when(nc > 0)
        def _():
            w13_scr[:, 0:D_FF] = wbuf13[wslot, 0].astype(BF16)
            w13_scr[:, D_FF:2 * D_FF] = wbuf13[wslot, 1].astype(BF16)
            w2_scr[...] = wbuf2[wslot].astype(BF16)

            def chunk(c, carry):
                g = chunk0 + c
                slot = g % (X_AHEAD + 1)

                @pl.when(g + X_AHEAD < n_used)
                def _():
                    start(x_copies(g + X_AHEAD))

                wait(x_copies(g))

                @pl.when(g >= 2)
                def _():
                    wait(y_copies(g - 2))

                a, b = _unpack_bf16_pair(jnp.concatenate([xbuf[slot, 0], xbuf[slot, 1]], axis=1))
                x = jnp.concatenate([a.astype(BF16), b.astype(BF16)], axis=1)
                h13 = jnp.dot(x, w13_scr[...], preferred_element_type=F32)
                h1 = h13[:, 0:D_FF]
                act = (h1 * jax.nn.sigmoid(h1) * h13[:, D_FF:2 * D_FF]).astype(BF16)
                y = jnp.dot(act, w2_scr[...], preferred_element_type=F32)
                yp = _pack_bf16_pair(y[:, 0:D // 2], y[:, D // 2:D])
                ybuf[g % 2, 0] = yp[:, 0:ROW_W]
                ybuf[g % 2, 1] = yp[:, ROW_W:2 * ROW_W]
                start(y_copies(g))
                return carry

            lax.fori_loop(0, nc, chunk, 0)

    @pl.when(e == N_EXP)
    def _():
        @pl.when(n_used >= 2)
        def _():
            wait(y_copies(n_used - 2))

        wait(y_copies(n_used - 1))
        ybuf[0] = jnp.zeros((2, BLK, ROW_W), U32)

        def fill_one(g, carry):
            rows = pl.ds(pl.multiple_of(g * BLK, BLK), BLK)
            cps = [pltpu.make_async_copy(ybuf.at[0, h], dst.at[rows], ysem.at[0])
                   for h, dst in enumerate((ypa_ref, ypb_ref))]
            start(cps)
            wait(cps)
            return carry

        lax.fori_loop(n_used, n_blk, fill_one, 0)


def _experts(chunk_start, n_chunk, xpa, xpb, w1, w3, w2, l):
    n_rows = xpa.shape[0]
    n_blk = n_rows // BLK
    hbm = pl.BlockSpec(memory_space=pl.ANY)
    half = jax.ShapeDtypeStruct((n_rows, ROW_W), U32)
    return pl.pallas_call(
        functools.partial(_experts_kernel, l, n_blk),
        grid_spec=pltpu.PrefetchScalarGridSpec(
            num_scalar_prefetch=2,
            grid=(N_EXP + 1,),
            in_specs=[hbm, hbm, hbm, hbm, hbm],
            out_specs=[hbm, hbm],
            scratch_shapes=[pltpu.VMEM((D, 2 * D_FF), BF16), pltpu.VMEM((D_FF, D), BF16),
                            pltpu.VMEM((W_AHEAD + 1, 2, D, D_FF), F32), pltpu.VMEM((W_AHEAD + 1, D_FF, D), F32),
                            pltpu.VMEM((X_AHEAD + 1, 2, BLK, ROW_W), U32), pltpu.VMEM((2, 2, BLK, ROW_W), U32),
                            pltpu.SemaphoreType.DMA((W_AHEAD + 1,)), pltpu.SemaphoreType.DMA((X_AHEAD + 1,)),
                            pltpu.SemaphoreType.DMA((2,))],
        ),
        out_shape=[half, half],
        compiler_params=_cparams(("arbitrary",)),
        name="experts",
    )(chunk_start, n_chunk, xpa, xpb, w1, w3, w2)


def _combine_kernel(final, ga0_ref, gb0_ref, ga1_ref, gb1_ref, x1_ref, route_ref, mod_ref, fg_ref, o_ref):
    gws = lax.bitcast_convert_type(route_ref[...], F32)
    gws = jnp.concatenate([gws, jnp.zeros((CHUNK - 8, T_CMB), F32)], axis=0).T
    gw0 = gws[:, 4:5]
    gw1 = gws[:, 5:6]
    a0, b0 = _unpack_bf16_pair(jnp.concatenate([ga0_ref[...], gb0_ref[...]], axis=1))
    a1, b1 = _unpack_bf16_pair(jnp.concatenate([ga1_ref[...], gb1_ref[...]], axis=1))
    y = jnp.concatenate([gw0 * a0 + gw1 * a1, gw0 * b0 + gw1 * b1], axis=1)
    gate2 = mod_ref[0, 0][5:6, :]
    x2 = x1_ref[...] + gate2 * y
    if final:
        ms = jnp.mean(x2 * x2, axis=-1, keepdims=True)
        x2 = x2 * lax.rsqrt(ms + EPS) * fg_ref[...]
    o_ref[...] = x2


def _combine(ga, gb, x1, route, mod, l, final_g, bsz, final):
    n_tok = x1.shape[0]
    n_i = n_tok // T_CMB
    per_b = n_i // bsz
    first = pl.BlockSpec((T_CMB, ROW_W), lambda i: (i, 0))
    second = pl.BlockSpec((T_CMB, ROW_W), lambda i: (i + n_i, 0))
    return pl.pallas_call(
        functools.partial(_combine_kernel, final),
        grid=(n_i,),
        in_specs=[
            first, first, second, second,
            pl.BlockSpec((T_CMB, D), lambda i: (i, 0)),
            pl.BlockSpec((8, T_CMB), lambda i: (0, i)),
            pl.BlockSpec((1, 1, 8, D), lambda i: (l, i // per_b, 0, 0)),
            pl.BlockSpec((1, D), lambda i: (0, 0)),
        ],
        out_specs=pl.BlockSpec((T_CMB, D), lambda i: (i, 0)),
        out_shape=jax.ShapeDtypeStruct((n_tok, D), F32),
        compiler_params=_cparams(("arbitrary",)),
        name="combine",
    )(ga, gb, ga, gb, x1, route, mod, final_g)


def _dft_table_kernel(t1_ref, t2_ref, o_ref):
    half = t2_ref.shape[2]
    c2 = t2_ref[0]
    s2 = t2_ref[1]
    for j in range(TAB_GROUP):
        c1 = t1_ref[j, 0:1, :]
        s1 = t1_ref[j, 1:2, :]
        rows = slice(j * TAB_ROWS, (j + 1) * TAB_ROWS)
        o_ref[rows, 0:half] = (c1 * c2 - s1 * s2).astype(BF16)
        o_ref[rows, half:2 * half] = (-(s1 * c2 + c1 * s2)).astype(BF16)


def _dft_tables(seq):
    scale = 1.0 / (seq * HEAD) ** 0.5
    n_hi = seq // TAB_ROWS
    n = lax.broadcasted_iota(I32, (1, seq // 2), 1)
    kh = lax.broadcasted_iota(I32, (n_hi // 2, 1), 0)
    a1 = ((kh * n) % n_hi).astype(F32) * (2.0 * jnp.pi / n_hi)
    t1 = jnp.stack([jnp.cos(a1), jnp.sin(a1)], axis=1)
    kl = lax.broadcasted_iota(I32, (TAB_ROWS, 1), 0)
    a2 = ((kl * n) % seq).astype(F32) * (2.0 * jnp.pi / seq)
    t2 = jnp.stack([jnp.cos(a2) * scale, jnp.sin(a2) * scale], axis=0)
    cs = pl.pallas_call(
        _dft_table_kernel,
        grid=(n_hi // 2 // TAB_GROUP,),
        in_specs=[pl.BlockSpec((TAB_GROUP, 2, seq // 2), lambda i: (i, 0, 0)),
                  pl.BlockSpec((2, TAB_ROWS, seq // 2), lambda i: (0, 0, 0))],
        out_specs=pl.BlockSpec((TAB_GROUP * TAB_ROWS, seq), lambda i: (i, 0)),
        out_shape=jax.ShapeDtypeStruct((seq // 2, seq), BF16),
        compiler_params=_cparams(("arbitrary",)),
        name="dft_table",
    )(t1, t2)
    d = lax.broadcasted_iota(I32, (FOUR_W, FOUR_W), 0)
    q = lax.broadcasted_iota(I32, (FOUR_W, FOUR_W), 1)
    same = (d // HEAD) == (q // HEAD)
    ang64 = ((d * q) % HEAD).astype(F32) * (2.0 * jnp.pi / HEAD)
    dft64 = jnp.concatenate([jnp.where(same, jnp.cos(ang64), 0.0),
                             jnp.where(same, jnp.sin(ang64), 0.0)], axis=1).astype(BF16)
    return cs, dft64


def _routing_tables(route, counts_f):
    counts = counts_f[:, 0].astype(I32)
    pc = (counts + BLK - 1) // BLK * BLK
    pends = jnp.cumsum(pc)
    pstarts = pends - pc
    eid = lax.broadcasted_iota(I32, (N_EXP, 1), 0)

    def dest_of(e, r):
        return jnp.sum(jnp.where(e[None, :] == eid, pstarts[:, None], 0), axis=0) + r

    dest = jnp.concatenate([dest_of(route[0], route[2]), dest_of(route[1], route[3])])
    chunk_start = jnp.concatenate([pstarts, pends[-1:]]) // BLK
    n_chunk = jnp.concatenate([pc // BLK, jnp.zeros((1,), I32)])
    return dest, chunk_start, n_chunk


def kernel(x, c, ln1_g, ln2_g, w_ada, b_ada, w_in, w_out, conv_w, conv_b, conv_gn_g, conv_gn_b, sgu_ln_g,
           sgu_ln_b, sgu_w, sgu_b, router_w, router_b, exp_w1, exp_w3, exp_w2, final_g):
    bsz, seq, _ = x.shape
    depth = w_in.shape[0]
    n_tok = bsz * seq
    n_blk = (2 * n_tok + N_EXP * (BLK - 1) + BLK - 1) // BLK
    r3 = lambda a: a.reshape(depth, 1, a.shape[-1])

    mod = jnp.pad(_ada_mod(c, w_ada, b_ada).reshape(depth, bsz, 6, D), ((0, 0), (0, 0), (0, 2), (0, 0)))
    cs, dft64 = _dft_tables(seq)
    hd = lax.broadcasted_iota(I32, (CONV_W, CONV_W), 0) // HEAD
    gavg = jnp.where(hd == hd.T, 1.0 / HEAD, 0.0).astype(BF16)
    w_in_bf = w_in.astype(BF16)
    w_out_bf = w_out.astype(BF16)
    sgu_w_bf = sgu_w.astype(BF16)
    sgu_bias = jnp.repeat(jnp.swapaxes(sgu_b, 1, 2), HEAD, axis=2)
    conv_w_p = jnp.pad(conv_w, ((0, 0), (0, 1), (0, 0)))
    rwt_hi = router_w.T.astype(BF16)
    rwt_lo = (router_w.T - rwt_hi.astype(F32)).astype(BF16)
    router_wt = jnp.concatenate([rwt_hi, rwt_lo], axis=0)
    router_bc = router_b.reshape(N_EXP, 1)
    fg = final_g.reshape(1, D)

    for l in range(depth):
        yac, pq = _mixer_in(x, mod, l, r3(ln1_g), w_in_bf, conv_w_p, r3(conv_b), r3(conv_gn_g), r3(conv_gn_b),
                            r3(sgu_ln_g), r3(sgu_ln_b), sgu_w_bf, sgu_bias, dft64, gavg)
        yb_lo, yb_hi = _seq_dft(cs, pq.reshape(bsz, 2 * seq, FOUR_W))
        x1, h2pa, h2pb, route, counts = _mixer_out(
            yac.reshape(n_tok, CONV_W + SGU_W), yb_lo.reshape(n_tok // 2, FOUR_W),
            yb_hi.reshape(n_tok // 2, FOUR_W), x.reshape(n_tok, D),
            mod, l, r3(ln2_g), w_out_bf, router_wt, router_bc, bsz)
        dest, chunk_start, n_chunk = _routing_tables(route, counts)
        d0 = dest[:n_tok].reshape(1, n_tok)
        d1 = dest[n_tok:].reshape(1, n_tok)
        xpa, xpb = _sc_scatter2(h2pa, h2pb, d0, d1, n_blk * BLK)
        ypa, ypb = _experts(chunk_start, n_chunk, xpa, xpb, exp_w1, exp_w3, exp_w2, l)
        ga, gb = _sc_gather(ypa, ypb, dest.reshape(1, 2 * n_tok))
        x = _combine(ga, gb, x1, route, mod, l, fg, bsz, l == depth - 1).reshape(bsz, seq, D)
    return x
```

```python
import functools

import jax
import jax.numpy as jnp
from jax import lax
from jax.experimental import pallas as pl
from jax.experimental.pallas import tpu as pltpu
from jax.experimental.pallas import tpu_sc as plsc

F32 = jnp.float32
BF16 = jnp.bfloat16
I32 = jnp.int32
U32 = jnp.uint32
HIGHEST = lax.Precision.HIGHEST

D = 1024
HEAD = 64
CONV_W = 384
FOUR_W = 256
SGU_W = 384
SGU_HEADS = SGU_W // HEAD
Z_COLS = 2 * CONV_W + FOUR_W + 2 * SGU_W
KSIZE = 31
HALO = 16
CHUNK = 128
N_EXP = 64
N_GRP = 8
EPG = N_EXP // N_GRP
D_FF = D // 2
EPS = 1e-6
GELU_C = 0.7978845608028654
GELU_A = 0.044715

T_MIX = 1024
T_DFT = 512
T_CMB = 512
ROW_W = 256
SC_WIN = 128
BLK = 384
X_AHEAD = 3
W_AHEAD = 1
CONV_ROWS = 64
TAB_ROWS = 64
TAB_GROUP = 4
W_PIECES = 4
VMEM_LIMIT = 56 * 1024 * 1024


def _cparams(sem):
    return pltpu.CompilerParams(dimension_semantics=sem, vmem_limit_bytes=VMEM_LIMIT)


def _pack_bf16_pair(a, b):
    ua = lax.bitcast_convert_type(a.astype(BF16).astype(F32), U32) >> 16
    ub = lax.bitcast_convert_type(b.astype(BF16).astype(F32), U32) & jnp.uint32(0xFFFF0000)
    return ua | ub


def _unpack_bf16_pair(p):
    a = lax.bitcast_convert_type(p << 16, F32)
    b = lax.bitcast_convert_type(p & jnp.uint32(0xFFFF0000), F32)
    return a, b


def _ada_kernel(c_ref, *refs):
    w_refs, b_ref, o_ref = refs[:-2], refs[-2], refs[-1]
    c = c_ref[...]
    ca = c * jax.nn.sigmoid(c)
    tn = w_refs[0].shape[2]
    for j, w_ref in enumerate(w_refs):
        cols = slice(j * tn, (j + 1) * tn)
        o_ref[0, :, cols] = jnp.dot(ca, w_ref[0], precision=HIGHEST, preferred_element_type=F32) + b_ref[0, :, cols]


def _ada_mod(c, w_ada, b_ada):
    depth, _, ncol = w_ada.shape
    bsz = c.shape[0]
    n_slab = 4
    n_half = 2
    tn = ncol // (n_slab * n_half)
    return pl.pallas_call(
        _ada_kernel,
        grid=(depth, n_half),
        in_specs=[pl.BlockSpec((bsz, D), lambda l, h: (0, 0))]
        + [pl.BlockSpec((1, D, tn), functools.partial(lambda j, l, h: (l, 0, n_slab * h + j), j))
           for j in range(n_slab)]
        + [pl.BlockSpec((1, 1, n_slab * tn), lambda l, h: (l, 0, h))],
        out_specs=pl.BlockSpec((1, bsz, n_slab * tn), lambda l, h: (l, 0, h)),
        out_shape=jax.ShapeDtypeStruct((depth, bsz, ncol), F32),
        compiler_params=_cparams(("arbitrary", "arbitrary")),
        name="ada_mod",
    )(c, *([w_ada] * n_slab), b_ada.reshape(depth, 1, ncol))


def _mixer_in_kernel(xm_ref, xp_ref, xn_ref, mod_ref, g1_ref, win_ref, cw_ref, cb_ref, gng_ref, gnb_ref,
                     lng_ref, lnb_ref, sw_ref, sb_ref, dft_ref, gavg_ref,
                     yac_ref, pq_ref, glu_scr, sh_scr, conv_scr):
    i = pl.program_id(1)
    n_i = pl.num_programs(1)
    T = T_MIX
    mod = mod_ref[0, 0]
    shift1 = mod[0:1, :]
    gain1 = g1_ref[0] * (1.0 + mod[1:2, :])

    def norm_mod(x):
        ms = jnp.mean(x * x, axis=-1, keepdims=True)
        return x * lax.rsqrt(ms + EPS) * gain1 + shift1

    h = norm_mod(xm_ref[0]).astype(BF16)
    z = jnp.dot(h, win_ref[0], preferred_element_type=F32)

    hh = norm_mod(jnp.concatenate([xp_ref[0], xn_ref[0]], axis=0)).astype(BF16)
    zh = jnp.dot(hh, win_ref[0, :, 0:2 * CONV_W], preferred_element_type=F32)
    glu_h = zh[:, 0:CONV_W] * jax.nn.sigmoid(zh[:, CONV_W:2 * CONV_W])
    glu_scr[0:HALO, :] = jnp.where(i > 0, glu_h[0:HALO], 0.0)
    glu_scr[HALO + T:2 * HALO + T, :] = jnp.where(i < n_i - 1, glu_h[HALO:2 * HALO], 0.0)
    glu_scr[HALO:HALO + T, :] = z[:, 0:CONV_W] * jax.nn.sigmoid(z[:, CONV_W:2 * CONV_W])

    off = HALO - KSIZE // 2
    for b in range(8):
        sh_scr[b] = glu_scr[b:b + T + 3 * 8, :]

    for c in range(T // CONV_ROWS):
        r0 = c * CONV_ROWS
        acc = jnp.broadcast_to(cb_ref[0], (CONV_ROWS, CONV_W))
        for k in range(KSIZE):
            a, b = divmod(k + off, 8)
            acc = acc + sh_scr[b, r0 + 8 * a:r0 + 8 * a + CONV_ROWS, :] * cw_ref[0, k:k + 1, :]
        conv_scr[r0:r0 + CONV_ROWS, :] = acc
    hc = conv_scr[...]
    gavg = gavg_ref[...]
    mu = jnp.dot(hc.astype(BF16), gavg, preferred_element_type=F32)
    dc = hc - mu
    var = jnp.dot((dc * dc).astype(BF16), gavg, preferred_element_type=F32)
    hn = dc * lax.rsqrt(var + EPS) * gng_ref[0] + gnb_ref[0]
    ya = hn * jax.nn.sigmoid(hn)
    yac_ref[0, :, 0:CONV_W] = ya.astype(BF16)

    zb = z[:, 2 * CONV_W:2 * CONV_W + FOUR_W].astype(BF16)
    pq = jnp.dot(zb, dft_ref[...], preferred_element_type=F32)
    pq_ref[0, 0] = pq[:, 0:FOUR_W].astype(BF16)
    pq_ref[0, 1] = pq[:, FOUR_W:2 * FOUR_W].astype(BF16)

    c0 = 2 * CONV_W + FOUR_W
    zc = z[:, c0:c0 + 2 * SGU_W]
    zc = 0.5 * zc * (1.0 + jnp.tanh(GELU_C * (zc + GELU_A * (zc * zc * zc))))
    u = zc[:, 0:SGU_W]
    v = zc[:, SGU_W:2 * SGU_W]
    vm = jnp.mean(v, axis=-1, keepdims=True)
    vd = v - vm
    vv = jnp.mean(vd * vd, axis=-1, keepdims=True)
    vn = (vd * lax.rsqrt(vv + EPS) * lng_ref[0] + lnb_ref[0]).astype(BF16)
    n_chunk = T // CHUNK
    lane = lax.broadcasted_iota(I32, (CHUNK, 2 * HEAD), 1)
    for pr in range(SGU_HEADS // 2):
        cols = slice(2 * HEAD * pr, 2 * HEAD * (pr + 1))
        rhs = jnp.concatenate([vn[n * CHUNK:(n + 1) * CHUNK, cols] for n in range(n_chunk)], axis=1)
        lo = jnp.dot(sw_ref[0, 2 * pr], rhs, preferred_element_type=F32)
        hi = jnp.dot(sw_ref[0, 2 * pr + 1], rhs, preferred_element_type=F32)
        for n in range(n_chunk):
            sl = slice(n * 2 * HEAD, (n + 1) * 2 * HEAD)
            vs = jnp.where(lane < HEAD, lo[:, sl], hi[:, sl]) + sb_ref[0, :, cols]
            rows = slice(n * CHUNK, (n + 1) * CHUNK)
            yac_ref[0, rows, CONV_W + 2 * HEAD * pr:CONV_W + 2 * HEAD * (pr + 1)] = (u[rows, cols] * vs).astype(BF16)


def _mixer_in(x, mod, l, ln1_g, w_in_bf, conv_w, conv_b, gn_g, gn_b, ln_g, ln_b, sgu_w_bf, sgu_bias, dft64, gavg):
    bsz, seq, _ = x.shape
    T = T_MIX
    n_i = seq // T
    hb = T // HALO
    n_h = seq // HALO
    vec = lambda w: pl.BlockSpec((1, 1, w), lambda b, i: (l, 0, 0))
    return pl.pallas_call(
        _mixer_in_kernel,
        grid=(bsz, n_i),
        in_specs=[
            pl.BlockSpec((1, T, D), lambda b, i: (b, i, 0)),
            pl.BlockSpec((1, HALO, D), lambda b, i: (b, jnp.maximum(i * hb - 1, 0), 0)),
            pl.BlockSpec((1, HALO, D), lambda b, i: (b, jnp.minimum((i + 1) * hb, n_h - 1), 0)),
            pl.BlockSpec((1, 1, 8, D), lambda b, i: (l, b, 0, 0)),
            vec(D),
            pl.BlockSpec((1, D, Z_COLS), lambda b, i: (l, 0, 0)),
            pl.BlockSpec((1, KSIZE + 1, CONV_W), lambda b, i: (l, 0, 0)),
            vec(CONV_W), vec(CONV_W), vec(CONV_W), vec(SGU_W), vec(SGU_W),
            pl.BlockSpec((1, SGU_HEADS, CHUNK, CHUNK), lambda b, i: (l, 0, 0, 0)),
            pl.BlockSpec((1, CHUNK, SGU_W), lambda b, i: (l, 0, 0)),
            pl.BlockSpec((FOUR_W, 2 * FOUR_W), lambda b, i: (0, 0)),
            pl.BlockSpec((CONV_W, CONV_W), lambda b, i: (0, 0)),
        ],
        out_specs=[
            pl.BlockSpec((1, T, CONV_W + SGU_W), lambda b, i: (b, i, 0)),
            pl.BlockSpec((1, 2, T, FOUR_W), lambda b, i: (b, 0, i, 0)),
        ],
        out_shape=[
            jax.ShapeDtypeStruct((bsz, seq, CONV_W + SGU_W), BF16),
            jax.ShapeDtypeStruct((bsz, 2, seq, FOUR_W), BF16),
        ],
        scratch_shapes=[
            pltpu.VMEM((T + 2 * HALO, CONV_W), F32),
            pltpu.VMEM((8, T + 3 * 8, CONV_W), F32),
            pltpu.VMEM((T, CONV_W), F32),
        ],
        compiler_params=_cparams(("arbitrary", "arbitrary")),
        name="mixer_in",
    )(x, x, x, mod, ln1_g, w_in_bf, conv_w, conv_b, gn_g, gn_b, ln_g, ln_b, sgu_w_bf, sgu_bias, dft64, gavg)


def _dft_fold(scale, pq_ref, fold_ref, ph_ref):
    seq = pq_ref.shape[1] // 2
    half = seq // 2
    nb = seq // CHUNK
    rev = _block_reversal()
    row0 = lax.broadcasted_iota(I32, (CHUNK, FOUR_W), 0) == 0
    alt = jnp.where(lax.broadcasted_iota(I32, (CHUNK, 1), 0) % 2 == 0, 1.0, -1.0)
    alt_sum = jnp.zeros((1, FOUR_W), F32)
    for part, sign in ((0, 1.0), (1, -1.0)):
        base = part * seq
        for m in range(half // CHUNK):
            lo = pq_ref[0, base + CHUNK * m:base + CHUNK * (m + 1), :].astype(F32)
            up = pq_ref[0, base + CHUNK * (nb - 1 - m):base + CHUNK * (nb - m), :]
            mirrored = jnp.dot(rev, up, preferred_element_type=F32)
            if m >= 1:
                first = pq_ref[0, base + CHUNK * (nb - m):base + CHUNK * (nb - m) + 1, :].astype(F32)
                mirrored = jnp.where(row0, first, mirrored)
            folded = (lo + sign * mirrored).astype(BF16)
            fold_ref[part * half + CHUNK * m:part * half + CHUNK * (m + 1), :] = folded
            if part == 0:
                alt_sum = alt_sum + jnp.sum(folded.astype(F32) * alt, axis=0, keepdims=True)
    ph = pq_ref[0, half:half + 1, :].astype(F32) * scale
    rid = lax.broadcasted_iota(I32, (8, FOUR_W), 0)
    ph_ref[...] = jnp.where(rid == 1, alt_sum * scale + ph, ph)


def _block_reversal():
    rr = lax.broadcasted_iota(I32, (CHUNK, CHUNK), 0)
    cc = lax.broadcasted_iota(I32, (CHUNK, CHUNK), 1)
    return jnp.where((rr >= 1) & (cc == CHUNK - rr), 1.0, 0.0).astype(BF16)


def _seq_dft_kernel(scale, cs_ref, pq_ref, lo_ref, hi_ref, fold_scr, ph_scr, carry_scr):
    jj = pl.program_id(0)
    b = pl.program_id(1)
    fold_ref = fold_scr.at[b]
    half = fold_ref.shape[0] // 2

    @pl.when(jj == 0)
    def _():
        _dft_fold(scale, pq_ref, fold_ref, ph_scr.at[b])
        carry_scr[b] = jnp.broadcast_to(ph_scr[b][1:2, :], (8, FOUR_W))

    ph = ph_scr[b][0:1, :]
    alt = jnp.where(lax.broadcasted_iota(I32, (T_DFT, 1), 0) % 2 == 0, 1.0, -1.0)
    a = jnp.dot(cs_ref[:, 0:half], fold_ref[0:half, :], preferred_element_type=F32) + alt * ph
    minus_b = jnp.dot(cs_ref[:, half:2 * half], fold_ref[half:2 * half, :], preferred_element_type=F32)
    lo_ref[0] = (a + minus_b).astype(BF16)
    mirror_src = (a - minus_b).astype(BF16)
    rev = _block_reversal()
    row0 = lax.broadcasted_iota(I32, (CHUNK, FOUR_W), 0) == 0
    nbt = T_DFT // CHUNK
    for m in range(nbt):
        blk = mirror_src[CHUNK * (nbt - 1 - m):CHUNK * (nbt - m), :]
        mirrored = jnp.dot(rev, blk, preferred_element_type=F32)
        if m >= 1:
            first = mirror_src[CHUNK * (nbt - m):CHUNK * (nbt - m) + 1, :].astype(F32)
        else:
            first = carry_scr[b][0:1, :]
        hi_ref[0, CHUNK * m:CHUNK * (m + 1), :] = jnp.where(row0, first, mirrored).astype(BF16)
    carry_scr[b] = jnp.broadcast_to(mirror_src[0:1, :].astype(F32), (8, FOUR_W))


def _seq_dft(cs, pq):
    bsz, two_s, _ = pq.shape
    seq = two_s // 2
    scale = 1.0 / (seq * HEAD) ** 0.5
    n_t = seq // 2 // T_DFT
    half_out = jax.ShapeDtypeStruct((bsz, seq // 2, FOUR_W), BF16)
    return pl.pallas_call(
        functools.partial(_seq_dft_kernel, scale),
        grid=(n_t, bsz),
        in_specs=[pl.BlockSpec((T_DFT, seq), lambda jj, b: (n_t - 1 - jj, 0)),
                  pl.BlockSpec((1, two_s, FOUR_W), lambda jj, b: (jnp.where(jj == 0, b, 0), 0, 0))],
        out_specs=[pl.BlockSpec((1, T_DFT, FOUR_W), lambda jj, b: (b, n_t - 1 - jj, 0)),
                   pl.BlockSpec((1, T_DFT, FOUR_W), lambda jj, b: (b, jj, 0))],
        out_shape=[half_out, half_out],
        scratch_shapes=[pltpu.VMEM((bsz, seq, FOUR_W), BF16), pltpu.VMEM((bsz, 8, FOUR_W), F32),
                        pltpu.VMEM((bsz, 8, FOUR_W), F32)],
        compiler_params=_cparams(("arbitrary", "arbitrary")),
        name="seq_dft",
    )(cs, pq)


def _mixer_out_kernel(per_b, yac_ref, yblo_ref, ybhi_ref, x_ref, mod_ref, g2_ref, wout_ref, rwt_ref, rb_ref, upper_ref,
                      x1_ref, h2pa_ref, h2pb_ref, route_ref, cnt_ref, cnt_scr):
    i = pl.program_id(0)
    T = T_MIX
    yb = jnp.where(i % per_b < per_b // 2, yblo_ref[...], ybhi_ref[...])

    @pl.when(i == 0)
    def _():
        cnt_scr[...] = jnp.zeros_like(cnt_scr)

    mod = mod_ref[0, 0]
    gate1 = mod[2:3, :]
    shift2 = mod[3:4, :]
    gain2 = g2_ref[0] * (1.0 + mod[4:5, :])
    yac = yac_ref[...]
    ycat = jnp.concatenate([yac[:, 0:CONV_W], yb, yac[:, CONV_W:CONV_W + SGU_W]], axis=1)
    o = jnp.dot(ycat, wout_ref[0], preferred_element_type=F32)
    x1 = x_ref[...] + gate1 * o
    x1_ref[...] = x1
    ms = jnp.mean(x1 * x1, axis=-1, keepdims=True)
    h2 = x1 * lax.rsqrt(ms + EPS) * gain2 + shift2
    h2p = _pack_bf16_pair(h2[:, 0:D // 2], h2[:, D // 2:D])
    h2pa_ref[...] = h2p[:, 0:ROW_W]
    h2pb_ref[...] = h2p[:, ROW_W:2 * ROW_W]

    h_hi = h2.astype(BF16)
    h_lo = (h2 - h_hi.astype(F32)).astype(BF16)
    nt = (((1,), (1,)), ((), ()))
    part = lax.dot_general(rwt_ref[...], h_hi, nt, preferred_element_type=F32)
    logits = (part[0:N_EXP] + part[N_EXP:2 * N_EXP]
              + lax.dot_general(rwt_ref[0:N_EXP, :], h_lo, nt, preferred_element_type=F32))
    mx = jnp.max(logits, axis=0, keepdims=True)
    ex = jnp.exp(logits - mx)
    probs = ex / jnp.sum(ex, axis=0, keepdims=True)
    sel = probs + rb_ref[...]
    sel3 = sel.reshape(N_GRP, EPG, T)
    probs3 = probs.reshape(N_GRP, EPG, T)
    jj = lax.broadcasted_iota(I32, (N_GRP, EPG, T), 1)
    m1 = jnp.max(sel3, axis=1, keepdims=True)
    i1 = jnp.min(jnp.where(sel3 == m1, jj, EPG), axis=1, keepdims=True)
    rest = jnp.where(jj == i1, -jnp.inf, sel3)
    m2 = jnp.max(rest, axis=1, keepdims=True)
    i2 = jnp.min(jnp.where(rest == m2, jj, EPG), axis=1, keepdims=True)
    gscore = m1 + m2
    gg = lax.broadcasted_iota(I32, (N_GRP, 1, T), 0)
    gmax = jnp.max(gscore, axis=0, keepdims=True)
    gidx = jnp.min(jnp.where(gscore == gmax, gg, N_GRP), axis=0, keepdims=True)
    ing = gg == gidx
    pick = lambda a, zero: jnp.sum(jnp.where(ing, a, zero), axis=0)
    p1 = jnp.sum(jnp.where(jj == i1, probs3, 0.0), axis=1, keepdims=True)
    p2 = jnp.sum(jnp.where(jj == i2, probs3, 0.0), axis=1, keepdims=True)
    pa = pick(p1, 0.0)
    pb = pick(p2, 0.0)
    gbase = gidx[0] * EPG
    e0 = gbase + pick(i1, 0)
    e1 = gbase + pick(i2, 0)
    den = pa + pb
    gw0 = pa / den
    gw1 = pb / den

    ee = lax.broadcasted_iota(I32, (N_EXP, T), 0)
    oh0 = ee == e0
    oh1 = ee == e1
    amat = jnp.where(oh0 | oh1, 1.0, 0.0)
    before = jnp.dot(amat.astype(BF16), upper_ref[...], preferred_element_type=F32) + cnt_scr[...]
    r0 = jnp.sum(jnp.where(oh0, before, 0.0), axis=0, keepdims=True)
    r1 = jnp.sum(jnp.where(oh1, before, 0.0), axis=0, keepdims=True)
    cnt_scr[...] = cnt_scr[...] + jnp.sum(amat, axis=1, keepdims=True)
    cnt_ref[...] = cnt_scr[...]

    rid = lax.broadcasted_iota(I32, (8, T), 0)
    route = jnp.zeros((8, T), I32)
    bits = lambda w: lax.bitcast_convert_type(w, I32)
    for k, val in enumerate((e0, e1, r0.astype(I32), r1.astype(I32), bits(gw0), bits(gw1))):
        route = jnp.where(rid == k, val, route)
    route_ref[...] = route


def _mixer_out(yac, yb_lo, yb_hi, x, mod, l, ln2_g, w_out_bf, router_wt, router_b, bsz):
    n_tok = x.shape[0]
    T = T_MIX
    per_b = n_tok // bsz // T
    hb = per_b // 2
    row = lambda w: pl.BlockSpec((T, w), lambda i: (i, 0))
    lo_spec = pl.BlockSpec((T, FOUR_W), lambda i: ((i // per_b) * hb + jnp.minimum(i % per_b, hb - 1), 0))
    hi_spec = pl.BlockSpec((T, FOUR_W), lambda i: ((i // per_b) * hb + jnp.maximum(i % per_b - hb, 0), 0))
    upper = (lax.broadcasted_iota(I32, (T, T), 0) < lax.broadcasted_iota(I32, (T, T), 1)).astype(BF16)
    return pl.pallas_call(
        functools.partial(_mixer_out_kernel, per_b),
        grid=(n_tok // T,),
        in_specs=[
            row(CONV_W + SGU_W), lo_spec, hi_spec, row(D),
            pl.BlockSpec((1, 1, 8, D), lambda i: (l, i // per_b, 0, 0)),
            pl.BlockSpec((1, 1, D), lambda i: (l, 0, 0)),
            pl.BlockSpec((1, D, D), lambda i: (l, 0, 0)),
            pl.BlockSpec((2 * N_EXP, D), lambda i: (0, 0)),
            pl.BlockSpec((N_EXP, 1), lambda i: (0, 0)),
            pl.BlockSpec((T, T), lambda i: (0, 0)),
        ],
        out_specs=[row(D), row(ROW_W), row(ROW_W), pl.BlockSpec((8, T), lambda i: (0, i)),
                   pl.BlockSpec((N_EXP, 1), lambda i: (0, 0))],
        out_shape=[
            jax.ShapeDtypeStruct((n_tok, D), F32),
            jax.ShapeDtypeStruct((n_tok, ROW_W), U32),
            jax.ShapeDtypeStruct((n_tok, ROW_W), U32),
            jax.ShapeDtypeStruct((8, n_tok), I32),
            jax.ShapeDtypeStruct((N_EXP, 1), F32),
        ],
        scratch_shapes=[pltpu.VMEM((N_EXP, 1), F32)],
        compiler_params=_cparams(("arbitrary",)),
        name="mixer_out",
    )(yac, yb_lo, yb_hi, x, mod, ln2_g, w_out_bf, router_wt, router_b, upper)


def _sc_mesh():
    return plsc.VectorSubcoreMesh(core_axis_name="c", subcore_axis_name="s")


def _sc_scatter2(src_a, src_b, idx0, idx1, n_rows):
    n = src_a.shape[0]
    out = jax.ShapeDtypeStruct((n_rows, ROW_W), src_a.dtype)

    @functools.partial(pl.kernel, out_type=[out, out], mesh=_sc_mesh())
    def scatter(xa_hbm, xb_hbm, i0_hbm, i1_hbm, oa_hbm, ob_hbm):
        for x_hbm, o_hbm in ((xa_hbm, oa_hbm), (xb_hbm, ob_hbm)):
            def body(x_vmem, i0_vmem, i1_vmem, o_hbm=o_hbm):
                pltpu.sync_copy(x_vmem, o_hbm.at[i0_vmem.at[0]])
                pltpu.sync_copy(x_vmem, o_hbm.at[i1_vmem.at[0]])

            pltpu.emit_pipeline(
                body, grid=(n // SC_WIN,),
                in_specs=[pl.BlockSpec((SC_WIN, ROW_W), index_map=lambda i: (i, 0)),
                          pl.BlockSpec((1, SC_WIN), index_map=lambda i: (0, i)),
                          pl.BlockSpec((1, SC_WIN), index_map=lambda i: (0, i))],
                out_specs=[],
                core_axis_name=("c", "s"), dimension_semantics=(pltpu.PARALLEL,),
            )(x_hbm, i0_hbm, i1_hbm)

    return scatter(src_a, src_b, idx0, idx1)


def _sc_gather(src_a, src_b, idx):
    m = idx.shape[1]
    out = jax.ShapeDtypeStruct((m, ROW_W), src_a.dtype)

    @functools.partial(pl.kernel, out_type=[out, out], mesh=_sc_mesh())
    def gather(xa_hbm, xb_hbm, i_hbm, oa_hbm, ob_hbm):
        for x_hbm, o_hbm in ((xa_hbm, oa_hbm), (xb_hbm, ob_hbm)):
            def body(i_vmem, o_vmem, x_hbm=x_hbm):
                pltpu.sync_copy(x_hbm.at[i_vmem.at[0]], o_vmem)

            pltpu.emit_pipeline(
                body, grid=(m // SC_WIN,),
                in_specs=[pl.BlockSpec((1, SC_WIN), index_map=lambda i: (0, i))],
                out_specs=[pl.BlockSpec((SC_WIN, ROW_W), index_map=lambda i: (i, 0))],
                core_axis_name=("c", "s"), dimension_semantics=(pltpu.PARALLEL,),
            )(i_hbm, o_hbm)

    return gather(src_a, src_b, idx)


def _experts_kernel(l, n_blk, start_ref, nchunk_ref, xpa_ref, xpb_ref, w1_ref, w3_ref, w2_ref, ypa_ref, ypb_ref,
                    w13_scr, w2_scr, wbuf13, wbuf2, xbuf, ybuf, wsem, xsem, ysem):
    e = pl.program_id(0)
    nc = nchunk_ref[e]
    chunk0 = start_ref[e]
    n_used = start_ref[N_EXP]

    def w_copies(ex):
        slot = ex % (W_AHEAD + 1)
        cps = []
        for p in range(W_PIECES):
            r13 = pl.ds(p * (D // W_PIECES), D // W_PIECES)
            r2 = pl.ds(p * (D_FF // W_PIECES), D_FF // W_PIECES)
            cps.append(pltpu.make_async_copy(w1_ref.at[l, ex, r13], wbuf13.at[slot, 0, r13], wsem.at[slot]))
            cps.append(pltpu.make_async_copy(w3_ref.at[l, ex, r13], wbuf13.at[slot, 1, r13], wsem.at[slot]))
            cps.append(pltpu.make_async_copy(w2_ref.at[l, ex, r2], wbuf2.at[slot, r2], wsem.at[slot]))
        return cps

    def x_copies(g):
        rows = pl.ds(pl.multiple_of(g * BLK, BLK), BLK)
        slot = g % (X_AHEAD + 1)
        return [pltpu.make_async_copy(src.at[rows], xbuf.at[slot, h], xsem.at[slot])
                for h, src in enumerate((xpa_ref, xpb_ref))]

    def y_copies(g):
        rows = pl.ds(pl.multiple_of(g * BLK, BLK), BLK)
        slot = g % 2
        return [pltpu.make_async_copy(ybuf.at[slot, h], dst.at[rows], ysem.at[slot])
                for h, dst in enumerate((ypa_ref, ypb_ref))]

    def start(cps):
        for cp in cps:
            cp.start()

    def wait(cps):
        for cp in cps:
            cp.wait()

    @pl.when(e == 0)
    def _():
        for j in range(X_AHEAD):
            @pl.when(j < n_used)
            def _():
                start(x_copies(j))

        for j in range(W_AHEAD):
            start(w_copies(j))

    @pl.when(e < N_EXP)
    def _():
        wslot = e % (W_AHEAD + 1)
        wait(w_copies(e))

        @pl.when(e + W_AHEAD < N_EXP)
        def _():
            start(w_copies(e + W_AHEAD))

        @pl.when(nc > 0)
        def _():
            w13_scr[:, 0:D_FF] = wbuf13[wslot, 0].astype(BF16)
            w13_scr[:, D_FF:2 * D_FF] = wbuf13[wslot, 1].astype(BF16)
            w2_scr[...] = wbuf2[wslot].astype(BF16)

            def chunk(c, carry):
                g = chunk0 + c
                slot = g % (X_AHEAD + 1)

                @pl.when(g + X_AHEAD < n_used)
                def _():
                    start(x_copies(g + X_AHEAD))

                wait(x_copies(g))

                @pl.when(g >= 2)
                def _():
                    wait(y_copies(g - 2))

                a, b = _unpack_bf16_pair(jnp.concatenate([xbuf[slot, 0], xbuf[slot, 1]], axis=1))
                x = jnp.concatenate([a.astype(BF16), b.astype(BF16)], axis=1)
                h13 = jnp.dot(x, w13_scr[...], preferred_element_type=F32)
                h1 = h13[:, 0:D_FF]
                act = (h1 * jax.nn.sigmoid(h1) * h13[:, D_FF:2 * D_FF]).astype(BF16)
                y = jnp.dot(act, w2_scr[...], preferred_element_type=F32)
                yp = _pack_bf16_pair(y[:, 0:D // 2], y[:, D // 2:D])
                ybuf[g % 2, 0] = yp[:, 0:ROW_W]
                ybuf[g % 2, 1] = yp[:, ROW_W:2 * ROW_W]
                start(y_copies(g))
                return carry

            lax.fori_loop(0, nc, chunk, 0)

    @pl.when(e == N_EXP)
    def _():
        @pl.when(n_used >= 2)
        def _():
            wait(y_copies(n_used - 2))

        wait(y_copies(n_used - 1))
        ybuf[0] = jnp.zeros((2, BLK, ROW_W), U32)

        def fill_one(g, carry):
            rows = pl.ds(pl.multiple_of(g * BLK, BLK), BLK)
            cps = [pltpu.make_async_copy(ybuf.at[0, h], dst.at[rows], ysem.at[0])
                   for h, dst in enumerate((ypa_ref, ypb_ref))]
            start(cps)
            wait(cps)
            return carry

        lax.fori_loop(n_used, n_blk, fill_one, 0)


def _experts(chunk_start, n_chunk, xpa, xpb, w1, w3, w2, l):
    n_rows = xpa.shape[0]
    n_blk = n_rows // BLK
    hbm = pl.BlockSpec(memory_space=pl.ANY)
    half = jax.ShapeDtypeStruct((n_rows, ROW_W), U32)
    return pl.pallas_call(
        functools.partial(_experts_kernel, l, n_blk),
        grid_spec=pltpu.PrefetchScalarGridSpec(
            num_scalar_prefetch=2,
            grid=(N_EXP + 1,),
            in_specs=[hbm, hbm, hbm, hbm, hbm],
            out_specs=[hbm, hbm],
            scratch_shapes=[pltpu.VMEM((D, 2 * D_FF), BF16), pltpu.VMEM((D_FF, D), BF16),
                            pltpu.VMEM((W_AHEAD + 1, 2, D, D_FF), F32), pltpu.VMEM((W_AHEAD + 1, D_FF, D), F32),
                            pltpu.VMEM((X_AHEAD + 1, 2, BLK, ROW_W), U32), pltpu.VMEM((2, 2, BLK, ROW_W), U32),
                            pltpu.SemaphoreType.DMA((W_AHEAD + 1,)), pltpu.SemaphoreType.DMA((X_AHEAD + 1,)),
                            pltpu.SemaphoreType.DMA((2,))],
        ),
        out_shape=[half, half],
        compiler_params=_cparams(("arbitrary",)),
        name="experts",
    )(chunk_start, n_chunk, xpa, xpb, w1, w3, w2)


def _combine_kernel(final, ga0_ref, gb0_ref, ga1_ref, gb1_ref, x1_ref, route_ref, mod_ref, fg_ref, o_ref):
    gws = lax.bitcast_convert_type(route_ref[...], F32)
    gws = jnp.concatenate([gws, jnp.zeros((CHUNK - 8, T_CMB), F32)], axis=0).T
    gw0 = gws[:, 4:5]
    gw1 = gws[:, 5:6]
    a0, b0 = _unpack_bf16_pair(jnp.concatenate([ga0_ref[...], gb0_ref[...]], axis=1))
    a1, b1 = _unpack_bf16_pair(jnp.concatenate([ga1_ref[...], gb1_ref[...]], axis=1))
    y = jnp.concatenate([gw0 * a0 + gw1 * a1, gw0 * b0 + gw1 * b1], axis=1)
    gate2 = mod_ref[0, 0][5:6, :]
    x2 = x1_ref[...] + gate2 * y
    if final:
        ms = jnp.mean(x2 * x2, axis=-1, keepdims=True)
        x2 = x2 * lax.rsqrt(ms + EPS) * fg_ref[...]
    o_ref[...] = x2


def _combine(ga, gb, x1, route, mod, l, final_g, bsz, final):
    n_tok = x1.shape[0]
    n_i = n_tok // T_CMB
    per_b = n_i // bsz
    first = pl.BlockSpec((T_CMB, ROW_W), lambda i: (i, 0))
    second = pl.BlockSpec((T_CMB, ROW_W), lambda i: (i + n_i, 0))
    return pl.pallas_call(
        functools.partial(_combine_kernel, final),
        grid=(n_i,),
        in_specs=[
            first, first, second, second,
            pl.BlockSpec((T_CMB, D), lambda i: (i, 0)),
            pl.BlockSpec((8, T_CMB), lambda i: (0, i)),
            pl.BlockSpec((1, 1, 8, D), lambda i: (l, i // per_b, 0, 0)),
            pl.BlockSpec((1, D), lambda i: (0, 0)),
        ],
        out_specs=pl.BlockSpec((T_CMB, D), lambda i: (i, 0)),
        out_shape=jax.ShapeDtypeStruct((n_tok, D), F32),
        compiler_params=_cparams(("arbitrary",)),
        name="combine",
    )(ga, gb, ga, gb, x1, route, mod, final_g)


def _dft_table_kernel(t1_ref, t2_ref, o_ref):
    half = t2_ref.shape[2]
    c2 = t2_ref[0]
    s2 = t2_ref[1]
    for j in range(TAB_GROUP):
        c1 = t1_ref[j, 0:1, :]
        s1 = t1_ref[j, 1:2, :]
        rows = slice(j * TAB_ROWS, (j + 1) * TAB_ROWS)
        o_ref[rows, 0:half] = (c1 * c2 - s1 * s2).astype(BF16)
        o_ref[rows, half:2 * half] = (-(s1 * c2 + c1 * s2)).astype(BF16)


def _dft_tables(seq):
    scale = 1.0 / (seq * HEAD) ** 0.5
    n_hi = seq // TAB_ROWS
    n = lax.broadcasted_iota(I32, (1, seq // 2), 1)
    kh = lax.broadcasted_iota(I32, (n_hi // 2, 1), 0)
    a1 = ((kh * n) % n_hi).astype(F32) * (2.0 * jnp.pi / n_hi)
    t1 = jnp.stack([jnp.cos(a1), jnp.sin(a1)], axis=1)
    kl = lax.broadcasted_iota(I32, (TAB_ROWS, 1), 0)
    a2 = ((kl * n) % seq).astype(F32) * (2.0 * jnp.pi / seq)
    t2 = jnp.stack([jnp.cos(a2) * scale, jnp.sin(a2) * scale], axis=0)
    cs = pl.pallas_call(
        _dft_table_kernel,
        grid=(n_hi // 2 // TAB_GROUP,),
        in_specs=[pl.BlockSpec((TAB_GROUP, 2, seq // 2), lambda i: (i, 0, 0)),
                  pl.BlockSpec((2, TAB_ROWS, seq // 2), lambda i: (0, 0, 0))],
        out_specs=pl.BlockSpec((TAB_GROUP * TAB_ROWS, seq), lambda i: (i, 0)),
        out_shape=jax.ShapeDtypeStruct((seq // 2, seq), BF16),
        compiler_params=_cparams(("arbitrary",)),
        name="dft_table",
    )(t1, t2)
    d = lax.broadcasted_iota(I32, (FOUR_W, FOUR_W), 0)
    q = lax.broadcasted_iota(I32, (FOUR_W, FOUR_W), 1)
    same = (d // HEAD) == (q // HEAD)
    ang64 = ((d * q) % HEAD).astype(F32) * (2.0 * jnp.pi / HEAD)
    dft64 = jnp.concatenate([jnp.where(same, jnp.cos(ang64), 0.0),
                             jnp.where(same, jnp.sin(ang64), 0.0)], axis=1).astype(BF16)
    return cs, dft64


def _routing_tables(route, counts_f):
    counts = counts_f[:, 0].astype(I32)
    pc = (counts + BLK - 1) // BLK * BLK
    pends = jnp.cumsum(pc)
    pstarts = pends - pc
    eid = lax.broadcasted_iota(I32, (N_EXP, 1), 0)

    def dest_of(e, r):
        return jnp.sum(jnp.where(e[None, :] == eid, pstarts[:, None], 0), axis=0) + r

    dest = jnp.concatenate([dest_of(route[0], route[2]), dest_of(route[1], route[3])])
    chunk_start = jnp.concatenate([pstarts, pends[-1:]]) // BLK
    n_chunk = jnp.concatenate([pc // BLK, jnp.zeros((1,), I32)])
    return dest, chunk_start, n_chunk


def kernel(x, c, ln1_g, ln2_g, w_ada, b_ada, w_in, w_out, conv_w, conv_b, conv_gn_g, conv_gn_b, sgu_ln_g,
           sgu_ln_b, sgu_w, sgu_b, router_w, router_b, exp_w1, exp_w3, exp_w2, final_g):
    bsz, seq, _ = x.shape
    depth = w_in.shape[0]
    n_tok = bsz * seq
    n_blk = (2 * n_tok + N_EXP * (BLK - 1) + BLK - 1) // BLK
    r3 = lambda a: a.reshape(depth, 1, a.shape[-1])

    mod = jnp.pad(_ada_mod(c, w_ada, b_ada).reshape(depth, bsz, 6, D), ((0, 0), (0, 0), (0, 2), (0, 0)))
    cs, dft64 = _dft_tables(seq)
    hd = lax.broadcasted_iota(I32, (CONV_W, CONV_W), 0) // HEAD
    gavg = jnp.where(hd == hd.T, 1.0 / HEAD, 0.0).astype(BF16)
    w_in_bf = w_in.astype(BF16)
    w_out_bf = w_out.astype(BF16)
    sgu_w_bf = sgu_w.astype(BF16)
    sgu_bias = jnp.repeat(jnp.swapaxes(sgu_b, 1, 2), HEAD, axis=2)
    conv_w_p = jnp.pad(conv_w, ((0, 0), (0, 1), (0, 0)))
    rwt_hi = router_w.T.astype(BF16)
    rwt_lo = (router_w.T - rwt_hi.astype(F32)).astype(BF16)
    router_wt = jnp.concatenate([rwt_hi, rwt_lo], axis=0)
    router_bc = router_b.reshape(N_EXP, 1)
    fg = final_g.reshape(1, D)

    for l in range(depth):
        yac, pq = _mixer_in(x, mod, l, r3(ln1_g), w_in_bf, conv_w_p, r3(conv_b), r3(conv_gn_g), r3(conv_gn_b),
                            r3(sgu_ln_g), r3(sgu_ln_b), sgu_w_bf, sgu_bias, dft64, gavg)
        yb_lo, yb_hi = _seq_dft(cs, pq.reshape(bsz, 2 * seq, FOUR_W))
        x1, h2pa, h2pb, route, counts = _mixer_out(
            yac.reshape(n_tok, CONV_W + SGU_W), yb_lo.reshape(n_tok // 2, FOUR_W),
            yb_hi.reshape(n_tok // 2, FOUR_W), x.reshape(n_tok, D),
            mod, l, r3(ln2_g), w_out_bf, router_wt, router_bc, bsz)
        dest, chunk_start, n_chunk = _routing_tables(route, counts)
        d0 = dest[:n_tok].reshape(1, n_tok)
        d1 = dest[n_tok:].reshape(1, n_tok)
        xpa, xpb = _sc_scatter2(h2pa, h2pb, d0, d1, n_blk * BLK)
        ypa, ypb = _experts(chunk_start, n_chunk, xpa, xpb, exp_w1, exp_w3, exp_w2, l)
        ga, gb = _sc_gather(ypa, ypb, dest.reshape(1, 2 * n_tok))
        x = _combine(ga, gb, x1, route, mod, l, fg, bsz, l == depth - 1).reshape(bsz, seq, D)
    return x
```

```python
import functools

import jax
import jax.numpy as jnp
from jax import lax
from jax.experimental import pallas as pl
from jax.experimental.pallas import tpu as pltpu
from jax.experimental.pallas import tpu_sc as plsc

F32 = jnp.float32
BF16 = jnp.bfloat16
I32 = jnp.int32
U32 = jnp.uint32

D = 1024
HEAD = 64
CONV_W = 384
FOUR_W = 256
SGU_W = 384
SGU_HEADS = SGU_W // HEAD
Z_COLS = 2 * CONV_W + FOUR_W + 2 * SGU_W
KSIZE = 31
HALO = 16
CHUNK = 128
N_EXP = 64
N_GRP = 8
EPG = N_EXP // N_GRP
D_FF = D // 2
EPS = 1e-6
GELU_C = 0.7978845608028654
GELU_A = 0.044715

T_MIX = 1024
T_DFT = 512
T_CMB = 1024
ROW_W = 256
SC_WIN = 128
BLK = 384
X_AHEAD = 3
W_AHEAD = 1
CONV_ROWS = 64
TAB_ROWS = 64
TAB_GROUP = 4
W_PIECES = 4
VMEM_LIMIT = 56 * 1024 * 1024


def _cparams(sem):
    return pltpu.CompilerParams(dimension_semantics=sem, vmem_limit_bytes=VMEM_LIMIT)


def _pack_bf16_pair(a, b):
    ua = lax.bitcast_convert_type(a.astype(BF16).astype(F32), U32) >> 16
    ub = lax.bitcast_convert_type(b.astype(BF16).astype(F32), U32) & jnp.uint32(0xFFFF0000)
    return ua | ub


def _unpack_bf16_pair(p):
    a = lax.bitcast_convert_type(p << 16, F32)
    b = lax.bitcast_convert_type(p & jnp.uint32(0xFFFF0000), F32)
    return a, b


def _ada_kernel(c_ref, *refs):
    w_refs, b_ref, o_ref = refs[:-2], refs[-2], refs[-1]
    c = c_ref[...]
    ca = c * jax.nn.sigmoid(c)
    ca_hi = ca.astype(BF16)
    ca_lo = (ca - ca_hi.astype(F32)).astype(BF16)
    tn = w_refs[0].shape[2]
    for j, w_ref in enumerate(w_refs):
        cols = slice(j * tn, (j + 1) * tn)
        w = w_ref[0]
        w_hi = w.astype(BF16)
        w_lo = (w - w_hi.astype(F32)).astype(BF16)
        acc = jnp.dot(ca_hi, w_hi, preferred_element_type=F32)
        acc = acc + jnp.dot(ca_hi, w_lo, preferred_element_type=F32)
        acc = acc + jnp.dot(ca_lo, w_hi, preferred_element_type=F32)
        o_ref[0, :, cols] = acc + b_ref[0, :, cols]


def _ada_mod(c, w_ada, b_ada):
    depth, _, ncol = w_ada.shape
    bsz = c.shape[0]
    n_slab = 4
    n_half = 2
    tn = ncol // (n_slab * n_half)
    return pl.pallas_call(
        _ada_kernel,
        grid=(depth, n_half),
        in_specs=[pl.BlockSpec((bsz, D), lambda l, h: (0, 0))]
        + [pl.BlockSpec((1, D, tn), functools.partial(lambda j, l, h: (l, 0, n_slab * h + j), j))
           for j in range(n_slab)]
        + [pl.BlockSpec((1, 1, n_slab * tn), lambda l, h: (l, 0, h))],
        out_specs=pl.BlockSpec((1, bsz, n_slab * tn), lambda l, h: (l, 0, h)),
        out_shape=jax.ShapeDtypeStruct((depth, bsz, ncol), F32),
        compiler_params=_cparams(("arbitrary", "arbitrary")),
        name="ada_mod",
    )(c, *([w_ada] * n_slab), b_ada.reshape(depth, 1, ncol))


def _mixer_in_kernel(xm_ref, xp_ref, xn_ref, mod_ref, g1_ref, win_ref, cw_ref, cb_ref, gng_ref, gnb_ref,
                     lng_ref, lnb_ref, sw_ref, sb_ref, dft_ref, gavg_ref,
                     yac_ref, pq_ref, glu_scr, sh_scr, conv_scr):
    i = pl.program_id(1)
    n_i = pl.num_programs(1)
    T = T_MIX
    mod = mod_ref[0, 0]
    shift1 = mod[0:1, :]
    gain1 = g1_ref[0] * (1.0 + mod[1:2, :])

    def norm_mod(x):
        ms = jnp.mean(x * x, axis=-1, keepdims=True)
        return x * lax.rsqrt(ms + EPS) * gain1 + shift1

    h = norm_mod(xm_ref[0]).astype(BF16)
    z = jnp.dot(h, win_ref[0], preferred_element_type=F32)

    hh = norm_mod(jnp.concatenate([xp_ref[0], xn_ref[0]], axis=0)).astype(BF16)
    zh = jnp.dot(hh, win_ref[0, :, 0:2 * CONV_W], preferred_element_type=F32)
    glu_h = zh[:, 0:CONV_W] * jax.nn.sigmoid(zh[:, CONV_W:2 * CONV_W])
    glu_scr[0:HALO, :] = jnp.where(i > 0, glu_h[0:HALO], 0.0)
    glu_scr[HALO + T:2 * HALO + T, :] = jnp.where(i < n_i - 1, glu_h[HALO:2 * HALO], 0.0)
    glu_scr[HALO:HALO + T, :] = z[:, 0:CONV_W] * jax.nn.sigmoid(z[:, CONV_W:2 * CONV_W])

    off = HALO - KSIZE // 2
    for b in range(8):
        sh_scr[b] = glu_scr[b:b + T + 3 * 8, :]

    for c in range(T // CONV_ROWS):
        r0 = c * CONV_ROWS
        acc = jnp.broadcast_to(cb_ref[0], (CONV_ROWS, CONV_W))
        for k in range(KSIZE):
            a, b = divmod(k + off, 8)
            acc = acc + sh_scr[b, r0 + 8 * a:r0 + 8 * a + CONV_ROWS, :] * cw_ref[0, k:k + 1, :]
        conv_scr[r0:r0 + CONV_ROWS, :] = acc
    hc = conv_scr[...]
    gavg = gavg_ref[...]
    mu = jnp.dot(hc.astype(BF16), gavg, preferred_element_type=F32)
    dc = hc - mu
    var = jnp.dot((dc * dc).astype(BF16), gavg, preferred_element_type=F32)
    hn = dc * lax.rsqrt(var + EPS) * gng_ref[0] + gnb_ref[0]
    ya = hn * jax.nn.sigmoid(hn)
    yac_ref[0, :, 0:CONV_W] = ya.astype(BF16)

    zb = z[:, 2 * CONV_W:2 * CONV_W + FOUR_W].astype(BF16)
    pq = jnp.dot(zb, dft_ref[...], preferred_element_type=F32)
    pq_ref[0, 0] = pq[:, 0:FOUR_W].astype(BF16)
    pq_ref[0, 1] = pq[:, FOUR_W:2 * FOUR_W].astype(BF16)

    c0 = 2 * CONV_W + FOUR_W
    zc = z[:, c0:c0 + 2 * SGU_W]
    zc = 0.5 * zc * (1.0 + jnp.tanh(GELU_C * (zc + GELU_A * (zc * zc * zc))))
    u = zc[:, 0:SGU_W]
    v = zc[:, SGU_W:2 * SGU_W]
    vm = jnp.mean(v, axis=-1, keepdims=True)
    vd = v - vm
    vv = jnp.mean(vd * vd, axis=-1, keepdims=True)
    vn = (vd * lax.rsqrt(vv + EPS) * lng_ref[0] + lnb_ref[0]).astype(BF16)
    n_chunk = T // CHUNK
    lane = lax.broadcasted_iota(I32, (CHUNK, 2 * HEAD), 1)
    for pr in range(SGU_HEADS // 2):
        cols = slice(2 * HEAD * pr, 2 * HEAD * (pr + 1))
        rhs = jnp.concatenate([vn[n * CHUNK:(n + 1) * CHUNK, cols] for n in range(n_chunk)], axis=1)
        lo = jnp.dot(sw_ref[0, 2 * pr], rhs, preferred_element_type=F32)
        hi = jnp.dot(sw_ref[0, 2 * pr + 1], rhs, preferred_element_type=F32)
        for n in range(n_chunk):
            sl = slice(n * 2 * HEAD, (n + 1) * 2 * HEAD)
            vs = jnp.where(lane < HEAD, lo[:, sl], hi[:, sl]) + sb_ref[0, :, cols]
            rows = slice(n * CHUNK, (n + 1) * CHUNK)
            yac_ref[0, rows, CONV_W + 2 * HEAD * pr:CONV_W + 2 * HEAD * (pr + 1)] = (u[rows, cols] * vs).astype(BF16)


def _mixer_in(x, mod, l, ln1_g, w_in_bf, conv_w, conv_b, gn_g, gn_b, ln_g, ln_b, sgu_w_bf, sgu_bias, dft64, gavg):
    bsz, seq, _ = x.shape
    T = T_MIX
    n_i = seq // T
    hb = T // HALO
    n_h = seq // HALO
    vec = lambda w: pl.BlockSpec((1, 1, w), lambda b, i: (l, 0, 0))
    return pl.pallas_call(
        _mixer_in_kernel,
        grid=(bsz, n_i),
        in_specs=[
            pl.BlockSpec((1, T, D), lambda b, i: (b, i, 0)),
            pl.BlockSpec((1, HALO, D), lambda b, i: (b, jnp.maximum(i * hb - 1, 0), 0)),
            pl.BlockSpec((1, HALO, D), lambda b, i: (b, jnp.minimum((i + 1) * hb, n_h - 1), 0)),
            pl.BlockSpec((1, 1, 8, D), lambda b, i: (l, b, 0, 0)),
            vec(D),
            pl.BlockSpec((1, D, Z_COLS), lambda b, i: (l, 0, 0)),
            pl.BlockSpec((1, KSIZE + 1, CONV_W), lambda b, i: (l, 0, 0)),
            vec(CONV_W), vec(CONV_W), vec(CONV_W), vec(SGU_W), vec(SGU_W),
            pl.BlockSpec((1, SGU_HEADS, CHUNK, CHUNK), lambda b, i: (l, 0, 0, 0)),
            pl.BlockSpec((1, CHUNK, SGU_W), lambda b, i: (l, 0, 0)),
            pl.BlockSpec((FOUR_W, 2 * FOUR_W), lambda b, i: (0, 0)),
            pl.BlockSpec((CONV_W, CONV_W), lambda b, i: (0, 0)),
        ],
        out_specs=[
            pl.BlockSpec((1, T, CONV_W + SGU_W), lambda b, i: (b, i, 0)),
            pl.BlockSpec((1, 2, T, FOUR_W), lambda b, i: (b, 0, i, 0)),
        ],
        out_shape=[
            jax.ShapeDtypeStruct((bsz, seq, CONV_W + SGU_W), BF16),
            jax.ShapeDtypeStruct((bsz, 2, seq, FOUR_W), BF16),
        ],
        scratch_shapes=[
            pltpu.VMEM((T + 2 * HALO, CONV_W), F32),
            pltpu.VMEM((8, T + 3 * 8, CONV_W), F32),
            pltpu.VMEM((T, CONV_W), F32),
        ],
        compiler_params=_cparams(("arbitrary", "arbitrary")),
        name="mixer_in",
    )(x, x, x, mod, ln1_g, w_in_bf, conv_w, conv_b, gn_g, gn_b, ln_g, ln_b, sgu_w_bf, sgu_bias, dft64, gavg)


def _dft_fold(scale, pq_ref, fold_ref, ph_ref):
    seq = pq_ref.shape[1] // 2
    half = seq // 2
    nb = seq // CHUNK
    rev = _block_reversal()
    row0 = lax.broadcasted_iota(I32, (CHUNK, FOUR_W), 0) == 0
    alt = jnp.where(lax.broadcasted_iota(I32, (CHUNK, 1), 0) % 2 == 0, 1.0, -1.0)
    alt_sum = jnp.zeros((1, FOUR_W), F32)
    for part, sign in ((0, 1.0), (1, -1.0)):
        base = part * seq
        for m in range(half // CHUNK):
            lo = pq_ref[0, base + CHUNK * m:base + CHUNK * (m + 1), :].astype(F32)
            up = pq_ref[0, base + CHUNK * (nb - 1 - m):base + CHUNK * (nb - m), :]
            mirrored = jnp.dot(rev, up, preferred_element_type=F32)
            if m >= 1:
                first = pq_ref[0, base + CHUNK * (nb - m):base + CHUNK * (nb - m) + 1, :].astype(F32)
                mirrored = jnp.where(row0, first, mirrored)
            folded = (lo + sign * mirrored).astype(BF16)
            fold_ref[part * half + CHUNK * m:part * half + CHUNK * (m + 1), :] = folded
            if part == 0:
                alt_sum = alt_sum + jnp.sum(folded.astype(F32) * alt, axis=0, keepdims=True)
    ph = pq_ref[0, half:half + 1, :].astype(F32) * scale
    rid = lax.broadcasted_iota(I32, (8, FOUR_W), 0)
    ph_ref[...] = jnp.where(rid == 1, alt_sum * scale + ph, ph)


def _block_reversal():
    rr = lax.broadcasted_iota(I32, (CHUNK, CHUNK), 0)
    cc = lax.broadcasted_iota(I32, (CHUNK, CHUNK), 1)
    return jnp.where((rr >= 1) & (cc == CHUNK - rr), 1.0, 0.0).astype(BF16)


def _seq_dft_kernel(scale, cs_ref, pq_ref, lo_ref, hi_ref, fold_scr, ph_scr, carry_scr):
    jj = pl.program_id(0)
    b = pl.program_id(1)
    fold_ref = fold_scr.at[b]
    half = fold_ref.shape[0] // 2

    @pl.when(jj == 0)
    def _():
        _dft_fold(scale, pq_ref, fold_ref, ph_scr.at[b])
        carry_scr[b] = jnp.broadcast_to(ph_scr[b][1:2, :], (8, FOUR_W))

    ph = ph_scr[b][0:1, :]
    alt = jnp.where(lax.broadcasted_iota(I32, (T_DFT, 1), 0) % 2 == 0, 1.0, -1.0)
    a = jnp.dot(cs_ref[:, 0:half], fold_ref[0:half, :], preferred_element_type=F32) + alt * ph
    minus_b = jnp.dot(cs_ref[:, half:2 * half], fold_ref[half:2 * half, :], preferred_element_type=F32)
    lo_ref[0] = (a + minus_b).astype(BF16)
    mirror_src = (a - minus_b).astype(BF16)
    rev = _block_reversal()
    row0 = lax.broadcasted_iota(I32, (CHUNK, FOUR_W), 0) == 0
    nbt = T_DFT // CHUNK
    for m in range(nbt):
        blk = mirror_src[CHUNK * (nbt - 1 - m):CHUNK * (nbt - m), :]
        mirrored = jnp.dot(rev, blk, preferred_element_type=F32)
        if m >= 1:
            first = mirror_src[CHUNK * (nbt - m):CHUNK * (nbt - m) + 1, :].astype(F32)
        else:
            first = carry_scr[b][0:1, :]
        hi_ref[0, CHUNK * m:CHUNK * (m + 1), :] = jnp.where(row0, first, mirrored).astype(BF16)
    carry_scr[b] = jnp.broadcast_to(mirror_src[0:1, :].astype(F32), (8, FOUR_W))


def _seq_dft(cs, pq):
    bsz, two_s, _ = pq.shape
    seq = two_s // 2
    scale = 1.0 / (seq * HEAD) ** 0.5
    n_t = seq // 2 // T_DFT
    half_out = jax.ShapeDtypeStruct((bsz, seq // 2, FOUR_W), BF16)
    return pl.pallas_call(
        functools.partial(_seq_dft_kernel, scale),
        grid=(n_t, bsz),
        in_specs=[pl.BlockSpec((T_DFT, seq), lambda jj, b: (n_t - 1 - jj, 0)),
                  pl.BlockSpec((1, two_s, FOUR_W), lambda jj, b: (jnp.where(jj == 0, b, 0), 0, 0))],
        out_specs=[pl.BlockSpec((1, T_DFT, FOUR_W), lambda jj, b: (b, n_t - 1 - jj, 0)),
                   pl.BlockSpec((1, T_DFT, FOUR_W), lambda jj, b: (b, jj, 0))],
        out_shape=[half_out, half_out],
        scratch_shapes=[pltpu.VMEM((bsz, seq, FOUR_W), BF16), pltpu.VMEM((bsz, 8, FOUR_W), F32),
                        pltpu.VMEM((bsz, 8, FOUR_W), F32)],
        compiler_params=_cparams(("arbitrary", "arbitrary")),
        name="seq_dft",
    )(cs, pq)


def _mixer_out_kernel(per_b, yac_ref, yblo_ref, ybhi_ref, x_ref, mod_ref, g2_ref, wout_ref, rwt_ref, rb_ref, upper_ref,
                      x1_ref, h2pa_ref, h2pb_ref, route_ref, cnt_ref, cnt_scr):
    i = pl.program_id(0)
    T = T_MIX
    yb = jnp.where(i % per_b < per_b // 2, yblo_ref[...], ybhi_ref[...])

    @pl.when(i == 0)
    def _():
        cnt_scr[...] = jnp.zeros_like(cnt_scr)

    mod = mod_ref[0, 0]
    gate1 = mod[2:3, :]
    shift2 = mod[3:4, :]
    gain2 = g2_ref[0] * (1.0 + mod[4:5, :])
    yac = yac_ref[...]
    ycat = jnp.concatenate([yac[:, 0:CONV_W], yb, yac[:, CONV_W:CONV_W + SGU_W]], axis=1)
    o = jnp.dot(ycat, wout_ref[0], preferred_element_type=F32)
    x1 = x_ref[...] + gate1 * o
    x1_ref[...] = x1
    ms = jnp.mean(x1 * x1, axis=-1, keepdims=True)
    h2 = x1 * lax.rsqrt(ms + EPS) * gain2 + shift2
    h2p = _pack_bf16_pair(h2[:, 0:D // 2], h2[:, D // 2:D])
    h2pa_ref[...] = h2p[:, 0:ROW_W]
    h2pb_ref[...] = h2p[:, ROW_W:2 * ROW_W]

    h_hi = h2.astype(BF16)
    h_lo = (h2 - h_hi.astype(F32)).astype(BF16)
    nt = (((1,), (1,)), ((), ()))
    part = lax.dot_general(rwt_ref[...], h_hi, nt, preferred_element_type=F32)
    logits = (part[0:N_EXP] + part[N_EXP:2 * N_EXP]
              + lax.dot_general(rwt_ref[0:N_EXP, :], h_lo, nt, preferred_element_type=F32))
    mx = jnp.max(logits, axis=0, keepdims=True)
    ex = jnp.exp(logits - mx)
    probs = ex / jnp.sum(ex, axis=0, keepdims=True)
    sel = probs + rb_ref[...]
    sel3 = sel.reshape(N_GRP, EPG, T)
    probs3 = probs.reshape(N_GRP, EPG, T)
    jj = lax.broadcasted_iota(I32, (N_GRP, EPG, T), 1)
    m1 = jnp.max(sel3, axis=1, keepdims=True)
    i1 = jnp.min(jnp.where(sel3 == m1, jj, EPG), axis=1, keepdims=True)
    rest = jnp.where(jj == i1, -jnp.inf, sel3)
    m2 = jnp.max(rest, axis=1, keepdims=True)
    i2 = jnp.min(jnp.where(rest == m2, jj, EPG), axis=1, keepdims=True)
    gscore = m1 + m2
    gg = lax.broadcasted_iota(I32, (N_GRP, 1, T), 0)
    gmax = jnp.max(gscore, axis=0, keepdims=True)
    gidx = jnp.min(jnp.where(gscore == gmax, gg, N_GRP), axis=0, keepdims=True)
    ing = gg == gidx
    pick = lambda a, zero: jnp.sum(jnp.where(ing, a, zero), axis=0)
    p1 = jnp.sum(jnp.where(jj == i1, probs3, 0.0), axis=1, keepdims=True)
    p2 = jnp.sum(jnp.where(jj == i2, probs3, 0.0), axis=1, keepdims=True)
    pa = pick(p1, 0.0)
    pb = pick(p2, 0.0)
    gbase = gidx[0] * EPG
    e0 = gbase + pick(i1, 0)
    e1 = gbase + pick(i2, 0)
    den = pa + pb
    gw0 = pa / den
    gw1 = pb / den

    ee = lax.broadcasted_iota(I32, (N_EXP, T), 0)
    oh0 = ee == e0
    oh1 = ee == e1
    amat = jnp.where(oh0 | oh1, 1.0, 0.0)
    before = jnp.dot(amat.astype(BF16), upper_ref[...], preferred_element_type=F32) + cnt_scr[...]
    r0 = jnp.sum(jnp.where(oh0, before, 0.0), axis=0, keepdims=True)
    r1 = jnp.sum(jnp.where(oh1, before, 0.0), axis=0, keepdims=True)
    cnt_scr[...] = cnt_scr[...] + jnp.sum(amat, axis=1, keepdims=True)
    cnt_ref[...] = cnt_scr[...]

    rid = lax.broadcasted_iota(I32, (8, T), 0)
    route = jnp.zeros((8, T), I32)
    bits = lambda w: lax.bitcast_convert_type(w, I32)
    for k, val in enumerate((e0, e1, r0.astype(I32), r1.astype(I32), bits(gw0), bits(gw1))):
        route = jnp.where(rid == k, val, route)
    route_ref[...] = route


def _mixer_out(yac, yb_lo, yb_hi, x, mod, l, ln2_g, w_out_bf, router_wt, router_b, bsz):
    n_tok = x.shape[0]
    T = T_MIX
    per_b = n_tok // bsz // T
    hb = per_b // 2
    row = lambda w: pl.BlockSpec((T, w), lambda i: (i, 0))
    lo_spec = pl.BlockSpec((T, FOUR_W), lambda i: ((i // per_b) * hb + jnp.minimum(i % per_b, hb - 1), 0))
    hi_spec = pl.BlockSpec((T, FOUR_W), lambda i: ((i // per_b) * hb + jnp.maximum(i % per_b - hb, 0), 0))
    upper = (lax.broadcasted_iota(I32, (T, T), 0) < lax.broadcasted_iota(I32, (T, T), 1)).astype(BF16)
    return pl.pallas_call(
        functools.partial(_mixer_out_kernel, per_b),
        grid=(n_tok // T,),
        in_specs=[
            row(CONV_W + SGU_W), lo_spec, hi_spec, row(D),
            pl.BlockSpec((1, 1, 8, D), lambda i: (l, i // per_b, 0, 0)),
            pl.BlockSpec((1, 1, D), lambda i: (l, 0, 0)),
            pl.BlockSpec((1, D, D), lambda i: (l, 0, 0)),
            pl.BlockSpec((2 * N_EXP, D), lambda i: (0, 0)),
            pl.BlockSpec((N_EXP, 1), lambda i: (0, 0)),
            pl.BlockSpec((T, T), lambda i: (0, 0)),
        ],
        out_specs=[row(D), row(ROW_W), row(ROW_W), pl.BlockSpec((8, T), lambda i: (0, i)),
                   pl.BlockSpec((N_EXP, 1), lambda i: (0, 0))],
        out_shape=[
            jax.ShapeDtypeStruct((n_tok, D), F32),
            jax.ShapeDtypeStruct((n_tok, ROW_W), U32),
            jax.ShapeDtypeStruct((n_tok, ROW_W), U32),
            jax.ShapeDtypeStruct((8, n_tok), I32),
            jax.ShapeDtypeStruct((N_EXP, 1), F32),
        ],
        scratch_shapes=[pltpu.VMEM((N_EXP, 1), F32)],
        compiler_params=_cparams(("arbitrary",)),
        name="mixer_out",
    )(yac, yb_lo, yb_hi, x, mod, ln2_g, w_out_bf, router_wt, router_b, upper)


def _sc_mesh():
    return plsc.VectorSubcoreMesh(core_axis_name="c", subcore_axis_name="s")


def _sc_scatter2(src_a, src_b, idx0, idx1, n_rows):
    n = src_a.shape[0]
    out = jax.ShapeDtypeStruct((n_rows, ROW_W), src_a.dtype)

    @functools.partial(pl.kernel, out_type=[out, out], mesh=_sc_mesh())
    def scatter(xa_hbm, xb_hbm, i0_hbm, i1_hbm, oa_hbm, ob_hbm):
        for x_hbm, o_hbm in ((xa_hbm, oa_hbm), (xb_hbm, ob_hbm)):
            def body(x_vmem, i0_vmem, i1_vmem, o_hbm=o_hbm):
                pltpu.sync_copy(x_vmem, o_hbm.at[i0_vmem.at[0]])
                pltpu.sync_copy(x_vmem, o_hbm.at[i1_vmem.at[0]])

            pltpu.emit_pipeline(
                body, grid=(n // SC_WIN,),
                in_specs=[pl.BlockSpec((SC_WIN, ROW_W), index_map=lambda i: (i, 0)),
                          pl.BlockSpec((1, SC_WIN), index_map=lambda i: (0, i)),
                          pl.BlockSpec((1, SC_WIN), index_map=lambda i: (0, i))],
                out_specs=[],
                core_axis_name=("c", "s"), dimension_semantics=(pltpu.PARALLEL,),
            )(x_hbm, i0_hbm, i1_hbm)

    return scatter(src_a, src_b, idx0, idx1)


def _sc_gather(src_a, src_b, idx):
    m = idx.shape[1]
    out = jax.ShapeDtypeStruct((m, ROW_W), src_a.dtype)

    @functools.partial(pl.kernel, out_type=[out, out], mesh=_sc_mesh())
    def gather(xa_hbm, xb_hbm, i_hbm, oa_hbm, ob_hbm):
        for x_hbm, o_hbm in ((xa_hbm, oa_hbm), (xb_hbm, ob_hbm)):
            def body(i_vmem, o_vmem, x_hbm=x_hbm):
                pltpu.sync_copy(x_hbm.at[i_vmem.at[0]], o_vmem)

            pltpu.emit_pipeline(
                body, grid=(m // SC_WIN,),
                in_specs=[pl.BlockSpec((1, SC_WIN), index_map=lambda i: (0, i))],
                out_specs=[pl.BlockSpec((SC_WIN, ROW_W), index_map=lambda i: (i, 0))],
                core_axis_name=("c", "s"), dimension_semantics=(pltpu.PARALLEL,),
            )(i_hbm, o_hbm)

    return gather(src_a, src_b, idx)


def _experts_kernel(l, n_blk, start_ref, nchunk_ref, xpa_ref, xpb_ref, w1_ref, w3_ref, w2_ref, ypa_ref, ypb_ref,
                    w13_scr, w2_scr, wbuf13, wbuf2, xbuf, ybuf, wsem, xsem, ysem):
    e = pl.program_id(0)
    nc = nchunk_ref[e]
    chunk0 = start_ref[e]
    n_used = start_ref[N_EXP]

    def w_copies(ex):
        slot = ex % (W_AHEAD + 1)
        cps = []
        for p in range(W_PIECES):
            r13 = pl.ds(p * (D // W_PIECES), D // W_PIECES)
            r2 = pl.ds(p * (D_FF // W_PIECES), D_FF // W_PIECES)
            cps.append(pltpu.make_async_copy(w1_ref.at[l, ex, r13], wbuf13.at[slot, 0, r13], wsem.at[slot]))
            cps.append(pltpu.make_async_copy(w3_ref.at[l, ex, r13], wbuf13.at[slot, 1, r13], wsem.at[slot]))
            cps.append(pltpu.make_async_copy(w2_ref.at[l, ex, r2], wbuf2.at[slot, r2], wsem.at[slot]))
        return cps

    def x_copies(g):
        rows = pl.ds(pl.multiple_of(g * BLK, BLK), BLK)
        slot = g % (X_AHEAD + 1)
        return [pltpu.make_async_copy(src.at[rows], xbuf.at[slot, h], xsem.at[slot])
                for h, src in enumerate((xpa_ref, xpb_ref))]

    def y_copies(g):
        rows = pl.ds(pl.multiple_of(g * BLK, BLK), BLK)
        slot = g % 2
        return [pltpu.make_async_copy(ybuf.at[slot, h], dst.at[rows], ysem.at[slot])
                for h, dst in enumerate((ypa_ref, ypb_ref))]

    def start(cps):
        for cp in cps:
            cp.start()

    def wait(cps):
        for cp in cps:
            cp.wait()

    @pl.when(e == 0)
    def _():
        for j in range(X_AHEAD):
            @pl.when(j < n_used)
            def _():
                start(x_copies(j))

        for j in range(W_AHEAD):
            start(w_copies(j))

    @pl.when(e < N_EXP)
    def _():
        wslot = e % (W_AHEAD + 1)
        wait(w_copies(e))

        @pl.when(e + W_AHEAD < N_EXP)
        def _():
            start(w_copies(e + W_AHEAD))

        @pl.when(nc > 0)
        def _():
            w13_scr[:, 0:D_FF] = wbuf13[wslot, 0].astype(BF16)
            w13_scr[:, D_FF:2 * D_FF] = wbuf13[wslot, 1].astype(BF16)
            w2_scr[...] = wbuf2[wslot].astype(BF16)

            def chunk(c, carry):
                g = chunk0 + c
                slot = g % (X_AHEAD + 1)

                @pl.when(g + X_AHEAD < n_used)
                def _():
                    start(x_copies(g + X_AHEAD))

                wait(x_copies(g))

                @pl.when(g >= 2)
                def _():
                    wait(y_copies(g - 2))

                a, b = _unpack_bf16_pair(jnp.concatenate([xbuf[slot, 0], xbuf[slot, 1]], axis=1))
                x = jnp.concatenate([a.astype(BF16), b.astype(BF16)], axis=1)
                h13 = jnp.dot(x, w13_scr[...], preferred_element_type=F32)
                h1 = h13[:, 0:D_FF]
                act = (h1 * jax.nn.sigmoid(h1) * h13[:, D_FF:2 * D_FF]).astype(BF16)
                y = jnp.dot(act, w2_scr[...], preferred_element_type=F32)
                yp = _pack_bf16_pair(y[:, 0:D // 2], y[:, D // 2:D])
                ybuf[g % 2, 0] = yp[:, 0:ROW_W]
                ybuf[g % 2, 1] = yp[:, ROW_W:2 * ROW_W]
                start(y_copies(g))
                return carry

            lax.fori_loop(0, nc, chunk, 0)

    @pl.when(e == N_EXP)
    def _():
        @pl.when(n_used >= 2)
        def _():
            wait(y_copies(n_used - 2))

        wait(y_copies(n_used - 1))
        ybuf[0] = jnp.zeros((2, BLK, ROW_W), U32)

        def fill_one(g, carry):
            rows = pl.ds(pl.multiple_of(g * BLK, BLK), BLK)
            cps = [pltpu.make_async_copy(ybuf.at[0, h], dst.at[rows], ysem.at[0])
                   for h, dst in enumerate((ypa_ref, ypb_ref))]
            start(cps)
            wait(cps)
            return carry

        lax.fori_loop(n_used, n_blk, fill_one, 0)


def _experts(chunk_start, n_chunk, xpa, xpb, w1, w3, w2, l):
    n_rows = xpa.shape[0]
    n_blk = n_rows // BLK
    hbm = pl.BlockSpec(memory_space=pl.ANY)
    half = jax.ShapeDtypeStruct((n_rows, ROW_W), U32)
    return pl.pallas_call(
        functools.partial(_experts_kernel, l, n_blk),
        grid_spec=pltpu.PrefetchScalarGridSpec(
            num_scalar_prefetch=2,
            grid=(N_EXP + 1,),
            in_specs=[hbm, hbm, hbm, hbm, hbm],
            out_specs=[hbm, hbm],
            scratch_shapes=[pltpu.VMEM((D, 2 * D_FF), BF16), pltpu.VMEM((D_FF, D), BF16),
                            pltpu.VMEM((W_AHEAD + 1, 2, D, D_FF), F32), pltpu.VMEM((W_AHEAD + 1, D_FF, D), F32),
                            pltpu.VMEM((X_AHEAD + 1, 2, BLK, ROW_W), U32), pltpu.VMEM((2, 2, BLK, ROW_W), U32),
                            pltpu.SemaphoreType.DMA((W_AHEAD + 1,)), pltpu.SemaphoreType.DMA((X_AHEAD + 1,)),
                            pltpu.SemaphoreType.DMA((2,))],
        ),
        out_shape=[half, half],
        compiler_params=_cparams(("arbitrary",)),
        name="experts",
    )(chunk_start, n_chunk, xpa, xpb, w1, w3, w2)


def _combine_kernel(final, ga0_ref, gb0_ref, ga1_ref, gb1_ref, x1_ref, route_ref, mod_ref, fg_ref, o_ref):
    gws = lax.bitcast_convert_type(route_ref[...], F32)
    gws = jnp.concatenate([gws, jnp.zeros((CHUNK - 8, T_CMB), F32)], axis=0).T
    gw0 = gws[:, 4:5]
    gw1 = gws[:, 5:6]
    a0, b0 = _unpack_bf16_pair(jnp.concatenate([ga0_ref[...], gb0_ref[...]], axis=1))
    a1, b1 = _unpack_bf16_pair(jnp.concatenate([ga1_ref[...], gb1_ref[...]], axis=1))
    y = jnp.concatenate([gw0 * a0 + gw1 * a1, gw0 * b0 + gw1 * b1], axis=1)
    gate2 = mod_ref[0, 0][5:6, :]
    x2 = x1_ref[...] + gate2 * y
    if final:
        ms = jnp.mean(x2 * x2, axis=-1, keepdims=True)
        x2 = x2 * lax.rsqrt(ms + EPS) * fg_ref[...]
    o_ref[...] = x2


def _combine(ga, gb, x1, route, mod, l, final_g, bsz, final):
    n_tok = x1.shape[0]
    n_i = n_tok // T_CMB
    per_b = n_i // bsz
    first = pl.BlockSpec((T_CMB, ROW_W), lambda i: (i, 0))
    second = pl.BlockSpec((T_CMB, ROW_W), lambda i: (i + n_i, 0))
    return pl.pallas_call(
        functools.partial(_combine_kernel, final),
        grid=(n_i,),
        in_specs=[
            first, first, second, second,
            pl.BlockSpec((T_CMB, D), lambda i: (i, 0)),
            pl.BlockSpec((8, T_CMB), lambda i: (0, i)),
            pl.BlockSpec((1, 1, 8, D), lambda i: (l, i // per_b, 0, 0)),
            pl.BlockSpec((1, D), lambda i: (0, 0)),
        ],
        out_specs=pl.BlockSpec((T_CMB, D), lambda i: (i, 0)),
        out_shape=jax.ShapeDtypeStruct((n_tok, D), F32),
        compiler_params=_cparams(("arbitrary",)),
        name="combine",
    )(ga, gb, ga, gb, x1, route, mod, final_g)


def _dft_table_kernel(t1_ref, t2_ref, o_ref):
    half = t2_ref.shape[2]
    c2 = t2_ref[0]
    s2 = t2_ref[1]
    for j in range(TAB_GROUP):
        c1 = t1_ref[j, 0:1, :]
        s1 = t1_ref[j, 1:2, :]
        rows = slice(j * TAB_ROWS, (j + 1) * TAB_ROWS)
        o_ref[rows, 0:half] = (c1 * c2 - s1 * s2).astype(BF16)
        o_ref[rows, half:2 * half] = (-(s1 * c2 + c1 * s2)).astype(BF16)


def _dft_tables(seq):
    scale = 1.0 / (seq * HEAD) ** 0.5
    n_hi = seq // TAB_ROWS
    n = lax.broadcasted_iota(I32, (1, seq // 2), 1)
    kh = lax.broadcasted_iota(I32, (n_hi // 2, 1), 0)
    a1 = ((kh * n) % n_hi).astype(F32) * (2.0 * jnp.pi / n_hi)
    t1 = jnp.stack([jnp.cos(a1), jnp.sin(a1)], axis=1)
    kl = lax.broadcasted_iota(I32, (TAB_ROWS, 1), 0)
    a2 = ((kl * n) % seq).astype(F32) * (2.0 * jnp.pi / seq)
    t2 = jnp.stack([jnp.cos(a2) * scale, jnp.sin(a2) * scale], axis=0)
    cs = pl.pallas_call(
        _dft_table_kernel,
        grid=(n_hi // 2 // TAB_GROUP,),
        in_specs=[pl.BlockSpec((TAB_GROUP, 2, seq // 2), lambda i: (i, 0, 0)),
                  pl.BlockSpec((2, TAB_ROWS, seq // 2), lambda i: (0, 0, 0))],
        out_specs=pl.BlockSpec((TAB_GROUP * TAB_ROWS, seq), lambda i: (i, 0)),
        out_shape=jax.ShapeDtypeStruct((seq // 2, seq), BF16),
        compiler_params=_cparams(("arbitrary",)),
        name="dft_table",
    )(t1, t2)
    d = lax.broadcasted_iota(I32, (FOUR_W, FOUR_W), 0)
    q = lax.broadcasted_iota(I32, (FOUR_W, FOUR_W), 1)
    same = (d // HEAD) == (q // HEAD)
    ang64 = ((d * q) % HEAD).astype(F32) * (2.0 * jnp.pi / HEAD)
    dft64 = jnp.concatenate([jnp.where(same, jnp.cos(ang64), 0.0),
                             jnp.where(same, jnp.sin(ang64), 0.0)], axis=1).astype(BF16)
    return cs, dft64


def _routing_tables(route, counts_f):
    counts = counts_f[:, 0].astype(I32)
    pc = (counts + BLK - 1) // BLK * BLK
    pends = jnp.cumsum(pc)
    pstarts = pends - pc
    eid = lax.broadcasted_iota(I32, (N_EXP, 1), 0)

    def dest_of(e, r):
        return jnp.sum(jnp.where(e[None, :] == eid, pstarts[:, None], 0), axis=0) + r

    dest = jnp.concatenate([dest_of(route[0], route[2]), dest_of(route[1], route[3])])
    chunk_start = jnp.concatenate([pstarts, pends[-1:]]) // BLK
    n_chunk = jnp.concatenate([pc // BLK, jnp.zeros((1,), I32)])
    return dest, chunk_start, n_chunk


def kernel(x, c, ln1_g, ln2_g, w_ada, b_ada, w_in, w_out, conv_w, conv_b, conv_gn_g, conv_gn_b, sgu_ln_g,
           sgu_ln_b, sgu_w, sgu_b, router_w, router_b, exp_w1, exp_w3, exp_w2, final_g):
    bsz, seq, _ = x.shape
    depth = w_in.shape[0]
    n_tok = bsz * seq
    n_blk = (2 * n_tok + N_EXP * (BLK - 1) + BLK - 1) // BLK
    r3 = lambda a: a.reshape(depth, 1, a.shape[-1])

    mod = jnp.pad(_ada_mod(c, w_ada, b_ada).reshape(depth, bsz, 6, D), ((0, 0), (0, 0), (0, 2), (0, 0)))
    cs, dft64 = _dft_tables(seq)
    hd = lax.broadcasted_iota(I32, (CONV_W, CONV_W), 0) // HEAD
    gavg = jnp.where(hd == hd.T, 1.0 / HEAD, 0.0).astype(BF16)
    w_in_bf = w_in.astype(BF16)
    w_out_bf = w_out.astype(BF16)
    sgu_w_bf = sgu_w.astype(BF16)
    sgu_bias = jnp.repeat(jnp.swapaxes(sgu_b, 1, 2), HEAD, axis=2)
    conv_w_p = jnp.pad(conv_w, ((0, 0), (0, 1), (0, 0)))
    rwt_hi = router_w.T.astype(BF16)
    rwt_lo = (router_w.T - rwt_hi.astype(F32)).astype(BF16)
    router_wt = jnp.concatenate([rwt_hi, rwt_lo], axis=0)
    router_bc = router_b.reshape(N_EXP, 1)
    fg = final_g.reshape(1, D)

    for l in range(depth):
        yac, pq = _mixer_in(x, mod, l, r3(ln1_g), w_in_bf, conv_w_p, r3(conv_b), r3(conv_gn_g), r3(conv_gn_b),
                            r3(sgu_ln_g), r3(sgu_ln_b), sgu_w_bf, sgu_bias, dft64, gavg)
        yb_lo, yb_hi = _seq_dft(cs, pq.reshape(bsz, 2 * seq, FOUR_W))
        x1, h2pa, h2pb, route, counts = _mixer_out(
            yac.reshape(n_tok, CONV_W + SGU_W), yb_lo.reshape(n_tok // 2, FOUR_W),
            yb_hi.reshape(n_tok // 2, FOUR_W), x.reshape(n_tok, D),
            mod, l, r3(ln2_g), w_out_bf, router_wt, router_bc, bsz)
        dest, chunk_start, n_chunk = _routing_tables(route, counts)
        d0 = dest[:n_tok].reshape(1, n_tok)
        d1 = dest[n_tok:].reshape(1, n_tok)
        xpa, xpb = _sc_scatter2(h2pa, h2pb, d0, d1, n_blk * BLK)
        ypa, ypb = _experts(chunk_start, n_chunk, xpa, xpb, exp_w1, exp_w3, exp_w2, l)
        ga, gb = _sc_gather(ypa, ypb, dest.reshape(1, 2 * n_tok))
        x = _combine(ga, gb, x1, route, mod, l, fg, bsz, l == depth - 1).reshape(bsz, seq, D)
    return x
```

```python
import functools

import jax
import jax.numpy as jnp
from jax import lax
from jax.experimental import pallas as pl
from jax.experimental.pallas import tpu as pltpu
from jax.experimental.pallas import tpu_sc as plsc

F32 = jnp.float32
BF16 = jnp.bfloat16
I32 = jnp.int32
U32 = jnp.uint32

D = 1024
HEAD = 64
CONV_W = 384
FOUR_W = 256
SGU_W = 384
SGU_HEADS = SGU_W // HEAD
Z_COLS = 2 * CONV_W + FOUR_W + 2 * SGU_W
KSIZE = 31
HALO = 16
CHUNK = 128
N_EXP = 64
N_GRP = 8
EPG = N_EXP // N_GRP
D_FF = D // 2
EPS = 1e-6
GELU_C = 0.7978845608028654
GELU_A = 0.044715

T_MIX = 1024
T_DFT = 512
T_CMB = 1024
ROW_W = 256
SC_WIN = 128
BLK = 384
X_AHEAD = 3
W_AHEAD = 1
CONV_ROWS = 64
TAB_ROWS = 64
TAB_GROUP = 4
W_PIECES = 4
VMEM_LIMIT = 56 * 1024 * 1024


def _cparams(sem):
    return pltpu.CompilerParams(dimension_semantics=sem, vmem_limit_bytes=VMEM_LIMIT)


def _pack_bf16_pair(a, b):
    ua = lax.bitcast_convert_type(a.astype(BF16).astype(F32), U32) >> 16
    ub = lax.bitcast_convert_type(b.astype(BF16).astype(F32), U32) & jnp.uint32(0xFFFF0000)
    return ua | ub


def _unpack_bf16_pair(p):
    a = lax.bitcast_convert_type(p << 16, F32)
    b = lax.bitcast_convert_type(p & jnp.uint32(0xFFFF0000), F32)
    return a, b


def _ada_kernel(c_ref, *refs):
    w_refs, b_ref, o_ref = refs[:-2], refs[-2], refs[-1]
    c = c_ref[...]
    ca = c * jax.nn.sigmoid(c)
    ca_hi = ca.astype(BF16)
    ca_lo = (ca - ca_hi.astype(F32)).astype(BF16)
    tn = w_refs[0].shape[2]
    for j, w_ref in enumerate(w_refs):
        cols = slice(j * tn, (j + 1) * tn)
        w = w_ref[0]
        w_hi = w.astype(BF16)
        w_lo = (w - w_hi.astype(F32)).astype(BF16)
        acc = jnp.dot(ca_hi, w_hi, preferred_element_type=F32)
        acc = acc + jnp.dot(ca_hi, w_lo, preferred_element_type=F32)
        acc = acc + jnp.dot(ca_lo, w_hi, preferred_element_type=F32)
        o_ref[0, :, cols] = acc + b_ref[0, :, cols]


def _ada_mod(c, w_ada, b_ada):
    depth, _, ncol = w_ada.shape
    bsz = c.shape[0]
    n_slab = 4
    n_half = 2
    tn = ncol // (n_slab * n_half)
    return pl.pallas_call(
        _ada_kernel,
        grid=(depth, n_half),
        in_specs=[pl.BlockSpec((bsz, D), lambda l, h: (0, 0))]
        + [pl.BlockSpec((1, D, tn), functools.partial(lambda j, l, h: (l, 0, n_slab * h + j), j))
           for j in range(n_slab)]
        + [pl.BlockSpec((1, 1, n_slab * tn), lambda l, h: (l, 0, h))],
        out_specs=pl.BlockSpec((1, bsz, n_slab * tn), lambda l, h: (l, 0, h)),
        out_shape=jax.ShapeDtypeStruct((depth, bsz, ncol), F32),
        compiler_params=_cparams(("arbitrary", "arbitrary")),
        name="ada_mod",
    )(c, *([w_ada] * n_slab), b_ada.reshape(depth, 1, ncol))


def _mixer_in_kernel(xm_ref, xp_ref, xn_ref, mod_ref, g1_ref, win_ref, cw_ref, cb_ref, gng_ref, gnb_ref,
                     lng_ref, lnb_ref, sw_ref, sb_ref, dft_ref, gavg_ref,
                     yac_ref, pq_ref, glu_scr, sh_scr, conv_scr):
    i = pl.program_id(1)
    n_i = pl.num_programs(1)
    T = T_MIX
    mod = mod_ref[0, 0]
    shift1 = mod[0:1, :]
    gain1 = g1_ref[0] * (1.0 + mod[1:2, :])

    def norm_mod(x):
        ms = jnp.mean(x * x, axis=-1, keepdims=True)
        return x * lax.rsqrt(ms + EPS) * gain1 + shift1

    h = norm_mod(xm_ref[0]).astype(BF16)
    z = jnp.dot(h, win_ref[0], preferred_element_type=F32)

    hh = norm_mod(jnp.concatenate([xp_ref[0], xn_ref[0]], axis=0)).astype(BF16)
    zh = jnp.dot(hh, win_ref[0, :, 0:2 * CONV_W], preferred_element_type=F32)
    glu_h = zh[:, 0:CONV_W] * jax.nn.sigmoid(zh[:, CONV_W:2 * CONV_W])
    glu_scr[0:HALO, :] = jnp.where(i > 0, glu_h[0:HALO], 0.0)
    glu_scr[HALO + T:2 * HALO + T, :] = jnp.where(i < n_i - 1, glu_h[HALO:2 * HALO], 0.0)
    glu_scr[HALO:HALO + T, :] = z[:, 0:CONV_W] * jax.nn.sigmoid(z[:, CONV_W:2 * CONV_W])

    off = HALO - KSIZE // 2
    for b in range(8):
        sh_scr[b] = glu_scr[b:b + T + 3 * 8, :]

    for c in range(T // CONV_ROWS):
        r0 = c * CONV_ROWS
        acc = jnp.broadcast_to(cb_ref[0], (CONV_ROWS, CONV_W))
        for k in range(KSIZE):
            a, b = divmod(k + off, 8)
            acc = acc + sh_scr[b, r0 + 8 * a:r0 + 8 * a + CONV_ROWS, :] * cw_ref[0, k:k + 1, :]
        conv_scr[r0:r0 + CONV_ROWS, :] = acc
    hc = conv_scr[...]
    gavg = gavg_ref[...]
    mu = jnp.dot(hc.astype(BF16), gavg, preferred_element_type=F32)
    dc = hc - mu
    var = jnp.dot((dc * dc).astype(BF16), gavg, preferred_element_type=F32)
    hn = dc * lax.rsqrt(var + EPS) * gng_ref[0] + gnb_ref[0]
    ya = hn * jax.nn.sigmoid(hn)
    yac_ref[0, :, 0:CONV_W] = ya.astype(BF16)

    zb = z[:, 2 * CONV_W:2 * CONV_W + FOUR_W].astype(BF16)
    pq = jnp.dot(zb, dft_ref[...], preferred_element_type=F32)
    pq_ref[0, 0] = pq[:, 0:FOUR_W].astype(BF16)
    pq_ref[0, 1] = pq[:, FOUR_W:2 * FOUR_W].astype(BF16)

    c0 = 2 * CONV_W + FOUR_W
    zc = z[:, c0:c0 + 2 * SGU_W]
    zc = 0.5 * zc * (1.0 + jnp.tanh(GELU_C * (zc + GELU_A * (zc * zc * zc))))
    u = zc[:, 0:SGU_W]
    v = zc[:, SGU_W:2 * SGU_W]
    vm = jnp.mean(v, axis=-1, keepdims=True)
    vd = v - vm
    vv = jnp.mean(vd * vd, axis=-1, keepdims=True)
    vn = (vd * lax.rsqrt(vv + EPS) * lng_ref[0] + lnb_ref[0]).astype(BF16)
    n_chunk = T // CHUNK
    lane = lax.broadcasted_iota(I32, (CHUNK, 2 * HEAD), 1)
    for pr in range(SGU_HEADS // 2):
        cols = slice(2 * HEAD * pr, 2 * HEAD * (pr + 1))
        rhs = jnp.concatenate([vn[n * CHUNK:(n + 1) * CHUNK, cols] for n in range(n_chunk)], axis=1)
        lo = jnp.dot(sw_ref[0, 2 * pr], rhs, preferred_element_type=F32)
        hi = jnp.dot(sw_ref[0, 2 * pr + 1], rhs, preferred_element_type=F32)
        for n in range(n_chunk):
            sl = slice(n * 2 * HEAD, (n + 1) * 2 * HEAD)
            vs = jnp.where(lane < HEAD, lo[:, sl], hi[:, sl]) + sb_ref[0, :, cols]
            rows = slice(n * CHUNK, (n + 1) * CHUNK)
            yac_ref[0, rows, CONV_W + 2 * HEAD * pr:CONV_W + 2 * HEAD * (pr + 1)] = (u[rows, cols] * vs).astype(BF16)


def _mixer_in(x, mod, l, ln1_g, w_in_bf, conv_w, conv_b, gn_g, gn_b, ln_g, ln_b, sgu_w_bf, sgu_bias, dft64, gavg):
    bsz, seq, _ = x.shape
    T = T_MIX
    n_i = seq // T
    hb = T // HALO
    n_h = seq // HALO
    vec = lambda w: pl.BlockSpec((1, 1, w), lambda b, i: (l, 0, 0))
    return pl.pallas_call(
        _mixer_in_kernel,
        grid=(bsz, n_i),
        in_specs=[
            pl.BlockSpec((1, T, D), lambda b, i: (b, i, 0)),
            pl.BlockSpec((1, HALO, D), lambda b, i: (b, jnp.maximum(i * hb - 1, 0), 0)),
            pl.BlockSpec((1, HALO, D), lambda b, i: (b, jnp.minimum((i + 1) * hb, n_h - 1), 0)),
            pl.BlockSpec((1, 1, 8, D), lambda b, i: (l, b, 0, 0)),
            vec(D),
            pl.BlockSpec((1, D, Z_COLS), lambda b, i: (l, 0, 0)),
            pl.BlockSpec((1, KSIZE + 1, CONV_W), lambda b, i: (l, 0, 0)),
            vec(CONV_W), vec(CONV_W), vec(CONV_W), vec(SGU_W), vec(SGU_W),
            pl.BlockSpec((1, SGU_HEADS, CHUNK, CHUNK), lambda b, i: (l, 0, 0, 0)),
            pl.BlockSpec((1, CHUNK, SGU_W), lambda b, i: (l, 0, 0)),
            pl.BlockSpec((FOUR_W, 2 * FOUR_W), lambda b, i: (0, 0)),
            pl.BlockSpec((CONV_W, CONV_W), lambda b, i: (0, 0)),
        ],
        out_specs=[
            pl.BlockSpec((1, T, CONV_W + SGU_W), lambda b, i: (b, i, 0)),
            pl.BlockSpec((1, 2, T, FOUR_W), lambda b, i: (b, 0, i, 0)),
        ],
        out_shape=[
            jax.ShapeDtypeStruct((bsz, seq, CONV_W + SGU_W), BF16),
            jax.ShapeDtypeStruct((bsz, 2, seq, FOUR_W), BF16),
        ],
        scratch_shapes=[
            pltpu.VMEM((T + 2 * HALO, CONV_W), F32),
            pltpu.VMEM((8, T + 3 * 8, CONV_W), F32),
            pltpu.VMEM((T, CONV_W), F32),
        ],
        compiler_params=_cparams(("arbitrary", "arbitrary")),
        name="mixer_in",
    )(x, x, x, mod, ln1_g, w_in_bf, conv_w, conv_b, gn_g, gn_b, ln_g, ln_b, sgu_w_bf, sgu_bias, dft64, gavg)


def _dft_fold(scale, pq_ref, fold_ref, ph_ref):
    seq = pq_ref.shape[1] // 2
    half = seq // 2
    nb = seq // CHUNK
    rev = _block_reversal()
    row0 = lax.broadcasted_iota(I32, (CHUNK, FOUR_W), 0) == 0
    alt = jnp.where(lax.broadcasted_iota(I32, (CHUNK, 1), 0) % 2 == 0, 1.0, -1.0)
    alt_sum = jnp.zeros((1, FOUR_W), F32)
    for part, sign in ((0, 1.0), (1, -1.0)):
        base = part * seq
        for m in range(half // CHUNK):
            lo = pq_ref[0, base + CHUNK * m:base + CHUNK * (m + 1), :].astype(F32)
            up = pq_ref[0, base + CHUNK * (nb - 1 - m):base + CHUNK * (nb - m), :]
            mirrored = jnp.dot(rev, up, preferred_element_type=F32)
            if m >= 1:
                first = pq_ref[0, base + CHUNK * (nb - m):base + CHUNK * (nb - m) + 1, :].astype(F32)
                mirrored = jnp.where(row0, first, mirrored)
            folded = (lo + sign * mirrored).astype(BF16)
            fold_ref[part * half + CHUNK * m:part * half + CHUNK * (m + 1), :] = folded
            if part == 0:
                alt_sum = alt_sum + jnp.sum(folded.astype(F32) * alt, axis=0, keepdims=True)
    ph = pq_ref[0, half:half + 1, :].astype(F32) * scale
    rid = lax.broadcasted_iota(I32, (8, FOUR_W), 0)
    ph_ref[...] = jnp.where(rid == 1, alt_sum * scale + ph, ph)


def _block_reversal():
    rr = lax.broadcasted_iota(I32, (CHUNK, CHUNK), 0)
    cc = lax.broadcasted_iota(I32, (CHUNK, CHUNK), 1)
    return jnp.where((rr >= 1) & (cc == CHUNK - rr), 1.0, 0.0).astype(BF16)


def _seq_dft_kernel(scale, cs_ref, pq_ref, lo_ref, hi_ref, fold_scr, ph_scr, carry_scr):
    jj = pl.program_id(0)
    b = pl.program_id(1)
    fold_ref = fold_scr.at[b]
    half = fold_ref.shape[0] // 2

    @pl.when(jj == 0)
    def _():
        _dft_fold(scale, pq_ref, fold_ref, ph_scr.at[b])
        carry_scr[b] = jnp.broadcast_to(ph_scr[b][1:2, :], (8, FOUR_W))

    ph = ph_scr[b][0:1, :]
    alt = jnp.where(lax.broadcasted_iota(I32, (T_DFT, 1), 0) % 2 == 0, 1.0, -1.0)
    a = jnp.dot(cs_ref[:, 0:half], fold_ref[0:half, :], preferred_element_type=F32) + alt * ph
    minus_b = jnp.dot(cs_ref[:, half:2 * half], fold_ref[half:2 * half, :], preferred_element_type=F32)
    lo_ref[0] = (a + minus_b).astype(BF16)
    mirror_src = (a - minus_b).astype(BF16)
    rev = _block_reversal()
    row0 = lax.broadcasted_iota(I32, (CHUNK, FOUR_W), 0) == 0
    nbt = T_DFT // CHUNK
    for m in range(nbt):
        blk = mirror_src[CHUNK * (nbt - 1 - m):CHUNK * (nbt - m), :]
        mirrored = jnp.dot(rev, blk, preferred_element_type=F32)
        if m >= 1:
            first = mirror_src[CHUNK * (nbt - m):CHUNK * (nbt - m) + 1, :].astype(F32)
        else:
            first = carry_scr[b][0:1, :]
        hi_ref[0, CHUNK * m:CHUNK * (m + 1), :] = jnp.where(row0, first, mirrored).astype(BF16)
    carry_scr[b] = jnp.broadcast_to(mirror_src[0:1, :].astype(F32), (8, FOUR_W))


def _seq_dft(cs, pq):
    bsz, two_s, _ = pq.shape
    seq = two_s // 2
    scale = 1.0 / (seq * HEAD) ** 0.5
    n_t = seq // 2 // T_DFT
    half_out = jax.ShapeDtypeStruct((bsz, seq // 2, FOUR_W), BF16)
    return pl.pallas_call(
        functools.partial(_seq_dft_kernel, scale),
        grid=(n_t, bsz),
        in_specs=[pl.BlockSpec((T_DFT, seq), lambda jj, b: (n_t - 1 - jj, 0)),
                  pl.BlockSpec((1, two_s, FOUR_W), lambda jj, b: (jnp.where(jj == 0, b, 0), 0, 0))],
        out_specs=[pl.BlockSpec((1, T_DFT, FOUR_W), lambda jj, b: (b, n_t - 1 - jj, 0)),
                   pl.BlockSpec((1, T_DFT, FOUR_W), lambda jj, b: (b, jj, 0))],
        out_shape=[half_out, half_out],
        scratch_shapes=[pltpu.VMEM((bsz, seq, FOUR_W), BF16), pltpu.VMEM((bsz, 8, FOUR_W), F32),
                        pltpu.VMEM((bsz, 8, FOUR_W), F32)],
        compiler_params=_cparams(("arbitrary", "arbitrary")),
        name="seq_dft",
    )(cs, pq)


def _mixer_out_kernel(per_b, yac_ref, yblo_ref, ybhi_ref, x_ref, mod_ref, g2_ref, wout_ref, rwt_ref, rb_ref, upper_ref,
                      x1_ref, h2pa_ref, h2pb_ref, route_ref, cnt_ref, cnt_scr):
    i = pl.program_id(0)
    T = T_MIX
    yb = jnp.where(i % per_b < per_b // 2, yblo_ref[...], ybhi_ref[...])

    @pl.when(i == 0)
    def _():
        cnt_scr[...] = jnp.zeros_like(cnt_scr)

    mod = mod_ref[0, 0]
    gate1 = mod[2:3, :]
    shift2 = mod[3:4, :]
    gain2 = g2_ref[0] * (1.0 + mod[4:5, :])
    yac = yac_ref[...]
    ycat = jnp.concatenate([yac[:, 0:CONV_W], yb, yac[:, CONV_W:CONV_W + SGU_W]], axis=1)
    o = jnp.dot(ycat, wout_ref[0], preferred_element_type=F32)
    x1 = x_ref[...] + gate1 * o
    x1_ref[...] = x1
    ms = jnp.mean(x1 * x1, axis=-1, keepdims=True)
    h2 = x1 * lax.rsqrt(ms + EPS) * gain2 + shift2
    h2p = _pack_bf16_pair(h2[:, 0:D // 2], h2[:, D // 2:D])
    h2pa_ref[...] = h2p[:, 0:ROW_W]
    h2pb_ref[...] = h2p[:, ROW_W:2 * ROW_W]

    h_hi = h2.astype(BF16)
    h_lo = (h2 - h_hi.astype(F32)).astype(BF16)
    nt = (((1,), (1,)), ((), ()))
    part = lax.dot_general(rwt_ref[...], h_hi, nt, preferred_element_type=F32)
    logits = (part[0:N_EXP] + part[N_EXP:2 * N_EXP]
              + lax.dot_general(rwt_ref[0:N_EXP, :], h_lo, nt, preferred_element_type=F32))
    mx = jnp.max(logits, axis=0, keepdims=True)
    ex = jnp.exp(logits - mx)
    probs = ex / jnp.sum(ex, axis=0, keepdims=True)
    sel = probs + rb_ref[...]
    sel3 = sel.reshape(N_GRP, EPG, T)
    probs3 = probs.reshape(N_GRP, EPG, T)
    jj = lax.broadcasted_iota(I32, (N_GRP, EPG, T), 1)
    m1 = jnp.max(sel3, axis=1, keepdims=True)
    i1 = jnp.min(jnp.where(sel3 == m1, jj, EPG), axis=1, keepdims=True)
    rest = jnp.where(jj == i1, -jnp.inf, sel3)
    m2 = jnp.max(rest, axis=1, keepdims=True)
    i2 = jnp.min(jnp.where(rest == m2, jj, EPG), axis=1, keepdims=True)
    gscore = m1 + m2
    gg = lax.broadcasted_iota(I32, (N_GRP, 1, T), 0)
    gmax = jnp.max(gscore, axis=0, keepdims=True)
    gidx = jnp.min(jnp.where(gscore == gmax, gg, N_GRP), axis=0, keepdims=True)
    ing = gg == gidx
    pick = lambda a, zero: jnp.sum(jnp.where(ing, a, zero), axis=0)
    p1 = jnp.sum(jnp.where(jj == i1, probs3, 0.0), axis=1, keepdims=True)
    p2 = jnp.sum(jnp.where(jj == i2, probs3, 0.0), axis=1, keepdims=True)
    pa = pick(p1, 0.0)
    pb = pick(p2, 0.0)
    gbase = gidx[0] * EPG
    e0 = gbase + pick(i1, 0)
    e1 = gbase + pick(i2, 0)
    den = pa + pb
    gw0 = pa / den
    gw1 = pb / den

    ee = lax.broadcasted_iota(I32, (N_EXP, T), 0)
    oh0 = ee == e0
    oh1 = ee == e1
    amat = jnp.where(oh0 | oh1, 1.0, 0.0)
    before = jnp.dot(amat.astype(BF16), upper_ref[...], preferred_element_type=F32) + cnt_scr[...]
    r0 = jnp.sum(jnp.where(oh0, before, 0.0), axis=0, keepdims=True)
    r1 = jnp.sum(jnp.where(oh1, before, 0.0), axis=0, keepdims=True)
    cnt_scr[...] = cnt_scr[...] + jnp.sum(amat, axis=1, keepdims=True)
    cnt_ref[...] = cnt_scr[...]

    rid = lax.broadcasted_iota(I32, (8, T), 0)
    route = jnp.zeros((8, T), I32)
    bits = lambda w: lax.bitcast_convert_type(w, I32)
    for k, val in enumerate((e0, e1, r0.astype(I32), r1.astype(I32), bits(gw0), bits(gw1))):
        route = jnp.where(rid == k, val, route)
    route_ref[...] = route


def _mixer_out(yac, yb_lo, yb_hi, x, mod, l, ln2_g, w_out_bf, router_wt, router_b, bsz):
    n_tok = x.shape[0]
    T = T_MIX
    per_b = n_tok // bsz // T
    hb = per_b // 2
    row = lambda w: pl.BlockSpec((T, w), lambda i: (i, 0))
    lo_spec = pl.BlockSpec((T, FOUR_W), lambda i: ((i // per_b) * hb + jnp.minimum(i % per_b, hb - 1), 0))
    hi_spec = pl.BlockSpec((T, FOUR_W), lambda i: ((i // per_b) * hb + jnp.maximum(i % per_b - hb, 0), 0))
    upper = (lax.broadcasted_iota(I32, (T, T), 0) < lax.broadcasted_iota(I32, (T, T), 1)).astype(BF16)
    return pl.pallas_call(
        functools.partial(_mixer_out_kernel, per_b),
        grid=(n_tok // T,),
        in_specs=[
            row(CONV_W + SGU_W), lo_spec, hi_spec, row(D),
            pl.BlockSpec((1, 1, 8, D), lambda i: (l, i // per_b, 0, 0)),
            pl.BlockSpec((1, 1, D), lambda i: (l, 0, 0)),
            pl.BlockSpec((1, D, D), lambda i: (l, 0, 0)),
            pl.BlockSpec((2 * N_EXP, D), lambda i: (0, 0)),
            pl.BlockSpec((N_EXP, 1), lambda i: (0, 0)),
            pl.BlockSpec((T, T), lambda i: (0, 0)),
        ],
        out_specs=[row(D), row(ROW_W), row(ROW_W), pl.BlockSpec((8, T), lambda i: (0, i)),
                   pl.BlockSpec((N_EXP, 1), lambda i: (0, 0))],
        out_shape=[
            jax.ShapeDtypeStruct((n_tok, D), F32),
            jax.ShapeDtypeStruct((n_tok, ROW_W), U32),
            jax.ShapeDtypeStruct((n_tok, ROW_W), U32),
            jax.ShapeDtypeStruct((8, n_tok), I32),
            jax.ShapeDtypeStruct((N_EXP, 1), F32),
        ],
        scratch_shapes=[pltpu.VMEM((N_EXP, 1), F32)],
        compiler_params=_cparams(("arbitrary",)),
        name="mixer_out",
    )(yac, yb_lo, yb_hi, x, mod, ln2_g, w_out_bf, router_wt, router_b, upper)


def _sc_mesh():
    return plsc.VectorSubcoreMesh(core_axis_name="c", subcore_axis_name="s")


def _sc_scatter2(src_a, src_b, idx0, idx1, n_rows):
    n = src_a.shape[0]
    out = jax.ShapeDtypeStruct((n_rows, ROW_W), src_a.dtype)

    @functools.partial(pl.kernel, out_type=[out, out], mesh=_sc_mesh())
    def scatter(xa_hbm, xb_hbm, i0_hbm, i1_hbm, oa_hbm, ob_hbm):
        for x_hbm, o_hbm in ((xa_hbm, oa_hbm), (xb_hbm, ob_hbm)):
            def body(x_vmem, i0_vmem, i1_vmem, o_hbm=o_hbm):
                pltpu.sync_copy(x_vmem, o_hbm.at[i0_vmem.at[0]])
                pltpu.sync_copy(x_vmem, o_hbm.at[i1_vmem.at[0]])

            pltpu.emit_pipeline(
                body, grid=(n // SC_WIN,),
                in_specs=[pl.BlockSpec((SC_WIN, ROW_W), index_map=lambda i: (i, 0)),
                          pl.BlockSpec((1, SC_WIN), index_map=lambda i: (0, i)),
                          pl.BlockSpec((1, SC_WIN), index_map=lambda i: (0, i))],
                out_specs=[],
                core_axis_name=("c", "s"), dimension_semantics=(pltpu.PARALLEL,),
            )(x_hbm, i0_hbm, i1_hbm)

    return scatter(src_a, src_b, idx0, idx1)


def _sc_gather(src_a, src_b, idx):
    m = idx.shape[1]
    out = jax.ShapeDtypeStruct((m, ROW_W), src_a.dtype)

    @functools.partial(pl.kernel, out_type=[out, out], mesh=_sc_mesh())
    def gather(xa_hbm, xb_hbm, i_hbm, oa_hbm, ob_hbm):
        for x_hbm, o_hbm in ((xa_hbm, oa_hbm), (xb_hbm, ob_hbm)):
            def body(i_vmem, o_vmem, x_hbm=x_hbm):
                pltpu.sync_copy(x_hbm.at[i_vmem.at[0]], o_vmem)

            pltpu.emit_pipeline(
                body, grid=(m // SC_WIN,),
                in_specs=[pl.BlockSpec((1, SC_WIN), index_map=lambda i: (0, i))],
                out_specs=[pl.BlockSpec((SC_WIN, ROW_W), index_map=lambda i: (i, 0))],
                core_axis_name=("c", "s"), dimension_semantics=(pltpu.PARALLEL,),
            )(i_hbm, o_hbm)

    return gather(src_a, src_b, idx)


def _experts_kernel(l, n_blk, start_ref, nchunk_ref, tail_ref, xpa_ref, xpb_ref, w1_ref, w3_ref, w2_ref, ypa_ref, ypb_ref,
                    w13_scr, w2_scr, wbuf13, wbuf2, xbuf, ybuf, wsem, xsem, ysem):
    e = pl.program_id(0)
    nc = nchunk_ref[e]
    chunk0 = start_ref[e]
    n_used = start_ref[N_EXP]

    def w_copies(ex):
        slot = ex % (W_AHEAD + 1)
        cps = []
        for p in range(W_PIECES):
            r13 = pl.ds(p * (D // W_PIECES), D // W_PIECES)
            r2 = pl.ds(p * (D_FF // W_PIECES), D_FF // W_PIECES)
            cps.append(pltpu.make_async_copy(w1_ref.at[l, ex, r13], wbuf13.at[slot, 0, r13], wsem.at[slot]))
            cps.append(pltpu.make_async_copy(w3_ref.at[l, ex, r13], wbuf13.at[slot, 1, r13], wsem.at[slot]))
            cps.append(pltpu.make_async_copy(w2_ref.at[l, ex, r2], wbuf2.at[slot, r2], wsem.at[slot]))
        return cps

    def x_copies(g):
        rows = pl.ds(pl.multiple_of(g * BLK, BLK), BLK)
        slot = g % (X_AHEAD + 1)
        return [pltpu.make_async_copy(src.at[rows], xbuf.at[slot, h], xsem.at[slot])
                for h, src in enumerate((xpa_ref, xpb_ref))]

    def y_copies(g):
        rows = pl.ds(pl.multiple_of(g * BLK, BLK), BLK)
        slot = g % 2
        return [pltpu.make_async_copy(ybuf.at[slot, h], dst.at[rows], ysem.at[slot])
                for h, dst in enumerate((ypa_ref, ypb_ref))]

    def start(cps):
        for cp in cps:
            cp.start()

    def wait(cps):
        for cp in cps:
            cp.wait()

    @pl.when(e == 0)
    def _():
        for j in range(X_AHEAD):
            @pl.when(j < n_used)
            def _():
                start(x_copies(j))

        for j in range(W_AHEAD):
            start(w_copies(j))

    @pl.when(e < N_EXP)
    def _():
        wslot = e % (W_AHEAD + 1)
        wait(w_copies(e))

        @pl.when(e + W_AHEAD < N_EXP)
        def _():
            start(w_copies(e + W_AHEAD))

        @pl.when(nc > 0)
        def _():
            w13_scr[:, 0:D_FF] = wbuf13[wslot, 0].astype(BF16)
            w13_scr[:, D_FF:2 * D_FF] = wbuf13[wslot, 1].astype(BF16)
            w2_scr[...] = wbuf2[wslot].astype(BF16)

            def chunk(c, carry):
                g = chunk0 + c
                slot = g % (X_AHEAD + 1)

                @pl.when(g + X_AHEAD < n_used)
                def _():
                    start(x_copies(g + X_AHEAD))

                wait(x_copies(g))

                @pl.when(g >= 2)
                def _():
                    wait(y_copies(g - 2))

                def swiglu(rows):
                    a, b = _unpack_bf16_pair(jnp.concatenate([xbuf[slot, 0, 0:rows], xbuf[slot, 1, 0:rows]], axis=1))
                    x = jnp.concatenate([a.astype(BF16), b.astype(BF16)], axis=1)
                    h13 = jnp.dot(x, w13_scr[...], preferred_element_type=F32)
                    h1 = h13[:, 0:D_FF]
                    act = (h1 * jax.nn.sigmoid(h1) * h13[:, D_FF:2 * D_FF]).astype(BF16)
                    y = jnp.dot(act, w2_scr[...], preferred_element_type=F32)
                    yp = _pack_bf16_pair(y[:, 0:D // 2], y[:, D // 2:D])
                    ybuf[g % 2, 0, 0:rows] = yp[:, 0:ROW_W]
                    ybuf[g % 2, 1, 0:rows] = yp[:, ROW_W:2 * ROW_W]

                short = (c == nc - 1) & (tail_ref[e] <= BLK // 2)

                @pl.when(short)
                def _():
                    swiglu(BLK // 2)
                    ybuf[g % 2, :, BLK // 2:BLK] = jnp.zeros((2, BLK // 2, ROW_W), U32)

                @pl.when(jnp.logical_not(short))
                def _():
                    swiglu(BLK)

                start(y_copies(g))
                return carry

            lax.fori_loop(0, nc, chunk, 0)

    @pl.when(e == N_EXP)
    def _():
        @pl.when(n_used >= 2)
        def _():
            wait(y_copies(n_used - 2))

        wait(y_copies(n_used - 1))
        ybuf[0] = jnp.zeros((2, BLK, ROW_W), U32)

        def fill_one(g, carry):
            rows = pl.ds(pl.multiple_of(g * BLK, BLK), BLK)
            cps = [pltpu.make_async_copy(ybuf.at[0, h], dst.at[rows], ysem.at[0])
                   for h, dst in enumerate((ypa_ref, ypb_ref))]
            start(cps)
            wait(cps)
            return carry

        lax.fori_loop(n_used, n_blk, fill_one, 0)


def _experts(chunk_start, n_chunk, tail_rows, xpa, xpb, w1, w3, w2, l):
    n_rows = xpa.shape[0]
    n_blk = n_rows // BLK
    hbm = pl.BlockSpec(memory_space=pl.ANY)
    half = jax.ShapeDtypeStruct((n_rows, ROW_W), U32)
    return pl.pallas_call(
        functools.partial(_experts_kernel, l, n_blk),
        grid_spec=pltpu.PrefetchScalarGridSpec(
            num_scalar_prefetch=3,
            grid=(N_EXP + 1,),
            in_specs=[hbm, hbm, hbm, hbm, hbm],
            out_specs=[hbm, hbm],
            scratch_shapes=[pltpu.VMEM((D, 2 * D_FF), BF16), pltpu.VMEM((D_FF, D), BF16),
                            pltpu.VMEM((W_AHEAD + 1, 2, D, D_FF), F32), pltpu.VMEM((W_AHEAD + 1, D_FF, D), F32),
                            pltpu.VMEM((X_AHEAD + 1, 2, BLK, ROW_W), U32), pltpu.VMEM((2, 2, BLK, ROW_W), U32),
                            pltpu.SemaphoreType.DMA((W_AHEAD + 1,)), pltpu.SemaphoreType.DMA((X_AHEAD + 1,)),
                            pltpu.SemaphoreType.DMA((2,))],
        ),
        out_shape=[half, half],
        compiler_params=_cparams(("arbitrary",)),
        name="experts",
    )(chunk_start, n_chunk, tail_rows, xpa, xpb, w1, w3, w2)


def _combine_kernel(final, ga0_ref, gb0_ref, ga1_ref, gb1_ref, x1_ref, route_ref, mod_ref, fg_ref, o_ref):
    gws = lax.bitcast_convert_type(route_ref[...], F32)
    gws = jnp.concatenate([gws, jnp.zeros((CHUNK - 8, T_CMB), F32)], axis=0).T
    gw0 = gws[:, 4:5]
    gw1 = gws[:, 5:6]
    a0, b0 = _unpack_bf16_pair(jnp.concatenate([ga0_ref[...], gb0_ref[...]], axis=1))
    a1, b1 = _unpack_bf16_pair(jnp.concatenate([ga1_ref[...], gb1_ref[...]], axis=1))
    y = jnp.concatenate([gw0 * a0 + gw1 * a1, gw0 * b0 + gw1 * b1], axis=1)
    gate2 = mod_ref[0, 0][5:6, :]
    x2 = x1_ref[...] + gate2 * y
    if final:
        ms = jnp.mean(x2 * x2, axis=-1, keepdims=True)
        x2 = x2 * lax.rsqrt(ms + EPS) * fg_ref[...]
    o_ref[...] = x2


def _combine(ga, gb, x1, route, mod, l, final_g, bsz, final):
    n_tok = x1.shape[0]
    n_i = n_tok // T_CMB
    per_b = n_i // bsz
    first = pl.BlockSpec((T_CMB, ROW_W), lambda i: (i, 0))
    second = pl.BlockSpec((T_CMB, ROW_W), lambda i: (i + n_i, 0))
    return pl.pallas_call(
        functools.partial(_combine_kernel, final),
        grid=(n_i,),
        in_specs=[
            first, first, second, second,
            pl.BlockSpec((T_CMB, D), lambda i: (i, 0)),
            pl.BlockSpec((8, T_CMB), lambda i: (0, i)),
            pl.BlockSpec((1, 1, 8, D), lambda i: (l, i // per_b, 0, 0)),
            pl.BlockSpec((1, D), lambda i: (0, 0)),
        ],
        out_specs=pl.BlockSpec((T_CMB, D), lambda i: (i, 0)),
        out_shape=jax.ShapeDtypeStruct((n_tok, D), F32),
        compiler_params=_cparams(("arbitrary",)),
        name="combine",
    )(ga, gb, ga, gb, x1, route, mod, final_g)


def _dft_table_kernel(t1_ref, t2_ref, o_ref):
    half = t2_ref.shape[2]
    c2 = t2_ref[0]
    s2 = t2_ref[1]
    for j in range(TAB_GROUP):
        c1 = t1_ref[j, 0:1, :]
        s1 = t1_ref[j, 1:2, :]
        rows = slice(j * TAB_ROWS, (j + 1) * TAB_ROWS)
        o_ref[rows, 0:half] = (c1 * c2 - s1 * s2).astype(BF16)
        o_ref[rows, half:2 * half] = (-(s1 * c2 + c1 * s2)).astype(BF16)


def _dft_tables(seq):
    scale = 1.0 / (seq * HEAD) ** 0.5
    n_hi = seq // TAB_ROWS
    n = lax.broadcasted_iota(I32, (1, seq // 2), 1)
    kh = lax.broadcasted_iota(I32, (n_hi // 2, 1), 0)
    a1 = ((kh * n) % n_hi).astype(F32) * (2.0 * jnp.pi / n_hi)
    t1 = jnp.stack([jnp.cos(a1), jnp.sin(a1)], axis=1)
    kl = lax.broadcasted_iota(I32, (TAB_ROWS, 1), 0)
    a2 = ((kl * n) % seq).astype(F32) * (2.0 * jnp.pi / seq)
    t2 = jnp.stack([jnp.cos(a2) * scale, jnp.sin(a2) * scale], axis=0)
    cs = pl.pallas_call(
        _dft_table_kernel,
        grid=(n_hi // 2 // TAB_GROUP,),
        in_specs=[pl.BlockSpec((TAB_GROUP, 2, seq // 2), lambda i: (i, 0, 0)),
                  pl.BlockSpec((2, TAB_ROWS, seq // 2), lambda i: (0, 0, 0))],
        out_specs=pl.BlockSpec((TAB_GROUP * TAB_ROWS, seq), lambda i: (i, 0)),
        out_shape=jax.ShapeDtypeStruct((seq // 2, seq), BF16),
        compiler_params=_cparams(("arbitrary",)),
        name="dft_table",
    )(t1, t2)
    d = lax.broadcasted_iota(I32, (FOUR_W, FOUR_W), 0)
    q = lax.broadcasted_iota(I32, (FOUR_W, FOUR_W), 1)
    same = (d // HEAD) == (q // HEAD)
    ang64 = ((d * q) % HEAD).astype(F32) * (2.0 * jnp.pi / HEAD)
    dft64 = jnp.concatenate([jnp.where(same, jnp.cos(ang64), 0.0),
                             jnp.where(same, jnp.sin(ang64), 0.0)], axis=1).astype(BF16)
    return cs, dft64


def _routing_tables(route, counts_f):
    counts = counts_f[:, 0].astype(I32)
    pc = (counts + BLK - 1) // BLK * BLK
    pends = jnp.cumsum(pc)
    pstarts = pends - pc
    eid = lax.broadcasted_iota(I32, (N_EXP, 1), 0)

    def dest_of(e, r):
        return jnp.sum(jnp.where(e[None, :] == eid, pstarts[:, None], 0), axis=0) + r

    dest = jnp.concatenate([dest_of(route[0], route[2]), dest_of(route[1], route[3])])
    chunk_start = jnp.concatenate([pstarts, pends[-1:]]) // BLK
    n_chunk = jnp.concatenate([pc // BLK, jnp.zeros((1,), I32)])
    tail_rows = jnp.concatenate([counts - (pc - BLK), jnp.zeros((1,), I32)])
    return dest, chunk_start, n_chunk, tail_rows


def kernel(x, c, ln1_g, ln2_g, w_ada, b_ada, w_in, w_out, conv_w, conv_b, conv_gn_g, conv_gn_b, sgu_ln_g,
           sgu_ln_b, sgu_w, sgu_b, router_w, router_b, exp_w1, exp_w3, exp_w2, final_g):
    bsz, seq, _ = x.shape
    depth = w_in.shape[0]
    n_tok = bsz * seq
    n_blk = (2 * n_tok + N_EXP * (BLK - 1) + BLK - 1) // BLK
    r3 = lambda a: a.reshape(depth, 1, a.shape[-1])

    mod = jnp.pad(_ada_mod(c, w_ada, b_ada).reshape(depth, bsz, 6, D), ((0, 0), (0, 0), (0, 2), (0, 0)))
    cs, dft64 = _dft_tables(seq)
    hd = lax.broadcasted_iota(I32, (CONV_W, CONV_W), 0) // HEAD
    gavg = jnp.where(hd == hd.T, 1.0 / HEAD, 0.0).astype(BF16)
    w_in_bf = w_in.astype(BF16)
    w_out_bf = w_out.astype(BF16)
    sgu_w_bf = sgu_w.astype(BF16)
    sgu_bias = jnp.repeat(jnp.swapaxes(sgu_b, 1, 2), HEAD, axis=2)
    conv_w_p = jnp.pad(conv_w, ((0, 0), (0, 1), (0, 0)))
    rwt_hi = router_w.T.astype(BF16)
    rwt_lo = (router_w.T - rwt_hi.astype(F32)).astype(BF16)
    router_wt = jnp.concatenate([rwt_hi, rwt_lo], axis=0)
    router_bc = router_b.reshape(N_EXP, 1)
    fg = final_g.reshape(1, D)

    for l in range(depth):
        yac, pq = _mixer_in(x, mod, l, r3(ln1_g), w_in_bf, conv_w_p, r3(conv_b), r3(conv_gn_g), r3(conv_gn_b),
                            r3(sgu_ln_g), r3(sgu_ln_b), sgu_w_bf, sgu_bias, dft64, gavg)
        yb_lo, yb_hi = _seq_dft(cs, pq.reshape(bsz, 2 * seq, FOUR_W))
        x1, h2pa, h2pb, route, counts = _mixer_out(
            yac.reshape(n_tok, CONV_W + SGU_W), yb_lo.reshape(n_tok // 2, FOUR_W),
            yb_hi.reshape(n_tok // 2, FOUR_W), x.reshape(n_tok, D),
            mod, l, r3(ln2_g), w_out_bf, router_wt, router_bc, bsz)
        dest, chunk_start, n_chunk, tail_rows = _routing_tables(route, counts)
        d0 = dest[:n_tok].reshape(1, n_tok)
        d1 = dest[n_tok:].reshape(1, n_tok)
        xpa, xpb = _sc_scatter2(h2pa, h2pb, d0, d1, n_blk * BLK)
        ypa, ypb = _experts(chunk_start, n_chunk, tail_rows, xpa, xpb, exp_w1, exp_w3, exp_w2, l)
        ga, gb = _sc_gather(ypa, ypb, dest.reshape(1, 2 * n_tok))
        x = _combine(ga, gb, x1, route, mod, l, fg, bsz, l == depth - 1).reshape(bsz, seq, D)
    return x
```

```python
import functools

import jax
import jax.numpy as jnp
from jax import lax
from jax.experimental import pallas as pl
from jax.experimental.pallas import tpu as pltpu
from jax.experimental.pallas import tpu_sc as plsc

F32 = jnp.float32
BF16 = jnp.bfloat16
I32 = jnp.int32
U32 = jnp.uint32

D = 1024
HEAD = 64
CONV_W = 384
FOUR_W = 256
SGU_W = 384
SGU_HEADS = SGU_W // HEAD
Z_COLS = 2 * CONV_W + FOUR_W + 2 * SGU_W
KSIZE = 31
HALO = 16
CHUNK = 128
N_EXP = 64
N_GRP = 8
EPG = N_EXP // N_GRP
D_FF = D // 2
EPS = 1e-6
GELU_C = 0.7978845608028654
GELU_A = 0.044715

T_MIX = 1024
T_DFT = 512
T_CMB = 1024
ROW_W = 256
SC_WIN = 128
BLK = 384
X_AHEAD = 3
W_AHEAD = 1
CONV_ROWS = 64
TAB_ROWS = 64
TAB_GROUP = 4
W_PIECES = 4
VMEM_LIMIT = 56 * 1024 * 1024


def _cparams(sem):
    return pltpu.CompilerParams(dimension_semantics=sem, vmem_limit_bytes=VMEM_LIMIT)


def _pack_bf16_pair(a, b):
    ua = lax.bitcast_convert_type(a.astype(BF16).astype(F32), U32) >> 16
    ub = lax.bitcast_convert_type(b.astype(BF16).astype(F32), U32) & jnp.uint32(0xFFFF0000)
    return ua | ub


def _unpack_bf16_pair(p):
    a = lax.bitcast_convert_type(p << 16, F32)
    b = lax.bitcast_convert_type(p & jnp.uint32(0xFFFF0000), F32)
    return a, b


def _ada_kernel(c_ref, *refs):
    w_refs, b_ref, o_ref = refs[:-2], refs[-2], refs[-1]
    c = c_ref[...]
    ca = c * jax.nn.sigmoid(c)
    ca_hi = ca.astype(BF16)
    ca_lo = (ca - ca_hi.astype(F32)).astype(BF16)
    tn = w_refs[0].shape[2]
    for j, w_ref in enumerate(w_refs):
        cols = slice(j * tn, (j + 1) * tn)
        w = w_ref[0]
        w_hi = w.astype(BF16)
        w_lo = (w - w_hi.astype(F32)).astype(BF16)
        acc = jnp.dot(ca_hi, w_hi, preferred_element_type=F32)
        acc = acc + jnp.dot(ca_hi, w_lo, preferred_element_type=F32)
        acc = acc + jnp.dot(ca_lo, w_hi, preferred_element_type=F32)
        o_ref[0, :, cols] = acc + b_ref[0, :, cols]


def _ada_mod(c, w_ada, b_ada):
    depth, _, ncol = w_ada.shape
    bsz = c.shape[0]
    n_slab = 4
    n_half = 2
    tn = ncol // (n_slab * n_half)
    return pl.pallas_call(
        _ada_kernel,
        grid=(depth, n_half),
        in_specs=[pl.BlockSpec((bsz, D), lambda l, h: (0, 0))]
        + [pl.BlockSpec((1, D, tn), functools.partial(lambda j, l, h: (l, 0, n_slab * h + j), j))
           for j in range(n_slab)]
        + [pl.BlockSpec((1, 1, n_slab * tn), lambda l, h: (l, 0, h))],
        out_specs=pl.BlockSpec((1, bsz, n_slab * tn), lambda l, h: (l, 0, h)),
        out_shape=jax.ShapeDtypeStruct((depth, bsz, ncol), F32),
        compiler_params=_cparams(("arbitrary", "arbitrary")),
        name="ada_mod",
    )(c, *([w_ada] * n_slab), b_ada.reshape(depth, 1, ncol))


def _mixer_in_kernel(xm_ref, xp_ref, xn_ref, mod_ref, g1_ref, win_ref, cw_ref, cb_ref, gng_ref, gnb_ref,
                     lng_ref, lnb_ref, sw_ref, sb_ref, dft_ref, gavg_ref,
                     yac_ref, pq_ref, glu_scr, sh_scr, conv_scr):
    i = pl.program_id(1)
    n_i = pl.num_programs(1)
    T = T_MIX
    mod = mod_ref[0, 0]
    shift1 = mod[0:1, :]
    gain1 = g1_ref[0] * (1.0 + mod[1:2, :])

    def norm_mod(x):
        ms = jnp.mean(x * x, axis=-1, keepdims=True)
        return x * lax.rsqrt(ms + EPS) * gain1 + shift1

    h = norm_mod(xm_ref[0]).astype(BF16)
    z = jnp.dot(h, win_ref[0], preferred_element_type=F32)

    hh = norm_mod(jnp.concatenate([xp_ref[0], xn_ref[0]], axis=0)).astype(BF16)
    zh = jnp.dot(hh, win_ref[0, :, 0:2 * CONV_W], preferred_element_type=F32)
    glu_h = zh[:, 0:CONV_W] * jax.nn.sigmoid(zh[:, CONV_W:2 * CONV_W])
    glu_scr[0:HALO, :] = jnp.where(i > 0, glu_h[0:HALO], 0.0)
    glu_scr[HALO + T:2 * HALO + T, :] = jnp.where(i < n_i - 1, glu_h[HALO:2 * HALO], 0.0)
    glu_scr[HALO:HALO + T, :] = z[:, 0:CONV_W] * jax.nn.sigmoid(z[:, CONV_W:2 * CONV_W])

    off = HALO - KSIZE // 2
    for b in range(8):
        sh_scr[b] = glu_scr[b:b + T + 3 * 8, :]

    for c in range(T // CONV_ROWS):
        r0 = c * CONV_ROWS
        acc = jnp.broadcast_to(cb_ref[0], (CONV_ROWS, CONV_W))
        for k in range(KSIZE):
            a, b = divmod(k + off, 8)
            acc = acc + sh_scr[b, r0 + 8 * a:r0 + 8 * a + CONV_ROWS, :] * cw_ref[0, k:k + 1, :]
        conv_scr[r0:r0 + CONV_ROWS, :] = acc
    hc = conv_scr[...]
    gavg = gavg_ref[...]
    mu = jnp.dot(hc.astype(BF16), gavg, preferred_element_type=F32)
    dc = hc - mu
    var = jnp.dot((dc * dc).astype(BF16), gavg, preferred_element_type=F32)
    hn = dc * lax.rsqrt(var + EPS) * gng_ref[0] + gnb_ref[0]
    ya = hn * jax.nn.sigmoid(hn)
    yac_ref[0, :, 0:CONV_W] = ya.astype(BF16)

    zb = z[:, 2 * CONV_W:2 * CONV_W + FOUR_W].astype(BF16)
    pq = jnp.dot(zb, dft_ref[...], preferred_element_type=F32)
    pq_ref[0, 0] = pq[:, 0:FOUR_W].astype(BF16)
    pq_ref[0, 1] = pq[:, FOUR_W:2 * FOUR_W].astype(BF16)

    c0 = 2 * CONV_W + FOUR_W
    zc = z[:, c0:c0 + 2 * SGU_W]
    zc = 0.5 * zc * (1.0 + jnp.tanh(GELU_C * (zc + GELU_A * (zc * zc * zc))))
    u = zc[:, 0:SGU_W]
    v = zc[:, SGU_W:2 * SGU_W]
    vm = jnp.mean(v, axis=-1, keepdims=True)
    vd = v - vm
    vv = jnp.mean(vd * vd, axis=-1, keepdims=True)
    vn = (vd * lax.rsqrt(vv + EPS) * lng_ref[0] + lnb_ref[0]).astype(BF16)
    n_chunk = T // CHUNK
    lane = lax.broadcasted_iota(I32, (CHUNK, 2 * HEAD), 1)
    for pr in range(SGU_HEADS // 2):
        cols = slice(2 * HEAD * pr, 2 * HEAD * (pr + 1))
        rhs = jnp.concatenate([vn[n * CHUNK:(n + 1) * CHUNK, cols] for n in range(n_chunk)], axis=1)
        lo = jnp.dot(sw_ref[0, 2 * pr], rhs, preferred_element_type=F32)
        hi = jnp.dot(sw_ref[0, 2 * pr + 1], rhs, preferred_element_type=F32)
        for n in range(n_chunk):
            sl = slice(n * 2 * HEAD, (n + 1) * 2 * HEAD)
            vs = jnp.where(lane < HEAD, lo[:, sl], hi[:, sl]) + sb_ref[0, :, cols]
            rows = slice(n * CHUNK, (n + 1) * CHUNK)
            yac_ref[0, rows, CONV_W + 2 * HEAD * pr:CONV_W + 2 * HEAD * (pr + 1)] = (u[rows, cols] * vs).astype(BF16)


def _mixer_in(x, mod, l, ln1_g, w_in_bf, conv_w, conv_b, gn_g, gn_b, ln_g, ln_b, sgu_w_bf, sgu_bias, dft64, gavg):
    bsz, seq, _ = x.shape
    T = T_MIX
    n_i = seq // T
    hb = T // HALO
    n_h = seq // HALO
    vec = lambda w: pl.BlockSpec((1, 1, w), lambda b, i: (l, 0, 0))
    return pl.pallas_call(
        _mixer_in_kernel,
        grid=(bsz, n_i),
        in_specs=[
            pl.BlockSpec((1, T, D), lambda b, i: (b, i, 0)),
            pl.BlockSpec((1, HALO, D), lambda b, i: (b, jnp.maximum(i * hb - 1, 0), 0)),
            pl.BlockSpec((1, HALO, D), lambda b, i: (b, jnp.minimum((i + 1) * hb, n_h - 1), 0)),
            pl.BlockSpec((1, 1, 8, D), lambda b, i: (l, b, 0, 0)),
            vec(D),
            pl.BlockSpec((1, D, Z_COLS), lambda b, i: (l, 0, 0)),
            pl.BlockSpec((1, KSIZE + 1, CONV_W), lambda b, i: (l, 0, 0)),
            vec(CONV_W), vec(CONV_W), vec(CONV_W), vec(SGU_W), vec(SGU_W),
            pl.BlockSpec((1, SGU_HEADS, CHUNK, CHUNK), lambda b, i: (l, 0, 0, 0)),
            pl.BlockSpec((1, CHUNK, SGU_W), lambda b, i: (l, 0, 0)),
            pl.BlockSpec((FOUR_W, 2 * FOUR_W), lambda b, i: (0, 0)),
            pl.BlockSpec((CONV_W, CONV_W), lambda b, i: (0, 0)),
        ],
        out_specs=[
            pl.BlockSpec((1, T, CONV_W + SGU_W), lambda b, i: (b, i, 0)),
            pl.BlockSpec((1, 2, T, FOUR_W), lambda b, i: (b, 0, i, 0)),
        ],
        out_shape=[
            jax.ShapeDtypeStruct((bsz, seq, CONV_W + SGU_W), BF16),
            jax.ShapeDtypeStruct((bsz, 2, seq, FOUR_W), BF16),
        ],
        scratch_shapes=[
            pltpu.VMEM((T + 2 * HALO, CONV_W), F32),
            pltpu.VMEM((8, T + 3 * 8, CONV_W), F32),
            pltpu.VMEM((T, CONV_W), F32),
        ],
        compiler_params=_cparams(("arbitrary", "arbitrary")),
        name="mixer_in",
    )(x, x, x, mod, ln1_g, w_in_bf, conv_w, conv_b, gn_g, gn_b, ln_g, ln_b, sgu_w_bf, sgu_bias, dft64, gavg)


def _dft_fold(scale, pq_ref, fold_ref, ph_ref):
    seq = pq_ref.shape[1] // 2
    half = seq // 2
    nb = seq // CHUNK
    rev = _block_reversal()
    row0 = lax.broadcasted_iota(I32, (CHUNK, FOUR_W), 0) == 0
    alt = jnp.where(lax.broadcasted_iota(I32, (CHUNK, 1), 0) % 2 == 0, 1.0, -1.0)
    alt_sum = jnp.zeros((1, FOUR_W), F32)
    for part, sign in ((0, 1.0), (1, -1.0)):
        base = part * seq
        for m in range(half // CHUNK):
            lo = pq_ref[0, base + CHUNK * m:base + CHUNK * (m + 1), :].astype(F32)
            up = pq_ref[0, base + CHUNK * (nb - 1 - m):base + CHUNK * (nb - m), :]
            mirrored = jnp.dot(rev, up, preferred_element_type=F32)
            if m >= 1:
                first = pq_ref[0, base + CHUNK * (nb - m):base + CHUNK * (nb - m) + 1, :].astype(F32)
                mirrored = jnp.where(row0, first, mirrored)
            folded = (lo + sign * mirrored).astype(BF16)
            fold_ref[part * half + CHUNK * m:part * half + CHUNK * (m + 1), :] = folded
            if part == 0:
                alt_sum = alt_sum + jnp.sum(folded.astype(F32) * alt, axis=0, keepdims=True)
    ph = pq_ref[0, half:half + 1, :].astype(F32) * scale
    rid = lax.broadcasted_iota(I32, (8, FOUR_W), 0)
    ph_ref[...] = jnp.where(rid == 1, alt_sum * scale + ph, ph)


def _block_reversal():
    rr = lax.broadcasted_iota(I32, (CHUNK, CHUNK), 0)
    cc = lax.broadcasted_iota(I32, (CHUNK, CHUNK), 1)
    return jnp.where((rr >= 1) & (cc == CHUNK - rr), 1.0, 0.0).astype(BF16)


def _seq_dft_kernel(scale, cs_ref, pq_ref, lo_ref, hi_ref, fold_scr, ph_scr, carry_scr):
    jj = pl.program_id(0)
    b = pl.program_id(1)
    fold_ref = fold_scr.at[b]
    half = fold_ref.shape[0] // 2

    @pl.when(jj == 0)
    def _():
        _dft_fold(scale, pq_ref, fold_ref, ph_scr.at[b])
        carry_scr[b] = jnp.broadcast_to(ph_scr[b][1:2, :], (8, FOUR_W))

    ph = ph_scr[b][0:1, :]
    alt = jnp.where(lax.broadcasted_iota(I32, (T_DFT, 1), 0) % 2 == 0, 1.0, -1.0)
    a = jnp.dot(cs_ref[:, 0:half], fold_ref[0:half, :], preferred_element_type=F32) + alt * ph
    minus_b = jnp.dot(cs_ref[:, half:2 * half], fold_ref[half:2 * half, :], preferred_element_type=F32)
    lo_ref[0] = (a + minus_b).astype(BF16)
    mirror_src = (a - minus_b).astype(BF16)
    rev = _block_reversal()
    row0 = lax.broadcasted_iota(I32, (CHUNK, FOUR_W), 0) == 0
    nbt = T_DFT // CHUNK
    for m in range(nbt):
        blk = mirror_src[CHUNK * (nbt - 1 - m):CHUNK * (nbt - m), :]
        mirrored = jnp.dot(rev, blk, preferred_element_type=F32)
        if m >= 1:
            first = mirror_src[CHUNK * (nbt - m):CHUNK * (nbt - m) + 1, :].astype(F32)
        else:
            first = carry_scr[b][0:1, :]
        hi_ref[0, CHUNK * m:CHUNK * (m + 1), :] = jnp.where(row0, first, mirrored).astype(BF16)
    carry_scr[b] = jnp.broadcast_to(mirror_src[0:1, :].astype(F32), (8, FOUR_W))


def _seq_dft(cs, pq):
    bsz, two_s, _ = pq.shape
    seq = two_s // 2
    scale = 1.0 / (seq * HEAD) ** 0.5
    n_t = seq // 2 // T_DFT
    half_out = jax.ShapeDtypeStruct((bsz, seq // 2, FOUR_W), BF16)
    return pl.pallas_call(
        functools.partial(_seq_dft_kernel, scale),
        grid=(n_t, bsz),
        in_specs=[pl.BlockSpec((T_DFT, seq), lambda jj, b: (n_t - 1 - jj, 0)),
                  pl.BlockSpec((1, two_s, FOUR_W), lambda jj, b: (jnp.where(jj == 0, b, 0), 0, 0))],
        out_specs=[pl.BlockSpec((1, T_DFT, FOUR_W), lambda jj, b: (b, n_t - 1 - jj, 0)),
                   pl.BlockSpec((1, T_DFT, FOUR_W), lambda jj, b: (b, jj, 0))],
        out_shape=[half_out, half_out],
        scratch_shapes=[pltpu.VMEM((bsz, seq, FOUR_W), BF16), pltpu.VMEM((bsz, 8, FOUR_W), F32),
                        pltpu.VMEM((bsz, 8, FOUR_W), F32)],
        compiler_params=_cparams(("arbitrary", "arbitrary")),
        name="seq_dft",
    )(cs, pq)


def _mixer_out_kernel(per_b, yac_ref, yblo_ref, ybhi_ref, x_ref, mod_ref, g2_ref, wout_ref, rwt_ref, rb_ref, upper_ref,
                      x1_ref, h2pa_ref, h2pb_ref, route_ref, cnt_ref, cnt_scr):
    i = pl.program_id(0)
    T = T_MIX
    yb = jnp.where(i % per_b < per_b // 2, yblo_ref[...], ybhi_ref[...])

    @pl.when(i == 0)
    def _():
        cnt_scr[...] = jnp.zeros_like(cnt_scr)

    mod = mod_ref[0, 0]
    gate1 = mod[2:3, :]
    shift2 = mod[3:4, :]
    gain2 = g2_ref[0] * (1.0 + mod[4:5, :])
    yac = yac_ref[...]
    ycat = jnp.concatenate([yac[:, 0:CONV_W], yb, yac[:, CONV_W:CONV_W + SGU_W]], axis=1)
    o = jnp.dot(ycat, wout_ref[0], preferred_element_type=F32)
    x1 = x_ref[...] + gate1 * o
    x1_ref[...] = x1
    ms = jnp.mean(x1 * x1, axis=-1, keepdims=True)
    h2 = x1 * lax.rsqrt(ms + EPS) * gain2 + shift2
    h2p = _pack_bf16_pair(h2[:, 0:D // 2], h2[:, D // 2:D])
    h2pa_ref[...] = h2p[:, 0:ROW_W]
    h2pb_ref[...] = h2p[:, ROW_W:2 * ROW_W]

    h_hi = h2.astype(BF16)
    h_lo = (h2 - h_hi.astype(F32)).astype(BF16)
    nt = (((1,), (1,)), ((), ()))
    part = lax.dot_general(rwt_ref[...], h_hi, nt, preferred_element_type=F32)
    logits = (part[0:N_EXP] + part[N_EXP:2 * N_EXP]
              + lax.dot_general(rwt_ref[0:N_EXP, :], h_lo, nt, preferred_element_type=F32))
    mx = jnp.max(logits, axis=0, keepdims=True)
    ex = jnp.exp(logits - mx)
    probs = ex / jnp.sum(ex, axis=0, keepdims=True)
    sel = probs + rb_ref[...]
    sel3 = sel.reshape(N_GRP, EPG, T)
    probs3 = probs.reshape(N_GRP, EPG, T)
    jj = lax.broadcasted_iota(I32, (N_GRP, EPG, T), 1)
    m1 = jnp.max(sel3, axis=1, keepdims=True)
    i1 = jnp.min(jnp.where(sel3 == m1, jj, EPG), axis=1, keepdims=True)
    rest = jnp.where(jj == i1, -jnp.inf, sel3)
    m2 = jnp.max(rest, axis=1, keepdims=True)
    i2 = jnp.min(jnp.where(rest == m2, jj, EPG), axis=1, keepdims=True)
    gscore = m1 + m2
    gg = lax.broadcasted_iota(I32, (N_GRP, 1, T), 0)
    gmax = jnp.max(gscore, axis=0, keepdims=True)
    gidx = jnp.min(jnp.where(gscore == gmax, gg, N_GRP), axis=0, keepdims=True)
    ing = gg == gidx
    pick = lambda a, zero: jnp.sum(jnp.where(ing, a, zero), axis=0)
    p1 = jnp.sum(jnp.where(jj == i1, probs3, 0.0), axis=1, keepdims=True)
    p2 = jnp.sum(jnp.where(jj == i2, probs3, 0.0), axis=1, keepdims=True)
    pa = pick(p1, 0.0)
    pb = pick(p2, 0.0)
    gbase = gidx[0] * EPG
    e0 = gbase + pick(i1, 0)
    e1 = gbase + pick(i2, 0)
    den = pa + pb
    gw0 = pa / den
    gw1 = pb / den

    ee = lax.broadcasted_iota(I32, (N_EXP, T), 0)
    oh0 = ee == e0
    oh1 = ee == e1
    amat = jnp.where(oh0 | oh1, 1.0, 0.0)
    before = jnp.dot(amat.astype(BF16), upper_ref[...], preferred_element_type=F32) + cnt_scr[...]
    r0 = jnp.sum(jnp.where(oh0, before, 0.0), axis=0, keepdims=True)
    r1 = jnp.sum(jnp.where(oh1, before, 0.0), axis=0, keepdims=True)
    cnt_scr[...] = cnt_scr[...] + jnp.sum(amat, axis=1, keepdims=True)
    cnt_ref[...] = cnt_scr[...]

    rid = lax.broadcasted_iota(I32, (8, T), 0)
    route = jnp.zeros((8, T), I32)
    bits = lambda w: lax.bitcast_convert_type(w, I32)
    for k, val in enumerate((e0, e1, r0.astype(I32), r1.astype(I32), bits(gw0), bits(gw1))):
        route = jnp.where(rid == k, val, route)
    route_ref[...] = route


def _mixer_out(yac, yb_lo, yb_hi, x, mod, l, ln2_g, w_out_bf, router_wt, router_b, bsz):
    n_tok = x.shape[0]
    T = T_MIX
    per_b = n_tok // bsz // T
    hb = per_b // 2
    row = lambda w: pl.BlockSpec((T, w), lambda i: (i, 0))
    lo_spec = pl.BlockSpec((T, FOUR_W), lambda i: ((i // per_b) * hb + jnp.minimum(i % per_b, hb - 1), 0))
    hi_spec = pl.BlockSpec((T, FOUR_W), lambda i: ((i // per_b) * hb + jnp.maximum(i % per_b - hb, 0), 0))
    upper = (lax.broadcasted_iota(I32, (T, T), 0) < lax.broadcasted_iota(I32, (T, T), 1)).astype(BF16)
    return pl.pallas_call(
        functools.partial(_mixer_out_kernel, per_b),
        grid=(n_tok // T,),
        in_specs=[
            row(CONV_W + SGU_W), lo_spec, hi_spec, row(D),
            pl.BlockSpec((1, 1, 8, D), lambda i: (l, i // per_b, 0, 0)),
            pl.BlockSpec((1, 1, D), lambda i: (l, 0, 0)),
            pl.BlockSpec((1, D, D), lambda i: (l, 0, 0)),
            pl.BlockSpec((2 * N_EXP, D), lambda i: (0, 0)),
            pl.BlockSpec((N_EXP, 1), lambda i: (0, 0)),
            pl.BlockSpec((T, T), lambda i: (0, 0)),
        ],
        out_specs=[row(D), row(ROW_W), row(ROW_W), pl.BlockSpec((8, T), lambda i: (0, i)),
                   pl.BlockSpec((N_EXP, 1), lambda i: (0, 0))],
        out_shape=[
            jax.ShapeDtypeStruct((n_tok, D), F32),
            jax.ShapeDtypeStruct((n_tok, ROW_W), U32),
            jax.ShapeDtypeStruct((n_tok, ROW_W), U32),
            jax.ShapeDtypeStruct((8, n_tok), I32),
            jax.ShapeDtypeStruct((N_EXP, 1), F32),
        ],
        scratch_shapes=[pltpu.VMEM((N_EXP, 1), F32)],
        compiler_params=_cparams(("arbitrary",)),
        name="mixer_out",
    )(yac, yb_lo, yb_hi, x, mod, ln2_g, w_out_bf, router_wt, router_b, upper)


def _sc_mesh():
    return plsc.VectorSubcoreMesh(core_axis_name="c", subcore_axis_name="s")


def _sc_scatter2(src_a, src_b, idx0, idx1, n_rows):
    n = src_a.shape[0]
    out = jax.ShapeDtypeStruct((n_rows, ROW_W), src_a.dtype)

    @functools.partial(pl.kernel, out_type=[out, out], mesh=_sc_mesh())
    def scatter(xa_hbm, xb_hbm, i0_hbm, i1_hbm, oa_hbm, ob_hbm):
        for x_hbm, o_hbm in ((xa_hbm, oa_hbm), (xb_hbm, ob_hbm)):
            def body(x_vmem, i0_vmem, i1_vmem, o_hbm=o_hbm):
                pltpu.sync_copy(x_vmem, o_hbm.at[i0_vmem.at[0]])
                pltpu.sync_copy(x_vmem, o_hbm.at[i1_vmem.at[0]])

            pltpu.emit_pipeline(
                body, grid=(n // SC_WIN,),
                in_specs=[pl.BlockSpec((SC_WIN, ROW_W), index_map=lambda i: (i, 0)),
                          pl.BlockSpec((1, SC_WIN), index_map=lambda i: (0, i)),
                          pl.BlockSpec((1, SC_WIN), index_map=lambda i: (0, i))],
                out_specs=[],
                core_axis_name=("c", "s"), dimension_semantics=(pltpu.PARALLEL,),
            )(x_hbm, i0_hbm, i1_hbm)

    return scatter(src_a, src_b, idx0, idx1)


def _sc_gather(src_a, src_b, idx):
    m = idx.shape[1]
    out = jax.ShapeDtypeStruct((m, ROW_W), src_a.dtype)

    @functools.partial(pl.kernel, out_type=[out, out], mesh=_sc_mesh())
    def gather(xa_hbm, xb_hbm, i_hbm, oa_hbm, ob_hbm):
        for x_hbm, o_hbm in ((xa_hbm, oa_hbm), (xb_hbm, ob_hbm)):
            def body(i_vmem, o_vmem, x_hbm=x_hbm):
                pltpu.sync_copy(x_hbm.at[i_vmem.at[0]], o_vmem)

            pltpu.emit_pipeline(
                body, grid=(m // SC_WIN,),
                in_specs=[pl.BlockSpec((1, SC_WIN), index_map=lambda i: (0, i))],
                out_specs=[pl.BlockSpec((SC_WIN, ROW_W), index_map=lambda i: (i, 0))],
                core_axis_name=("c", "s"), dimension_semantics=(pltpu.PARALLEL,),
            )(i_hbm, o_hbm)

    return gather(src_a, src_b, idx)


def _experts_kernel(l, n_blk, start_ref, nchunk_ref, tail_ref, xpa_ref, xpb_ref, w1_ref, w3_ref, w2_ref, ypa_ref, ypb_ref,
                    w13_scr, w2_scr, wbuf13, wbuf2, xbuf, ybuf, wsem, xsem, ysem):
    e = pl.program_id(0)
    nc = nchunk_ref[e]
    chunk0 = start_ref[e]
    n_used = start_ref[N_EXP]

    def w_copies(ex):
        slot = ex % (W_AHEAD + 1)
        cps = []
        for p in range(W_PIECES):
            r13 = pl.ds(p * (D // W_PIECES), D // W_PIECES)
            r2 = pl.ds(p * (D_FF // W_PIECES), D_FF // W_PIECES)
            cps.append(pltpu.make_async_copy(w1_ref.at[l, ex, r13], wbuf13.at[slot, 0, r13], wsem.at[slot]))
            cps.append(pltpu.make_async_copy(w3_ref.at[l, ex, r13], wbuf13.at[slot, 1, r13], wsem.at[slot]))
            cps.append(pltpu.make_async_copy(w2_ref.at[l, ex, r2], wbuf2.at[slot, r2], wsem.at[slot]))
        return cps

    def x_copies(g):
        rows = pl.ds(pl.multiple_of(g * BLK, BLK), BLK)
        slot = g % (X_AHEAD + 1)
        return [pltpu.make_async_copy(src.at[rows], xbuf.at[slot, h], xsem.at[slot])
                for h, src in enumerate((xpa_ref, xpb_ref))]

    def y_copies(g):
        rows = pl.ds(pl.multiple_of(g * BLK, BLK), BLK)
        slot = g % 2
        return [pltpu.make_async_copy(ybuf.at[slot, h], dst.at[rows], ysem.at[slot])
                for h, dst in enumerate((ypa_ref, ypb_ref))]

    def start(cps):
        for cp in cps:
            cp.start()

    def wait(cps):
        for cp in cps:
            cp.wait()

    @pl.when(e == 0)
    def _():
        for j in range(X_AHEAD):
            @pl.when(j < n_used)
            def _():
                start(x_copies(j))

        for j in range(W_AHEAD):
            start(w_copies(j))

    @pl.when(e < N_EXP)
    def _():
        wslot = e % (W_AHEAD + 1)
        wait(w_copies(e))

        @pl.when(e + W_AHEAD < N_EXP)
        def _():
            start(w_copies(e + W_AHEAD))

        @pl.when(nc > 0)
        def _():
            w13_scr[:, 0:D_FF] = wbuf13[wslot, 0].astype(BF16)
            w13_scr[:, D_FF:2 * D_FF] = wbuf13[wslot, 1].astype(BF16)
            w2_scr[...] = wbuf2[wslot].astype(BF16)

            def chunk(c, carry):
                g = chunk0 + c
                slot = g % (X_AHEAD + 1)

                @pl.when(g + X_AHEAD < n_used)
                def _():
                    start(x_copies(g + X_AHEAD))

                wait(x_copies(g))

                @pl.when(g >= 2)
                def _():
                    wait(y_copies(g - 2))

                def swiglu(rows):
                    a, b = _unpack_bf16_pair(jnp.concatenate([xbuf[slot, 0, 0:rows], xbuf[slot, 1, 0:rows]], axis=1))
                    x = jnp.concatenate([a.astype(BF16), b.astype(BF16)], axis=1)
                    h13 = jnp.dot(x, w13_scr[...], preferred_element_type=F32)
                    h1 = h13[:, 0:D_FF]
                    act = (h1 * jax.nn.sigmoid(h1) * h13[:, D_FF:2 * D_FF]).astype(BF16)
                    y = jnp.dot(act, w2_scr[...], preferred_element_type=F32)
                    yp = _pack_bf16_pair(y[:, 0:D // 2], y[:, D // 2:D])
                    ybuf[g % 2, 0, 0:rows] = yp[:, 0:ROW_W]
                    ybuf[g % 2, 1, 0:rows] = yp[:, ROW_W:2 * ROW_W]

                need = jnp.where(c == nc - 1, (tail_ref[e] + CHUNK - 1) // CHUNK * CHUNK, BLK)
                for rows in range(CHUNK, BLK, CHUNK):
                    @pl.when(need == rows)
                    def _(rows=rows):
                        swiglu(rows)
                        ybuf[g % 2, :, rows:BLK] = jnp.zeros((2, BLK - rows, ROW_W), U32)

                @pl.when(need == BLK)
                def _():
                    swiglu(BLK)

                start(y_copies(g))
                return carry

            lax.fori_loop(0, nc, chunk, 0)

    @pl.when(e == N_EXP)
    def _():
        @pl.when(n_used >= 2)
        def _():
            wait(y_copies(n_used - 2))

        wait(y_copies(n_used - 1))
        ybuf[0] = jnp.zeros((2, BLK, ROW_W), U32)

        def fill_one(g, carry):
            rows = pl.ds(pl.multiple_of(g * BLK, BLK), BLK)
            cps = [pltpu.make_async_copy(ybuf.at[0, h], dst.at[rows], ysem.at[0])
                   for h, dst in enumerate((ypa_ref, ypb_ref))]
            start(cps)
            wait(cps)
            return carry

        lax.fori_loop(n_used, n_blk, fill_one, 0)


def _experts(chunk_start, n_chunk, tail_rows, xpa, xpb, w1, w3, w2, l):
    n_rows = xpa.shape[0]
    n_blk = n_rows // BLK
    hbm = pl.BlockSpec(memory_space=pl.ANY)
    half = jax.ShapeDtypeStruct((n_rows, ROW_W), U32)
    return pl.pallas_call(
        functools.partial(_experts_kernel, l, n_blk),
        grid_spec=pltpu.PrefetchScalarGridSpec(
            num_scalar_prefetch=3,
            grid=(N_EXP + 1,),
            in_specs=[hbm, hbm, hbm, hbm, hbm],
            out_specs=[hbm, hbm],
            scratch_shapes=[pltpu.VMEM((D, 2 * D_FF), BF16), pltpu.VMEM((D_FF, D), BF16),
                            pltpu.VMEM((W_AHEAD + 1, 2, D, D_FF), F32), pltpu.VMEM((W_AHEAD + 1, D_FF, D), F32),
                            pltpu.VMEM((X_AHEAD + 1, 2, BLK, ROW_W), U32), pltpu.VMEM((2, 2, BLK, ROW_W), U32),
                            pltpu.SemaphoreType.DMA((W_AHEAD + 1,)), pltpu.SemaphoreType.DMA((X_AHEAD + 1,)),
                            pltpu.SemaphoreType.DMA((2,))],
        ),
        out_shape=[half, half],
        compiler_params=_cparams(("arbitrary",)),
        name="experts",
    )(chunk_start, n_chunk, tail_rows, xpa, xpb, w1, w3, w2)


def _combine_kernel(final, ga0_ref, gb0_ref, ga1_ref, gb1_ref, x1_ref, route_ref, mod_ref, fg_ref, o_ref):
    gws = lax.bitcast_convert_type(route_ref[...], F32)
    gws = jnp.concatenate([gws, jnp.zeros((CHUNK - 8, T_CMB), F32)], axis=0).T
    gw0 = gws[:, 4:5]
    gw1 = gws[:, 5:6]
    a0, b0 = _unpack_bf16_pair(jnp.concatenate([ga0_ref[...], gb0_ref[...]], axis=1))
    a1, b1 = _unpack_bf16_pair(jnp.concatenate([ga1_ref[...], gb1_ref[...]], axis=1))
    y = jnp.concatenate([gw0 * a0 + gw1 * a1, gw0 * b0 + gw1 * b1], axis=1)
    gate2 = mod_ref[0, 0][5:6, :]
    x2 = x1_ref[...] + gate2 * y
    if final:
        ms = jnp.mean(x2 * x2, axis=-1, keepdims=True)
        x2 = x2 * lax.rsqrt(ms + EPS) * fg_ref[...]
    o_ref[...] = x2


def _combine(ga, gb, x1, route, mod, l, final_g, bsz, final):
    n_tok = x1.shape[0]
    n_i = n_tok // T_CMB
    per_b = n_i // bsz
    first = pl.BlockSpec((T_CMB, ROW_W), lambda i: (i, 0))
    second = pl.BlockSpec((T_CMB, ROW_W), lambda i: (i + n_i, 0))
    return pl.pallas_call(
        functools.partial(_combine_kernel, final),
        grid=(n_i,),
        in_specs=[
            first, first, second, second,
            pl.BlockSpec((T_CMB, D), lambda i: (i, 0)),
            pl.BlockSpec((8, T_CMB), lambda i: (0, i)),
            pl.BlockSpec((1, 1, 8, D), lambda i: (l, i // per_b, 0, 0)),
            pl.BlockSpec((1, D), lambda i: (0, 0)),
        ],
        out_specs=pl.BlockSpec((T_CMB, D), lambda i: (i, 0)),
        out_shape=jax.ShapeDtypeStruct((n_tok, D), F32),
        compiler_params=_cparams(("arbitrary",)),
        name="combine",
    )(ga, gb, ga, gb, x1, route, mod, final_g)


def _dft_table_kernel(t1_ref, t2_ref, o_ref):
    half = t2_ref.shape[2]
    c2 = t2_ref[0]
    s2 = t2_ref[1]
    for j in range(TAB_GROUP):
        c1 = t1_ref[j, 0:1, :]
        s1 = t1_ref[j, 1:2, :]
        rows = slice(j * TAB_ROWS, (j + 1) * TAB_ROWS)
        o_ref[rows, 0:half] = (c1 * c2 - s1 * s2).astype(BF16)
        o_ref[rows, half:2 * half] = (-(s1 * c2 + c1 * s2)).astype(BF16)


def _dft_tables(seq):
    scale = 1.0 / (seq * HEAD) ** 0.5
    n_hi = seq // TAB_ROWS
    n = lax.broadcasted_iota(I32, (1, seq // 2), 1)
    kh = lax.broadcasted_iota(I32, (n_hi // 2, 1), 0)
    a1 = ((kh * n) % n_hi).astype(F32) * (2.0 * jnp.pi / n_hi)
    t1 = jnp.stack([jnp.cos(a1), jnp.sin(a1)], axis=1)
    kl = lax.broadcasted_iota(I32, (TAB_ROWS, 1), 0)
    a2 = ((kl * n) % seq).astype(F32) * (2.0 * jnp.pi / seq)
    t2 = jnp.stack([jnp.cos(a2) * scale, jnp.sin(a2) * scale], axis=0)
    cs = pl.pallas_call(
        _dft_table_kernel,
        grid=(n_hi // 2 // TAB_GROUP,),
        in_specs=[pl.BlockSpec((TAB_GROUP, 2, seq // 2), lambda i: (i, 0, 0)),
                  pl.BlockSpec((2, TAB_ROWS, seq // 2), lambda i: (0, 0, 0))],
        out_specs=pl.BlockSpec((TAB_GROUP * TAB_ROWS, seq), lambda i: (i, 0)),
        out_shape=jax.ShapeDtypeStruct((seq // 2, seq), BF16),
        compiler_params=_cparams(("arbitrary",)),
        name="dft_table",
    )(t1, t2)
    d = lax.broadcasted_iota(I32, (FOUR_W, FOUR_W), 0)
    q = lax.broadcasted_iota(I32, (FOUR_W, FOUR_W), 1)
    same = (d // HEAD) == (q // HEAD)
    ang64 = ((d * q) % HEAD).astype(F32) * (2.0 * jnp.pi / HEAD)
    dft64 = jnp.concatenate([jnp.where(same, jnp.cos(ang64), 0.0),
                             jnp.where(same, jnp.sin(ang64), 0.0)], axis=1).astype(BF16)
    return cs, dft64


def _routing_tables(route, counts_f):
    counts = counts_f[:, 0].astype(I32)
    pc = (counts + BLK - 1) // BLK * BLK
    pends = jnp.cumsum(pc)
    pstarts = pends - pc
    eid = lax.broadcasted_iota(I32, (N_EXP, 1), 0)

    def dest_of(e, r):
        return jnp.sum(jnp.where(e[None, :] == eid, pstarts[:, None], 0), axis=0) + r

    dest = jnp.concatenate([dest_of(route[0], route[2]), dest_of(route[1], route[3])])
    chunk_start = jnp.concatenate([pstarts, pends[-1:]]) // BLK
    n_chunk = jnp.concatenate([pc // BLK, jnp.zeros((1,), I32)])
    tail_rows = jnp.concatenate([counts - (pc - BLK), jnp.zeros((1,), I32)])
    return dest, chunk_start, n_chunk, tail_rows


def kernel(x, c, ln1_g, ln2_g, w_ada, b_ada, w_in, w_out, conv_w, conv_b, conv_gn_g, conv_gn_b, sgu_ln_g,
           sgu_ln_b, sgu_w, sgu_b, router_w, router_b, exp_w1, exp_w3, exp_w2, final_g):
    bsz, seq, _ = x.shape
    depth = w_in.shape[0]
    n_tok = bsz * seq
    n_blk = (2 * n_tok + N_EXP * (BLK - 1) + BLK - 1) // BLK
    r3 = lambda a: a.reshape(depth, 1, a.shape[-1])

    mod = jnp.pad(_ada_mod(c, w_ada, b_ada).reshape(depth, bsz, 6, D), ((0, 0), (0, 0), (0, 2), (0, 0)))
    cs, dft64 = _dft_tables(seq)
    hd = lax.broadcasted_iota(I32, (CONV_W, CONV_W), 0) // HEAD
    gavg = jnp.where(hd == hd.T, 1.0 / HEAD, 0.0).astype(BF16)
    w_in_bf = w_in.astype(BF16)
    w_out_bf = w_out.astype(BF16)
    sgu_w_bf = sgu_w.astype(BF16)
    sgu_bias = jnp.repeat(jnp.swapaxes(sgu_b, 1, 2), HEAD, axis=2)
    conv_w_p = jnp.pad(conv_w, ((0, 0), (0, 1), (0, 0)))
    rwt_hi = router_w.T.astype(BF16)
    rwt_lo = (router_w.T - rwt_hi.astype(F32)).astype(BF16)
    router_wt = jnp.concatenate([rwt_hi, rwt_lo], axis=0)
    router_bc = router_b.reshape(N_EXP, 1)
    fg = final_g.reshape(1, D)

    for l in range(depth):
        yac, pq = _mixer_in(x, mod, l, r3(ln1_g), w_in_bf, conv_w_p, r3(conv_b), r3(conv_gn_g), r3(conv_gn_b),
                            r3(sgu_ln_g), r3(sgu_ln_b), sgu_w_bf, sgu_bias, dft64, gavg)
        yb_lo, yb_hi = _seq_dft(cs, pq.reshape(bsz, 2 * seq, FOUR_W))
        x1, h2pa, h2pb, route, counts = _mixer_out(
            yac.reshape(n_tok, CONV_W + SGU_W), yb_lo.reshape(n_tok // 2, FOUR_W),
            yb_hi.reshape(n_tok // 2, FOUR_W), x.reshape(n_tok, D),
            mod, l, r3(ln2_g), w_out_bf, router_wt, router_bc, bsz)
        dest, chunk_start, n_chunk, tail_rows = _routing_tables(route, counts)
        d0 = dest[:n_tok].reshape(1, n_tok)
        d1 = dest[n_tok:].reshape(1, n_tok)
        xpa, xpb = _sc_scatter2(h2pa, h2pb, d0, d1, n_blk * BLK)
        ypa, ypb = _experts(chunk_start, n_chunk, tail_rows, xpa, xpb, exp_w1, exp_w3, exp_w2, l)
        ga, gb = _sc_gather(ypa, ypb, dest.reshape(1, 2 * n_tok))
        x = _combine(ga, gb, x1, route, mod, l, fg, bsz, l == depth - 1).reshape(bsz, seq, D)
    return x
```

```python
import functools

import jax
import jax.numpy as jnp
from jax import lax
from jax.experimental import pallas as pl
from jax.experimental.pallas import tpu as pltpu
from jax.experimental.pallas import tpu_sc as plsc

F32 = jnp.float32
BF16 = jnp.bfloat16
I32 = jnp.int32
U32 = jnp.uint32

D = 1024
HEAD = 64
CONV_W = 384
FOUR_W = 256
SGU_W = 384
SGU_HEADS = SGU_W // HEAD
Z_COLS = 2 * CONV_W + FOUR_W + 2 * SGU_W
KSIZE = 31
HALO = 16
CHUNK = 128
N_EXP = 64
N_GRP = 8
EPG = N_EXP // N_GRP
D_FF = D // 2
EPS = 1e-6
GELU_C = 0.7978845608028654
GELU_A = 0.044715

T_MIX = 1024
T_DFT = 512
T_CMB = 1024
ROW_W = 256
SC_WIN = 128
BLK = 768
TAIL_STEP = 256
X_AHEAD = 3
W_AHEAD = 1
CONV_ROWS = 64
TAB_ROWS = 64
TAB_GROUP = 4
W_PIECES = 4
VMEM_LIMIT = 56 * 1024 * 1024


def _cparams(sem):
    return pltpu.CompilerParams(dimension_semantics=sem, vmem_limit_bytes=VMEM_LIMIT)


def _pack_bf16_pair(a, b):
    ua = lax.bitcast_convert_type(a.astype(BF16).astype(F32), U32) >> 16
    ub = lax.bitcast_convert_type(b.astype(BF16).astype(F32), U32) & jnp.uint32(0xFFFF0000)
    return ua | ub


def _unpack_bf16_pair(p):
    a = lax.bitcast_convert_type(p << 16, F32)
    b = lax.bitcast_convert_type(p & jnp.uint32(0xFFFF0000), F32)
    return a, b


def _ada_kernel(c_ref, *refs):
    w_refs, b_ref, o_ref = refs[:-2], refs[-2], refs[-1]
    c = c_ref[...]
    ca = c * jax.nn.sigmoid(c)
    ca_hi = ca.astype(BF16)
    ca_lo = (ca - ca_hi.astype(F32)).astype(BF16)
    tn = w_refs[0].shape[2]
    for j, w_ref in enumerate(w_refs):
        cols = slice(j * tn, (j + 1) * tn)
        w = w_ref[0]
        w_hi = w.astype(BF16)
        w_lo = (w - w_hi.astype(F32)).astype(BF16)
        acc = jnp.dot(ca_hi, w_hi, preferred_element_type=F32)
        acc = acc + jnp.dot(ca_hi, w_lo, preferred_element_type=F32)
        acc = acc + jnp.dot(ca_lo, w_hi, preferred_element_type=F32)
        o_ref[0, :, cols] = acc + b_ref[0, :, cols]


def _ada_mod(c, w_ada, b_ada):
    depth, _, ncol = w_ada.shape
    bsz = c.shape[0]
    n_slab = 4
    n_half = 2
    tn = ncol // (n_slab * n_half)
    return pl.pallas_call(
        _ada_kernel,
        grid=(depth, n_half),
        in_specs=[pl.BlockSpec((bsz, D), lambda l, h: (0, 0))]
        + [pl.BlockSpec((1, D, tn), functools.partial(lambda j, l, h: (l, 0, n_slab * h + j), j))
           for j in range(n_slab)]
        + [pl.BlockSpec((1, 1, n_slab * tn), lambda l, h: (l, 0, h))],
        out_specs=pl.BlockSpec((1, bsz, n_slab * tn), lambda l, h: (l, 0, h)),
        out_shape=jax.ShapeDtypeStruct((depth, bsz, ncol), F32),
        compiler_params=_cparams(("arbitrary", "arbitrary")),
        name="ada_mod",
    )(c, *([w_ada] * n_slab), b_ada.reshape(depth, 1, ncol))


def _mixer_in_kernel(xm_ref, xp_ref, xn_ref, mod_ref, g1_ref, win_ref, cw_ref, cb_ref, gng_ref, gnb_ref,
                     lng_ref, lnb_ref, sw_ref, sb_ref, dft_ref, gavg_ref,
                     yac_ref, pq_ref, glu_scr, sh_scr, conv_scr):
    i = pl.program_id(1)
    n_i = pl.num_programs(1)
    T = T_MIX
    mod = mod_ref[0, 0]
    shift1 = mod[0:1, :]
    gain1 = g1_ref[0] * (1.0 + mod[1:2, :])

    def norm_mod(x):
        ms = jnp.mean(x * x, axis=-1, keepdims=True)
        return x * lax.rsqrt(ms + EPS) * gain1 + shift1

    h = norm_mod(xm_ref[0]).astype(BF16)
    z = jnp.dot(h, win_ref[0], preferred_element_type=F32)

    hh = norm_mod(jnp.concatenate([xp_ref[0], xn_ref[0]], axis=0)).astype(BF16)
    zh = jnp.dot(hh, win_ref[0, :, 0:2 * CONV_W], preferred_element_type=F32)
    glu_h = zh[:, 0:CONV_W] * jax.nn.sigmoid(zh[:, CONV_W:2 * CONV_W])
    glu_scr[0:HALO, :] = jnp.where(i > 0, glu_h[0:HALO], 0.0)
    glu_scr[HALO + T:2 * HALO + T, :] = jnp.where(i < n_i - 1, glu_h[HALO:2 * HALO], 0.0)
    glu_scr[HALO:HALO + T, :] = z[:, 0:CONV_W] * jax.nn.sigmoid(z[:, CONV_W:2 * CONV_W])

    off = HALO - KSIZE // 2
    for b in range(8):
        sh_scr[b] = glu_scr[b:b + T + 3 * 8, :]

    for c in range(T // CONV_ROWS):
        r0 = c * CONV_ROWS
        acc = jnp.broadcast_to(cb_ref[0], (CONV_ROWS, CONV_W))
        for k in range(KSIZE):
            a, b = divmod(k + off, 8)
            acc = acc + sh_scr[b, r0 + 8 * a:r0 + 8 * a + CONV_ROWS, :] * cw_ref[0, k:k + 1, :]
        conv_scr[r0:r0 + CONV_ROWS, :] = acc
    hc = conv_scr[...]
    gavg = gavg_ref[...]
    mu = jnp.dot(hc.astype(BF16), gavg, preferred_element_type=F32)
    dc = hc - mu
    var = jnp.dot((dc * dc).astype(BF16), gavg, preferred_element_type=F32)
    hn = dc * lax.rsqrt(var + EPS) * gng_ref[0] + gnb_ref[0]
    ya = hn * jax.nn.sigmoid(hn)
    yac_ref[0, :, 0:CONV_W] = ya.astype(BF16)

    zb = z[:, 2 * CONV_W:2 * CONV_W + FOUR_W].astype(BF16)
    pq = jnp.dot(zb, dft_ref[...], preferred_element_type=F32)
    pq_ref[0, 0] = pq[:, 0:FOUR_W].astype(BF16)
    pq_ref[0, 1] = pq[:, FOUR_W:2 * FOUR_W].astype(BF16)

    c0 = 2 * CONV_W + FOUR_W
    zc = z[:, c0:c0 + 2 * SGU_W]
    zc = 0.5 * zc * (1.0 + jnp.tanh(GELU_C * (zc + GELU_A * (zc * zc * zc))))
    u = zc[:, 0:SGU_W]
    v = zc[:, SGU_W:2 * SGU_W]
    vm = jnp.mean(v, axis=-1, keepdims=True)
    vd = v - vm
    vv = jnp.mean(vd * vd, axis=-1, keepdims=True)
    vn = (vd * lax.rsqrt(vv + EPS) * lng_ref[0] + lnb_ref[0]).astype(BF16)
    n_chunk = T // CHUNK
    lane = lax.broadcasted_iota(I32, (CHUNK, 2 * HEAD), 1)
    for pr in range(SGU_HEADS // 2):
        cols = slice(2 * HEAD * pr, 2 * HEAD * (pr + 1))
        rhs = jnp.concatenate([vn[n * CHUNK:(n + 1) * CHUNK, cols] for n in range(n_chunk)], axis=1)
        lo = jnp.dot(sw_ref[0, 2 * pr], rhs, preferred_element_type=F32)
        hi = jnp.dot(sw_ref[0, 2 * pr + 1], rhs, preferred_element_type=F32)
        for n in range(n_chunk):
            sl = slice(n * 2 * HEAD, (n + 1) * 2 * HEAD)
            vs = jnp.where(lane < HEAD, lo[:, sl], hi[:, sl]) + sb_ref[0, :, cols]
            rows = slice(n * CHUNK, (n + 1) * CHUNK)
            yac_ref[0, rows, CONV_W + 2 * HEAD * pr:CONV_W + 2 * HEAD * (pr + 1)] = (u[rows, cols] * vs).astype(BF16)


def _mixer_in(x, mod, l, ln1_g, w_in_bf, conv_w, conv_b, gn_g, gn_b, ln_g, ln_b, sgu_w_bf, sgu_bias, dft64, gavg):
    bsz, seq, _ = x.shape
    T = T_MIX
    n_i = seq // T
    hb = T // HALO
    n_h = seq // HALO
    vec = lambda w: pl.BlockSpec((1, 1, w), lambda b, i: (l, 0, 0))
    return pl.pallas_call(
        _mixer_in_kernel,
        grid=(bsz, n_i),
        in_specs=[
            pl.BlockSpec((1, T, D), lambda b, i: (b, i, 0)),
            pl.BlockSpec((1, HALO, D), lambda b, i: (b, jnp.maximum(i * hb - 1, 0), 0)),
            pl.BlockSpec((1, HALO, D), lambda b, i: (b, jnp.minimum((i + 1) * hb, n_h - 1), 0)),
            pl.BlockSpec((1, 1, 8, D), lambda b, i: (l, b, 0, 0)),
            vec(D),
            pl.BlockSpec((1, D, Z_COLS), lambda b, i: (l, 0, 0)),
            pl.BlockSpec((1, KSIZE + 1, CONV_W), lambda b, i: (l, 0, 0)),
            vec(CONV_W), vec(CONV_W), vec(CONV_W), vec(SGU_W), vec(SGU_W),
            pl.BlockSpec((1, SGU_HEADS, CHUNK, CHUNK), lambda b, i: (l, 0, 0, 0)),
            pl.BlockSpec((1, CHUNK, SGU_W), lambda b, i: (l, 0, 0)),
            pl.BlockSpec((FOUR_W, 2 * FOUR_W), lambda b, i: (0, 0)),
            pl.BlockSpec((CONV_W, CONV_W), lambda b, i: (0, 0)),
        ],
        out_specs=[
            pl.BlockSpec((1, T, CONV_W + SGU_W), lambda b, i: (b, i, 0)),
            pl.BlockSpec((1, 2, T, FOUR_W), lambda b, i: (b, 0, i, 0)),
        ],
        out_shape=[
            jax.ShapeDtypeStruct((bsz, seq, CONV_W + SGU_W), BF16),
            jax.ShapeDtypeStruct((bsz, 2, seq, FOUR_W), BF16),
        ],
        scratch_shapes=[
            pltpu.VMEM((T + 2 * HALO, CONV_W), F32),
            pltpu.VMEM((8, T + 3 * 8, CONV_W), F32),
            pltpu.VMEM((T, CONV_W), F32),
        ],
        compiler_params=_cparams(("arbitrary", "arbitrary")),
        name="mixer_in",
    )(x, x, x, mod, ln1_g, w_in_bf, conv_w, conv_b, gn_g, gn_b, ln_g, ln_b, sgu_w_bf, sgu_bias, dft64, gavg)


def _dft_fold(scale, pq_ref, fold_ref, ph_ref):
    seq = pq_ref.shape[1] // 2
    half = seq // 2
    nb = seq // CHUNK
    rev = _block_reversal()
    row0 = lax.broadcasted_iota(I32, (CHUNK, FOUR_W), 0) == 0
    alt = jnp.where(lax.broadcasted_iota(I32, (CHUNK, 1), 0) % 2 == 0, 1.0, -1.0)
    alt_sum = jnp.zeros((1, FOUR_W), F32)
    for part, sign in ((0, 1.0), (1, -1.0)):
        base = part * seq
        for m in range(half // CHUNK):
            lo = pq_ref[0, base + CHUNK * m:base + CHUNK * (m + 1), :].astype(F32)
            up = pq_ref[0, base + CHUNK * (nb - 1 - m):base + CHUNK * (nb - m), :]
            mirrored = jnp.dot(rev, up, preferred_element_type=F32)
            if m >= 1:
                first = pq_ref[0, base + CHUNK * (nb - m):base + CHUNK * (nb - m) + 1, :].astype(F32)
                mirrored = jnp.where(row0, first, mirrored)
            folded = (lo + sign * mirrored).astype(BF16)
            fold_ref[part * half + CHUNK * m:part * half + CHUNK * (m + 1), :] = folded
            if part == 0:
                alt_sum = alt_sum + jnp.sum(folded.astype(F32) * alt, axis=0, keepdims=True)
    ph = pq_ref[0, half:half + 1, :].astype(F32) * scale
    rid = lax.broadcasted_iota(I32, (8, FOUR_W), 0)
    ph_ref[...] = jnp.where(rid == 1, alt_sum * scale + ph, ph)


def _block_reversal():
    rr = lax.broadcasted_iota(I32, (CHUNK, CHUNK), 0)
    cc = lax.broadcasted_iota(I32, (CHUNK, CHUNK), 1)
    return jnp.where((rr >= 1) & (cc == CHUNK - rr), 1.0, 0.0).astype(BF16)


def _seq_dft_kernel(scale, cs_ref, pq_ref, lo_ref, hi_ref, fold_scr, ph_scr, carry_scr):
    jj = pl.program_id(0)
    b = pl.program_id(1)
    fold_ref = fold_scr.at[b]
    half = fold_ref.shape[0] // 2

    @pl.when(jj == 0)
    def _():
        _dft_fold(scale, pq_ref, fold_ref, ph_scr.at[b])
        carry_scr[b] = jnp.broadcast_to(ph_scr[b][1:2, :], (8, FOUR_W))

    ph = ph_scr[b][0:1, :]
    alt = jnp.where(lax.broadcasted_iota(I32, (T_DFT, 1), 0) % 2 == 0, 1.0, -1.0)
    a = jnp.dot(cs_ref[:, 0:half], fold_ref[0:half, :], preferred_element_type=F32) + alt * ph
    minus_b = jnp.dot(cs_ref[:, half:2 * half], fold_ref[half:2 * half, :], preferred_element_type=F32)
    lo_ref[0] = (a + minus_b).astype(BF16)
    mirror_src = (a - minus_b).astype(BF16)
    rev = _block_reversal()
    row0 = lax.broadcasted_iota(I32, (CHUNK, FOUR_W), 0) == 0
    nbt = T_DFT // CHUNK
    for m in range(nbt):
        blk = mirror_src[CHUNK * (nbt - 1 - m):CHUNK * (nbt - m), :]
        mirrored = jnp.dot(rev, blk, preferred_element_type=F32)
        if m >= 1:
            first = mirror_src[CHUNK * (nbt - m):CHUNK * (nbt - m) + 1, :].astype(F32)
        else:
            first = carry_scr[b][0:1, :]
        hi_ref[0, CHUNK * m:CHUNK * (m + 1), :] = jnp.where(row0, first, mirrored).astype(BF16)
    carry_scr[b] = jnp.broadcast_to(mirror_src[0:1, :].astype(F32), (8, FOUR_W))


def _seq_dft(cs, pq):
    bsz, two_s, _ = pq.shape
    seq = two_s // 2
    scale = 1.0 / (seq * HEAD) ** 0.5
    n_t = seq // 2 // T_DFT
    half_out = jax.ShapeDtypeStruct((bsz, seq // 2, FOUR_W), BF16)
    return pl.pallas_call(
        functools.partial(_seq_dft_kernel, scale),
        grid=(n_t, bsz),
        in_specs=[pl.BlockSpec((T_DFT, seq), lambda jj, b: (n_t - 1 - jj, 0)),
                  pl.BlockSpec((1, two_s, FOUR_W), lambda jj, b: (jnp.where(jj == 0, b, 0), 0, 0))],
        out_specs=[pl.BlockSpec((1, T_DFT, FOUR_W), lambda jj, b: (b, n_t - 1 - jj, 0)),
                   pl.BlockSpec((1, T_DFT, FOUR_W), lambda jj, b: (b, jj, 0))],
        out_shape=[half_out, half_out],
        scratch_shapes=[pltpu.VMEM((bsz, seq, FOUR_W), BF16), pltpu.VMEM((bsz, 8, FOUR_W), F32),
                        pltpu.VMEM((bsz, 8, FOUR_W), F32)],
        compiler_params=_cparams(("arbitrary", "arbitrary")),
        name="seq_dft",
    )(cs, pq)


def _mixer_out_kernel(per_b, yac_ref, yblo_ref, ybhi_ref, x_ref, mod_ref, g2_ref, wout_ref, rwt_ref, rb_ref, upper_ref,
                      x1_ref, h2pa_ref, h2pb_ref, route_ref, cnt_ref, cnt_scr):
    i = pl.program_id(0)
    T = T_MIX
    yb = jnp.where(i % per_b < per_b // 2, yblo_ref[...], ybhi_ref[...])

    @pl.when(i == 0)
    def _():
        cnt_scr[...] = jnp.zeros_like(cnt_scr)

    mod = mod_ref[0, 0]
    gate1 = mod[2:3, :]
    shift2 = mod[3:4, :]
    gain2 = g2_ref[0] * (1.0 + mod[4:5, :])
    yac = yac_ref[...]
    ycat = jnp.concatenate([yac[:, 0:CONV_W], yb, yac[:, CONV_W:CONV_W + SGU_W]], axis=1)
    o = jnp.dot(ycat, wout_ref[0], preferred_element_type=F32)
    x1 = x_ref[...] + gate1 * o
    x1_ref[...] = x1
    ms = jnp.mean(x1 * x1, axis=-1, keepdims=True)
    h2 = x1 * lax.rsqrt(ms + EPS) * gain2 + shift2
    h2p = _pack_bf16_pair(h2[:, 0:D // 2], h2[:, D // 2:D])
    h2pa_ref[...] = h2p[:, 0:ROW_W]
    h2pb_ref[...] = h2p[:, ROW_W:2 * ROW_W]

    h_hi = h2.astype(BF16)
    h_lo = (h2 - h_hi.astype(F32)).astype(BF16)
    nt = (((1,), (1,)), ((), ()))
    part = lax.dot_general(rwt_ref[...], h_hi, nt, preferred_element_type=F32)
    logits = (part[0:N_EXP] + part[N_EXP:2 * N_EXP]
              + lax.dot_general(rwt_ref[0:N_EXP, :], h_lo, nt, preferred_element_type=F32))
    mx = jnp.max(logits, axis=0, keepdims=True)
    ex = jnp.exp(logits - mx)
    probs = ex / jnp.sum(ex, axis=0, keepdims=True)
    sel = probs + rb_ref[...]
    sel3 = sel.reshape(N_GRP, EPG, T)
    probs3 = probs.reshape(N_GRP, EPG, T)
    jj = lax.broadcasted_iota(I32, (N_GRP, EPG, T), 1)
    m1 = jnp.max(sel3, axis=1, keepdims=True)
    i1 = jnp.min(jnp.where(sel3 == m1, jj, EPG), axis=1, keepdims=True)
    rest = jnp.where(jj == i1, -jnp.inf, sel3)
    m2 = jnp.max(rest, axis=1, keepdims=True)
    i2 = jnp.min(jnp.where(rest == m2, jj, EPG), axis=1, keepdims=True)
    gscore = m1 + m2
    gg = lax.broadcasted_iota(I32, (N_GRP, 1, T), 0)
    gmax = jnp.max(gscore, axis=0, keepdims=True)
    gidx = jnp.min(jnp.where(gscore == gmax, gg, N_GRP), axis=0, keepdims=True)
    ing = gg == gidx
    pick = lambda a, zero: jnp.sum(jnp.where(ing, a, zero), axis=0)
    p1 = jnp.sum(jnp.where(jj == i1, probs3, 0.0), axis=1, keepdims=True)
    p2 = jnp.sum(jnp.where(jj == i2, probs3, 0.0), axis=1, keepdims=True)
    pa = pick(p1, 0.0)
    pb = pick(p2, 0.0)
    gbase = gidx[0] * EPG
    e0 = gbase + pick(i1, 0)
    e1 = gbase + pick(i2, 0)
    den = pa + pb
    gw0 = pa / den
    gw1 = pb / den

    ee = lax.broadcasted_iota(I32, (N_EXP, T), 0)
    oh0 = ee == e0
    oh1 = ee == e1
    amat = jnp.where(oh0 | oh1, 1.0, 0.0)
    before = jnp.dot(amat.astype(BF16), upper_ref[...], preferred_element_type=F32) + cnt_scr[...]
    r0 = jnp.sum(jnp.where(oh0, before, 0.0), axis=0, keepdims=True)
    r1 = jnp.sum(jnp.where(oh1, before, 0.0), axis=0, keepdims=True)
    cnt_scr[...] = cnt_scr[...] + jnp.sum(amat, axis=1, keepdims=True)
    cnt_ref[...] = cnt_scr[...]

    rid = lax.broadcasted_iota(I32, (8, T), 0)
    route = jnp.zeros((8, T), I32)
    bits = lambda w: lax.bitcast_convert_type(w, I32)
    for k, val in enumerate((e0, e1, r0.astype(I32), r1.astype(I32), bits(gw0), bits(gw1))):
        route = jnp.where(rid == k, val, route)
    route_ref[...] = route


def _mixer_out(yac, yb_lo, yb_hi, x, mod, l, ln2_g, w_out_bf, router_wt, router_b, bsz):
    n_tok = x.shape[0]
    T = T_MIX
    per_b = n_tok // bsz // T
    hb = per_b // 2
    row = lambda w: pl.BlockSpec((T, w), lambda i: (i, 0))
    lo_spec = pl.BlockSpec((T, FOUR_W), lambda i: ((i // per_b) * hb + jnp.minimum(i % per_b, hb - 1), 0))
    hi_spec = pl.BlockSpec((T, FOUR_W), lambda i: ((i // per_b) * hb + jnp.maximum(i % per_b - hb, 0), 0))
    upper = (lax.broadcasted_iota(I32, (T, T), 0) < lax.broadcasted_iota(I32, (T, T), 1)).astype(BF16)
    return pl.pallas_call(
        functools.partial(_mixer_out_kernel, per_b),
        grid=(n_tok // T,),
        in_specs=[
            row(CONV_W + SGU_W), lo_spec, hi_spec, row(D),
            pl.BlockSpec((1, 1, 8, D), lambda i: (l, i // per_b, 0, 0)),
            pl.BlockSpec((1, 1, D), lambda i: (l, 0, 0)),
            pl.BlockSpec((1, D, D), lambda i: (l, 0, 0)),
            pl.BlockSpec((2 * N_EXP, D), lambda i: (0, 0)),
            pl.BlockSpec((N_EXP, 1), lambda i: (0, 0)),
            pl.BlockSpec((T, T), lambda i: (0, 0)),
        ],
        out_specs=[row(D), row(ROW_W), row(ROW_W), pl.BlockSpec((8, T), lambda i: (0, i)),
                   pl.BlockSpec((N_EXP, 1), lambda i: (0, 0))],
        out_shape=[
            jax.ShapeDtypeStruct((n_tok, D), F32),
            jax.ShapeDtypeStruct((n_tok, ROW_W), U32),
            jax.ShapeDtypeStruct((n_tok, ROW_W), U32),
            jax.ShapeDtypeStruct((8, n_tok), I32),
            jax.ShapeDtypeStruct((N_EXP, 1), F32),
        ],
        scratch_shapes=[pltpu.VMEM((N_EXP, 1), F32)],
        compiler_params=_cparams(("arbitrary",)),
        name="mixer_out",
    )(yac, yb_lo, yb_hi, x, mod, ln2_g, w_out_bf, router_wt, router_b, upper)


def _sc_mesh():
    return plsc.VectorSubcoreMesh(core_axis_name="c", subcore_axis_name="s")


def _sc_scatter2(src_a, src_b, idx0, idx1, n_rows):
    n = src_a.shape[0]
    out = jax.ShapeDtypeStruct((n_rows, ROW_W), src_a.dtype)

    @functools.partial(pl.kernel, out_type=[out, out], mesh=_sc_mesh())
    def scatter(xa_hbm, xb_hbm, i0_hbm, i1_hbm, oa_hbm, ob_hbm):
        for x_hbm, o_hbm in ((xa_hbm, oa_hbm), (xb_hbm, ob_hbm)):
            def body(x_vmem, i0_vmem, i1_vmem, o_hbm=o_hbm):
                pltpu.sync_copy(x_vmem, o_hbm.at[i0_vmem.at[0]])
                pltpu.sync_copy(x_vmem, o_hbm.at[i1_vmem.at[0]])

            pltpu.emit_pipeline(
                body, grid=(n // SC_WIN,),
                in_specs=[pl.BlockSpec((SC_WIN, ROW_W), index_map=lambda i: (i, 0)),
                          pl.BlockSpec((1, SC_WIN), index_map=lambda i: (0, i)),
                          pl.BlockSpec((1, SC_WIN), index_map=lambda i: (0, i))],
                out_specs=[],
                core_axis_name=("c", "s"), dimension_semantics=(pltpu.PARALLEL,),
            )(x_hbm, i0_hbm, i1_hbm)

    return scatter(src_a, src_b, idx0, idx1)


def _sc_gather(src_a, src_b, idx):
    m = idx.shape[1]
    out = jax.ShapeDtypeStruct((m, ROW_W), src_a.dtype)

    @functools.partial(pl.kernel, out_type=[out, out], mesh=_sc_mesh())
    def gather(xa_hbm, xb_hbm, i_hbm, oa_hbm, ob_hbm):
        for x_hbm, o_hbm in ((xa_hbm, oa_hbm), (xb_hbm, ob_hbm)):
            def body(i_vmem, o_vmem, x_hbm=x_hbm):
                pltpu.sync_copy(x_hbm.at[i_vmem.at[0]], o_vmem)

            pltpu.emit_pipeline(
                body, grid=(m // SC_WIN,),
                in_specs=[pl.BlockSpec((1, SC_WIN), index_map=lambda i: (0, i))],
                out_specs=[pl.BlockSpec((SC_WIN, ROW_W), index_map=lambda i: (i, 0))],
                core_axis_name=("c", "s"), dimension_semantics=(pltpu.PARALLEL,),
            )(i_hbm, o_hbm)

    return gather(src_a, src_b, idx)


def _experts_kernel(l, n_blk, start_ref, nchunk_ref, tail_ref, xpa_ref, xpb_ref, w1_ref, w3_ref, w2_ref, ypa_ref, ypb_ref,
                    w13_scr, w2_scr, wbuf13, wbuf2, xbuf, ybuf, wsem, xsem, ysem):
    e = pl.program_id(0)
    nc = nchunk_ref[e]
    chunk0 = start_ref[e]
    n_used = start_ref[N_EXP]

    def w_copies(ex):
        slot = ex % (W_AHEAD + 1)
        cps = []
        for p in range(W_PIECES):
            r13 = pl.ds(p * (D // W_PIECES), D // W_PIECES)
            r2 = pl.ds(p * (D_FF // W_PIECES), D_FF // W_PIECES)
            cps.append(pltpu.make_async_copy(w1_ref.at[l, ex, r13], wbuf13.at[slot, 0, r13], wsem.at[slot]))
            cps.append(pltpu.make_async_copy(w3_ref.at[l, ex, r13], wbuf13.at[slot, 1, r13], wsem.at[slot]))
            cps.append(pltpu.make_async_copy(w2_ref.at[l, ex, r2], wbuf2.at[slot, r2], wsem.at[slot]))
        return cps

    def x_copies(g):
        rows = pl.ds(pl.multiple_of(g * BLK, BLK), BLK)
        slot = g % (X_AHEAD + 1)
        return [pltpu.make_async_copy(src.at[rows], xbuf.at[slot, h], xsem.at[slot])
                for h, src in enumerate((xpa_ref, xpb_ref))]

    def y_copies(g):
        rows = pl.ds(pl.multiple_of(g * BLK, BLK), BLK)
        slot = g % 2
        return [pltpu.make_async_copy(ybuf.at[slot, h], dst.at[rows], ysem.at[slot])
                for h, dst in enumerate((ypa_ref, ypb_ref))]

    def start(cps):
        for cp in cps:
            cp.start()

    def wait(cps):
        for cp in cps:
            cp.wait()

    @pl.when(e == 0)
    def _():
        for j in range(X_AHEAD):
            @pl.when(j < n_used)
            def _():
                start(x_copies(j))

        for j in range(W_AHEAD):
            start(w_copies(j))

    @pl.when(e < N_EXP)
    def _():
        wslot = e % (W_AHEAD + 1)
        wait(w_copies(e))

        @pl.when(e + W_AHEAD < N_EXP)
        def _():
            start(w_copies(e + W_AHEAD))

        @pl.when(nc > 0)
        def _():
            w13_scr[:, 0:D_FF] = wbuf13[wslot, 0].astype(BF16)
            w13_scr[:, D_FF:2 * D_FF] = wbuf13[wslot, 1].astype(BF16)
            w2_scr[...] = wbuf2[wslot].astype(BF16)

            def chunk(c, carry):
                g = chunk0 + c
                slot = g % (X_AHEAD + 1)

                @pl.when(g + X_AHEAD < n_used)
                def _():
                    start(x_copies(g + X_AHEAD))

                wait(x_copies(g))

                @pl.when(g >= 2)
                def _():
                    wait(y_copies(g - 2))

                def swiglu(rows):
                    a, b = _unpack_bf16_pair(jnp.concatenate([xbuf[slot, 0, 0:rows], xbuf[slot, 1, 0:rows]], axis=1))
                    x = jnp.concatenate([a.astype(BF16), b.astype(BF16)], axis=1)
                    h13 = jnp.dot(x, w13_scr[...], preferred_element_type=F32)
                    h1 = h13[:, 0:D_FF]
                    act = (h1 * jax.nn.sigmoid(h1) * h13[:, D_FF:2 * D_FF]).astype(BF16)
                    y = jnp.dot(act, w2_scr[...], preferred_element_type=F32)
                    yp = _pack_bf16_pair(y[:, 0:D // 2], y[:, D // 2:D])
                    ybuf[g % 2, 0, 0:rows] = yp[:, 0:ROW_W]
                    ybuf[g % 2, 1, 0:rows] = yp[:, ROW_W:2 * ROW_W]

                need = jnp.where(c == nc - 1, (tail_ref[e] + TAIL_STEP - 1) // TAIL_STEP * TAIL_STEP, BLK)
                for rows in range(TAIL_STEP, BLK, TAIL_STEP):
                    @pl.when(need == rows)
                    def _(rows=rows):
                        swiglu(rows)
                        ybuf[g % 2, :, rows:BLK] = jnp.zeros((2, BLK - rows, ROW_W), U32)

                @pl.when(need == BLK)
                def _():
                    swiglu(BLK)

                start(y_copies(g))
                return carry

            lax.fori_loop(0, nc, chunk, 0)

    @pl.when(e == N_EXP)
    def _():
        @pl.when(n_used >= 2)
        def _():
            wait(y_copies(n_used - 2))

        wait(y_copies(n_used - 1))
        ybuf[0] = jnp.zeros((2, BLK, ROW_W), U32)

        def fill_one(g, carry):
            rows = pl.ds(pl.multiple_of(g * BLK, BLK), BLK)
            cps = [pltpu.make_async_copy(ybuf.at[0, h], dst.at[rows], ysem.at[0])
                   for h, dst in enumerate((ypa_ref, ypb_ref))]
            start(cps)
            wait(cps)
            return carry

        lax.fori_loop(n_used, n_blk, fill_one, 0)


def _experts(chunk_start, n_chunk, tail_rows, xpa, xpb, w1, w3, w2, l):
    n_rows = xpa.shape[0]
    n_blk = n_rows // BLK
    hbm = pl.BlockSpec(memory_space=pl.ANY)
    half = jax.ShapeDtypeStruct((n_rows, ROW_W), U32)
    return pl.pallas_call(
        functools.partial(_experts_kernel, l, n_blk),
        grid_spec=pltpu.PrefetchScalarGridSpec(
            num_scalar_prefetch=3,
            grid=(N_EXP + 1,),
            in_specs=[hbm, hbm, hbm, hbm, hbm],
            out_specs=[hbm, hbm],
            scratch_shapes=[pltpu.VMEM((D, 2 * D_FF), BF16), pltpu.VMEM((D_FF, D), BF16),
                            pltpu.VMEM((W_AHEAD + 1, 2, D, D_FF), F32), pltpu.VMEM((W_AHEAD + 1, D_FF, D), F32),
                            pltpu.VMEM((X_AHEAD + 1, 2, BLK, ROW_W), U32), pltpu.VMEM((2, 2, BLK, ROW_W), U32),
                            pltpu.SemaphoreType.DMA((W_AHEAD + 1,)), pltpu.SemaphoreType.DMA((X_AHEAD + 1,)),
                            pltpu.SemaphoreType.DMA((2,))],
        ),
        out_shape=[half, half],
        compiler_params=_cparams(("arbitrary",)),
        name="experts",
    )(chunk_start, n_chunk, tail_rows, xpa, xpb, w1, w3, w2)


def _combine_kernel(final, ga0_ref, gb0_ref, ga1_ref, gb1_ref, x1_ref, route_ref, mod_ref, fg_ref, o_ref):
    gws = lax.bitcast_convert_type(route_ref[...], F32)
    gws = jnp.concatenate([gws, jnp.zeros((CHUNK - 8, T_CMB), F32)], axis=0).T
    gw0 = gws[:, 4:5]
    gw1 = gws[:, 5:6]
    a0, b0 = _unpack_bf16_pair(jnp.concatenate([ga0_ref[...], gb0_ref[...]], axis=1))
    a1, b1 = _unpack_bf16_pair(jnp.concatenate([ga1_ref[...], gb1_ref[...]], axis=1))
    y = jnp.concatenate([gw0 * a0 + gw1 * a1, gw0 * b0 + gw1 * b1], axis=1)
    gate2 = mod_ref[0, 0][5:6, :]
    x2 = x1_ref[...] + gate2 * y
    if final:
        ms = jnp.mean(x2 * x2, axis=-1, keepdims=True)
        x2 = x2 * lax.rsqrt(ms + EPS) * fg_ref[...]
    o_ref[...] = x2


def _combine(ga, gb, x1, route, mod, l, final_g, bsz, final):
    n_tok = x1.shape[0]
    n_i = n_tok // T_CMB
    per_b = n_i // bsz
    first = pl.BlockSpec((T_CMB, ROW_W), lambda i: (i, 0))
    second = pl.BlockSpec((T_CMB, ROW_W), lambda i: (i + n_i, 0))
    return pl.pallas_call(
        functools.partial(_combine_kernel, final),
        grid=(n_i,),
        in_specs=[
            first, first, second, second,
            pl.BlockSpec((T_CMB, D), lambda i: (i, 0)),
            pl.BlockSpec((8, T_CMB), lambda i: (0, i)),
            pl.BlockSpec((1, 1, 8, D), lambda i: (l, i // per_b, 0, 0)),
            pl.BlockSpec((1, D), lambda i: (0, 0)),
        ],
        out_specs=pl.BlockSpec((T_CMB, D), lambda i: (i, 0)),
        out_shape=jax.ShapeDtypeStruct((n_tok, D), F32),
        compiler_params=_cparams(("arbitrary",)),
        name="combine",
    )(ga, gb, ga, gb, x1, route, mod, final_g)


def _dft_table_kernel(t1_ref, t2_ref, o_ref):
    half = t2_ref.shape[2]
    c2 = t2_ref[0]
    s2 = t2_ref[1]
    for j in range(TAB_GROUP):
        c1 = t1_ref[j, 0:1, :]
        s1 = t1_ref[j, 1:2, :]
        rows = slice(j * TAB_ROWS, (j + 1) * TAB_ROWS)
        o_ref[rows, 0:half] = (c1 * c2 - s1 * s2).astype(BF16)
        o_ref[rows, half:2 * half] = (-(s1 * c2 + c1 * s2)).astype(BF16)


def _dft_tables(seq):
    scale = 1.0 / (seq * HEAD) ** 0.5
    n_hi = seq // TAB_ROWS
    n = lax.broadcasted_iota(I32, (1, seq // 2), 1)
    kh = lax.broadcasted_iota(I32, (n_hi // 2, 1), 0)
    a1 = ((kh * n) % n_hi).astype(F32) * (2.0 * jnp.pi / n_hi)
    t1 = jnp.stack([jnp.cos(a1), jnp.sin(a1)], axis=1)
    kl = lax.broadcasted_iota(I32, (TAB_ROWS, 1), 0)
    a2 = ((kl * n) % seq).astype(F32) * (2.0 * jnp.pi / seq)
    t2 = jnp.stack([jnp.cos(a2) * scale, jnp.sin(a2) * scale], axis=0)
    cs = pl.pallas_call(
        _dft_table_kernel,
        grid=(n_hi // 2 // TAB_GROUP,),
        in_specs=[pl.BlockSpec((TAB_GROUP, 2, seq // 2), lambda i: (i, 0, 0)),
                  pl.BlockSpec((2, TAB_ROWS, seq // 2), lambda i: (0, 0, 0))],
        out_specs=pl.BlockSpec((TAB_GROUP * TAB_ROWS, seq), lambda i: (i, 0)),
        out_shape=jax.ShapeDtypeStruct((seq // 2, seq), BF16),
        compiler_params=_cparams(("arbitrary",)),
        name="dft_table",
    )(t1, t2)
    d = lax.broadcasted_iota(I32, (FOUR_W, FOUR_W), 0)
    q = lax.broadcasted_iota(I32, (FOUR_W, FOUR_W), 1)
    same = (d // HEAD) == (q // HEAD)
    ang64 = ((d * q) % HEAD).astype(F32) * (2.0 * jnp.pi / HEAD)
    dft64 = jnp.concatenate([jnp.where(same, jnp.cos(ang64), 0.0),
                             jnp.where(same, jnp.sin(ang64), 0.0)], axis=1).astype(BF16)
    return cs, dft64


def _routing_tables(route, counts_f):
    counts = counts_f[:, 0].astype(I32)
    pc = (counts + BLK - 1) // BLK * BLK
    pends = jnp.cumsum(pc)
    pstarts = pends - pc
    eid = lax.broadcasted_iota(I32, (N_EXP, 1), 0)

    def dest_of(e, r):
        return jnp.sum(jnp.where(e[None, :] == eid, pstarts[:, None], 0), axis=0) + r

    dest = jnp.concatenate([dest_of(route[0], route[2]), dest_of(route[1], route[3])])
    chunk_start = jnp.concatenate([pstarts, pends[-1:]]) // BLK
    n_chunk = jnp.concatenate([pc // BLK, jnp.zeros((1,), I32)])
    tail_rows = jnp.concatenate([counts - (pc - BLK), jnp.zeros((1,), I32)])
    return dest, chunk_start, n_chunk, tail_rows


def kernel(x, c, ln1_g, ln2_g, w_ada, b_ada, w_in, w_out, conv_w, conv_b, conv_gn_g, conv_gn_b, sgu_ln_g,
           sgu_ln_b, sgu_w, sgu_b, router_w, router_b, exp_w1, exp_w3, exp_w2, final_g):
    bsz, seq, _ = x.shape
    depth = w_in.shape[0]
    n_tok = bsz * seq
    n_blk = (2 * n_tok + N_EXP * (BLK - 1) + BLK - 1) // BLK
    r3 = lambda a: a.reshape(depth, 1, a.shape[-1])

    mod = jnp.pad(_ada_mod(c, w_ada, b_ada).reshape(depth, bsz, 6, D), ((0, 0), (0, 0), (0, 2), (0, 0)))
    cs, dft64 = _dft_tables(seq)
    hd = lax.broadcasted_iota(I32, (CONV_W, CONV_W), 0) // HEAD
    gavg = jnp.where(hd == hd.T, 1.0 / HEAD, 0.0).astype(BF16)
    w_in_bf = w_in.astype(BF16)
    w_out_bf = w_out.astype(BF16)
    sgu_w_bf = sgu_w.astype(BF16)
    sgu_bias = jnp.repeat(jnp.swapaxes(sgu_b, 1, 2), HEAD, axis=2)
    conv_w_p = jnp.pad(conv_w, ((0, 0), (0, 1), (0, 0)))
    rwt_hi = router_w.T.astype(BF16)
    rwt_lo = (router_w.T - rwt_hi.astype(F32)).astype(BF16)
    router_wt = jnp.concatenate([rwt_hi, rwt_lo], axis=0)
    router_bc = router_b.reshape(N_EXP, 1)
    fg = final_g.reshape(1, D)

    for l in range(depth):
        yac, pq = _mixer_in(x, mod, l, r3(ln1_g), w_in_bf, conv_w_p, r3(conv_b), r3(conv_gn_g), r3(conv_gn_b),
                            r3(sgu_ln_g), r3(sgu_ln_b), sgu_w_bf, sgu_bias, dft64, gavg)
        yb_lo, yb_hi = _seq_dft(cs, pq.reshape(bsz, 2 * seq, FOUR_W))
        x1, h2pa, h2pb, route, counts = _mixer_out(
            yac.reshape(n_tok, CONV_W + SGU_W), yb_lo.reshape(n_tok // 2, FOUR_W),
            yb_hi.reshape(n_tok // 2, FOUR_W), x.reshape(n_tok, D),
            mod, l, r3(ln2_g), w_out_bf, router_wt, router_bc, bsz)
        dest, chunk_start, n_chunk, tail_rows = _routing_tables(route, counts)
        d0 = dest[:n_tok].reshape(1, n_tok)
        d1 = dest[n_tok:].reshape(1, n_tok)
        xpa, xpb = _sc_scatter2(h2pa, h2pb, d0, d1, n_blk * BLK)
        ypa, ypb = _experts(chunk_start, n_chunk, tail_rows, xpa, xpb, exp_w1, exp_w3, exp_w2, l)
        ga, gb = _sc_gather(ypa, ypb, dest.reshape(1, 2 * n_tok))
        x = _combine(ga, gb, x1, route, mod, l, fg, bsz, l == depth - 1).reshape(bsz, seq, D)
    return x
```

```python
import functools

import jax
import jax.numpy as jnp
from jax import lax
from jax.experimental import pallas as pl
from jax.experimental.pallas import tpu as pltpu
from jax.experimental.pallas import tpu_sc as plsc

F32 = jnp.float32
BF16 = jnp.bfloat16
I32 = jnp.int32
U32 = jnp.uint32

D = 1024
HEAD = 64
CONV_W = 384
FOUR_W = 256
SGU_W = 384
SGU_HEADS = SGU_W // HEAD
Z_COLS = 2 * CONV_W + FOUR_W + 2 * SGU_W
KSIZE = 31
HALO = 16
CHUNK = 128
N_EXP = 64
N_GRP = 8
EPG = N_EXP // N_GRP
D_FF = D // 2
EPS = 1e-6
GELU_C = 0.7978845608028654
GELU_A = 0.044715

T_MIX = 1024
T_DFT = 1024
T_CMB = 1024
ROW_W = 256
SC_WIN = 128
BLK = 384
X_AHEAD = 3
W_AHEAD = 1
CONV_ROWS = 64
TAB_ROWS = 64
TAB_GROUP = 4
W_PIECES = 4
VMEM_LIMIT = 56 * 1024 * 1024


def _cparams(sem):
    return pltpu.CompilerParams(dimension_semantics=sem, vmem_limit_bytes=VMEM_LIMIT)


def _pack_bf16_pair(a, b):
    ua = lax.bitcast_convert_type(a.astype(BF16).astype(F32), U32) >> 16
    ub = lax.bitcast_convert_type(b.astype(BF16).astype(F32), U32) & jnp.uint32(0xFFFF0000)
    return ua | ub


def _unpack_bf16_pair(p):
    a = lax.bitcast_convert_type(p << 16, F32)
    b = lax.bitcast_convert_type(p & jnp.uint32(0xFFFF0000), F32)
    return a, b


def _ada_kernel(c_ref, *refs):
    w_refs, b_ref, o_ref = refs[:-2], refs[-2], refs[-1]
    c = c_ref[...]
    ca = c * jax.nn.sigmoid(c)
    ca_hi = ca.astype(BF16)
    ca_lo = (ca - ca_hi.astype(F32)).astype(BF16)
    tn = w_refs[0].shape[2]
    for j, w_ref in enumerate(w_refs):
        cols = slice(j * tn, (j + 1) * tn)
        w = w_ref[0]
        w_hi = w.astype(BF16)
        w_lo = (w - w_hi.astype(F32)).astype(BF16)
        acc = jnp.dot(ca_hi, w_hi, preferred_element_type=F32)
        acc = acc + jnp.dot(ca_hi, w_lo, preferred_element_type=F32)
        acc = acc + jnp.dot(ca_lo, w_hi, preferred_element_type=F32)
        o_ref[0, :, cols] = acc + b_ref[0, :, cols]


def _ada_mod(c, w_ada, b_ada):
    depth, _, ncol = w_ada.shape
    bsz = c.shape[0]
    n_slab = 4
    n_half = 2
    tn = ncol // (n_slab * n_half)
    return pl.pallas_call(
        _ada_kernel,
        grid=(depth, n_half),
        in_specs=[pl.BlockSpec((bsz, D), lambda l, h: (0, 0))]
        + [pl.BlockSpec((1, D, tn), functools.partial(lambda j, l, h: (l, 0, n_slab * h + j), j))
           for j in range(n_slab)]
        + [pl.BlockSpec((1, 1, n_slab * tn), lambda l, h: (l, 0, h))],
        out_specs=pl.BlockSpec((1, bsz, n_slab * tn), lambda l, h: (l, 0, h)),
        out_shape=jax.ShapeDtypeStruct((depth, bsz, ncol), F32),
        compiler_params=_cparams(("arbitrary", "arbitrary")),
        name="ada_mod",
    )(c, *([w_ada] * n_slab), b_ada.reshape(depth, 1, ncol))


def _mixer_in_kernel(xm_ref, xp_ref, xn_ref, mod_ref, g1_ref, win_ref, cw_ref, cb_ref, gng_ref, gnb_ref,
                     lng_ref, lnb_ref, sw_ref, sb_ref, dft_ref, gavg_ref,
                     yac_ref, pq_ref, glu_scr, sh_scr, conv_scr):
    i = pl.program_id(1)
    n_i = pl.num_programs(1)
    T = T_MIX
    mod = mod_ref[0, 0]
    shift1 = mod[0:1, :]
    gain1 = g1_ref[0] * (1.0 + mod[1:2, :])

    def norm_mod(x):
        ms = jnp.mean(x * x, axis=-1, keepdims=True)
        return x * lax.rsqrt(ms + EPS) * gain1 + shift1

    h = norm_mod(xm_ref[0]).astype(BF16)
    z = jnp.dot(h, win_ref[0], preferred_element_type=F32)

    hh = norm_mod(jnp.concatenate([xp_ref[0], xn_ref[0]], axis=0)).astype(BF16)
    zh = jnp.dot(hh, win_ref[0, :, 0:2 * CONV_W], preferred_element_type=F32)
    glu_h = zh[:, 0:CONV_W] * jax.nn.sigmoid(zh[:, CONV_W:2 * CONV_W])
    glu_scr[0:HALO, :] = jnp.where(i > 0, glu_h[0:HALO], 0.0)
    glu_scr[HALO + T:2 * HALO + T, :] = jnp.where(i < n_i - 1, glu_h[HALO:2 * HALO], 0.0)
    glu_scr[HALO:HALO + T, :] = z[:, 0:CONV_W] * jax.nn.sigmoid(z[:, CONV_W:2 * CONV_W])

    off = HALO - KSIZE // 2
    for b in range(8):
        sh_scr[b] = glu_scr[b:b + T + 3 * 8, :]

    for c in range(T // CONV_ROWS):
        r0 = c * CONV_ROWS
        acc = jnp.broadcast_to(cb_ref[0], (CONV_ROWS, CONV_W))
        for k in range(KSIZE):
            a, b = divmod(k + off, 8)
            acc = acc + sh_scr[b, r0 + 8 * a:r0 + 8 * a + CONV_ROWS, :] * cw_ref[0, k:k + 1, :]
        conv_scr[r0:r0 + CONV_ROWS, :] = acc
    hc = conv_scr[...]
    gavg = gavg_ref[...]
    mu = jnp.dot(hc.astype(BF16), gavg, preferred_element_type=F32)
    dc = hc - mu
    var = jnp.dot((dc * dc).astype(BF16), gavg, preferred_element_type=F32)
    hn = dc * lax.rsqrt(var + EPS) * gng_ref[0] + gnb_ref[0]
    ya = hn * jax.nn.sigmoid(hn)
    yac_ref[0, :, 0:CONV_W] = ya.astype(BF16)

    zb = z[:, 2 * CONV_W:2 * CONV_W + FOUR_W].astype(BF16)
    pq = jnp.dot(zb, dft_ref[...], preferred_element_type=F32)
    pq_ref[0, 0] = pq[:, 0:FOUR_W].astype(BF16)
    pq_ref[0, 1] = pq[:, FOUR_W:2 * FOUR_W].astype(BF16)

    c0 = 2 * CONV_W + FOUR_W
    zc = z[:, c0:c0 + 2 * SGU_W]
    zc = 0.5 * zc * (1.0 + jnp.tanh(GELU_C * (zc + GELU_A * (zc * zc * zc))))
    u = zc[:, 0:SGU_W]
    v = zc[:, SGU_W:2 * SGU_W]
    vm = jnp.mean(v, axis=-1, keepdims=True)
    vd = v - vm
    vv = jnp.mean(vd * vd, axis=-1, keepdims=True)
    vn = (vd * lax.rsqrt(vv + EPS) * lng_ref[0] + lnb_ref[0]).astype(BF16)
    n_chunk = T // CHUNK
    lane = lax.broadcasted_iota(I32, (CHUNK, 2 * HEAD), 1)
    for pr in range(SGU_HEADS // 2):
        cols = slice(2 * HEAD * pr, 2 * HEAD * (pr + 1))
        rhs = jnp.concatenate([vn[n * CHUNK:(n + 1) * CHUNK, cols] for n in range(n_chunk)], axis=1)
        lo = jnp.dot(sw_ref[0, 2 * pr], rhs, preferred_element_type=F32)
        hi = jnp.dot(sw_ref[0, 2 * pr + 1], rhs, preferred_element_type=F32)
        for n in range(n_chunk):
            sl = slice(n * 2 * HEAD, (n + 1) * 2 * HEAD)
            vs = jnp.where(lane < HEAD, lo[:, sl], hi[:, sl]) + sb_ref[0, :, cols]
            rows = slice(n * CHUNK, (n + 1) * CHUNK)
            yac_ref[0, rows, CONV_W + 2 * HEAD * pr:CONV_W + 2 * HEAD * (pr + 1)] = (u[rows, cols] * vs).astype(BF16)


def _mixer_in(x, mod, l, ln1_g, w_in_bf, conv_w, conv_b, gn_g, gn_b, ln_g, ln_b, sgu_w_bf, sgu_bias, dft64, gavg):
    bsz, seq, _ = x.shape
    T = T_MIX
    n_i = seq // T
    hb = T // HALO
    n_h = seq // HALO
    vec = lambda w: pl.BlockSpec((1, 1, w), lambda b, i: (l, 0, 0))
    return pl.pallas_call(
        _mixer_in_kernel,
        grid=(bsz, n_i),
        in_specs=[
            pl.BlockSpec((1, T, D), lambda b, i: (b, i, 0)),
            pl.BlockSpec((1, HALO, D), lambda b, i: (b, jnp.maximum(i * hb - 1, 0), 0)),
            pl.BlockSpec((1, HALO, D), lambda b, i: (b, jnp.minimum((i + 1) * hb, n_h - 1), 0)),
            pl.BlockSpec((1, 1, 8, D), lambda b, i: (l, b, 0, 0)),
            vec(D),
            pl.BlockSpec((1, D, Z_COLS), lambda b, i: (l, 0, 0)),
            pl.BlockSpec((1, KSIZE + 1, CONV_W), lambda b, i: (l, 0, 0)),
            vec(CONV_W), vec(CONV_W), vec(CONV_W), vec(SGU_W), vec(SGU_W),
            pl.BlockSpec((1, SGU_HEADS, CHUNK, CHUNK), lambda b, i: (l, 0, 0, 0)),
            pl.BlockSpec((1, CHUNK, SGU_W), lambda b, i: (l, 0, 0)),
            pl.BlockSpec((FOUR_W, 2 * FOUR_W), lambda b, i: (0, 0)),
            pl.BlockSpec((CONV_W, CONV_W), lambda b, i: (0, 0)),
        ],
        out_specs=[
            pl.BlockSpec((1, T, CONV_W + SGU_W), lambda b, i: (b, i, 0)),
            pl.BlockSpec((1, 2, T, FOUR_W), lambda b, i: (b, 0, i, 0)),
        ],
        out_shape=[
            jax.ShapeDtypeStruct((bsz, seq, CONV_W + SGU_W), BF16),
            jax.ShapeDtypeStruct((bsz, 2, seq, FOUR_W), BF16),
        ],
        scratch_shapes=[
            pltpu.VMEM((T + 2 * HALO, CONV_W), F32),
            pltpu.VMEM((8, T + 3 * 8, CONV_W), F32),
            pltpu.VMEM((T, CONV_W), F32),
        ],
        compiler_params=_cparams(("arbitrary", "arbitrary")),
        name="mixer_in",
    )(x, x, x, mod, ln1_g, w_in_bf, conv_w, conv_b, gn_g, gn_b, ln_g, ln_b, sgu_w_bf, sgu_bias, dft64, gavg)


def _dft_fold(scale, pq_ref, fold_ref, ph_ref):
    seq = pq_ref.shape[1] // 2
    half = seq // 2
    nb = seq // CHUNK
    rev = _block_reversal()
    row0 = lax.broadcasted_iota(I32, (CHUNK, FOUR_W), 0) == 0
    alt = jnp.where(lax.broadcasted_iota(I32, (CHUNK, 1), 0) % 2 == 0, 1.0, -1.0)
    alt_sum = jnp.zeros((1, FOUR_W), F32)
    for part, sign in ((0, 1.0), (1, -1.0)):
        base = part * seq
        for m in range(half // CHUNK):
            lo = pq_ref[0, base + CHUNK * m:base + CHUNK * (m + 1), :].astype(F32)
            up = pq_ref[0, base + CHUNK * (nb - 1 - m):base + CHUNK * (nb - m), :]
            mirrored = jnp.dot(rev, up, preferred_element_type=F32)
            if m >= 1:
                first = pq_ref[0, base + CHUNK * (nb - m):base + CHUNK * (nb - m) + 1, :].astype(F32)
                mirrored = jnp.where(row0, first, mirrored)
            folded = (lo + sign * mirrored).astype(BF16)
            fold_ref[part * half + CHUNK * m:part * half + CHUNK * (m + 1), :] = folded
            if part == 0:
                alt_sum = alt_sum + jnp.sum(folded.astype(F32) * alt, axis=0, keepdims=True)
    ph = pq_ref[0, half:half + 1, :].astype(F32) * scale
    rid = lax.broadcasted_iota(I32, (8, FOUR_W), 0)
    ph_ref[...] = jnp.where(rid == 1, alt_sum * scale + ph, ph)


def _block_reversal():
    rr = lax.broadcasted_iota(I32, (CHUNK, CHUNK), 0)
    cc = lax.broadcasted_iota(I32, (CHUNK, CHUNK), 1)
    return jnp.where((rr >= 1) & (cc == CHUNK - rr), 1.0, 0.0).astype(BF16)


def _seq_dft_kernel(scale, cs_ref, pq_ref, lo_ref, hi_ref, fold_scr, ph_scr, carry_scr):
    jj = pl.program_id(0)
    b = pl.program_id(1)
    fold_ref = fold_scr.at[b]
    half = fold_ref.shape[0] // 2

    @pl.when(jj == 0)
    def _():
        _dft_fold(scale, pq_ref, fold_ref, ph_scr.at[b])
        carry_scr[b] = jnp.broadcast_to(ph_scr[b][1:2, :], (8, FOUR_W))

    ph = ph_scr[b][0:1, :]
    alt = jnp.where(lax.broadcasted_iota(I32, (T_DFT, 1), 0) % 2 == 0, 1.0, -1.0)
    a = jnp.dot(cs_ref[:, 0:half], fold_ref[0:half, :], preferred_element_type=F32) + alt * ph
    minus_b = jnp.dot(cs_ref[:, half:2 * half], fold_ref[half:2 * half, :], preferred_element_type=F32)
    lo_ref[0] = (a + minus_b).astype(BF16)
    mirror_src = (a - minus_b).astype(BF16)
    rev = _block_reversal()
    row0 = lax.broadcasted_iota(I32, (CHUNK, FOUR_W), 0) == 0
    nbt = T_DFT // CHUNK
    for m in range(nbt):
        blk = mirror_src[CHUNK * (nbt - 1 - m):CHUNK * (nbt - m), :]
        mirrored = jnp.dot(rev, blk, preferred_element_type=F32)
        if m >= 1:
            first = mirror_src[CHUNK * (nbt - m):CHUNK * (nbt - m) + 1, :].astype(F32)
        else:
            first = carry_scr[b][0:1, :]
        hi_ref[0, CHUNK * m:CHUNK * (m + 1), :] = jnp.where(row0, first, mirrored).astype(BF16)
    carry_scr[b] = jnp.broadcast_to(mirror_src[0:1, :].astype(F32), (8, FOUR_W))


def _seq_dft(cs, pq):
    bsz, two_s, _ = pq.shape
    seq = two_s // 2
    scale = 1.0 / (seq * HEAD) ** 0.5
    n_t = seq // 2 // T_DFT
    half_out = jax.ShapeDtypeStruct((bsz, seq // 2, FOUR_W), BF16)
    return pl.pallas_call(
        functools.partial(_seq_dft_kernel, scale),
        grid=(n_t, bsz),
        in_specs=[pl.BlockSpec((T_DFT, seq), lambda jj, b: (n_t - 1 - jj, 0)),
                  pl.BlockSpec((1, two_s, FOUR_W), lambda jj, b: (jnp.where(jj == 0, b, 0), 0, 0))],
        out_specs=[pl.BlockSpec((1, T_DFT, FOUR_W), lambda jj, b: (b, n_t - 1 - jj, 0)),
                   pl.BlockSpec((1, T_DFT, FOUR_W), lambda jj, b: (b, jj, 0))],
        out_shape=[half_out, half_out],
        scratch_shapes=[pltpu.VMEM((bsz, seq, FOUR_W), BF16), pltpu.VMEM((bsz, 8, FOUR_W), F32),
                        pltpu.VMEM((bsz, 8, FOUR_W), F32)],
        compiler_params=_cparams(("arbitrary", "arbitrary")),
        name="seq_dft",
    )(cs, pq)


def _mixer_out_kernel(per_b, yac_ref, yblo_ref, ybhi_ref, x_ref, mod_ref, g2_ref, wout_ref, rwt_ref, rb_ref, upper_ref,
                      x1_ref, h2pa_ref, h2pb_ref, route_ref, cnt_ref, cnt_scr):
    i = pl.program_id(0)
    T = T_MIX
    yb = jnp.where(i % per_b < per_b // 2, yblo_ref[...], ybhi_ref[...])

    @pl.when(i == 0)
    def _():
        cnt_scr[...] = jnp.zeros_like(cnt_scr)

    mod = mod_ref[0, 0]
    gate1 = mod[2:3, :]
    shift2 = mod[3:4, :]
    gain2 = g2_ref[0] * (1.0 + mod[4:5, :])
    yac = yac_ref[...]
    ycat = jnp.concatenate([yac[:, 0:CONV_W], yb, yac[:, CONV_W:CONV_W + SGU_W]], axis=1)
    o = jnp.dot(ycat, wout_ref[0], preferred_element_type=F32)
    x1 = x_ref[...] + gate1 * o
    x1_ref[...] = x1
    ms = jnp.mean(x1 * x1, axis=-1, keepdims=True)
    h2 = x1 * lax.rsqrt(ms + EPS) * gain2 + shift2
    h2p = _pack_bf16_pair(h2[:, 0:D // 2], h2[:, D // 2:D])
    h2pa_ref[...] = h2p[:, 0:ROW_W]
    h2pb_ref[...] = h2p[:, ROW_W:2 * ROW_W]

    h_hi = h2.astype(BF16)
    h_lo = (h2 - h_hi.astype(F32)).astype(BF16)
    nt = (((1,), (1,)), ((), ()))
    part = lax.dot_general(rwt_ref[...], h_hi, nt, preferred_element_type=F32)
    logits = (part[0:N_EXP] + part[N_EXP:2 * N_EXP]
              + lax.dot_general(rwt_ref[0:N_EXP, :], h_lo, nt, preferred_element_type=F32))
    mx = jnp.max(logits, axis=0, keepdims=True)
    ex = jnp.exp(logits - mx)
    probs = ex / jnp.sum(ex, axis=0, keepdims=True)
    sel = probs + rb_ref[...]
    sel3 = sel.reshape(N_GRP, EPG, T)
    probs3 = probs.reshape(N_GRP, EPG, T)
    jj = lax.broadcasted_iota(I32, (N_GRP, EPG, T), 1)
    m1 = jnp.max(sel3, axis=1, keepdims=True)
    i1 = jnp.min(jnp.where(sel3 == m1, jj, EPG), axis=1, keepdims=True)
    rest = jnp.where(jj == i1, -jnp.inf, sel3)
    m2 = jnp.max(rest, axis=1, keepdims=True)
    i2 = jnp.min(jnp.where(rest == m2, jj, EPG), axis=1, keepdims=True)
    gscore = m1 + m2
    gg = lax.broadcasted_iota(I32, (N_GRP, 1, T), 0)
    gmax = jnp.max(gscore, axis=0, keepdims=True)
    gidx = jnp.min(jnp.where(gscore == gmax, gg, N_GRP), axis=0, keepdims=True)
    ing = gg == gidx
    pick = lambda a, zero: jnp.sum(jnp.where(ing, a, zero), axis=0)
    p1 = jnp.sum(jnp.where(jj == i1, probs3, 0.0), axis=1, keepdims=True)
    p2 = jnp.sum(jnp.where(jj == i2, probs3, 0.0), axis=1, keepdims=True)
    pa = pick(p1, 0.0)
    pb = pick(p2, 0.0)
    gbase = gidx[0] * EPG
    e0 = gbase + pick(i1, 0)
    e1 = gbase + pick(i2, 0)
    den = pa + pb
    gw0 = pa / den
    gw1 = pb / den

    ee = lax.broadcasted_iota(I32, (N_EXP, T), 0)
    oh0 = ee == e0
    oh1 = ee == e1
    amat = jnp.where(oh0 | oh1, 1.0, 0.0)
    before = jnp.dot(amat.astype(BF16), upper_ref[...], preferred_element_type=F32) + cnt_scr[...]
    r0 = jnp.sum(jnp.where(oh0, before, 0.0), axis=0, keepdims=True)
    r1 = jnp.sum(jnp.where(oh1, before, 0.0), axis=0, keepdims=True)
    cnt_scr[...] = cnt_scr[...] + jnp.sum(amat, axis=1, keepdims=True)
    cnt_ref[...] = cnt_scr[...]

    rid = lax.broadcasted_iota(I32, (8, T), 0)
    route = jnp.zeros((8, T), I32)
    bits = lambda w: lax.bitcast_convert_type(w, I32)
    for k, val in enumerate((e0, e1, r0.astype(I32), r1.astype(I32), bits(gw0), bits(gw1))):
        route = jnp.where(rid == k, val, route)
    route_ref[...] = route


def _mixer_out(yac, yb_lo, yb_hi, x, mod, l, ln2_g, w_out_bf, router_wt, router_b, bsz):
    n_tok = x.shape[0]
    T = T_MIX
    per_b = n_tok // bsz // T
    hb = per_b // 2
    row = lambda w: pl.BlockSpec((T, w), lambda i: (i, 0))
    lo_spec = pl.BlockSpec((T, FOUR_W), lambda i: ((i // per_b) * hb + jnp.minimum(i % per_b, hb - 1), 0))
    hi_spec = pl.BlockSpec((T, FOUR_W), lambda i: ((i // per_b) * hb + jnp.maximum(i % per_b - hb, 0), 0))
    upper = (lax.broadcasted_iota(I32, (T, T), 0) < lax.broadcasted_iota(I32, (T, T), 1)).astype(BF16)
    return pl.pallas_call(
        functools.partial(_mixer_out_kernel, per_b),
        grid=(n_tok // T,),
        in_specs=[
            row(CONV_W + SGU_W), lo_spec, hi_spec, row(D),
            pl.BlockSpec((1, 1, 8, D), lambda i: (l, i // per_b, 0, 0)),
            pl.BlockSpec((1, 1, D), lambda i: (l, 0, 0)),
            pl.BlockSpec((1, D, D), lambda i: (l, 0, 0)),
            pl.BlockSpec((2 * N_EXP, D), lambda i: (0, 0)),
            pl.BlockSpec((N_EXP, 1), lambda i: (0, 0)),
            pl.BlockSpec((T, T), lambda i: (0, 0)),
        ],
        out_specs=[row(D), row(ROW_W), row(ROW_W), pl.BlockSpec((8, T), lambda i: (0, i)),
                   pl.BlockSpec((N_EXP, 1), lambda i: (0, 0))],
        out_shape=[
            jax.ShapeDtypeStruct((n_tok, D), F32),
            jax.ShapeDtypeStruct((n_tok, ROW_W), U32),
            jax.ShapeDtypeStruct((n_tok, ROW_W), U32),
            jax.ShapeDtypeStruct((8, n_tok), I32),
            jax.ShapeDtypeStruct((N_EXP, 1), F32),
        ],
        scratch_shapes=[pltpu.VMEM((N_EXP, 1), F32)],
        compiler_params=_cparams(("arbitrary",)),
        name="mixer_out",
    )(yac, yb_lo, yb_hi, x, mod, ln2_g, w_out_bf, router_wt, router_b, upper)


def _sc_mesh():
    return plsc.VectorSubcoreMesh(core_axis_name="c", subcore_axis_name="s")


def _sc_scatter2(src_a, src_b, idx0, idx1, n_rows):
    n = src_a.shape[0]
    out = jax.ShapeDtypeStruct((n_rows, ROW_W), src_a.dtype)

    @functools.partial(pl.kernel, out_type=[out, out], mesh=_sc_mesh())
    def scatter(xa_hbm, xb_hbm, i0_hbm, i1_hbm, oa_hbm, ob_hbm):
        for x_hbm, o_hbm in ((xa_hbm, oa_hbm), (xb_hbm, ob_hbm)):
            def body(x_vmem, i0_vmem, i1_vmem, o_hbm=o_hbm):
                pltpu.sync_copy(x_vmem, o_hbm.at[i0_vmem.at[0]])
                pltpu.sync_copy(x_vmem, o_hbm.at[i1_vmem.at[0]])

            pltpu.emit_pipeline(
                body, grid=(n // SC_WIN,),
                in_specs=[pl.BlockSpec((SC_WIN, ROW_W), index_map=lambda i: (i, 0)),
                          pl.BlockSpec((1, SC_WIN), index_map=lambda i: (0, i)),
                          pl.BlockSpec((1, SC_WIN), index_map=lambda i: (0, i))],
                out_specs=[],
                core_axis_name=("c", "s"), dimension_semantics=(pltpu.PARALLEL,),
            )(x_hbm, i0_hbm, i1_hbm)

    return scatter(src_a, src_b, idx0, idx1)


def _sc_gather(src_a, src_b, idx):
    m = idx.shape[1]
    out = jax.ShapeDtypeStruct((m, ROW_W), src_a.dtype)

    @functools.partial(pl.kernel, out_type=[out, out], mesh=_sc_mesh())
    def gather(xa_hbm, xb_hbm, i_hbm, oa_hbm, ob_hbm):
        for x_hbm, o_hbm in ((xa_hbm, oa_hbm), (xb_hbm, ob_hbm)):
            def body(i_vmem, o_vmem, x_hbm=x_hbm):
                pltpu.sync_copy(x_hbm.at[i_vmem.at[0]], o_vmem)

            pltpu.emit_pipeline(
                body, grid=(m // SC_WIN,),
                in_specs=[pl.BlockSpec((1, SC_WIN), index_map=lambda i: (0, i))],
                out_specs=[pl.BlockSpec((SC_WIN, ROW_W), index_map=lambda i: (i, 0))],
                core_axis_name=("c", "s"), dimension_semantics=(pltpu.PARALLEL,),
            )(i_hbm, o_hbm)

    return gather(src_a, src_b, idx)


def _experts_kernel(l, n_blk, start_ref, nchunk_ref, tail_ref, xpa_ref, xpb_ref, w1_ref, w3_ref, w2_ref, ypa_ref, ypb_ref,
                    w13_scr, w2_scr, wbuf13, wbuf2, xbuf, ybuf, wsem, xsem, ysem):
    e = pl.program_id(0)
    nc = nchunk_ref[e]
    chunk0 = start_ref[e]
    n_used = start_ref[N_EXP]

    def w_copies(ex):
        slot = ex % (W_AHEAD + 1)
        cps = []
        for p in range(W_PIECES):
            r13 = pl.ds(p * (D // W_PIECES), D // W_PIECES)
            r2 = pl.ds(p * (D_FF // W_PIECES), D_FF // W_PIECES)
            cps.append(pltpu.make_async_copy(w1_ref.at[l, ex, r13], wbuf13.at[slot, 0, r13], wsem.at[slot]))
            cps.append(pltpu.make_async_copy(w3_ref.at[l, ex, r13], wbuf13.at[slot, 1, r13], wsem.at[slot]))
            cps.append(pltpu.make_async_copy(w2_ref.at[l, ex, r2], wbuf2.at[slot, r2], wsem.at[slot]))
        return cps

    def x_copies(g):
        rows = pl.ds(pl.multiple_of(g * BLK, BLK), BLK)
        slot = g % (X_AHEAD + 1)
        return [pltpu.make_async_copy(src.at[rows], xbuf.at[slot, h], xsem.at[slot])
                for h, src in enumerate((xpa_ref, xpb_ref))]

    def y_copies(g):
        rows = pl.ds(pl.multiple_of(g * BLK, BLK), BLK)
        slot = g % 2
        return [pltpu.make_async_copy(ybuf.at[slot, h], dst.at[rows], ysem.at[slot])
                for h, dst in enumerate((ypa_ref, ypb_ref))]

    def start(cps):
        for cp in cps:
            cp.start()

    def wait(cps):
        for cp in cps:
            cp.wait()

    @pl.when(e == 0)
    def _():
        for j in range(X_AHEAD):
            @pl.when(j < n_used)
            def _():
                start(x_copies(j))

        for j in range(W_AHEAD):
            start(w_copies(j))

    @pl.when(e < N_EXP)
    def _():
        wslot = e % (W_AHEAD + 1)
        wait(w_copies(e))

        @pl.when(e + W_AHEAD < N_EXP)
        def _():
            start(w_copies(e + W_AHEAD))

        @pl.when(nc > 0)
        def _():
            w13_scr[:, 0:D_FF] = wbuf13[wslot, 0].astype(BF16)
            w13_scr[:, D_FF:2 * D_FF] = wbuf13[wslot, 1].astype(BF16)
            w2_scr[...] = wbuf2[wslot].astype(BF16)

            def chunk(c, carry):
                g = chunk0 + c
                slot = g % (X_AHEAD + 1)

                @pl.when(g + X_AHEAD < n_used)
                def _():
                    start(x_copies(g + X_AHEAD))

                wait(x_copies(g))

                @pl.when(g >= 2)
                def _():
                    wait(y_copies(g - 2))

                def swiglu(rows):
                    a, b = _unpack_bf16_pair(jnp.concatenate([xbuf[slot, 0, 0:rows], xbuf[slot, 1, 0:rows]], axis=1))
                    x = jnp.concatenate([a.astype(BF16), b.astype(BF16)], axis=1)
                    h13 = jnp.dot(x, w13_scr[...], preferred_element_type=F32)
                    h1 = h13[:, 0:D_FF]
                    act = (h1 * jax.nn.sigmoid(h1) * h13[:, D_FF:2 * D_FF]).astype(BF16)
                    y = jnp.dot(act, w2_scr[...], preferred_element_type=F32)
                    yp = _pack_bf16_pair(y[:, 0:D // 2], y[:, D // 2:D])
                    ybuf[g % 2, 0, 0:rows] = yp[:, 0:ROW_W]
                    ybuf[g % 2, 1, 0:rows] = yp[:, ROW_W:2 * ROW_W]

                short = (c == nc - 1) & (tail_ref[e] <= BLK // 2)

                @pl.when(short)
                def _():
                    swiglu(BLK // 2)
                    ybuf[g % 2, :, BLK // 2:BLK] = jnp.zeros((2, BLK // 2, ROW_W), U32)

                @pl.when(jnp.logical_not(short))
                def _():
                    swiglu(BLK)

                start(y_copies(g))
                return carry

            lax.fori_loop(0, nc, chunk, 0)

    @pl.when(e == N_EXP)
    def _():
        @pl.when(n_used >= 2)
        def _():
            wait(y_copies(n_used - 2))

        wait(y_copies(n_used - 1))
        ybuf[0] = jnp.zeros((2, BLK, ROW_W), U32)

        def fill_one(g, carry):
            rows = pl.ds(pl.multiple_of(g * BLK, BLK), BLK)
            cps = [pltpu.make_async_copy(ybuf.at[0, h], dst.at[rows], ysem.at[0])
                   for h, dst in enumerate((ypa_ref, ypb_ref))]
            start(cps)
            wait(cps)
            return carry

        lax.fori_loop(n_used, n_blk, fill_one, 0)


def _experts(chunk_start, n_chunk, tail_rows, xpa, xpb, w1, w3, w2, l):
    n_rows = xpa.shape[0]
    n_blk = n_rows // BLK
    hbm = pl.BlockSpec(memory_space=pl.ANY)
    half = jax.ShapeDtypeStruct((n_rows, ROW_W), U32)
    return pl.pallas_call(
        functools.partial(_experts_kernel, l, n_blk),
        grid_spec=pltpu.PrefetchScalarGridSpec(
            num_scalar_prefetch=3,
            grid=(N_EXP + 1,),
            in_specs=[hbm, hbm, hbm, hbm, hbm],
            out_specs=[hbm, hbm],
            scratch_shapes=[pltpu.VMEM((D, 2 * D_FF), BF16), pltpu.VMEM((D_FF, D), BF16),
                            pltpu.VMEM((W_AHEAD + 1, 2, D, D_FF), F32), pltpu.VMEM((W_AHEAD + 1, D_FF, D), F32),
                            pltpu.VMEM((X_AHEAD + 1, 2, BLK, ROW_W), U32), pltpu.VMEM((2, 2, BLK, ROW_W), U32),
                            pltpu.SemaphoreType.DMA((W_AHEAD + 1,)), pltpu.SemaphoreType.DMA((X_AHEAD + 1,)),
                            pltpu.SemaphoreType.DMA((2,))],
        ),
        out_shape=[half, half],
        compiler_params=_cparams(("arbitrary",)),
        name="experts",
    )(chunk_start, n_chunk, tail_rows, xpa, xpb, w1, w3, w2)


def _combine_kernel(final, ga0_ref, gb0_ref, ga1_ref, gb1_ref, x1_ref, route_ref, mod_ref, fg_ref, o_ref):
    gws = lax.bitcast_convert_type(route_ref[...], F32)
    gws = jnp.concatenate([gws, jnp.zeros((CHUNK - 8, T_CMB), F32)], axis=0).T
    gw0 = gws[:, 4:5]
    gw1 = gws[:, 5:6]
    a0, b0 = _unpack_bf16_pair(jnp.concatenate([ga0_ref[...], gb0_ref[...]], axis=1))
    a1, b1 = _unpack_bf16_pair(jnp.concatenate([ga1_ref[...], gb1_ref[...]], axis=1))
    y = jnp.concatenate([gw0 * a0 + gw1 * a1, gw0 * b0 + gw1 * b1], axis=1)
    gate2 = mod_ref[0, 0][5:6, :]
    x2 = x1_ref[...] + gate2 * y
    if final:
        ms = jnp.mean(x2 * x2, axis=-1, keepdims=True)
        x2 = x2 * lax.rsqrt(ms + EPS) * fg_ref[...]
    o_ref[...] = x2


def _combine(ga, gb, x1, route, mod, l, final_g, bsz, final):
    n_tok = x1.shape[0]
    n_i = n_tok // T_CMB
    per_b = n_i // bsz
    first = pl.BlockSpec((T_CMB, ROW_W), lambda i: (i, 0))
    second = pl.BlockSpec((T_CMB, ROW_W), lambda i: (i + n_i, 0))
    return pl.pallas_call(
        functools.partial(_combine_kernel, final),
        grid=(n_i,),
        in_specs=[
            first, first, second, second,
            pl.BlockSpec((T_CMB, D), lambda i: (i, 0)),
            pl.BlockSpec((8, T_CMB), lambda i: (0, i)),
            pl.BlockSpec((1, 1, 8, D), lambda i: (l, i // per_b, 0, 0)),
            pl.BlockSpec((1, D), lambda i: (0, 0)),
        ],
        out_specs=pl.BlockSpec((T_CMB, D), lambda i: (i, 0)),
        out_shape=jax.ShapeDtypeStruct((n_tok, D), F32),
        compiler_params=_cparams(("arbitrary",)),
        name="combine",
    )(ga, gb, ga, gb, x1, route, mod, final_g)


def _dft_table_kernel(t1_ref, t2_ref, o_ref):
    half = t2_ref.shape[2]
    c2 = t2_ref[0]
    s2 = t2_ref[1]
    for j in range(TAB_GROUP):
        c1 = t1_ref[j, 0:1, :]
        s1 = t1_ref[j, 1:2, :]
        rows = slice(j * TAB_ROWS, (j + 1) * TAB_ROWS)
        o_ref[rows, 0:half] = (c1 * c2 - s1 * s2).astype(BF16)
        o_ref[rows, half:2 * half] = (-(s1 * c2 + c1 * s2)).astype(BF16)


def _dft_tables(seq):
    scale = 1.0 / (seq * HEAD) ** 0.5
    n_hi = seq // TAB_ROWS
    n = lax.broadcasted_iota(I32, (1, seq // 2), 1)
    kh = lax.broadcasted_iota(I32, (n_hi // 2, 1), 0)
    a1 = ((kh * n) % n_hi).astype(F32) * (2.0 * jnp.pi / n_hi)
    t1 = jnp.stack([jnp.cos(a1), jnp.sin(a1)], axis=1)
    kl = lax.broadcasted_iota(I32, (TAB_ROWS, 1), 0)
    a2 = ((kl * n) % seq).astype(F32) * (2.0 * jnp.pi / seq)
    t2 = jnp.stack([jnp.cos(a2) * scale, jnp.sin(a2) * scale], axis=0)
    cs = pl.pallas_call(
        _dft_table_kernel,
        grid=(n_hi // 2 // TAB_GROUP,),
        in_specs=[pl.BlockSpec((TAB_GROUP, 2, seq // 2), lambda i: (i, 0, 0)),
                  pl.BlockSpec((2, TAB_ROWS, seq // 2), lambda i: (0, 0, 0))],
        out_specs=pl.BlockSpec((TAB_GROUP * TAB_ROWS, seq), lambda i: (i, 0)),
        out_shape=jax.ShapeDtypeStruct((seq // 2, seq), BF16),
        compiler_params=_cparams(("arbitrary",)),
        name="dft_table",
    )(t1, t2)
    d = lax.broadcasted_iota(I32, (FOUR_W, FOUR_W), 0)
    q = lax.broadcasted_iota(I32, (FOUR_W, FOUR_W), 1)
    same = (d // HEAD) == (q // HEAD)
    ang64 = ((d * q) % HEAD).astype(F32) * (2.0 * jnp.pi / HEAD)
    dft64 = jnp.concatenate([jnp.where(same, jnp.cos(ang64), 0.0),
                             jnp.where(same, jnp.sin(ang64), 0.0)], axis=1).astype(BF16)
    return cs, dft64


def _routing_tables(route, counts_f):
    counts = counts_f[:, 0].astype(I32)
    pc = (counts + BLK - 1) // BLK * BLK
    pends = jnp.cumsum(pc)
    pstarts = pends - pc
    eid = lax.broadcasted_iota(I32, (N_EXP, 1), 0)

    def dest_of(e, r):
        return jnp.sum(jnp.where(e[None, :] == eid, pstarts[:, None], 0), axis=0) + r

    dest = jnp.concatenate([dest_of(route[0], route[2]), dest_of(route[1], route[3])])
    chunk_start = jnp.concatenate([pstarts, pends[-1:]]) // BLK
    n_chunk = jnp.concatenate([pc // BLK, jnp.zeros((1,), I32)])
    tail_rows = jnp.concatenate([counts - (pc - BLK), jnp.zeros((1,), I32)])
    return dest, chunk_start, n_chunk, tail_rows


def kernel(x, c, ln1_g, ln2_g, w_ada, b_ada, w_in, w_out, conv_w, conv_b, conv_gn_g, conv_gn_b, sgu_ln_g,
           sgu_ln_b, sgu_w, sgu_b, router_w, router_b, exp_w1, exp_w3, exp_w2, final_g):
    bsz, seq, _ = x.shape
    depth = w_in.shape[0]
    n_tok = bsz * seq
    n_blk = (2 * n_tok + N_EXP * (BLK - 1) + BLK - 1) // BLK
    r3 = lambda a: a.reshape(depth, 1, a.shape[-1])

    mod = jnp.pad(_ada_mod(c, w_ada, b_ada).reshape(depth, bsz, 6, D), ((0, 0), (0, 0), (0, 2), (0, 0)))
    cs, dft64 = _dft_tables(seq)
    hd = lax.broadcasted_iota(I32, (CONV_W, CONV_W), 0) // HEAD
    gavg = jnp.where(hd == hd.T, 1.0 / HEAD, 0.0).astype(BF16)
    w_in_bf = w_in.astype(BF16)
    w_out_bf = w_out.astype(BF16)
    sgu_w_bf = sgu_w.astype(BF16)
    sgu_bias = jnp.repeat(jnp.swapaxes(sgu_b, 1, 2), HEAD, axis=2)
    conv_w_p = jnp.pad(conv_w, ((0, 0), (0, 1), (0, 0)))
    rwt_hi = router_w.T.astype(BF16)
    rwt_lo = (router_w.T - rwt_hi.astype(F32)).astype(BF16)
    router_wt = jnp.concatenate([rwt_hi, rwt_lo], axis=0)
    router_bc = router_b.reshape(N_EXP, 1)
    fg = final_g.reshape(1, D)

    for l in range(depth):
        yac, pq = _mixer_in(x, mod, l, r3(ln1_g), w_in_bf, conv_w_p, r3(conv_b), r3(conv_gn_g), r3(conv_gn_b),
                            r3(sgu_ln_g), r3(sgu_ln_b), sgu_w_bf, sgu_bias, dft64, gavg)
        yb_lo, yb_hi = _seq_dft(cs, pq.reshape(bsz, 2 * seq, FOUR_W))
        x1, h2pa, h2pb, route, counts = _mixer_out(
            yac.reshape(n_tok, CONV_W + SGU_W), yb_lo.reshape(n_tok // 2, FOUR_W),
            yb_hi.reshape(n_tok // 2, FOUR_W), x.reshape(n_tok, D),
            mod, l, r3(ln2_g), w_out_bf, router_wt, router_bc, bsz)
        dest, chunk_start, n_chunk, tail_rows = _routing_tables(route, counts)
        d0 = dest[:n_tok].reshape(1, n_tok)
        d1 = dest[n_tok:].reshape(1, n_tok)
        xpa, xpb = _sc_scatter2(h2pa, h2pb, d0, d1, n_blk * BLK)
        ypa, ypb = _experts(chunk_start, n_chunk, tail_rows, xpa, xpb, exp_w1, exp_w3, exp_w2, l)
        ga, gb = _sc_gather(ypa, ypb, dest.reshape(1, 2 * n_tok))
        x = _combine(ga, gb, x1, route, mod, l, fg, bsz, l == depth - 1).reshape(bsz, seq, D)
    return x
```
